```python
import math
import jax, jax.numpy as jnp
from jax import lax
import numpy as np

D_MODEL = 1024
BATCH = 8
SEQ = 2048
DEPTH = 2
DEC_BATCH = 128
DEC_SEQ = 1
PAST_LEN = 16384
PAGE_SIZE = 128

N_EVEN = (DEPTH + 1) // 2
N_ODD = DEPTH // 2
MIX_WIDTH = D_MODEL
HG_HEADS = 4
HG_DK = 128
HG_DV = MIX_WIDTH // 2 // HG_HEADS
GLA_HEADS = 4
GLA_DV = MIX_WIDTH // 2 // GLA_HEADS
GLA_DK = GLA_DV // 2
GLA_RANK = 16
GLA_GATE_NORM = 16.0
ML_HEADS = 4
ML_DV = MIX_WIDTH // ML_HEADS
ML_DK = ML_DV // 2
CONV_W = 4
ML_QK = 2 * ML_HEADS * ML_DK
D_FF_DENSE = 2816
D_FF_EXPERT = 3584
N_EXPERTS = 8
TOP_K = 2
CHUNK = 64
EPS = 1e-5
ALPHA = (2.0 * DEPTH) ** 0.25
BETA = (8.0 * DEPTH) ** -0.25

EVEN_WIDTHS = (HG_HEADS * HG_DK, HG_HEADS * HG_DK, HG_HEADS * HG_DV, HG_HEADS * HG_DV,
               GLA_HEADS * GLA_DK, GLA_HEADS * GLA_DK, GLA_HEADS * GLA_DV, GLA_HEADS * GLA_DV,
               GLA_RANK)
EVEN_IN = (2 * HG_HEADS * HG_DK + 2 * HG_HEADS * HG_DV + 2 * GLA_HEADS * GLA_DK
           + 2 * GLA_HEADS * GLA_DV + GLA_RANK)
ODD_WIDTHS = (ML_QK, ML_HEADS * ML_DV, ML_HEADS * ML_DV, 2 * ML_HEADS)
ODD_IN = ML_QK + 2 * ML_HEADS * ML_DV + 2 * ML_HEADS

kernel_name = "hgrn2_gla_mlstm_moe_decoder_step"

F32 = jnp.float32


def _split(z, widths):
    out, start = [], 0
    for w in widths:
        out.append(z[..., start:start + w])
        start += w
    return out


def _layernorm(x, g, b):
    xf = x.astype(F32)
    mu = jnp.mean(xf, -1, keepdims=True)
    var = jnp.mean(jnp.square(xf - mu), -1, keepdims=True)
    return ((xf - mu) * lax.rsqrt(var + EPS) * g.astype(F32) + b.astype(F32)).astype(x.dtype)


def _rms_swish_gate(o, gate, w):
    bsz, L, H, dv = o.shape
    o = o * lax.rsqrt(jnp.mean(jnp.square(o), -1, keepdims=True) + EPS) * w.astype(F32)
    return (o * jax.nn.silu(gate.reshape(bsz, L, H, dv))).reshape(bsz, L, H * dv)


def _to_chunks(t, nc, cs):
    bsz = t.shape[0]
    return jnp.moveaxis(t.reshape(bsz, nc, cs, *t.shape[2:]), 1, 0)


def _gated_linear_attention(q, k, v, log_a, s0):
    bsz, L, H, _ = q.shape
    cs = math.gcd(L, CHUNK)
    nc = L // cs
    causal = jnp.tril(jnp.ones((cs, cs), dtype=bool))

    def step(s, inp):
        qc, kc, vc, gc = inp
        bc = jnp.cumsum(gc, axis=1)
        o_inter = jnp.einsum('bthk,bhkv->bthv', qc * jnp.exp(bc), s)
        log_d = bc[:, :, None] - bc[:, None, :]
        decay = jnp.exp(jnp.where(causal[None, :, :, None, None], log_d, -jnp.inf))
        scores = jnp.einsum('bthk,bshk,btshk->btsh', qc, kc, decay)
        o = o_inter + jnp.einsum('btsh,bshv->bthv', scores, vc)
        b_last = bc[:, -1]
        s_new = (jnp.exp(b_last)[..., None] * s
                 + jnp.einsum('bshk,bshv->bhkv', kc * jnp.exp(b_last[:, None] - bc), vc))
        return s_new, o

    xs = tuple(_to_chunks(t, nc, cs) for t in (q, k, v, log_a))
    s_fin, o = lax.scan(step, s0.astype(F32), xs)
    o = jnp.moveaxis(o, 0, 1).reshape(bsz, L, H, v.shape[-1])
    return o, s_fin.astype(s0.dtype)


def _mlstm(q, k, v, ig, lf, C0, n0, m0):
    bsz, L, H, _ = q.shape
    cs = math.gcd(L, CHUNK)
    nc = L // cs
    causal = jnp.tril(jnp.ones((cs, cs), dtype=bool))

    def step(carry, inp):
        C, n, m = carry
        qc, kc, vc, ic, fc = inp
        b = jnp.cumsum(fc, axis=1)
        log_d = b[:, :, None, :] - b[:, None, :, :] + ic[:, None, :, :]
        log_d = jnp.where(causal[None, :, :, None], log_d, -jnp.inf)
        log_prev = b + m[:, None, :]
        m_t = jnp.maximum(jnp.max(log_d, axis=2), log_prev)
        d = jnp.exp(log_d - m_t[:, :, None, :])
        w_prev = jnp.exp(log_prev - m_t)
        scores = jnp.einsum('bthk,bshk->btsh', qc, kc) * d
        num = (jnp.einsum('btsh,bshv->bthv', scores, vc)
               + w_prev[..., None] * jnp.einsum('bthk,bhkv->bthv', qc, C))
        den = jnp.sum(scores, axis=2) + w_prev * jnp.einsum('bthk,bhk->bth', qc, n)
        h = num / jnp.maximum(jnp.abs(den), jnp.exp(-m_t))[..., None]
        m_new = m_t[:, -1]
        w_c = jnp.exp(b[:, -1] + m - m_new)
        w_s = jnp.exp(b[:, -1:, :] - b + ic - m_new[:, None, :])
        C_new = w_c[..., None, None] * C + jnp.einsum('bsh,bshk,bshv->bhkv', w_s, kc, vc)
        n_new = w_c[..., None] * n + jnp.einsum('bsh,bshk->bhk', w_s, kc)
        return (C_new, n_new, m_new), h

    xs = tuple(_to_chunks(t, nc, cs) for t in (q, k, v, ig, lf))
    (C, n, m), h = lax.scan(step, (C0.astype(F32), n0.astype(F32), m0.astype(F32)), xs)
    h = jnp.moveaxis(h, 0, 1).reshape(bsz, L, H, v.shape[-1])
    return h, C.astype(C0.dtype), n.astype(n0.dtype), m.astype(m0.dtype)


def _even_mixer(x, s_hg, s_gla, w_in, lb, w_gk, b_gk, gn_hg, gn_gla, w_out):
    bsz, L, _ = x.shape
    z = (x @ w_in).astype(F32)
    hq, hf, hi, hg, gq, gk, gv, gg, gr = _split(z, EVEN_WIDTHS)
    heads = lambda t, h: t.reshape(bsz, L, h, -1)
    f = lb + (1.0 - lb) * jax.nn.sigmoid(hf)
    one_minus_f = (1.0 - lb) * jax.nn.sigmoid(-hf)
    o_hg, s_hg_new = _gated_linear_attention(
        heads(jax.nn.silu(hq), HG_HEADS), heads(one_minus_f, HG_HEADS),
        heads(hi, HG_HEADS), heads(jnp.log(f), HG_HEADS), s_hg)
    log_a = jax.nn.log_sigmoid(gr @ w_gk.astype(F32) + b_gk.astype(F32)) / GLA_GATE_NORM
    o_gla, s_gla_new = _gated_linear_attention(
        heads(gq * GLA_DK ** -0.5, GLA_HEADS), heads(gk, GLA_HEADS),
        heads(gv, GLA_HEADS), heads(log_a, GLA_HEADS), s_gla)
    y = jnp.concatenate([_rms_swish_gate(o_hg, hg, gn_hg), _rms_swish_gate(o_gla, gg, gn_gla)], axis=-1)
    return y.astype(x.dtype) @ w_out, s_hg_new, s_gla_new


def _odd_mixer(x, C0, n0, m0, conv0, w_in, b_gate, conv_w, conv_b, hn_w, w_out):
    bsz, L, _ = x.shape
    z = (x @ w_in).astype(F32)
    u, v, o_pre, gates = _split(z, ODD_WIDTHS)
    full = jnp.concatenate([conv0.astype(F32), u], axis=1)
    cw = conv_w.astype(F32)
    uc = conv_b.astype(F32)
    for j in range(CONV_W):
        uc = uc + full[:, j:j + L] * cw[j]
    q, k = _split(jax.nn.silu(uc), (ML_HEADS * ML_DK, ML_HEADS * ML_DK))
    gates = gates + b_gate.astype(F32)
    ig, fg = gates[..., :ML_HEADS], gates[..., ML_HEADS:]
    heads = lambda t: t.reshape(bsz, L, ML_HEADS, -1)
    h, C, n, m = _mlstm(heads(q * ML_DK ** -0.5), heads(k), heads(v), ig,
                        jax.nn.log_sigmoid(fg), C0, n0, m0)
    mu = jnp.mean(h, -1, keepdims=True)
    var = jnp.mean(jnp.square(h - mu), -1, keepdims=True)
    h = (h - mu) * lax.rsqrt(var + EPS) * hn_w.astype(F32).reshape(ML_HEADS, ML_DV)
    y = jax.nn.sigmoid(o_pre) * h.reshape(bsz, L, ML_HEADS * ML_DV)
    return y.astype(x.dtype) @ w_out, C, n, m, full[:, L:].astype(conv0.dtype)


def _swiglu(x, w1, w3, w2):
    return (jax.nn.silu(x @ w1) * (x @ w3)) @ w2


def _moe(x, w_router, w1, w3, w2):
    logits = (x @ w_router).astype(F32)
    top_v, top_i = lax.top_k(logits, TOP_K)
    top_w = jax.nn.softmax(top_v, axis=-1)
    gate = jnp.sum(jax.nn.one_hot(top_i, N_EXPERTS, dtype=F32) * top_w[..., None], axis=-2)
    y = jnp.zeros(x.shape, F32)
    for e in range(N_EXPERTS):
        y = y + gate[..., e:e + 1] * _swiglu(x, w1[e], w3[e], w2[e]).astype(F32)
    return y.astype(x.dtype)


def _trunk(x, s_hg, s_gla, s_C, s_n, s_m, s_conv, p):
    lb_all = jnp.cumsum(jax.nn.softmax(p['hg_lower_bounds'].astype(F32), axis=0), axis=0)
    hg_out, gla_out, C_out, n_out, m_out, conv_out = [], [], [], [], [], []
    for layer in range(DEPTH):
        i = layer // 2
        if layer % 2 == 0:
            mix, hg_new, gla_new = _even_mixer(
                x, s_hg[i], s_gla[i], p['w_in_even'][i], lb_all[i], p['w_gk'][i], p['b_gk'][i],
                p['gn_hg'][i], p['gn_gla'][i], p['w_out_even'][i])
            hg_out.append(hg_new)
            gla_out.append(gla_new)
            x = _layernorm(ALPHA * x + mix, p['ln1_g'][layer], p['ln1_b'][layer])
            ffn = _swiglu(x, p['w1_dense'][i], p['w3_dense'][i], p['w2_dense'][i])
        else:
            mix, C_new, n_new, m_new, conv_new = _odd_mixer(
                x, s_C[i], s_n[i], s_m[i], s_conv[i], p['w_in_odd'][i], p['b_gate_odd'][i],
                p['conv_w'][i], p['conv_b'][i], p['hn_w'][i], p['w_out_odd'][i])
            C_out.append(C_new)
            n_out.append(n_new)
            m_out.append(m_new)
            conv_out.append(conv_new)
            x = _layernorm(ALPHA * x + mix, p['ln1_g'][layer], p['ln1_b'][layer])
            ffn = _moe(x, p['w_router'][i], p['w1_moe'][i], p['w3_moe'][i], p['w2_moe'][i])
        x = _layernorm(ALPHA * x + ffn, p['ln2_g'][layer], p['ln2_b'][layer])
    return (x, jnp.stack(hg_out), jnp.stack(gla_out), jnp.stack(C_out), jnp.stack(n_out),
            jnp.stack(m_out), jnp.stack(conv_out))


def setup_inputs(seed: int = 0) -> dict:
    key = jax.random.key(seed)
    ks = list(jax.random.split(key, 48))
    nrm = lambda shape, scale: jax.random.normal(ks.pop(), shape, F32) * scale
    D = D_MODEL
    b_gate = jnp.concatenate([
        -1.0 + nrm((N_ODD, ML_HEADS), 0.1),
        jnp.linspace(3.0, 6.0, ML_HEADS, dtype=F32)[None] + nrm((N_ODD, ML_HEADS), 0.1)], axis=-1)
    return {
        "x_prompt": nrm((BATCH, SEQ, D), 1.0),
        "x_sample": nrm((DEC_BATCH, DEC_SEQ, D), 1.0),
        "state_hgrn": nrm((N_EVEN, DEC_BATCH, HG_HEADS, HG_DK, HG_DV), 0.5),
        "state_gla": nrm((N_EVEN, DEC_BATCH, GLA_HEADS, GLA_DK, GLA_DV), 0.5),
        "state_mlstm_C": nrm((N_ODD, DEC_BATCH, ML_HEADS, ML_DK, ML_DV), 0.3),
        "state_mlstm_n": nrm((N_ODD, DEC_BATCH, ML_HEADS, ML_DK), 0.3),
        "state_mlstm_m": nrm((N_ODD, DEC_BATCH, ML_HEADS), 1.0),
        "state_mlstm_conv": nrm((N_ODD, DEC_BATCH, CONV_W - 1, ML_QK), 1.0),
        "w_in_even": nrm((N_EVEN, D, EVEN_IN), D ** -0.5),
        "hg_lower_bounds": nrm((N_EVEN + 1, HG_HEADS * HG_DK), 0.1),
        "w_gk": nrm((N_EVEN, GLA_RANK, GLA_HEADS * GLA_DK), GLA_RANK ** -0.5),
        "b_gk": nrm((N_EVEN, GLA_HEADS * GLA_DK), 0.1),
        "gn_hg": 1.0 + nrm((N_EVEN, HG_DV), 0.01),
        "gn_gla": 1.0 + nrm((N_EVEN, GLA_DV), 0.01),
        "w_out_even": nrm((N_EVEN, MIX_WIDTH, D), MIX_WIDTH ** -0.5 * BETA),
        "w1_dense": nrm((N_EVEN, D, D_FF_DENSE), D ** -0.5),
        "w3_dense": nrm((N_EVEN, D, D_FF_DENSE), D ** -0.5),
        "w2_dense": nrm((N_EVEN, D_FF_DENSE, D), D_FF_DENSE ** -0.5 * BETA),
        "w_in_odd": nrm((N_ODD, D, ODD_IN), D ** -0.5),
        "b_gate_odd": b_gate,
        "conv_w": nrm((N_ODD, CONV_W, ML_QK), CONV_W ** -0.5),
        "conv_b": nrm((N_ODD, ML_QK), 0.01),
        "hn_w": 1.0 + nrm((N_ODD, ML_HEADS * ML_DV), 0.01),
        "w_out_odd": nrm((N_ODD, ML_HEADS * ML_DV, D), (ML_HEADS * ML_DV) ** -0.5 * BETA),
        "w_router": nrm((N_ODD, D, N_EXPERTS), D ** -0.5),
        "w1_moe": nrm((N_ODD, N_EXPERTS, D, D_FF_EXPERT), D ** -0.5),
        "w3_moe": nrm((N_ODD, N_EXPERTS, D, D_FF_EXPERT), D ** -0.5),
        "w2_moe": nrm((N_ODD, N_EXPERTS, D_FF_EXPERT, D), D_FF_EXPERT ** -0.5 * BETA),
        "ln1_g": 1.0 + nrm((DEPTH, D), 0.01),
        "ln1_b": nrm((DEPTH, D), 0.01),
        "ln2_g": 1.0 + nrm((DEPTH, D), 0.01),
        "ln2_b": nrm((DEPTH, D), 0.01),
    }


def reference(x_prompt, x_sample, state_hgrn, state_gla, state_mlstm_C, state_mlstm_n,
              state_mlstm_m, state_mlstm_conv, w_in_even, hg_lower_bounds, w_gk, b_gk,
              gn_hg, gn_gla, w_out_even, w1_dense, w3_dense, w2_dense, w_in_odd, b_gate_odd,
              conv_w, conv_b, hn_w, w_out_odd, w_router, w1_moe, w3_moe, w2_moe,
              ln1_g, ln1_b, ln2_g, ln2_b):
    params = dict(w_in_even=w_in_even, hg_lower_bounds=hg_lower_bounds, w_gk=w_gk, b_gk=b_gk,
                  gn_hg=gn_hg, gn_gla=gn_gla, w_out_even=w_out_even, w1_dense=w1_dense,
                  w3_dense=w3_dense, w2_dense=w2_dense, w_in_odd=w_in_odd, b_gate_odd=b_gate_odd,
                  conv_w=conv_w, conv_b=conv_b, hn_w=hn_w, w_out_odd=w_out_odd,
                  w_router=w_router, w1_moe=w1_moe, w3_moe=w3_moe, w2_moe=w2_moe,
                  ln1_g=ln1_g, ln1_b=ln1_b, ln2_g=ln2_g, ln2_b=ln2_b)
    z_hg = jnp.zeros((N_EVEN, BATCH, HG_HEADS, HG_DK, HG_DV), state_hgrn.dtype)
    z_gla = jnp.zeros((N_EVEN, BATCH, GLA_HEADS, GLA_DK, GLA_DV), state_gla.dtype)
    z_C = jnp.zeros((N_ODD, BATCH, ML_HEADS, ML_DK, ML_DV), state_mlstm_C.dtype)
    z_n = jnp.zeros((N_ODD, BATCH, ML_HEADS, ML_DK), state_mlstm_n.dtype)
    z_m = jnp.zeros((N_ODD, BATCH, ML_HEADS), state_mlstm_m.dtype)
    z_conv = jnp.zeros((N_ODD, BATCH, CONV_W - 1, ML_QK), state_mlstm_conv.dtype)
    y_prompt, hg_p, gla_p, C_p, n_p, m_p, conv_p = _trunk(
        x_prompt, z_hg, z_gla, z_C, z_n, z_m, z_conv, params)
    y_sample, hg_s, gla_s, C_s, n_s, m_s, conv_s = _trunk(
        x_sample, state_hgrn, state_gla, state_mlstm_C, state_mlstm_n, state_mlstm_m,
        state_mlstm_conv, params)
    return (y_prompt, y_sample, hg_p, gla_p, C_p, n_p, m_p, conv_p,
            hg_s, gla_s, C_s, n_s, m_s, conv_s)
```

```python
import functools
import math

import jax
import jax.numpy as jnp
import numpy as np
from jax import lax
from jax.experimental import pallas as pl
from jax.experimental.pallas import tpu as pltpu

F32 = jnp.float32
BF16 = jnp.bfloat16

D = 1024
BATCH = 8
SEQ = 2048
DEC_BATCH = 128
NP = BATCH * SEQ
NS = DEC_BATCH
NT = NP + NS
HG_H, HG_DK, HG_DV = 4, 128, 128
GLA_H, GLA_DK, GLA_DV = 4, 64, 128
GLA_RANK = 16
GLA_GATE_NORM = 16.0
ML_H, ML_DK, ML_DV = 4, 128, 256
CONV_W = 4
D_FF_DENSE = 2816
D_FF_EXPERT = 3584
N_EXPERTS = 8
EPS = 1e-5
DEPTH = 2
ALPHA = (2.0 * DEPTH) ** 0.25
EVEN_MAIN = 3584
ODD_MAIN = 3072

LANES = 128
VMEM_LIMIT = 56 * 1024 * 1024

TM = 384
CS = 128
NCHUNK = SEQ // CS
SG = 16
TMM = 512
TFF = 512
A_PAD = 2 * NT + N_EXPERTS * TMM
A_PAD = -(-A_PAD // TMM) * TMM
N_MOE_TILES = A_PAD // TMM
N_LEVELS = int(math.log2(CS))
POS_BLK = 1024

assert NT % TM == 0 and NP % CS == 0 and NS % SG == 0 and D_FF_EXPERT % TFF == 0


def _params(sem, limit=VMEM_LIMIT):
    return pltpu.CompilerParams(dimension_semantics=sem, vmem_limit_bytes=limit)


def _dot(a, b):
    return jnp.dot(a, b, preferred_element_type=F32)


def _dot_nt(a, b):
    return lax.dot_general(a, b, (((1,), (1,)), ((), ())), preferred_element_type=F32)


def _dot_tn(a, b):
    return lax.dot_general(a, b, (((0,), (0,)), ((), ())), preferred_element_type=F32)


def _split3(x):
    hi = x.astype(BF16)
    r1 = x - hi.astype(F32)
    mid = r1.astype(BF16)
    lo = (r1 - mid.astype(F32)).astype(BF16)
    return hi, mid, lo


def _dot_sel(sel, x):
    hi, mid, lo = _split3(x)
    return _dot(sel, hi) + _dot(sel, mid) + _dot(sel, lo)


def _sigmoid(x):
    return jax.nn.sigmoid(x)


def _silu(x):
    return x * jax.nn.sigmoid(x)


def _log_sigmoid(x):
    return jnp.minimum(x, 0.0) - jnp.log(1.0 + jnp.exp(-jnp.abs(x)))


def _layernorm(r, g, b):
    mu = jnp.mean(r, axis=-1, keepdims=True)
    c = r - mu
    var = jnp.mean(c * c, axis=-1, keepdims=True)
    return c * lax.rsqrt(var + EPS) * g + b


def _gla_matrices():
    sel = np.zeros(((2 + N_LEVELS) * CS, CS), np.float32)
    masks = np.zeros((N_LEVELS + 1, CS, CS), np.float32)
    for t in range(CS):
        sel[t, : t + 1] = 1.0
        sel[CS + t, t + 1:] = 1.0
        for l in range(N_LEVELS):
            half = 1 << l
            start = (t // (2 * half)) * (2 * half)
            mid = start + half
            row = (2 + l) * CS + t
            if t >= mid:
                sel[row, mid: t + 1] = 1.0
                masks[l, t, start:mid] = 1.0
            else:
                sel[row, t + 1: mid] = 1.0
        masks[N_LEVELS, t, t] = 1.0
    return sel, masks


def _tri(n, strict):
    return np.tril(np.ones((n, n), np.float32), -1 if strict else 0)


def _proj_kernel(x_ref, wa_ref, wb_ref, oa_ref, ob_ref):
    xb = x_ref[...].astype(BF16)
    oa_ref[...] = _dot(xb, wa_ref[...])
    ob_ref[...] = _dot(xb, wb_ref[...])


def _proj(x, wa, wb):
    na, nb = wa.shape[1], wb.shape[1]
    return pl.pallas_call(
        _proj_kernel,
        grid=(NT // TM,),
        in_specs=[
            pl.BlockSpec((TM, D), lambda i: (i, 0)),
            pl.BlockSpec((D, na), lambda i: (0, 0)),
            pl.BlockSpec((D, nb), lambda i: (0, 0)),
        ],
        out_specs=[
            pl.BlockSpec((TM, na), lambda i: (i, 0)),
            pl.BlockSpec((TM, nb), lambda i: (i, 0)),
        ],
        out_shape=[jax.ShapeDtypeStruct((NT, na), F32), jax.ShapeDtypeStruct((NT, nb), F32)],
        compiler_params=_params(("parallel",)),
        name="proj",
    )(x, wa, wb)


def _rms_gate(o, gate, w):
    o = o * lax.rsqrt(jnp.mean(o * o, axis=-1, keepdims=True) + EPS) * w
    return o * _silu(gate)


def _gla_chunk(q, k, v, g, st_ref, sel, masks_ref, heads, dk, dv):
    e = _dot_sel(sel, g)
    zf = jnp.exp(e)
    z_cum = zf[0:CS]
    z_end = zf[CS:2 * CS]
    st = st_ref[...]
    outs = []
    for h in range(heads):
        ks = slice(h * dk, (h + 1) * dk)
        vs = slice(h * dv, (h + 1) * dv)
        qh, kh = q[:, ks], k[:, ks]
        vh = v[:, vs].astype(BF16)
        scores = _dot_nt(qh.astype(BF16), kh.astype(BF16)) * masks_ref[N_LEVELS]
        for l in range(N_LEVELS):
            zl = zf[(2 + l) * CS:(3 + l) * CS, ks]
            scores = scores + _dot_nt((qh * zl).astype(BF16), (kh * zl).astype(BF16)) * masks_ref[l]
        o = _dot(scores.astype(BF16), vh)
        o = o + _dot_nt((qh * z_cum[:, ks]).astype(BF16), st[:, ks].astype(BF16))
        outs.append(o)
        upd = _dot_tn(vh, (kh * z_end[:, ks]).astype(BF16))
        st_ref[:, ks] = st[:, ks] * z_cum[CS - 1:CS, ks] + upd
    return outs


def _even_prompt_kernel(z_ref, zgr_ref, lbp_ref, wgk_ref, bgk_ref, gnh_ref, gng_ref, sel_ref, masks_ref,
                        y_ref, shg_ref, sgla_ref, st_hg, st_gla):
    c = pl.program_id(1)

    @pl.when(c == 0)
    def _():
        st_hg[...] = jnp.zeros_like(st_hg)
        st_gla[...] = jnp.zeros_like(st_gla)

    z = z_ref[...]
    hq, hf, hi, hg = z[:, 0:512], z[:, 512:1024], z[:, 1024:1536], z[:, 1536:2048]
    gq, gk, gv, gg = z[:, 2048:2304], z[:, 2304:2560], z[:, 2560:3072], z[:, 3072:3584]
    sel = sel_ref[...]

    p = lbp_ref[...]
    pe = jnp.exp(p - jnp.max(p, axis=0, keepdims=True))
    lb = pe[0:1] / jnp.sum(pe, axis=0, keepdims=True)
    f = lb + (1.0 - lb) * _sigmoid(hf)
    k_hg = (1.0 - lb) * _sigmoid(-hf)
    o_hg = _gla_chunk(_silu(hq), k_hg, hi, jnp.log(f), st_hg, sel, masks_ref, HG_H, HG_DK, HG_DV)

    la = _log_sigmoid(_dot(zgr_ref[...].astype(BF16), wgk_ref[...]) + bgk_ref[...]) / GLA_GATE_NORM
    o_gla = _gla_chunk(gq * GLA_DK ** -0.5, gk, gv, la, st_gla, sel, masks_ref, GLA_H, GLA_DK, GLA_DV)

    for h in range(HG_H):
        y_ref[:, h * 128:(h + 1) * 128] = _rms_gate(o_hg[h], hg[:, h * 128:(h + 1) * 128], gnh_ref[...]).astype(BF16)
    for h in range(GLA_H):
        y_ref[:, 512 + h * 128:512 + (h + 1) * 128] = _rms_gate(
            o_gla[h], gg[:, h * 128:(h + 1) * 128], gng_ref[...]).astype(BF16)

    @pl.when(c == NCHUNK - 1)
    def _():
        shg_ref[0] = st_hg[...].T
        sgla_ref[0] = st_gla[...].T


def _even_prompt(z, zgr, lbp, wgk, bgk, gnh, gng, sel, masks):
    const2 = lambda b, c: (0, 0)
    return pl.pallas_call(
        _even_prompt_kernel,
        grid=(BATCH, NCHUNK),
        in_specs=[
            pl.BlockSpec((CS, EVEN_MAIN), lambda b, c: (b * NCHUNK + c, 0)),
            pl.BlockSpec((CS, LANES), lambda b, c: (b * NCHUNK + c, 0)),
            pl.BlockSpec(lbp.shape, const2),
            pl.BlockSpec(wgk.shape, const2),
            pl.BlockSpec(bgk.shape, const2),
            pl.BlockSpec(gnh.shape, const2),
            pl.BlockSpec(gng.shape, const2),
            pl.BlockSpec(sel.shape, const2),
            pl.BlockSpec(masks.shape, lambda b, c: (0, 0, 0)),
        ],
        out_specs=[
            pl.BlockSpec((CS, D), lambda b, c: (b * NCHUNK + c, 0)),
            pl.BlockSpec((1, HG_H * HG_DK, HG_DV), lambda b, c: (b, 0, 0)),
            pl.BlockSpec((1, GLA_H * GLA_DK, GLA_DV), lambda b, c: (b, 0, 0)),
        ],
        out_shape=[
            jax.ShapeDtypeStruct((NP, D), BF16),
            jax.ShapeDtypeStruct((BATCH, HG_H * HG_DK, HG_DV), F32),
            jax.ShapeDtypeStruct((BATCH, GLA_H * GLA_DK, GLA_DV), F32),
        ],
        scratch_shapes=[pltpu.VMEM((HG_DV, HG_H * HG_DK), F32), pltpu.VMEM((GLA_DV, GLA_H * GLA_DK), F32)],
        compiler_params=_params(("parallel", "arbitrary")),
        name="even_prompt",
    )(z, zgr, lbp, wgk, bgk, gnh, gng, sel, masks)


def _even_sample_kernel(zr_ref, zt_ref, grt_ref, lbpt_ref, wgkt_ref, bgkt_ref, gnh_ref, gng_ref,
                        shg_ref, sgla_ref, y_ref, shg_out, sgla_out, o_scr):
    zt = zt_ref[0]
    hq_t, hf_t = zt[0:512], zt[512:1024]
    gq_t, gk_t = zt[2048:2304], zt[2304:2560]
    pt = lbpt_ref[...]
    pe = jnp.exp(pt - jnp.max(pt, axis=1, keepdims=True))
    lb = pe[:, 0:1] / jnp.sum(pe, axis=1, keepdims=True)
    a_hg = jnp.exp(jnp.log(lb + (1.0 - lb) * _sigmoid(hf_t)))
    k_hg = (1.0 - lb) * _sigmoid(-hf_t)
    q_hg = _silu(hq_t)
    la = _log_sigmoid(_dot(wgkt_ref[...], grt_ref[0].astype(BF16)) + bgkt_ref[...]) / GLA_GATE_NORM
    a_gla = jnp.exp(la)
    q_gla = gq_t * GLA_DK ** -0.5
    zr = zr_ref[...]
    hi, hg = zr[:, 1024:1536], zr[:, 1536:2048]
    gv, gg = zr[:, 2560:3072], zr[:, 3072:3584]

    for j in range(SG):
        for h in range(HG_H):
            ks = slice(h * HG_DK, (h + 1) * HG_DK)
            s_new = a_hg[ks, j:j + 1] * shg_ref[j, h] + k_hg[ks, j:j + 1] * hi[j:j + 1, h * 128:(h + 1) * 128]
            shg_out[j, h] = s_new
            o_scr[j:j + 1, h * 128:(h + 1) * 128] = jnp.sum(q_hg[ks, j:j + 1] * s_new, axis=0, keepdims=True)
        for h in range(GLA_H):
            ks = slice(h * GLA_DK, (h + 1) * GLA_DK)
            s_new = a_gla[ks, j:j + 1] * sgla_ref[j, h] + gk_t[ks, j:j + 1] * gv[j:j + 1, h * 128:(h + 1) * 128]
            sgla_out[j, h] = s_new
            o_scr[j:j + 1, 512 + h * 128:512 + (h + 1) * 128] = jnp.sum(
                q_gla[ks, j:j + 1] * s_new, axis=0, keepdims=True)

    o = o_scr[...]
    for h in range(HG_H):
        cs = slice(h * 128, (h + 1) * 128)
        y_ref[:, cs] = _rms_gate(o[:, cs], hg[:, cs], gnh_ref[...]).astype(BF16)
    for h in range(GLA_H):
        cs = slice(512 + h * 128, 512 + (h + 1) * 128)
        y_ref[:, cs] = _rms_gate(o[:, cs], gg[:, h * 128:(h + 1) * 128], gng_ref[...]).astype(BF16)


def _even_sample(z, zt3, grt3, lbpt, wgkt, bgkt, gnh, gng, s_hg, s_gla):
    c2 = lambda g: (0, 0)
    return pl.pallas_call(
        _even_sample_kernel,
        grid=(NS // SG,),
        in_specs=[
            pl.BlockSpec((SG, EVEN_MAIN), lambda g: (NP // SG + g, 0)),
            pl.BlockSpec((1, EVEN_MAIN, SG), lambda g: (g, 0, 0)),
            pl.BlockSpec((1, LANES, SG), lambda g: (g, 0, 0)),
            pl.BlockSpec(lbpt.shape, c2),
            pl.BlockSpec(wgkt.shape, c2),
            pl.BlockSpec(bgkt.shape, c2),
            pl.BlockSpec(gnh.shape, c2),
            pl.BlockSpec(gng.shape, c2),
            pl.BlockSpec((SG, HG_H, HG_DK, HG_DV), lambda g: (g, 0, 0, 0)),
            pl.BlockSpec((SG, GLA_H, GLA_DK, GLA_DV), lambda g: (g, 0, 0, 0)),
        ],
        out_specs=[
            pl.BlockSpec((SG, D), lambda g: (g, 0)),
            pl.BlockSpec((SG, HG_H, HG_DK, HG_DV), lambda g: (g, 0, 0, 0)),
            pl.BlockSpec((SG, GLA_H, GLA_DK, GLA_DV), lambda g: (g, 0, 0, 0)),
        ],
        out_shape=[
            jax.ShapeDtypeStruct((NS, D), BF16),
            jax.ShapeDtypeStruct((NS, HG_H, HG_DK, HG_DV), F32),
            jax.ShapeDtypeStruct((NS, GLA_H, GLA_DK, GLA_DV), F32),
        ],
        scratch_shapes=[pltpu.VMEM((SG, D), F32)],
        compiler_params=_params(("parallel",)),
        name="even_sample",
    )(z, zt3, grt3, lbpt, wgkt, bgkt, gnh, gng, s_hg, s_gla)


def _out_ln_kernel(x_ref, y_ref, w_ref, g_ref, b_ref, o_ref):
    r = ALPHA * x_ref[...] + _dot(y_ref[...], w_ref[...])
    o_ref[...] = _layernorm(r, g_ref[...], b_ref[...])


def _out_ln(x, y, w, g, b):
    c2 = lambda i: (0, 0)
    return pl.pallas_call(
        _out_ln_kernel,
        grid=(NT // TM,),
        in_specs=[
            pl.BlockSpec((TM, D), lambda i: (i, 0)),
            pl.BlockSpec((TM, D), lambda i: (i, 0)),
            pl.BlockSpec((D, D), c2),
            pl.BlockSpec((1, D), c2),
            pl.BlockSpec((1, D), c2),
        ],
        out_specs=pl.BlockSpec((TM, D), lambda i: (i, 0)),
        out_shape=jax.ShapeDtypeStruct((NT, D), F32),
        compiler_params=_params(("parallel",)),
        name="out_ln",
    )(x, y, w, g, b)


FF_SPLIT = 2


def _ffn_kernel(x_ref, w1_ref, w3_ref, w2_ref, g_ref, b_ref, o_ref):
    x = x_ref[...]
    xb = x.astype(BF16)
    step = D_FF_DENSE // FF_SPLIT
    acc = ALPHA * x
    for s in range(FF_SPLIT):
        cs = slice(s * step, (s + 1) * step)
        hmid = _silu(_dot(xb, w1_ref[:, cs])) * _dot(xb, w3_ref[:, cs])
        acc = acc + _dot(hmid.astype(BF16), w2_ref[cs, :])
    o_ref[...] = _layernorm(acc, g_ref[...], b_ref[...])


def _ffn(x, w1, w3, w2, g, b):
    c2 = lambda i: (0, 0)
    one = pl.Buffered(1)
    return pl.pallas_call(
        _ffn_kernel,
        grid=(NT // TM,),
        in_specs=[
            pl.BlockSpec((TM, D), lambda i: (i, 0)),
            pl.BlockSpec((D, D_FF_DENSE), c2, pipeline_mode=one),
            pl.BlockSpec((D, D_FF_DENSE), c2, pipeline_mode=one),
            pl.BlockSpec((D_FF_DENSE, D), c2, pipeline_mode=one),
            pl.BlockSpec((1, D), c2),
            pl.BlockSpec((1, D), c2),
        ],
        out_specs=pl.BlockSpec((TM, D), lambda i: (i, 0)),
        out_shape=jax.ShapeDtypeStruct((NT, D), F32),
        compiler_params=_params(("parallel",)),
        name="ffn_dense",
    )(x, w1, w3, w2, g, b)


def _mh_norm_gate(hh, o_pre, w):
    mu = jnp.mean(hh, axis=-1, keepdims=True)
    c = hh - mu
    var = jnp.mean(c * c, axis=-1, keepdims=True)
    return _sigmoid(o_pre) * (c * lax.rsqrt(var + EPS) * w)


def _odd_prompt_kernel(z_ref, zg_ref, bg_ref, cw_ref, cb_ref, hnw_ref, tri_ref,
                       y_ref, c_out, n_out, m_out, conv_out,
                       c_scr, n_scr, m_scr, u_scr):
    c = pl.program_id(1)

    @pl.when(c == 0)
    def _():
        c_scr[...] = jnp.zeros_like(c_scr)
        n_scr[...] = jnp.zeros_like(n_scr)
        m_scr[...] = jnp.zeros_like(m_scr)
        u_scr[0:8, :] = jnp.zeros((8, D), F32)

    u_scr[8:8 + CS, :] = z_ref[:, 0:D]
    uc = cb_ref[...]
    for j in range(CONV_W):
        uc = uc + u_scr[5 + j:5 + j + CS, :] * cw_ref[j:j + 1, :]
    tail = u_scr[CS:CS + 8, :]
    u_scr[0:8, :] = tail
    act = _silu(uc)
    q = act[:, 0:512] * ML_DK ** -0.5
    k = act[:, 512:1024]
    v = z_ref[:, D:2 * D]
    o_pre = z_ref[:, 2 * D:3 * D]

    gates = zg_ref[...] + bg_ref[...]
    lf = _log_sigmoid(gates)
    bcum = _dot_sel(tri_ref[...], lf)
    bcum_t = bcum.T
    gates_t = gates.T
    row = lax.broadcasted_iota(jnp.int32, (CS, CS), 0)
    col = lax.broadcasted_iota(jnp.int32, (CS, CS), 1)
    causal = col <= row
    m_all = m_scr[...]

    for h in range(ML_H):
        ks = slice(h * ML_DK, (h + 1) * ML_DK)
        vs = slice(h * ML_DV, (h + 1) * ML_DV)
        qh, kh = q[:, ks], k[:, ks]
        vh = v[:, vs].astype(BF16)
        b_col = bcum[:, 4 + h:5 + h]
        b_row = bcum_t[4 + h:5 + h, :]
        i_col = gates[:, h:h + 1]
        i_row = gates_t[h:h + 1, :]
        m_prev = m_all[:, h:h + 1]
        log_d = jnp.where(causal, b_col - b_row + i_row, -jnp.inf)
        log_prev = b_col + m_prev
        m_t = jnp.maximum(jnp.max(log_d, axis=-1, keepdims=True), log_prev)
        d = jnp.exp(log_d - m_t)
        w_prev = jnp.exp(log_prev - m_t)
        scores = _dot_nt(qh.astype(BF16), kh.astype(BF16)) * d
        c_h = c_scr[h]
        n_h = n_scr[h:h + 1, :]
        num = _dot(scores.astype(BF16), vh) + w_prev * _dot(qh.astype(BF16), c_h.astype(BF16))
        den = jnp.sum(scores, axis=-1, keepdims=True) + w_prev * jnp.sum(qh * n_h, axis=-1, keepdims=True)
        hh = num / jnp.maximum(jnp.abs(den), jnp.exp(-m_t))
        m_new = m_t[CS - 1:CS, :]
        b_last = b_col[CS - 1:CS, :]
        w_c = jnp.exp(b_last + m_prev - m_new)
        w_s = jnp.exp(b_last - b_col + i_col - m_new)
        kw = kh * w_s
        c_scr[h] = w_c * c_h + _dot_tn(kw.astype(BF16), vh)
        n_scr[h:h + 1, :] = w_c * n_h + jnp.sum(kw, axis=0, keepdims=True)
        m_scr[:, h:h + 1] = m_new
        y_ref[:, vs] = _mh_norm_gate(hh, o_pre[:, vs], hnw_ref[:, vs]).astype(BF16)

    @pl.when(c == NCHUNK - 1)
    def _():
        c_out[0] = c_scr[...]
        n_out[0] = n_scr[0:ML_H, :]
        m_out[0] = m_scr[...]
        conv_out[0] = tail[8 - (CONV_W - 1):8, :]


def _odd_prompt(z, zg, bg, cw, cb, hnw, tri):
    c2 = lambda b, c: (0, 0)
    return pl.pallas_call(
        _odd_prompt_kernel,
        grid=(BATCH, NCHUNK),
        in_specs=[
            pl.BlockSpec((CS, ODD_MAIN), lambda b, c: (b * NCHUNK + c, 0)),
            pl.BlockSpec((CS, LANES), lambda b, c: (b * NCHUNK + c, 0)),
            pl.BlockSpec((1, LANES), c2),
            pl.BlockSpec((CONV_W, D), c2),
            pl.BlockSpec((1, D), c2),
            pl.BlockSpec((1, D), c2),
            pl.BlockSpec((CS, CS), c2),
        ],
        out_specs=[
            pl.BlockSpec((CS, D), lambda b, c: (b * NCHUNK + c, 0)),
            pl.BlockSpec((1, ML_H, ML_DK, ML_DV), lambda b, c: (b, 0, 0, 0)),
            pl.BlockSpec((1, ML_H, ML_DK), lambda b, c: (b, 0, 0)),
            pl.BlockSpec((1, 1, LANES), lambda b, c: (b, 0, 0)),
            pl.BlockSpec((1, CONV_W - 1, D), lambda b, c: (b, 0, 0)),
        ],
        out_shape=[
            jax.ShapeDtypeStruct((NP, D), BF16),
            jax.ShapeDtypeStruct((BATCH, ML_H, ML_DK, ML_DV), F32),
            jax.ShapeDtypeStruct((BATCH, ML_H, ML_DK), F32),
            jax.ShapeDtypeStruct((BATCH, 1, LANES), F32),
            jax.ShapeDtypeStruct((BATCH, CONV_W - 1, D), F32),
        ],
        scratch_shapes=[
            pltpu.VMEM((ML_H, ML_DK, ML_DV), F32),
            pltpu.VMEM((8, ML_DK), F32),
            pltpu.VMEM((1, LANES), F32),
            pltpu.VMEM((CS + 8, D), F32),
        ],
        compiler_params=_params(("parallel", "arbitrary")),
        name="odd_prompt",
    )(z, zg, bg, cw, cb, hnw, tri)


def _odd_sample_kernel(zr_ref, zg_ref, ut_ref, conv_ref, convt_ref, bg_ref, cw_ref, cwt_ref, cb_ref, cbt_ref,
                       hnw_ref, c_ref, n_ref, m_ref,
                       y_ref, c_out, n_out, m_out, conv_out, h_scr):
    zr = zr_ref[...]
    u = zr[:, 0:D]
    v = zr[:, D:2 * D]
    o_pre = zr[:, 2 * D:3 * D]
    uc = cb_ref[...] + u * cw_ref[CONV_W - 1:CONV_W, :]
    uc_t = cbt_ref[...] + ut_ref[0] * cwt_ref[:, CONV_W - 1:CONV_W]
    for j in range(CONV_W - 1):
        uc = uc + conv_ref[:, j * D:(j + 1) * D] * cw_ref[j:j + 1, :]
        uc_t = uc_t + convt_ref[0, j] * cwt_ref[:, j:j + 1]
        conv_out[:, j * D:(j + 1) * D] = conv_ref[:, (j + 1) * D:(j + 2) * D] if j + 1 < CONV_W - 1 else u
    act = _silu(uc)
    k_row = act[:, 512:1024]
    act_t = _silu(uc_t)
    q_t = act_t[0:512] * ML_DK ** -0.5
    k_t = act_t[512:1024]
    q_row = act[:, 0:512] * ML_DK ** -0.5

    gates = zg_ref[...] + bg_ref[...]
    lf = _log_sigmoid(gates)
    m_in = m_ref[...]
    m_out[...] = m_in

    for j in range(SG):
        for h in range(ML_H):
            ks = slice(h * ML_DK, (h + 1) * ML_DK)
            vs = slice(h * ML_DV, (h + 1) * ML_DV)
            ig = gates[j:j + 1, h:h + 1]
            log_prev = lf[j:j + 1, 4 + h:5 + h] + m_in[j:j + 1, h:h + 1]
            m_t = jnp.maximum(ig, log_prev)
            d = jnp.exp(ig - m_t)
            w_prev = jnp.exp(log_prev - m_t)
            c_new = w_prev * c_ref[j, h] + (d * k_t[ks, j:j + 1]) * v[j:j + 1, vs]
            n_new = w_prev * n_ref[j, h:h + 1, :] + d * k_row[j:j + 1, ks]
            c_out[j, h] = c_new
            n_out[j, h:h + 1, :] = n_new
            m_out[j:j + 1, h:h + 1] = m_t
            num = jnp.sum(q_t[ks, j:j + 1] * c_new, axis=0, keepdims=True)
            den = jnp.sum(q_row[j:j + 1, ks] * n_new, axis=-1, keepdims=True)
            h_scr[j:j + 1, vs] = num / jnp.maximum(jnp.abs(den), jnp.exp(-m_t))

    hh = h_scr[...]
    for h in range(ML_H):
        vs = slice(h * ML_DV, (h + 1) * ML_DV)
        y_ref[:, vs] = _mh_norm_gate(hh[:, vs], o_pre[:, vs], hnw_ref[:, vs]).astype(BF16)


def _odd_sample(z, zg, ut3, conv, convt, bg, cw, cwt, cb, cbt, hnw, c_in, n_in, m_in):
    c2 = lambda g: (0, 0)
    return pl.pallas_call(
        _odd_sample_kernel,
        grid=(NS // SG,),
        in_specs=[
            pl.BlockSpec((SG, ODD_MAIN), lambda g: (NP // SG + g, 0)),
            pl.BlockSpec((SG, LANES), lambda g: (NP // SG + g, 0)),
            pl.BlockSpec((1, D, SG), lambda g: (g, 0, 0)),
            pl.BlockSpec((SG, (CONV_W - 1) * D), lambda g: (g, 0)),
            pl.BlockSpec((1, CONV_W - 1, D, SG), lambda g: (g, 0, 0, 0)),
            pl.BlockSpec((1, LANES), c2),
            pl.BlockSpec((CONV_W, D), c2),
            pl.BlockSpec((D, CONV_W), c2),
            pl.BlockSpec((1, D), c2),
            pl.BlockSpec((D, 1), c2),
            pl.BlockSpec((1, D), c2),
            pl.BlockSpec((SG, ML_H, ML_DK, ML_DV), lambda g: (g, 0, 0, 0)),
            pl.BlockSpec((SG, ML_H, ML_DK), lambda g: (g, 0, 0)),
            pl.BlockSpec((SG, LANES), lambda g: (g, 0)),
        ],
        out_specs=[
            pl.BlockSpec((SG, D), lambda g: (g, 0)),
            pl.BlockSpec((SG, ML_H, ML_DK, ML_DV), lambda g: (g, 0, 0, 0)),
            pl.BlockSpec((SG, ML_H, ML_DK), lambda g: (g, 0, 0)),
            pl.BlockSpec((SG, LANES), lambda g: (g, 0)),
            pl.BlockSpec((SG, (CONV_W - 1) * D), lambda g: (g, 0)),
        ],
        out_shape=[
            jax.ShapeDtypeStruct((NS, D), BF16),
            jax.ShapeDtypeStruct((NS, ML_H, ML_DK, ML_DV), F32),
            jax.ShapeDtypeStruct((NS, ML_H, ML_DK), F32),
            jax.ShapeDtypeStruct((NS, LANES), F32),
            jax.ShapeDtypeStruct((NS, (CONV_W - 1) * D), F32),
        ],
        scratch_shapes=[pltpu.VMEM((SG, D), F32)],
        compiler_params=_params(("parallel",)),
        name="odd_sample",
    )(z, zg, ut3, conv, convt, bg, cw, cwt, cb, cbt, hnw, c_in, n_in, m_in)


def _out_ln_router_kernel(x_ref, y_ref, w_ref, g_ref, b_ref, wr_ref, tri_ref,
                          o_ref, meta_ref, cnt_ref, carry):
    i = pl.program_id(0)

    @pl.when(i == 0)
    def _():
        carry[...] = jnp.zeros_like(carry)

    r = ALPHA * x_ref[...] + _dot(y_ref[...], w_ref[...])
    x3 = _layernorm(r, g_ref[...], b_ref[...])
    o_ref[...] = x3

    lane = lax.broadcasted_iota(jnp.int32, (TM, LANES), 1).astype(F32)
    logits = jnp.where(lane < N_EXPERTS, _dot(x3.astype(BF16), wr_ref[...]), -jnp.inf)
    m1 = jnp.max(logits, axis=-1, keepdims=True)
    i1 = jnp.min(jnp.where(logits == m1, lane, float(LANES)), axis=-1, keepdims=True)
    rest = jnp.where(lane == i1, -jnp.inf, logits)
    m2 = jnp.max(rest, axis=-1, keepdims=True)
    i2 = jnp.min(jnp.where(rest == m2, lane, float(LANES)), axis=-1, keepdims=True)
    e2 = jnp.exp(m2 - m1)
    tot = 1.0 + e2
    w1 = 1.0 / tot
    w2 = e2 / tot

    sel1 = lane == i1
    sel2 = lane == i2
    onehot = jnp.where(sel1 | sel2, 1.0, 0.0)
    before = _dot(tri_ref[...], onehot.astype(BF16)) + carry[...]
    r1 = jnp.sum(jnp.where(sel1, before, 0.0), axis=-1, keepdims=True)
    r2 = jnp.sum(jnp.where(sel2, before, 0.0), axis=-1, keepdims=True)
    carry[...] = carry[...] + jnp.sum(onehot, axis=0, keepdims=True)
    cnt_ref[...] = carry[...]

    meta = jnp.where(lane == 0.0, i1, 0.0)
    meta = jnp.where(lane == 1.0, i2, meta)
    meta = jnp.where(lane == 2.0, w1, meta)
    meta = jnp.where(lane == 3.0, w2, meta)
    meta = jnp.where(lane == 4.0, r1, meta)
    meta = jnp.where(lane == 5.0, r2, meta)
    meta_ref[...] = meta


def _out_ln_router(x, y, w, g, b, wr, tri):
    c2 = lambda i: (0, 0)
    return pl.pallas_call(
        _out_ln_router_kernel,
        grid=(NT // TM,),
        in_specs=[
            pl.BlockSpec((TM, D), lambda i: (i, 0)),
            pl.BlockSpec((TM, D), lambda i: (i, 0)),
            pl.BlockSpec((D, D), c2),
            pl.BlockSpec((1, D), c2),
            pl.BlockSpec((1, D), c2),
            pl.BlockSpec((D, LANES), c2),
            pl.BlockSpec((TM, TM), c2),
        ],
        out_specs=[
            pl.BlockSpec((TM, D), lambda i: (i, 0)),
            pl.BlockSpec((TM, LANES), lambda i: (i, 0)),
            pl.BlockSpec((1, LANES), c2),
        ],
        out_shape=[
            jax.ShapeDtypeStruct((NT, D), F32),
            jax.ShapeDtypeStruct((NT, LANES), F32),
            jax.ShapeDtypeStruct((1, LANES), F32),
        ],
        scratch_shapes=[pltpu.VMEM((1, LANES), F32)],
        compiler_params=_params(("arbitrary",)),
        name="out_ln_router",
    )(x, y, w, g, b, wr, tri)


def _scatter_kernel(pos_ref, x_ref, xs_in, xs_out, sem):
    del xs_in

    def row_copy(r, slot):
        return pltpu.make_async_copy(x_ref.at[pl.ds(r, 1)], xs_out.at[pl.ds(pos_ref[2 * r + slot], 1)], sem)

    def start(r, carry):
        row_copy(r, 0).start()
        row_copy(r, 1).start()
        return carry

    def wait(r, carry):
        row_copy(r, 0).wait()
        row_copy(r, 1).wait()
        return carry

    lax.fori_loop(0, TM, start, 0)
    lax.fori_loop(0, TM, wait, 0)


def _scatter(pos_flat, x, xs_init):
    return pl.pallas_call(
        _scatter_kernel,
        grid=(NT // TM,),
        in_specs=[
            pl.BlockSpec((POS_BLK,), lambda i: (i,), memory_space=pltpu.SMEM),
            pl.BlockSpec((TM, D), lambda i: (i, 0)),
            pl.BlockSpec(memory_space=pl.ANY),
        ],
        out_specs=pl.BlockSpec(memory_space=pl.ANY),
        out_shape=jax.ShapeDtypeStruct((A_PAD, D), F32),
        scratch_shapes=[pltpu.SemaphoreType.DMA(())],
        input_output_aliases={2: 0},
        compiler_params=_params(("arbitrary",)),
        name="moe_scatter",
    )(pos_flat, x, xs_init)


def _moe_ffn_kernel(te_ref, nu_ref, x_ref, w1_ref, w3_ref, w2_ref, o_ref, xb_scr):
    i = pl.program_id(0)
    j = pl.program_id(1)
    used = i < nu_ref[0]

    @pl.when(j == 0)
    def _():
        o_ref[...] = jnp.zeros_like(o_ref)
        xb_scr[...] = x_ref[...].astype(BF16)

    @pl.when(used)
    def _():
        xb = xb_scr[...]
        hmid = _silu(_dot(xb, w1_ref[...])) * _dot(xb, w3_ref[...])
        o_ref[...] += _dot(hmid.astype(BF16), w2_ref[...])


def _moe_ffn(tile_expert, n_used, xs, w1, w3, w2):
    nff = D_FF_EXPERT // TFF

    def wcol(i, j, te, nu):
        return (te[i], 0, jnp.where(i < nu[0], j, nff - 1))

    def wrow(i, j, te, nu):
        return (te[i], jnp.where(i < nu[0], j, nff - 1), 0)

    grid_spec = pltpu.PrefetchScalarGridSpec(
        num_scalar_prefetch=2,
        grid=(N_MOE_TILES, nff),
        in_specs=[
            pl.BlockSpec((TMM, D), lambda i, j, te, nu: (i, 0)),
            pl.BlockSpec((None, D, TFF), wcol),
            pl.BlockSpec((None, D, TFF), wcol),
            pl.BlockSpec((None, TFF, D), wrow),
        ],
        out_specs=pl.BlockSpec((TMM, D), lambda i, j, te, nu: (i, 0)),
        scratch_shapes=[pltpu.VMEM((TMM, D), BF16)],
    )
    return pl.pallas_call(
        _moe_ffn_kernel,
        grid_spec=grid_spec,
        out_shape=jax.ShapeDtypeStruct((A_PAD, D), F32),
        compiler_params=_params(("parallel", "arbitrary")),
        name="moe_ffn",
    )(tile_expert, n_used, xs, w1, w3, w2)


def _combine_kernel(pos_ref, x_ref, meta_ref, ys_ref, g_ref, b_ref, o_ref, buf, sem):
    def row_copy(r, slot):
        return pltpu.make_async_copy(ys_ref.at[pl.ds(pos_ref[2 * r + slot], 1)], buf.at[slot, pl.ds(r, 1)], sem)

    def start(r, carry):
        row_copy(r, 0).start()
        row_copy(r, 1).start()
        return carry

    def wait(r, carry):
        row_copy(r, 0).wait()
        row_copy(r, 1).wait()
        return carry

    lax.fori_loop(0, TM, start, 0)
    lax.fori_loop(0, TM, wait, 0)
    meta = meta_ref[...]
    moe = meta[:, 2:3] * buf[0] + meta[:, 3:4] * buf[1]
    o_ref[...] = _layernorm(ALPHA * x_ref[...] + moe, g_ref[...], b_ref[...])


def _combine(pos_flat, x, meta, ys, g, b):
    c2 = lambda i: (0, 0)
    return pl.pallas_call(
        _combine_kernel,
        grid=(NT // TM,),
        in_specs=[
            pl.BlockSpec((POS_BLK,), lambda i: (i,), memory_space=pltpu.SMEM),
            pl.BlockSpec((TM, D), lambda i: (i, 0)),
            pl.BlockSpec((TM, LANES), lambda i: (i, 0)),
            pl.BlockSpec(memory_space=pl.ANY),
            pl.BlockSpec((1, D), c2),
            pl.BlockSpec((1, D), c2),
        ],
        out_specs=pl.BlockSpec((TM, D), lambda i: (i, 0)),
        out_shape=jax.ShapeDtypeStruct((NT, D), F32),
        scratch_shapes=[pltpu.VMEM((2, TM, D), F32), pltpu.SemaphoreType.DMA(())],
        compiler_params=_params(("arbitrary",)),
        name="moe_combine",
    )(pos_flat, x, meta, ys, g, b)


def _pad_cols(w, n):
    return jnp.pad(w, ((0, 0), (0, n - w.shape[1])))


def kernel(x_prompt, x_sample, state_hgrn, state_gla, state_mlstm_C, state_mlstm_n, state_mlstm_m,
           state_mlstm_conv, w_in_even, hg_lower_bounds, w_gk, b_gk, gn_hg, gn_gla, w_out_even,
           w1_dense, w3_dense, w2_dense, w_in_odd, b_gate_odd, conv_w, conv_b, hn_w, w_out_odd,
           w_router, w1_moe, w3_moe, w2_moe, ln1_g, ln1_b, ln2_g, ln2_b):
    assert x_prompt.shape == (BATCH, SEQ, D) and x_sample.shape == (NS, 1, D)
    assert w_in_even.shape[0] == 1 and w_in_odd.shape[0] == 1 and hg_lower_bounds.shape[0] == 2
    sel_np, masks_np = _gla_matrices()
    sel = jnp.asarray(sel_np, BF16)
    masks = jnp.asarray(masks_np, F32)
    tri_cs = jnp.asarray(_tri(CS, False), BF16)
    tri_tm = jnp.asarray(_tri(TM, True), BF16)
    row = lambda a: a.reshape(1, -1)

    x0 = jnp.concatenate([x_prompt.reshape(NP, D), x_sample.reshape(NS, D)], axis=0)

    w_even = w_in_even[0].astype(BF16)
    z, zgr = _proj(x0, w_even[:, :EVEN_MAIN], _pad_cols(w_even[:, EVEN_MAIN:], LANES))
    wgk = jnp.pad(w_gk[0].astype(BF16), ((0, LANES - GLA_RANK), (0, 0)))
    lbp = hg_lower_bounds
    y_p, hg_p, gla_p = _even_prompt(z, zgr, lbp, wgk, row(b_gk[0]), row(gn_hg[0]), row(gn_gla[0]), sel, masks)

    zs = z[NP:].reshape(NS // SG, SG, EVEN_MAIN).transpose(0, 2, 1)
    grs = zgr[NP:].reshape(NS // SG, SG, LANES).transpose(0, 2, 1)
    y_s, hg_s, gla_s = _even_sample(z, zs, grs, lbp.T, wgk.T, b_gk[0].reshape(-1, 1),
                                    row(gn_hg[0]), row(gn_gla[0]), state_hgrn[0], state_gla[0])
    y = jnp.concatenate([y_p, y_s], axis=0)
    x1 = _out_ln(x0, y, w_out_even[0].astype(BF16), row(ln1_g[0]), row(ln1_b[0]))
    x2 = _ffn(x1, w1_dense[0].astype(BF16), w3_dense[0].astype(BF16), w2_dense[0].astype(BF16),
              row(ln2_g[0]), row(ln2_b[0]))

    w_odd = w_in_odd[0].astype(BF16)
    zo, zog = _proj(x2, w_odd[:, :ODD_MAIN], _pad_cols(w_odd[:, ODD_MAIN:], LANES))
    bg = jnp.pad(b_gate_odd[0], (0, LANES - 2 * ML_H)).reshape(1, LANES)
    yo_p, c_p, n_p, m_p, conv_p = _odd_prompt(zo, zog, bg, conv_w[0], row(conv_b[0]), row(hn_w[0]), tri_cs)

    ut = zo[NP:, :D].reshape(NS // SG, SG, D).transpose(0, 2, 1)
    conv_in = state_mlstm_conv[0]
    conv_t = conv_in.reshape(NS // SG, SG, CONV_W - 1, D).transpose(0, 2, 3, 1)
    m_in = jnp.pad(state_mlstm_m[0], ((0, 0), (0, LANES - ML_H)))
    yo_s, c_s, n_s, m_s, conv_s = _odd_sample(
        zo, zog, ut, conv_in.reshape(NS, (CONV_W - 1) * D), conv_t, bg, conv_w[0], conv_w[0].T, row(conv_b[0]), conv_b[0].reshape(-1, 1),
        row(hn_w[0]), state_mlstm_C[0], state_mlstm_n[0], m_in)
    yo = jnp.concatenate([yo_p, yo_s], axis=0)

    wr = _pad_cols(w_router[0].astype(BF16), LANES)
    x3, meta, cnt = _out_ln_router(x2, yo, w_out_odd[0].astype(BF16), row(ln1_g[1]), row(ln1_b[1]), wr, tri_tm)

    counts = cnt[0, :N_EXPERTS].astype(jnp.int32)
    padded = ((counts + TMM - 1) // TMM) * TMM
    ends = jnp.cumsum(padded)
    offsets = ends - padded
    idx = meta[:, 0:2].astype(jnp.int32)
    pos = offsets[idx] + meta[:, 4:6].astype(jnp.int32)
    pos_flat = jnp.pad(pos.reshape(NT // TM, 2 * TM), ((0, 0), (0, POS_BLK - 2 * TM))).reshape(-1)
    tile_start = jnp.arange(N_MOE_TILES, dtype=jnp.int32) * TMM
    tile_expert = jnp.minimum(jnp.sum(tile_start[:, None] >= ends[None, :], axis=1), N_EXPERTS - 1).astype(jnp.int32)
    n_used = (ends[-1] // TMM).astype(jnp.int32).reshape(1)

    xs = _scatter(pos_flat, x3, jnp.zeros((A_PAD, D), F32))
    ys = _moe_ffn(tile_expert, n_used, xs, w1_moe[0].astype(BF16), w3_moe[0].astype(BF16), w2_moe[0].astype(BF16))
    out = _combine(pos_flat, x3, meta, ys, row(ln2_g[1]), row(ln2_b[1]))

    y_prompt = out[:NP].reshape(BATCH, SEQ, D)
    y_sample = out[NP:].reshape(NS, 1, D)
    return (y_prompt, y_sample,
            hg_p.reshape(1, BATCH, HG_H, HG_DK, HG_DV), gla_p.reshape(1, BATCH, GLA_H, GLA_DK, GLA_DV),
            c_p[None], n_p[None], m_p[:, 0, :ML_H][None], conv_p[None],
            hg_s[None], gla_s[None], c_s[None], n_s[None], m_s[:, :ML_H][None], conv_s.reshape(1, NS, CONV_W - 1, D))
```

```python
import functools
import math

import jax
import jax.numpy as jnp
import numpy as np
from jax import lax
from jax.experimental import pallas as pl
from jax.experimental.pallas import tpu as pltpu

F32 = jnp.float32
BF16 = jnp.bfloat16

D = 1024
BATCH = 8
SEQ = 2048
DEC_BATCH = 128
NP = BATCH * SEQ
NS = DEC_BATCH
NT = NP + NS
HG_H, HG_DK, HG_DV = 4, 128, 128
GLA_H, GLA_DK, GLA_DV = 4, 64, 128
GLA_RANK = 16
GLA_GATE_NORM = 16.0
ML_H, ML_DK, ML_DV = 4, 128, 256
CONV_W = 4
D_FF_DENSE = 2816
D_FF_EXPERT = 3584
N_EXPERTS = 8
EPS = 1e-5
DEPTH = 2
ALPHA = (2.0 * DEPTH) ** 0.25
EVEN_MAIN = 3584
ODD_MAIN = 3072

LANES = 128
VMEM_LIMIT = 56 * 1024 * 1024

TM = 384
CS = 128
NCHUNK = SEQ // CS
SEQ_PER_STEP = 1
SG = 16
TMM = 512
TFF = 896
MOE_NFF = D_FF_EXPERT // TFF
MOE_ROWS_PER_STEP = TMM // MOE_NFF
N_MOE_TILES = -(-(2 * NT + N_EXPERTS * (TMM - 1)) // TMM)
MOE_SLOTS = N_MOE_TILES * TMM
MOE_OUT_ROWS = MOE_SLOTS + TMM
N_LEVELS = int(math.log2(CS))

assert NT % TM == 0 and NP % CS == 0 and NS % SG == 0 and D_FF_EXPERT % TFF == 0 and TMM % MOE_NFF == 0


def _params(sem, limit=VMEM_LIMIT):
    return pltpu.CompilerParams(dimension_semantics=sem, vmem_limit_bytes=limit)


def _dot(a, b):
    return jnp.dot(a, b, preferred_element_type=F32)


def _dot_nt(a, b):
    return lax.dot_general(a, b, (((1,), (1,)), ((), ())), preferred_element_type=F32)


def _dot_tn(a, b):
    return lax.dot_general(a, b, (((0,), (0,)), ((), ())), preferred_element_type=F32)


def _split3(x):
    hi = x.astype(BF16)
    r1 = x - hi.astype(F32)
    mid = r1.astype(BF16)
    lo = (r1 - mid.astype(F32)).astype(BF16)
    return hi, mid, lo


def _dot_sel(sel, x):
    hi, mid, lo = _split3(x)
    return _dot(sel, hi) + _dot(sel, mid) + _dot(sel, lo)


def _sigmoid(x):
    return jax.nn.sigmoid(x)


def _silu(x):
    return x * jax.nn.sigmoid(x)


def _log_sigmoid(x):
    return jnp.minimum(x, 0.0) - jnp.log(1.0 + jnp.exp(-jnp.abs(x)))


def _layernorm(r, g, b):
    mu = jnp.mean(r, axis=-1, keepdims=True)
    c = r - mu
    var = jnp.mean(c * c, axis=-1, keepdims=True)
    return c * lax.rsqrt(var + EPS) * g + b


def _gla_matrices():
    sel = np.zeros(((2 + N_LEVELS) * CS, CS), np.float32)
    masks = np.zeros((N_LEVELS + 1, CS, CS), np.float32)
    for t in range(CS):
        sel[t, : t + 1] = 1.0
        sel[CS + t, t + 1:] = 1.0
        for l in range(N_LEVELS):
            half = 1 << l
            start = (t // (2 * half)) * (2 * half)
            mid = start + half
            row = (2 + l) * CS + t
            if t >= mid:
                sel[row, mid: t + 1] = 1.0
                masks[l, t, start:mid] = 1.0
            else:
                sel[row, t + 1: mid] = 1.0
        masks[N_LEVELS, t, t] = 1.0
    return sel, masks


def _tri(n, strict):
    return np.tril(np.ones((n, n), np.float32), -1 if strict else 0)


def _proj_kernel(x_ref, wa_ref, wb_ref, oa_ref, ob_ref):
    xb = x_ref[...].astype(BF16)
    oa_ref[...] = _dot(xb, wa_ref[...])
    ob_ref[...] = _dot(xb, wb_ref[...])


def _proj(x, wa, wb):
    na, nb = wa.shape[1], wb.shape[1]
    return pl.pallas_call(
        _proj_kernel,
        grid=(NT // TM,),
        in_specs=[
            pl.BlockSpec((TM, D), lambda i: (i, 0)),
            pl.BlockSpec((D, na), lambda i: (0, 0)),
            pl.BlockSpec((D, nb), lambda i: (0, 0)),
        ],
        out_specs=[
            pl.BlockSpec((TM, na), lambda i: (i, 0)),
            pl.BlockSpec((TM, nb), lambda i: (i, 0)),
        ],
        out_shape=[jax.ShapeDtypeStruct((NT, na), F32), jax.ShapeDtypeStruct((NT, nb), F32)],
        compiler_params=_params(("parallel",)),
        name="proj",
    )(x, wa, wb)


def _rms_gate(o, gate, w):
    o = o * lax.rsqrt(jnp.mean(o * o, axis=-1, keepdims=True) + EPS) * w
    return o * _silu(gate)


def _gla_chunk(q, k, v, g, st_ref, sel, masks_ref, heads, dk, dv):
    e = _dot_sel(sel, g)
    zf = jnp.exp(e)
    z_cum = zf[0:CS]
    z_end = zf[CS:2 * CS]
    st = st_ref[...]
    outs = []
    for h in range(heads):
        ks = slice(h * dk, (h + 1) * dk)
        vs = slice(h * dv, (h + 1) * dv)
        qh, kh = q[:, ks], k[:, ks]
        vh = v[:, vs].astype(BF16)
        scores = _dot_nt(qh.astype(BF16), kh.astype(BF16)) * masks_ref[N_LEVELS]
        for l in range(N_LEVELS):
            zl = zf[(2 + l) * CS:(3 + l) * CS, ks]
            scores = scores + _dot_nt((qh * zl).astype(BF16), (kh * zl).astype(BF16)) * masks_ref[l]
        o = _dot(scores.astype(BF16), vh)
        o = o + _dot_nt((qh * z_cum[:, ks]).astype(BF16), st[:, ks].astype(BF16))
        outs.append(o)
        upd = _dot_tn(vh, (kh * z_end[:, ks]).astype(BF16))
        st_ref[:, ks] = st[:, ks] * z_cum[CS - 1:CS, ks] + upd
    return outs


def _even_prompt_kernel(*refs):
    z_refs = refs[0:SEQ_PER_STEP]
    zgr_refs = refs[SEQ_PER_STEP:2 * SEQ_PER_STEP]
    (lbp_ref, wgk_ref, bgk_ref, gnh_ref, gng_ref, sel_ref, masks_ref,
     y_ref, shg_ref, sgla_ref, st_hg, st_gla) = refs[2 * SEQ_PER_STEP:]
    c = pl.program_id(1)

    @pl.when(c == 0)
    def _():
        st_hg[...] = jnp.zeros_like(st_hg)
        st_gla[...] = jnp.zeros_like(st_gla)

    sel = sel_ref[...]
    p = lbp_ref[...]
    pe = jnp.exp(p - jnp.max(p, axis=0, keepdims=True))
    lb = pe[0:1] / jnp.sum(pe, axis=0, keepdims=True)

    for s in range(SEQ_PER_STEP):
        z = z_refs[s][...]
        hq, hf, hi, hg = z[:, 0:512], z[:, 512:1024], z[:, 1024:1536], z[:, 1536:2048]
        gq, gk, gv, gg = z[:, 2048:2304], z[:, 2304:2560], z[:, 2560:3072], z[:, 3072:3584]
        f = lb + (1.0 - lb) * _sigmoid(hf)
        k_hg = (1.0 - lb) * _sigmoid(-hf)
        o_hg = _gla_chunk(_silu(hq), k_hg, hi, jnp.log(f), st_hg.at[s], sel, masks_ref, HG_H, HG_DK, HG_DV)

        la = _log_sigmoid(_dot(zgr_refs[s][...].astype(BF16), wgk_ref[...]) + bgk_ref[...]) / GLA_GATE_NORM
        o_gla = _gla_chunk(gq * GLA_DK ** -0.5, gk, gv, la, st_gla.at[s], sel, masks_ref, GLA_H, GLA_DK, GLA_DV)

        for h in range(HG_H):
            cs = slice(h * 128, (h + 1) * 128)
            y_ref[s, :, cs] = _rms_gate(o_hg[h], hg[:, cs], gnh_ref[...]).astype(BF16)
        for h in range(GLA_H):
            cs = slice(h * 128, (h + 1) * 128)
            y_ref[s, :, 512 + h * 128:512 + (h + 1) * 128] = _rms_gate(o_gla[h], gg[:, cs], gng_ref[...]).astype(BF16)

    @pl.when(c == NCHUNK - 1)
    def _():
        for s in range(SEQ_PER_STEP):
            shg_ref[s] = st_hg[s].T
            sgla_ref[s] = st_gla[s].T


def _seq_row_specs(width):
    return [pl.BlockSpec((CS, width), functools.partial(lambda b, c, s: ((SEQ_PER_STEP * b + s) * NCHUNK + c, 0), s=s))
            for s in range(SEQ_PER_STEP)]


def _even_prompt(z, zgr, lbp, wgk, bgk, gnh, gng, sel, masks):
    const2 = lambda b, c: (0, 0)
    sp = SEQ_PER_STEP
    return pl.pallas_call(
        _even_prompt_kernel,
        grid=(BATCH // sp, NCHUNK),
        in_specs=_seq_row_specs(EVEN_MAIN) + _seq_row_specs(LANES) + [
            pl.BlockSpec(lbp.shape, const2),
            pl.BlockSpec(wgk.shape, const2),
            pl.BlockSpec(bgk.shape, const2),
            pl.BlockSpec(gnh.shape, const2),
            pl.BlockSpec(gng.shape, const2),
            pl.BlockSpec(sel.shape, const2),
            pl.BlockSpec(masks.shape, lambda b, c: (0, 0, 0)),
        ],
        out_specs=[
            pl.BlockSpec((sp, CS, D), lambda b, c: (b, c, 0)),
            pl.BlockSpec((sp, HG_H * HG_DK, HG_DV), lambda b, c: (b, 0, 0)),
            pl.BlockSpec((sp, GLA_H * GLA_DK, GLA_DV), lambda b, c: (b, 0, 0)),
        ],
        out_shape=[
            jax.ShapeDtypeStruct((BATCH, SEQ, D), BF16),
            jax.ShapeDtypeStruct((BATCH, HG_H * HG_DK, HG_DV), F32),
            jax.ShapeDtypeStruct((BATCH, GLA_H * GLA_DK, GLA_DV), F32),
        ],
        scratch_shapes=[pltpu.VMEM((sp, HG_DV, HG_H * HG_DK), F32), pltpu.VMEM((sp, GLA_DV, GLA_H * GLA_DK), F32)],
        compiler_params=_params(("parallel", "arbitrary")),
        name="even_prompt",
    )(*([z] * sp), *([zgr] * sp), lbp, wgk, bgk, gnh, gng, sel, masks)


def _even_sample_kernel(zr_ref, zt_ref, grt_ref, lbpt_ref, wgkt_ref, bgkt_ref, gnh_ref, gng_ref,
                        shg_ref, sgla_ref, y_ref, shg_out, sgla_out, o_scr):
    zt = zt_ref[0]
    hq_t, hf_t = zt[0:512], zt[512:1024]
    gq_t, gk_t = zt[2048:2304], zt[2304:2560]
    pt = lbpt_ref[...]
    pe = jnp.exp(pt - jnp.max(pt, axis=1, keepdims=True))
    lb = pe[:, 0:1] / jnp.sum(pe, axis=1, keepdims=True)
    a_hg = jnp.exp(jnp.log(lb + (1.0 - lb) * _sigmoid(hf_t)))
    k_hg = (1.0 - lb) * _sigmoid(-hf_t)
    q_hg = _silu(hq_t)
    la = _log_sigmoid(_dot(wgkt_ref[...], grt_ref[0].astype(BF16)) + bgkt_ref[...]) / GLA_GATE_NORM
    a_gla = jnp.exp(la)
    q_gla = gq_t * GLA_DK ** -0.5
    zr = zr_ref[...]
    hi, hg = zr[:, 1024:1536], zr[:, 1536:2048]
    gv, gg = zr[:, 2560:3072], zr[:, 3072:3584]

    for j in range(SG):
        for h in range(HG_H):
            ks = slice(h * HG_DK, (h + 1) * HG_DK)
            s_new = a_hg[ks, j:j + 1] * shg_ref[j, h] + k_hg[ks, j:j + 1] * hi[j:j + 1, h * 128:(h + 1) * 128]
            shg_out[j, h] = s_new
            o_scr[j:j + 1, h * 128:(h + 1) * 128] = jnp.sum(q_hg[ks, j:j + 1] * s_new, axis=0, keepdims=True)
        for h in range(GLA_H):
            ks = slice(h * GLA_DK, (h + 1) * GLA_DK)
            s_new = a_gla[ks, j:j + 1] * sgla_ref[j, h] + gk_t[ks, j:j + 1] * gv[j:j + 1, h * 128:(h + 1) * 128]
            sgla_out[j, h] = s_new
            o_scr[j:j + 1, 512 + h * 128:512 + (h + 1) * 128] = jnp.sum(
                q_gla[ks, j:j + 1] * s_new, axis=0, keepdims=True)

    o = o_scr[...]
    for h in range(HG_H):
        cs = slice(h * 128, (h + 1) * 128)
        y_ref[:, cs] = _rms_gate(o[:, cs], hg[:, cs], gnh_ref[...]).astype(BF16)
    for h in range(GLA_H):
        cs = slice(512 + h * 128, 512 + (h + 1) * 128)
        y_ref[:, cs] = _rms_gate(o[:, cs], gg[:, h * 128:(h + 1) * 128], gng_ref[...]).astype(BF16)


def _even_sample(z, zt3, grt3, lbpt, wgkt, bgkt, gnh, gng, s_hg, s_gla):
    c2 = lambda g: (0, 0)
    return pl.pallas_call(
        _even_sample_kernel,
        grid=(NS // SG,),
        in_specs=[
            pl.BlockSpec((SG, EVEN_MAIN), lambda g: (NP // SG + g, 0)),
            pl.BlockSpec((1, EVEN_MAIN, SG), lambda g: (g, 0, 0)),
            pl.BlockSpec((1, LANES, SG), lambda g: (g, 0, 0)),
            pl.BlockSpec(lbpt.shape, c2),
            pl.BlockSpec(wgkt.shape, c2),
            pl.BlockSpec(bgkt.shape, c2),
            pl.BlockSpec(gnh.shape, c2),
            pl.BlockSpec(gng.shape, c2),
            pl.BlockSpec((SG, HG_H, HG_DK, HG_DV), lambda g: (g, 0, 0, 0)),
            pl.BlockSpec((SG, GLA_H, GLA_DK, GLA_DV), lambda g: (g, 0, 0, 0)),
        ],
        out_specs=[
            pl.BlockSpec((SG, D), lambda g: (g, 0)),
            pl.BlockSpec((SG, HG_H, HG_DK, HG_DV), lambda g: (g, 0, 0, 0)),
            pl.BlockSpec((SG, GLA_H, GLA_DK, GLA_DV), lambda g: (g, 0, 0, 0)),
        ],
        out_shape=[
            jax.ShapeDtypeStruct((NS, D), BF16),
            jax.ShapeDtypeStruct((NS, HG_H, HG_DK, HG_DV), F32),
            jax.ShapeDtypeStruct((NS, GLA_H, GLA_DK, GLA_DV), F32),
        ],
        scratch_shapes=[pltpu.VMEM((SG, D), F32)],
        compiler_params=_params(("parallel",)),
        name="even_sample",
    )(z, zt3, grt3, lbpt, wgkt, bgkt, gnh, gng, s_hg, s_gla)


def _out_ln_kernel(x_ref, y_ref, w_ref, g_ref, b_ref, o_ref):
    r = ALPHA * x_ref[...] + _dot(y_ref[...], w_ref[...])
    o_ref[...] = _layernorm(r, g_ref[...], b_ref[...])


def _out_ln(x, y, w, g, b):
    c2 = lambda i: (0, 0)
    return pl.pallas_call(
        _out_ln_kernel,
        grid=(NT // TM,),
        in_specs=[
            pl.BlockSpec((TM, D), lambda i: (i, 0)),
            pl.BlockSpec((TM, D), lambda i: (i, 0)),
            pl.BlockSpec((D, D), c2),
            pl.BlockSpec((1, D), c2),
            pl.BlockSpec((1, D), c2),
        ],
        out_specs=pl.BlockSpec((TM, D), lambda i: (i, 0)),
        out_shape=jax.ShapeDtypeStruct((NT, D), F32),
        compiler_params=_params(("parallel",)),
        name="out_ln",
    )(x, y, w, g, b)


FF_SPLIT = 2


def _ffn_kernel(x_ref, w1_ref, w3_ref, w2_ref, g_ref, b_ref, o_ref):
    x = x_ref[...]
    xb = x.astype(BF16)
    step = D_FF_DENSE // FF_SPLIT
    acc = ALPHA * x
    for s in range(FF_SPLIT):
        cs = slice(s * step, (s + 1) * step)
        hmid = _silu(_dot(xb, w1_ref[:, cs])) * _dot(xb, w3_ref[:, cs])
        acc = acc + _dot(hmid.astype(BF16), w2_ref[cs, :])
    o_ref[...] = _layernorm(acc, g_ref[...], b_ref[...])


def _ffn(x, w1, w3, w2, g, b):
    c2 = lambda i: (0, 0)
    one = pl.Buffered(1)
    return pl.pallas_call(
        _ffn_kernel,
        grid=(NT // TM,),
        in_specs=[
            pl.BlockSpec((TM, D), lambda i: (i, 0)),
            pl.BlockSpec((D, D_FF_DENSE), c2, pipeline_mode=one),
            pl.BlockSpec((D, D_FF_DENSE), c2, pipeline_mode=one),
            pl.BlockSpec((D_FF_DENSE, D), c2, pipeline_mode=one),
            pl.BlockSpec((1, D), c2),
            pl.BlockSpec((1, D), c2),
        ],
        out_specs=pl.BlockSpec((TM, D), lambda i: (i, 0)),
        out_shape=jax.ShapeDtypeStruct((NT, D), F32),
        compiler_params=_params(("parallel",)),
        name="ffn_dense",
    )(x, w1, w3, w2, g, b)


def _mh_norm_gate(hh, o_pre, w):
    mu = jnp.mean(hh, axis=-1, keepdims=True)
    c = hh - mu
    var = jnp.mean(c * c, axis=-1, keepdims=True)
    return _sigmoid(o_pre) * (c * lax.rsqrt(var + EPS) * w)


def _odd_prompt_kernel(*refs):
    z_refs = refs[0:SEQ_PER_STEP]
    zg_refs = refs[SEQ_PER_STEP:2 * SEQ_PER_STEP]
    (bg_ref, cw_ref, cb_ref, hnw_ref, tri_ref,
     y_ref, c_out, n_out, m_out, conv_out,
     c_scr, n_scr, m_scr, u_scr) = refs[2 * SEQ_PER_STEP:]
    c = pl.program_id(1)

    @pl.when(c == 0)
    def _():
        c_scr[...] = jnp.zeros_like(c_scr)
        n_scr[...] = jnp.zeros_like(n_scr)
        m_scr[...] = jnp.zeros_like(m_scr)
        for s in range(SEQ_PER_STEP):
            u_scr[s, 0:8, :] = jnp.zeros((8, D), F32)

    row = lax.broadcasted_iota(jnp.int32, (CS, CS), 0)
    col = lax.broadcasted_iota(jnp.int32, (CS, CS), 1)
    causal = col <= row
    tails = []

    for s in range(SEQ_PER_STEP):
        z_ref = z_refs[s]
        u_scr[s, 8:8 + CS, :] = z_ref[:, 0:D]
        uc = cb_ref[...]
        for j in range(CONV_W):
            uc = uc + u_scr[s, 5 + j:5 + j + CS, :] * cw_ref[j:j + 1, :]
        tail = u_scr[s, CS:CS + 8, :]
        u_scr[s, 0:8, :] = tail
        tails.append(tail)
        act = _silu(uc)
        q = act[:, 0:512] * ML_DK ** -0.5
        k = act[:, 512:1024]
        v = z_ref[:, D:2 * D]
        o_pre = z_ref[:, 2 * D:3 * D]

        gates = zg_refs[s][...] + bg_ref[...]
        lf = _log_sigmoid(gates)
        bcum = _dot_sel(tri_ref[...], lf)
        bcum_t = bcum.T
        gates_t = gates.T
        m_all = m_scr[s]

        for h in range(ML_H):
            ks = slice(h * ML_DK, (h + 1) * ML_DK)
            vs = slice(h * ML_DV, (h + 1) * ML_DV)
            qh, kh = q[:, ks], k[:, ks]
            vh = v[:, vs].astype(BF16)
            b_col = bcum[:, 4 + h:5 + h]
            b_row = bcum_t[4 + h:5 + h, :]
            i_col = gates[:, h:h + 1]
            i_row = gates_t[h:h + 1, :]
            m_prev = m_all[:, h:h + 1]
            log_d = jnp.where(causal, b_col - b_row + i_row, -jnp.inf)
            log_prev = b_col + m_prev
            m_t = jnp.maximum(jnp.max(log_d, axis=-1, keepdims=True), log_prev)
            d = jnp.exp(log_d - m_t)
            w_prev = jnp.exp(log_prev - m_t)
            scores = _dot_nt(qh.astype(BF16), kh.astype(BF16)) * d
            c_h = c_scr[s, h]
            n_h = n_scr[s, h:h + 1, :]
            num = _dot(scores.astype(BF16), vh) + w_prev * _dot(qh.astype(BF16), c_h.astype(BF16))
            den = jnp.sum(scores, axis=-1, keepdims=True) + w_prev * jnp.sum(qh * n_h, axis=-1, keepdims=True)
            hh = num / jnp.maximum(jnp.abs(den), jnp.exp(-m_t))
            m_new = m_t[CS - 1:CS, :]
            b_last = b_col[CS - 1:CS, :]
            w_c = jnp.exp(b_last + m_prev - m_new)
            w_s = jnp.exp(b_last - b_col + i_col - m_new)
            kw = kh * w_s
            c_scr[s, h] = w_c * c_h + _dot_tn(kw.astype(BF16), vh)
            n_scr[s, h:h + 1, :] = w_c * n_h + jnp.sum(kw, axis=0, keepdims=True)
            m_scr[s, :, h:h + 1] = m_new
            y_ref[s, :, vs] = _mh_norm_gate(hh, o_pre[:, vs], hnw_ref[:, vs]).astype(BF16)

    @pl.when(c == NCHUNK - 1)
    def _():
        c_out[...] = c_scr[...]
        for s in range(SEQ_PER_STEP):
            n_out[s] = n_scr[s, 0:ML_H, :]
            m_out[s] = m_scr[s]
            conv_out[s] = tails[s][8 - (CONV_W - 1):8, :]


def _odd_prompt(z, zg, bg, cw, cb, hnw, tri):
    c2 = lambda b, c: (0, 0)
    sp = SEQ_PER_STEP
    return pl.pallas_call(
        _odd_prompt_kernel,
        grid=(BATCH // sp, NCHUNK),
        in_specs=_seq_row_specs(ODD_MAIN) + _seq_row_specs(LANES) + [
            pl.BlockSpec((1, LANES), c2),
            pl.BlockSpec((CONV_W, D), c2),
            pl.BlockSpec((1, D), c2),
            pl.BlockSpec((1, D), c2),
            pl.BlockSpec((CS, CS), c2),
        ],
        out_specs=[
            pl.BlockSpec((sp, CS, D), lambda b, c: (b, c, 0)),
            pl.BlockSpec((sp, ML_H, ML_DK, ML_DV), lambda b, c: (b, 0, 0, 0)),
            pl.BlockSpec((sp, ML_H, ML_DK), lambda b, c: (b, 0, 0)),
            pl.BlockSpec((sp, 1, LANES), lambda b, c: (b, 0, 0)),
            pl.BlockSpec((sp, CONV_W - 1, D), lambda b, c: (b, 0, 0)),
        ],
        out_shape=[
            jax.ShapeDtypeStruct((BATCH, SEQ, D), BF16),
            jax.ShapeDtypeStruct((BATCH, ML_H, ML_DK, ML_DV), F32),
            jax.ShapeDtypeStruct((BATCH, ML_H, ML_DK), F32),
            jax.ShapeDtypeStruct((BATCH, 1, LANES), F32),
            jax.ShapeDtypeStruct((BATCH, CONV_W - 1, D), F32),
        ],
        scratch_shapes=[
            pltpu.VMEM((sp, ML_H, ML_DK, ML_DV), F32),
            pltpu.VMEM((sp, 8, ML_DK), F32),
            pltpu.VMEM((sp, 1, LANES), F32),
            pltpu.VMEM((sp, CS + 8, D), F32),
        ],
        compiler_params=_params(("parallel", "arbitrary")),
        name="odd_prompt",
    )(*([z] * sp), *([zg] * sp), bg, cw, cb, hnw, tri)


def _odd_sample_kernel(zr_ref, zg_ref, ut_ref, conv_ref, convt_ref, bg_ref, cw_ref, cwt_ref, cb_ref, cbt_ref,
                       hnw_ref, c_ref, n_ref, m_ref,
                       y_ref, c_out, n_out, m_out, conv_out, h_scr):
    zr = zr_ref[...]
    u = zr[:, 0:D]
    v = zr[:, D:2 * D]
    o_pre = zr[:, 2 * D:3 * D]
    uc = cb_ref[...] + u * cw_ref[CONV_W - 1:CONV_W, :]
    uc_t = cbt_ref[...] + ut_ref[0] * cwt_ref[:, CONV_W - 1:CONV_W]
    for j in range(CONV_W - 1):
        uc = uc + conv_ref[:, j * D:(j + 1) * D] * cw_ref[j:j + 1, :]
        uc_t = uc_t + convt_ref[0, j] * cwt_ref[:, j:j + 1]
        conv_out[:, j * D:(j + 1) * D] = conv_ref[:, (j + 1) * D:(j + 2) * D] if j + 1 < CONV_W - 1 else u
    act = _silu(uc)
    k_row = act[:, 512:1024]
    act_t = _silu(uc_t)
    q_t = act_t[0:512] * ML_DK ** -0.5
    k_t = act_t[512:1024]
    q_row = act[:, 0:512] * ML_DK ** -0.5

    gates = zg_ref[...] + bg_ref[...]
    lf = _log_sigmoid(gates)
    m_in = m_ref[...]
    m_out[...] = m_in

    for j in range(SG):
        for h in range(ML_H):
            ks = slice(h * ML_DK, (h + 1) * ML_DK)
            vs = slice(h * ML_DV, (h + 1) * ML_DV)
            ig = gates[j:j + 1, h:h + 1]
            log_prev = lf[j:j + 1, 4 + h:5 + h] + m_in[j:j + 1, h:h + 1]
            m_t = jnp.maximum(ig, log_prev)
            d = jnp.exp(ig - m_t)
            w_prev = jnp.exp(log_prev - m_t)
            c_new = w_prev * c_ref[j, h] + (d * k_t[ks, j:j + 1]) * v[j:j + 1, vs]
            n_new = w_prev * n_ref[j, h:h + 1, :] + d * k_row[j:j + 1, ks]
            c_out[j, h] = c_new
            n_out[j, h:h + 1, :] = n_new
            m_out[j:j + 1, h:h + 1] = m_t
            num = jnp.sum(q_t[ks, j:j + 1] * c_new, axis=0, keepdims=True)
            den = jnp.sum(q_row[j:j + 1, ks] * n_new, axis=-1, keepdims=True)
            h_scr[j:j + 1, vs] = num / jnp.maximum(jnp.abs(den), jnp.exp(-m_t))

    hh = h_scr[...]
    for h in range(ML_H):
        vs = slice(h * ML_DV, (h + 1) * ML_DV)
        y_ref[:, vs] = _mh_norm_gate(hh[:, vs], o_pre[:, vs], hnw_ref[:, vs]).astype(BF16)


def _odd_sample(z, zg, ut3, conv, convt, bg, cw, cwt, cb, cbt, hnw, c_in, n_in, m_in):
    c2 = lambda g: (0, 0)
    return pl.pallas_call(
        _odd_sample_kernel,
        grid=(NS // SG,),
        in_specs=[
            pl.BlockSpec((SG, ODD_MAIN), lambda g: (NP // SG + g, 0)),
            pl.BlockSpec((SG, LANES), lambda g: (NP // SG + g, 0)),
            pl.BlockSpec((1, D, SG), lambda g: (g, 0, 0)),
            pl.BlockSpec((SG, (CONV_W - 1) * D), lambda g: (g, 0)),
            pl.BlockSpec((1, CONV_W - 1, D, SG), lambda g: (g, 0, 0, 0)),
            pl.BlockSpec((1, LANES), c2),
            pl.BlockSpec((CONV_W, D), c2),
            pl.BlockSpec((D, CONV_W), c2),
            pl.BlockSpec((1, D), c2),
            pl.BlockSpec((D, 1), c2),
            pl.BlockSpec((1, D), c2),
            pl.BlockSpec((SG, ML_H, ML_DK, ML_DV), lambda g: (g, 0, 0, 0)),
            pl.BlockSpec((SG, ML_H, ML_DK), lambda g: (g, 0, 0)),
            pl.BlockSpec((SG, LANES), lambda g: (g, 0)),
        ],
        out_specs=[
            pl.BlockSpec((SG, D), lambda g: (g, 0)),
            pl.BlockSpec((SG, ML_H, ML_DK, ML_DV), lambda g: (g, 0, 0, 0)),
            pl.BlockSpec((SG, ML_H, ML_DK), lambda g: (g, 0, 0)),
            pl.BlockSpec((SG, LANES), lambda g: (g, 0)),
            pl.BlockSpec((SG, (CONV_W - 1) * D), lambda g: (g, 0)),
        ],
        out_shape=[
            jax.ShapeDtypeStruct((NS, D), BF16),
            jax.ShapeDtypeStruct((NS, ML_H, ML_DK, ML_DV), F32),
            jax.ShapeDtypeStruct((NS, ML_H, ML_DK), F32),
            jax.ShapeDtypeStruct((NS, LANES), F32),
            jax.ShapeDtypeStruct((NS, (CONV_W - 1) * D), F32),
        ],
        scratch_shapes=[pltpu.VMEM((SG, D), F32)],
        compiler_params=_params(("parallel",)),
        name="odd_sample",
    )(z, zg, ut3, conv, convt, bg, cw, cwt, cb, cbt, hnw, c_in, n_in, m_in)


def _out_ln_router_kernel(x_ref, y_ref, w_ref, g_ref, b_ref, wr_ref, tri_ref,
                          o_ref, meta_ref, cnt_ref, carry):
    i = pl.program_id(0)

    @pl.when(i == 0)
    def _():
        carry[...] = jnp.zeros_like(carry)

    r = ALPHA * x_ref[...] + _dot(y_ref[...], w_ref[...])
    x3 = _layernorm(r, g_ref[...], b_ref[...])
    o_ref[...] = x3

    lane = lax.broadcasted_iota(jnp.int32, (TM, LANES), 1).astype(F32)
    logits = jnp.where(lane < N_EXPERTS, _dot(x3.astype(BF16), wr_ref[...]), -jnp.inf)
    m1 = jnp.max(logits, axis=-1, keepdims=True)
    i1 = jnp.min(jnp.where(logits == m1, lane, float(LANES)), axis=-1, keepdims=True)
    rest = jnp.where(lane == i1, -jnp.inf, logits)
    m2 = jnp.max(rest, axis=-1, keepdims=True)
    i2 = jnp.min(jnp.where(rest == m2, lane, float(LANES)), axis=-1, keepdims=True)
    e2 = jnp.exp(m2 - m1)
    tot = 1.0 + e2
    w1 = 1.0 / tot
    w2 = e2 / tot

    sel1 = lane == i1
    sel2 = lane == i2
    onehot = jnp.where(sel1 | sel2, 1.0, 0.0)
    before = _dot(tri_ref[...], onehot.astype(BF16)) + carry[...]
    r1 = jnp.sum(jnp.where(sel1, before, 0.0), axis=-1, keepdims=True)
    r2 = jnp.sum(jnp.where(sel2, before, 0.0), axis=-1, keepdims=True)
    carry[...] = carry[...] + jnp.sum(onehot, axis=0, keepdims=True)
    cnt_ref[...] = carry[...]

    meta = jnp.where(lane == 0.0, i1, 0.0)
    meta = jnp.where(lane == 1.0, i2, meta)
    meta = jnp.where(lane == 2.0, w1, meta)
    meta = jnp.where(lane == 3.0, w2, meta)
    meta = jnp.where(lane == 4.0, r1, meta)
    meta = jnp.where(lane == 5.0, r2, meta)
    meta_ref[...] = meta


def _out_ln_router(x, y, w, g, b, wr, tri):
    c2 = lambda i: (0, 0)
    return pl.pallas_call(
        _out_ln_router_kernel,
        grid=(NT // TM,),
        in_specs=[
            pl.BlockSpec((TM, D), lambda i: (i, 0)),
            pl.BlockSpec((TM, D), lambda i: (i, 0)),
            pl.BlockSpec((D, D), c2),
            pl.BlockSpec((1, D), c2),
            pl.BlockSpec((1, D), c2),
            pl.BlockSpec((D, LANES), c2),
            pl.BlockSpec((TM, TM), c2),
        ],
        out_specs=[
            pl.BlockSpec((TM, D), lambda i: (i, 0)),
            pl.BlockSpec((TM, LANES), lambda i: (i, 0)),
            pl.BlockSpec((1, LANES), c2),
        ],
        out_shape=[
            jax.ShapeDtypeStruct((NT, D), F32),
            jax.ShapeDtypeStruct((NT, LANES), F32),
            jax.ShapeDtypeStruct((1, LANES), F32),
        ],
        scratch_shapes=[pltpu.VMEM((1, LANES), F32)],
        compiler_params=_params(("arbitrary",)),
        name="out_ln_router",
    )(x, y, w, g, b, wr, tri)


def _moe_ffn_kernel(te_ref, nu_ref, gnext_ref, gcur_ref, sprev_ref, scur_ref, x_hbm, w1_ref, w3_ref, w2_ref,
                    out_hbm, xg, yacc, xb_scr, sem_g, sem_s):
    i = pl.program_id(0)
    j = pl.program_id(1)
    used = i < nu_ref[0]
    slot = i % 2
    other = 1 - slot

    def gather(tok, buf, r):
        return pltpu.make_async_copy(x_hbm.at[pl.ds(tok, 1)], xg.at[buf, pl.ds(r, 1)], sem_g)

    def scatter(buf, r, dst):
        return pltpu.make_async_copy(yacc.at[buf, pl.ds(r, 1)], out_hbm.at[pl.ds(dst, 1)], sem_s)

    def wait_rows(kind, n):
        for _ in range(n):
            (gather(0, 0, 0) if kind == "g" else scatter(0, 0, 0)).wait()

    def issue_neighbours():
        for r in range(MOE_ROWS_PER_STEP):
            rr = j * MOE_ROWS_PER_STEP + r
            gather(gnext_ref[rr], other, rr).start()
            scatter(other, rr, sprev_ref[rr]).start()

    @pl.when(j == 0)
    def _():
        @pl.when(i == 0)
        def _():
            yacc[1] = jnp.zeros((TMM, D), F32)
            for r in range(TMM):
                gather(gcur_ref[r], 0, r).start()

        wait_rows("g", TMM)

        @pl.when(i > 0)
        def _():
            wait_rows("s", TMM)

        xb_scr[...] = xg[slot].astype(BF16)
        yacc[slot] = jnp.zeros((TMM, D), F32)

    @pl.when(used)
    def _():
        issue_neighbours()
        xb = xb_scr[...]
        hmid = _silu(_dot(xb, w1_ref[...])) * _dot(xb, w3_ref[...])
        yacc[slot] += _dot(hmid.astype(BF16), w2_ref[...])

    @pl.when(jnp.logical_not(used))
    def _():
        issue_neighbours()

    @pl.when((i == N_MOE_TILES - 1) & (j == MOE_NFF - 1))
    def _():
        for r in range(TMM):
            scatter(slot, r, scur_ref[r]).start()
        wait_rows("s", 2 * TMM)
        wait_rows("g", TMM)


def _moe_ffn(tile_expert, n_used, gsrc, sdst, x, w1, w3, w2):
    nff = MOE_NFF

    def wcol(i, j, te, nu):
        return (te[i], 0, jnp.where(i < nu[0], j, nff - 1))

    def wrow(i, j, te, nu):
        return (te[i], jnp.where(i < nu[0], j, nff - 1), 0)

    smem = functools.partial(pl.BlockSpec, (TMM,), memory_space=pltpu.SMEM)
    grid_spec = pltpu.PrefetchScalarGridSpec(
        num_scalar_prefetch=2,
        grid=(N_MOE_TILES, nff),
        in_specs=[
            smem(lambda i, j, te, nu: (i + 1,)),
            smem(lambda i, j, te, nu: (i,)),
            smem(lambda i, j, te, nu: (i,)),
            smem(lambda i, j, te, nu: (i + 1,)),
            pl.BlockSpec(memory_space=pl.ANY),
            pl.BlockSpec((None, D, TFF), wcol),
            pl.BlockSpec((None, D, TFF), wcol),
            pl.BlockSpec((None, TFF, D), wrow),
        ],
        out_specs=pl.BlockSpec(memory_space=pl.ANY),
        scratch_shapes=[
            pltpu.VMEM((2, TMM, D), F32),
            pltpu.VMEM((2, TMM, D), F32),
            pltpu.VMEM((TMM, D), BF16),
            pltpu.SemaphoreType.DMA(()),
            pltpu.SemaphoreType.DMA(()),
        ],
    )
    return pl.pallas_call(
        _moe_ffn_kernel,
        grid_spec=grid_spec,
        out_shape=jax.ShapeDtypeStruct((MOE_OUT_ROWS, D), F32),
        compiler_params=_params(("arbitrary", "arbitrary")),
        name="moe_ffn",
    )(tile_expert, n_used, gsrc, gsrc, sdst, sdst, x, w1, w3, w2)


def _combine_kernel(x_ref, meta_ref, y0_ref, y1_ref, g_ref, b_ref, o_ref):
    meta = meta_ref[...]
    moe = meta[:, 2:3] * y0_ref[...] + meta[:, 3:4] * y1_ref[...]
    o_ref[...] = _layernorm(ALPHA * x_ref[...] + moe, g_ref[...], b_ref[...])


def _combine(x, meta, ys, g, b):
    c2 = lambda i: (0, 0)
    return pl.pallas_call(
        _combine_kernel,
        grid=(NT // TM,),
        in_specs=[
            pl.BlockSpec((TM, D), lambda i: (i, 0)),
            pl.BlockSpec((TM, LANES), lambda i: (i, 0)),
            pl.BlockSpec((TM, D), lambda i: (i, 0)),
            pl.BlockSpec((TM, D), lambda i: (i + NT // TM, 0)),
            pl.BlockSpec((1, D), c2),
            pl.BlockSpec((1, D), c2),
        ],
        out_specs=pl.BlockSpec((TM, D), lambda i: (i, 0)),
        out_shape=jax.ShapeDtypeStruct((NT, D), F32),
        compiler_params=_params(("parallel",)),
        name="moe_combine",
    )(x, meta, ys, ys, g, b)


def _pad_cols(w, n):
    return jnp.pad(w, ((0, 0), (0, n - w.shape[1])))


def kernel(x_prompt, x_sample, state_hgrn, state_gla, state_mlstm_C, state_mlstm_n, state_mlstm_m,
           state_mlstm_conv, w_in_even, hg_lower_bounds, w_gk, b_gk, gn_hg, gn_gla, w_out_even,
           w1_dense, w3_dense, w2_dense, w_in_odd, b_gate_odd, conv_w, conv_b, hn_w, w_out_odd,
           w_router, w1_moe, w3_moe, w2_moe, ln1_g, ln1_b, ln2_g, ln2_b):
    assert x_prompt.shape == (BATCH, SEQ, D) and x_sample.shape == (NS, 1, D)
    assert w_in_even.shape[0] == 1 and w_in_odd.shape[0] == 1 and hg_lower_bounds.shape[0] == 2
    sel_np, masks_np = _gla_matrices()
    sel = jnp.asarray(sel_np, BF16)
    masks = jnp.asarray(masks_np, F32)
    tri_cs = jnp.asarray(_tri(CS, False), BF16)
    tri_tm = jnp.asarray(_tri(TM, True), BF16)
    row = lambda a: a.reshape(1, -1)

    x0 = jnp.concatenate([x_prompt.reshape(NP, D), x_sample.reshape(NS, D)], axis=0)

    w_even = w_in_even[0].astype(BF16)
    z, zgr = _proj(x0, w_even[:, :EVEN_MAIN], _pad_cols(w_even[:, EVEN_MAIN:], LANES))
    wgk = jnp.pad(w_gk[0].astype(BF16), ((0, LANES - GLA_RANK), (0, 0)))
    lbp = hg_lower_bounds
    y_p, hg_p, gla_p = _even_prompt(z, zgr, lbp, wgk, row(b_gk[0]), row(gn_hg[0]), row(gn_gla[0]), sel, masks)

    zs = z[NP:].reshape(NS // SG, SG, EVEN_MAIN).transpose(0, 2, 1)
    grs = zgr[NP:].reshape(NS // SG, SG, LANES).transpose(0, 2, 1)
    y_s, hg_s, gla_s = _even_sample(z, zs, grs, lbp.T, wgk.T, b_gk[0].reshape(-1, 1),
                                    row(gn_hg[0]), row(gn_gla[0]), state_hgrn[0], state_gla[0])
    y = jnp.concatenate([y_p.reshape(NP, D), y_s], axis=0)
    x1 = _out_ln(x0, y, w_out_even[0].astype(BF16), row(ln1_g[0]), row(ln1_b[0]))
    x2 = _ffn(x1, w1_dense[0].astype(BF16), w3_dense[0].astype(BF16), w2_dense[0].astype(BF16),
              row(ln2_g[0]), row(ln2_b[0]))

    w_odd = w_in_odd[0].astype(BF16)
    zo, zog = _proj(x2, w_odd[:, :ODD_MAIN], _pad_cols(w_odd[:, ODD_MAIN:], LANES))
    bg = jnp.pad(b_gate_odd[0], (0, LANES - 2 * ML_H)).reshape(1, LANES)
    yo_p, c_p, n_p, m_p, conv_p = _odd_prompt(zo, zog, bg, conv_w[0], row(conv_b[0]), row(hn_w[0]), tri_cs)

    ut = zo[NP:, :D].reshape(NS // SG, SG, D).transpose(0, 2, 1)
    conv_in = state_mlstm_conv[0]
    conv_t = conv_in.reshape(NS // SG, SG, CONV_W - 1, D).transpose(0, 2, 3, 1)
    m_in = jnp.pad(state_mlstm_m[0], ((0, 0), (0, LANES - ML_H)))
    yo_s, c_s, n_s, m_s, conv_s = _odd_sample(
        zo, zog, ut, conv_in.reshape(NS, (CONV_W - 1) * D), conv_t, bg, conv_w[0], conv_w[0].T, row(conv_b[0]), conv_b[0].reshape(-1, 1),
        row(hn_w[0]), state_mlstm_C[0], state_mlstm_n[0], m_in)
    yo = jnp.concatenate([yo_p.reshape(NP, D), yo_s], axis=0)

    wr = _pad_cols(w_router[0].astype(BF16), LANES)
    x3, meta, cnt = _out_ln_router(x2, yo, w_out_odd[0].astype(BF16), row(ln1_g[1]), row(ln1_b[1]), wr, tri_tm)

    counts = cnt[0, :N_EXPERTS].astype(jnp.int32)
    padded = ((counts + TMM - 1) // TMM) * TMM
    ends = jnp.cumsum(padded)
    offsets = ends - padded
    idx = meta[:, 0:2].astype(jnp.int32)
    pos = offsets[idx] + meta[:, 4:6].astype(jnp.int32)
    tile_start = jnp.arange(N_MOE_TILES, dtype=jnp.int32) * TMM
    tile_expert = jnp.minimum(jnp.sum(tile_start[:, None] >= ends[None, :], axis=1), N_EXPERTS - 1).astype(jnp.int32)
    n_used = (ends[-1] // TMM).astype(jnp.int32).reshape(1)

    slot = jnp.arange(MOE_SLOTS, dtype=jnp.int32)
    e_slot = jnp.sum(slot[:, None] >= ends[None, :], axis=1)
    off_x = jnp.concatenate([offsets, ends[-1:]])
    cnt_x = jnp.concatenate([counts, jnp.zeros((1,), jnp.int32)])
    real_before_x = jnp.concatenate([jnp.cumsum(counts) - counts, jnp.full((1,), 2 * NT, jnp.int32)])
    local = slot - off_x[e_slot]
    real_before = real_before_x[e_slot] + jnp.minimum(local, cnt_x[e_slot])
    spill_row = 2 * NT + TMM + (slot - real_before)
    token = jnp.arange(NT, dtype=jnp.int32)
    dst_rows = jnp.stack([token, NT + token], axis=1)
    dst = spill_row.at[pos.reshape(-1)].set(dst_rows.reshape(-1), unique_indices=True)
    sdst = jnp.concatenate([2 * NT + jnp.arange(TMM, dtype=jnp.int32), dst])
    src_tok = jnp.where(dst < NT, dst, jnp.where(dst < 2 * NT, dst - NT, 0))
    gsrc = jnp.concatenate([src_tok, jnp.zeros((TMM,), jnp.int32)])

    ys = _moe_ffn(tile_expert, n_used, gsrc, sdst, x3,
                  w1_moe[0].astype(BF16), w3_moe[0].astype(BF16), w2_moe[0].astype(BF16))
    out = _combine(x3, meta, ys, row(ln2_g[1]), row(ln2_b[1]))

    y_prompt = out[:NP].reshape(BATCH, SEQ, D)
    y_sample = out[NP:].reshape(NS, 1, D)
    return (y_prompt, y_sample,
            hg_p.reshape(1, BATCH, HG_H, HG_DK, HG_DV), gla_p.reshape(1, BATCH, GLA_H, GLA_DK, GLA_DV),
            c_p[None], n_p[None], m_p[:, 0, :ML_H][None], conv_p[None],
            hg_s[None], gla_s[None], c_s[None], n_s[None], m_s[:, :ML_H][None], conv_s.reshape(1, NS, CONV_W - 1, D))
```

```python
import functools
import math

import jax
import jax.numpy as jnp
import numpy as np
from jax import lax
from jax.experimental import pallas as pl
from jax.experimental.pallas import tpu as pltpu

F32 = jnp.float32
BF16 = jnp.bfloat16

D = 1024
BATCH = 8
SEQ = 2048
DEC_BATCH = 128
NP = BATCH * SEQ
NS = DEC_BATCH
NT = NP + NS
HG_H, HG_DK, HG_DV = 4, 128, 128
GLA_H, GLA_DK, GLA_DV = 4, 64, 128
GLA_RANK = 16
GLA_GATE_NORM = 16.0
ML_H, ML_DK, ML_DV = 4, 128, 256
CONV_W = 4
D_FF_DENSE = 2816
D_FF_EXPERT = 3584
N_EXPERTS = 8
EPS = 1e-5
DEPTH = 2
ALPHA = (2.0 * DEPTH) ** 0.25
EVEN_MAIN = 3584
ODD_MAIN = 3072

LANES = 128
VMEM_LIMIT = 56 * 1024 * 1024

TM = 384
CS = 128
NCHUNK = SEQ // CS
SEQ_PER_STEP = 1
SG = 16
TMM = 512
TFF = 896
MOE_NFF = D_FF_EXPERT // TFF
MOE_ROWS_PER_STEP = TMM // MOE_NFF
N_MOE_TILES = -(-(2 * NT + N_EXPERTS * (TMM - 1)) // TMM)
MOE_SLOTS = N_MOE_TILES * TMM
MOE_OUT_ROWS = MOE_SLOTS + TMM
N_LEVELS = int(math.log2(CS))

assert NT % TM == 0 and NP % CS == 0 and NS % SG == 0 and D_FF_EXPERT % TFF == 0 and TMM % MOE_NFF == 0


def _params(sem, limit=VMEM_LIMIT):
    return pltpu.CompilerParams(dimension_semantics=sem, vmem_limit_bytes=limit)


def _dot(a, b):
    return jnp.dot(a, b, preferred_element_type=F32)


def _dot_nt(a, b):
    return lax.dot_general(a, b, (((1,), (1,)), ((), ())), preferred_element_type=F32)


def _dot_tn(a, b):
    return lax.dot_general(a, b, (((0,), (0,)), ((), ())), preferred_element_type=F32)


def _split3(x):
    hi = x.astype(BF16)
    r1 = x - hi.astype(F32)
    mid = r1.astype(BF16)
    lo = (r1 - mid.astype(F32)).astype(BF16)
    return hi, mid, lo


def _dot_sel(sel, x):
    hi, mid, lo = _split3(x)
    return _dot(sel, hi) + _dot(sel, mid) + _dot(sel, lo)


def _sigmoid(x):
    return jax.nn.sigmoid(x)


def _silu(x):
    return x * jax.nn.sigmoid(x)


def _log_sigmoid(x):
    return jnp.minimum(x, 0.0) - jnp.log(1.0 + jnp.exp(-jnp.abs(x)))


def _layernorm(r, g, b):
    mu = jnp.mean(r, axis=-1, keepdims=True)
    c = r - mu
    var = jnp.mean(c * c, axis=-1, keepdims=True)
    return c * lax.rsqrt(var + EPS) * g + b


def _gla_matrices():
    sel = np.zeros(((2 + N_LEVELS) * CS, CS), np.float32)
    masks = np.zeros((N_LEVELS + 1, CS, CS), np.float32)
    for t in range(CS):
        sel[t, : t + 1] = 1.0
        sel[CS + t, t + 1:] = 1.0
        for l in range(N_LEVELS):
            half = 1 << l
            start = (t // (2 * half)) * (2 * half)
            mid = start + half
            row = (2 + l) * CS + t
            if t >= mid:
                sel[row, mid: t + 1] = 1.0
                masks[l, t, start:mid] = 1.0
            else:
                sel[row, t + 1: mid] = 1.0
        masks[N_LEVELS, t, t] = 1.0
    return sel, masks


def _tri(n, strict):
    return np.tril(np.ones((n, n), np.float32), -1 if strict else 0)


def _proj_kernel(x_ref, wa_ref, wb_ref, oa_ref, ob_ref):
    xb = x_ref[...].astype(BF16)
    oa_ref[...] = _dot(xb, wa_ref[...])
    ob_ref[...] = _dot(xb, wb_ref[...])


def _proj(x, wa, wb):
    na, nb = wa.shape[1], wb.shape[1]
    return pl.pallas_call(
        _proj_kernel,
        grid=(NT // TM,),
        in_specs=[
            pl.BlockSpec((TM, D), lambda i: (i, 0)),
            pl.BlockSpec((D, na), lambda i: (0, 0)),
            pl.BlockSpec((D, nb), lambda i: (0, 0)),
        ],
        out_specs=[
            pl.BlockSpec((TM, na), lambda i: (i, 0)),
            pl.BlockSpec((TM, nb), lambda i: (i, 0)),
        ],
        out_shape=[jax.ShapeDtypeStruct((NT, na), F32), jax.ShapeDtypeStruct((NT, nb), F32)],
        compiler_params=_params(("parallel",)),
        name="proj",
    )(x, wa, wb)


def _rms_gate(o, gate, w):
    o = o * lax.rsqrt(jnp.mean(o * o, axis=-1, keepdims=True) + EPS) * w
    return o * _silu(gate)


def _gla_chunk(q, k, v, g, st_ref, sel, masks_ref, heads, dk, dv):
    e = _dot_sel(sel, g)
    zf = jnp.exp(e)
    z_cum = zf[0:CS]
    z_end = zf[CS:2 * CS]
    st = st_ref[...]
    outs = []
    for h in range(heads):
        ks = slice(h * dk, (h + 1) * dk)
        vs = slice(h * dv, (h + 1) * dv)
        qh, kh = q[:, ks], k[:, ks]
        vh = v[:, vs].astype(BF16)
        scores = _dot_nt(qh.astype(BF16), kh.astype(BF16)) * masks_ref[N_LEVELS]
        for l in range(N_LEVELS):
            zl = zf[(2 + l) * CS:(3 + l) * CS, ks]
            scores = scores + _dot_nt((qh * zl).astype(BF16), (kh * zl).astype(BF16)) * masks_ref[l]
        o = _dot(scores.astype(BF16), vh)
        o = o + _dot_nt((qh * z_cum[:, ks]).astype(BF16), st[:, ks].astype(BF16))
        outs.append(o)
        upd = _dot_tn(vh, (kh * z_end[:, ks]).astype(BF16))
        st_ref[:, ks] = st[:, ks] * z_cum[CS - 1:CS, ks] + upd
    return outs


def _even_prompt_kernel(*refs):
    z_refs = refs[0:SEQ_PER_STEP]
    zgr_refs = refs[SEQ_PER_STEP:2 * SEQ_PER_STEP]
    (lbp_ref, wgk_ref, bgk_ref, gnh_ref, gng_ref, sel_ref, masks_ref,
     y_ref, shg_ref, sgla_ref, st_hg, st_gla) = refs[2 * SEQ_PER_STEP:]
    c = pl.program_id(1)

    @pl.when(c == 0)
    def _():
        st_hg[...] = jnp.zeros_like(st_hg)
        st_gla[...] = jnp.zeros_like(st_gla)

    sel = sel_ref[...]
    p = lbp_ref[...]
    pe = jnp.exp(p - jnp.max(p, axis=0, keepdims=True))
    lb = pe[0:1] / jnp.sum(pe, axis=0, keepdims=True)

    for s in range(SEQ_PER_STEP):
        z = z_refs[s][...]
        hq, hf, hi, hg = z[:, 0:512], z[:, 512:1024], z[:, 1024:1536], z[:, 1536:2048]
        gq, gk, gv, gg = z[:, 2048:2304], z[:, 2304:2560], z[:, 2560:3072], z[:, 3072:3584]
        f = lb + (1.0 - lb) * _sigmoid(hf)
        k_hg = (1.0 - lb) * _sigmoid(-hf)
        o_hg = _gla_chunk(_silu(hq), k_hg, hi, jnp.log(f), st_hg.at[s], sel, masks_ref, HG_H, HG_DK, HG_DV)

        la = _log_sigmoid(_dot(zgr_refs[s][...].astype(BF16), wgk_ref[...]) + bgk_ref[...]) / GLA_GATE_NORM
        o_gla = _gla_chunk(gq * GLA_DK ** -0.5, gk, gv, la, st_gla.at[s], sel, masks_ref, GLA_H, GLA_DK, GLA_DV)

        for h in range(HG_H):
            cs = slice(h * 128, (h + 1) * 128)
            y_ref[s, :, cs] = _rms_gate(o_hg[h], hg[:, cs], gnh_ref[...]).astype(BF16)
        for h in range(GLA_H):
            cs = slice(h * 128, (h + 1) * 128)
            y_ref[s, :, 512 + h * 128:512 + (h + 1) * 128] = _rms_gate(o_gla[h], gg[:, cs], gng_ref[...]).astype(BF16)

    @pl.when(c == NCHUNK - 1)
    def _():
        for s in range(SEQ_PER_STEP):
            shg_ref[s] = st_hg[s].T
            sgla_ref[s] = st_gla[s].T


def _seq_row_specs(width):
    return [pl.BlockSpec((CS, width), functools.partial(lambda b, c, s: ((SEQ_PER_STEP * b + s) * NCHUNK + c, 0), s=s))
            for s in range(SEQ_PER_STEP)]


def _even_prompt(z, zgr, lbp, wgk, bgk, gnh, gng, sel, masks):
    const2 = lambda b, c: (0, 0)
    sp = SEQ_PER_STEP
    return pl.pallas_call(
        _even_prompt_kernel,
        grid=(BATCH // sp, NCHUNK),
        in_specs=_seq_row_specs(EVEN_MAIN) + _seq_row_specs(LANES) + [
            pl.BlockSpec(lbp.shape, const2),
            pl.BlockSpec(wgk.shape, const2),
            pl.BlockSpec(bgk.shape, const2),
            pl.BlockSpec(gnh.shape, const2),
            pl.BlockSpec(gng.shape, const2),
            pl.BlockSpec(sel.shape, const2),
            pl.BlockSpec(masks.shape, lambda b, c: (0, 0, 0)),
        ],
        out_specs=[
            pl.BlockSpec((sp, CS, D), lambda b, c: (b, c, 0)),
            pl.BlockSpec((sp, HG_H * HG_DK, HG_DV), lambda b, c: (b, 0, 0)),
            pl.BlockSpec((sp, GLA_H * GLA_DK, GLA_DV), lambda b, c: (b, 0, 0)),
        ],
        out_shape=[
            jax.ShapeDtypeStruct((BATCH, SEQ, D), BF16),
            jax.ShapeDtypeStruct((BATCH, HG_H * HG_DK, HG_DV), F32),
            jax.ShapeDtypeStruct((BATCH, GLA_H * GLA_DK, GLA_DV), F32),
        ],
        scratch_shapes=[pltpu.VMEM((sp, HG_DV, HG_H * HG_DK), F32), pltpu.VMEM((sp, GLA_DV, GLA_H * GLA_DK), F32)],
        compiler_params=_params(("parallel", "arbitrary")),
        name="even_prompt",
    )(*([z] * sp), *([zgr] * sp), lbp, wgk, bgk, gnh, gng, sel, masks)


def _even_sample_kernel(zr_ref, zt_ref, grt_ref, lbpt_ref, wgkt_ref, bgkt_ref, gnh_ref, gng_ref,
                        shg_ref, sgla_ref, y_ref, shg_out, sgla_out, o_scr):
    zt = zt_ref[0]
    hq_t, hf_t = zt[0:512], zt[512:1024]
    gq_t, gk_t = zt[2048:2304], zt[2304:2560]
    pt = lbpt_ref[...]
    pe = jnp.exp(pt - jnp.max(pt, axis=1, keepdims=True))
    lb = pe[:, 0:1] / jnp.sum(pe, axis=1, keepdims=True)
    a_hg = jnp.exp(jnp.log(lb + (1.0 - lb) * _sigmoid(hf_t)))
    k_hg = (1.0 - lb) * _sigmoid(-hf_t)
    q_hg = _silu(hq_t)
    la = _log_sigmoid(_dot(wgkt_ref[...], grt_ref[0].astype(BF16)) + bgkt_ref[...]) / GLA_GATE_NORM
    a_gla = jnp.exp(la)
    q_gla = gq_t * GLA_DK ** -0.5
    zr = zr_ref[...]
    hi, hg = zr[:, 1024:1536], zr[:, 1536:2048]
    gv, gg = zr[:, 2560:3072], zr[:, 3072:3584]

    for j in range(SG):
        for h in range(HG_H):
            ks = slice(h * HG_DK, (h + 1) * HG_DK)
            s_new = a_hg[ks, j:j + 1] * shg_ref[j, h] + k_hg[ks, j:j + 1] * hi[j:j + 1, h * 128:(h + 1) * 128]
            shg_out[j, h] = s_new
            o_scr[j:j + 1, h * 128:(h + 1) * 128] = jnp.sum(q_hg[ks, j:j + 1] * s_new, axis=0, keepdims=True)
        for h in range(GLA_H):
            ks = slice(h * GLA_DK, (h + 1) * GLA_DK)
            s_new = a_gla[ks, j:j + 1] * sgla_ref[j, h] + gk_t[ks, j:j + 1] * gv[j:j + 1, h * 128:(h + 1) * 128]
            sgla_out[j, h] = s_new
            o_scr[j:j + 1, 512 + h * 128:512 + (h + 1) * 128] = jnp.sum(
                q_gla[ks, j:j + 1] * s_new, axis=0, keepdims=True)

    o = o_scr[...]
    for h in range(HG_H):
        cs = slice(h * 128, (h + 1) * 128)
        y_ref[:, cs] = _rms_gate(o[:, cs], hg[:, cs], gnh_ref[...]).astype(BF16)
    for h in range(GLA_H):
        cs = slice(512 + h * 128, 512 + (h + 1) * 128)
        y_ref[:, cs] = _rms_gate(o[:, cs], gg[:, h * 128:(h + 1) * 128], gng_ref[...]).astype(BF16)


def _even_sample(z, zt3, grt3, lbpt, wgkt, bgkt, gnh, gng, s_hg, s_gla):
    c2 = lambda g: (0, 0)
    return pl.pallas_call(
        _even_sample_kernel,
        grid=(NS // SG,),
        in_specs=[
            pl.BlockSpec((SG, EVEN_MAIN), lambda g: (NP // SG + g, 0)),
            pl.BlockSpec((1, EVEN_MAIN, SG), lambda g: (g, 0, 0)),
            pl.BlockSpec((1, LANES, SG), lambda g: (g, 0, 0)),
            pl.BlockSpec(lbpt.shape, c2),
            pl.BlockSpec(wgkt.shape, c2),
            pl.BlockSpec(bgkt.shape, c2),
            pl.BlockSpec(gnh.shape, c2),
            pl.BlockSpec(gng.shape, c2),
            pl.BlockSpec((SG, HG_H, HG_DK, HG_DV), lambda g: (g, 0, 0, 0)),
            pl.BlockSpec((SG, GLA_H, GLA_DK, GLA_DV), lambda g: (g, 0, 0, 0)),
        ],
        out_specs=[
            pl.BlockSpec((SG, D), lambda g: (g, 0)),
            pl.BlockSpec((SG, HG_H, HG_DK, HG_DV), lambda g: (g, 0, 0, 0)),
            pl.BlockSpec((SG, GLA_H, GLA_DK, GLA_DV), lambda g: (g, 0, 0, 0)),
        ],
        out_shape=[
            jax.ShapeDtypeStruct((NS, D), BF16),
            jax.ShapeDtypeStruct((NS, HG_H, HG_DK, HG_DV), F32),
            jax.ShapeDtypeStruct((NS, GLA_H, GLA_DK, GLA_DV), F32),
        ],
        scratch_shapes=[pltpu.VMEM((SG, D), F32)],
        compiler_params=_params(("parallel",)),
        name="even_sample",
    )(z, zt3, grt3, lbpt, wgkt, bgkt, gnh, gng, s_hg, s_gla)


def _out_ln_kernel(x_ref, y_ref, w_ref, g_ref, b_ref, o_ref):
    r = ALPHA * x_ref[...] + _dot(y_ref[...], w_ref[...])
    o_ref[...] = _layernorm(r, g_ref[...], b_ref[...])


def _out_ln(x, y, w, g, b):
    c2 = lambda i: (0, 0)
    return pl.pallas_call(
        _out_ln_kernel,
        grid=(NT // TM,),
        in_specs=[
            pl.BlockSpec((TM, D), lambda i: (i, 0)),
            pl.BlockSpec((TM, D), lambda i: (i, 0)),
            pl.BlockSpec((D, D), c2),
            pl.BlockSpec((1, D), c2),
            pl.BlockSpec((1, D), c2),
        ],
        out_specs=pl.BlockSpec((TM, D), lambda i: (i, 0)),
        out_shape=jax.ShapeDtypeStruct((NT, D), F32),
        compiler_params=_params(("parallel",)),
        name="out_ln",
    )(x, y, w, g, b)


FF_SPLIT = 2


def _ffn_kernel(x_ref, w1_ref, w3_ref, w2_ref, g_ref, b_ref, o_ref):
    x = x_ref[...]
    xb = x.astype(BF16)
    step = D_FF_DENSE // FF_SPLIT
    acc = ALPHA * x
    for s in range(FF_SPLIT):
        cs = slice(s * step, (s + 1) * step)
        hmid = _silu(_dot(xb, w1_ref[:, cs])) * _dot(xb, w3_ref[:, cs])
        acc = acc + _dot(hmid.astype(BF16), w2_ref[cs, :])
    o_ref[...] = _layernorm(acc, g_ref[...], b_ref[...])


def _ffn(x, w1, w3, w2, g, b):
    c2 = lambda i: (0, 0)
    one = pl.Buffered(1)
    return pl.pallas_call(
        _ffn_kernel,
        grid=(NT // TM,),
        in_specs=[
            pl.BlockSpec((TM, D), lambda i: (i, 0)),
            pl.BlockSpec((D, D_FF_DENSE), c2, pipeline_mode=one),
            pl.BlockSpec((D, D_FF_DENSE), c2, pipeline_mode=one),
            pl.BlockSpec((D_FF_DENSE, D), c2, pipeline_mode=one),
            pl.BlockSpec((1, D), c2),
            pl.BlockSpec((1, D), c2),
        ],
        out_specs=pl.BlockSpec((TM, D), lambda i: (i, 0)),
        out_shape=jax.ShapeDtypeStruct((NT, D), F32),
        compiler_params=_params(("parallel",)),
        name="ffn_dense",
    )(x, w1, w3, w2, g, b)


def _mh_norm_gate(hh, o_pre, w):
    mu = jnp.mean(hh, axis=-1, keepdims=True)
    c = hh - mu
    var = jnp.mean(c * c, axis=-1, keepdims=True)
    return _sigmoid(o_pre) * (c * lax.rsqrt(var + EPS) * w)


def _odd_prompt_kernel(*refs):
    z_refs = refs[0:SEQ_PER_STEP]
    zg_refs = refs[SEQ_PER_STEP:2 * SEQ_PER_STEP]
    (bg_ref, cw_ref, cb_ref, hnw_ref, tri_ref,
     y_ref, c_out, n_out, m_out, conv_out,
     c_scr, n_scr, m_scr, u_scr) = refs[2 * SEQ_PER_STEP:]
    c = pl.program_id(1)

    @pl.when(c == 0)
    def _():
        c_scr[...] = jnp.zeros_like(c_scr)
        n_scr[...] = jnp.zeros_like(n_scr)
        m_scr[...] = jnp.zeros_like(m_scr)
        for s in range(SEQ_PER_STEP):
            u_scr[s, 0:8, :] = jnp.zeros((8, D), F32)

    row = lax.broadcasted_iota(jnp.int32, (CS, CS), 0)
    col = lax.broadcasted_iota(jnp.int32, (CS, CS), 1)
    causal = col <= row
    tails = []

    for s in range(SEQ_PER_STEP):
        z_ref = z_refs[s]
        u_scr[s, 8:8 + CS, :] = z_ref[:, 0:D]
        uc = cb_ref[...]
        for j in range(CONV_W):
            uc = uc + u_scr[s, 5 + j:5 + j + CS, :] * cw_ref[j:j + 1, :]
        tail = u_scr[s, CS:CS + 8, :]
        u_scr[s, 0:8, :] = tail
        tails.append(tail)
        act = _silu(uc)
        q = act[:, 0:512] * ML_DK ** -0.5
        k = act[:, 512:1024]
        v = z_ref[:, D:2 * D]
        o_pre = z_ref[:, 2 * D:3 * D]

        gates = zg_refs[s][...] + bg_ref[...]
        lf = _log_sigmoid(gates)
        bcum = _dot_sel(tri_ref[...], lf)
        bcum_t = bcum.T
        gates_t = gates.T
        m_all = m_scr[s]

        for h in range(ML_H):
            ks = slice(h * ML_DK, (h + 1) * ML_DK)
            vs = slice(h * ML_DV, (h + 1) * ML_DV)
            qh, kh = q[:, ks], k[:, ks]
            vh = v[:, vs].astype(BF16)
            b_col = bcum[:, 4 + h:5 + h]
            b_row = bcum_t[4 + h:5 + h, :]
            i_col = gates[:, h:h + 1]
            i_row = gates_t[h:h + 1, :]
            m_prev = m_all[:, h:h + 1]
            log_d = jnp.where(causal, b_col - b_row + i_row, -jnp.inf)
            log_prev = b_col + m_prev
            m_t = jnp.maximum(jnp.max(log_d, axis=-1, keepdims=True), log_prev)
            d = jnp.exp(log_d - m_t)
            w_prev = jnp.exp(log_prev - m_t)
            scores = _dot_nt(qh.astype(BF16), kh.astype(BF16)) * d
            c_h = c_scr[s, h]
            n_h = n_scr[s, h:h + 1, :]
            num = _dot(scores.astype(BF16), vh) + w_prev * _dot(qh.astype(BF16), c_h.astype(BF16))
            den = jnp.sum(scores, axis=-1, keepdims=True) + w_prev * jnp.sum(qh * n_h, axis=-1, keepdims=True)
            hh = num / jnp.maximum(jnp.abs(den), jnp.exp(-m_t))
            m_new = m_t[CS - 1:CS, :]
            b_last = b_col[CS - 1:CS, :]
            w_c = jnp.exp(b_last + m_prev - m_new)
            w_s = jnp.exp(b_last - b_col + i_col - m_new)
            kw = kh * w_s
            c_scr[s, h] = w_c * c_h + _dot_tn(kw.astype(BF16), vh)
            n_scr[s, h:h + 1, :] = w_c * n_h + jnp.sum(kw, axis=0, keepdims=True)
            m_scr[s, :, h:h + 1] = m_new
            y_ref[s, :, vs] = _mh_norm_gate(hh, o_pre[:, vs], hnw_ref[:, vs]).astype(BF16)

    @pl.when(c == NCHUNK - 1)
    def _():
        c_out[...] = c_scr[...]
        for s in range(SEQ_PER_STEP):
            n_out[s] = n_scr[s, 0:ML_H, :]
            m_out[s] = m_scr[s]
            conv_out[s] = tails[s][8 - (CONV_W - 1):8, :]


def _odd_prompt(z, zg, bg, cw, cb, hnw, tri):
    c2 = lambda b, c: (0, 0)
    sp = SEQ_PER_STEP
    return pl.pallas_call(
        _odd_prompt_kernel,
        grid=(BATCH // sp, NCHUNK),
        in_specs=_seq_row_specs(ODD_MAIN) + _seq_row_specs(LANES) + [
            pl.BlockSpec((1, LANES), c2),
            pl.BlockSpec((CONV_W, D), c2),
            pl.BlockSpec((1, D), c2),
            pl.BlockSpec((1, D), c2),
            pl.BlockSpec((CS, CS), c2),
        ],
        out_specs=[
            pl.BlockSpec((sp, CS, D), lambda b, c: (b, c, 0)),
            pl.BlockSpec((sp, ML_H, ML_DK, ML_DV), lambda b, c: (b, 0, 0, 0)),
            pl.BlockSpec((sp, ML_H, ML_DK), lambda b, c: (b, 0, 0)),
            pl.BlockSpec((sp, 1, LANES), lambda b, c: (b, 0, 0)),
            pl.BlockSpec((sp, CONV_W - 1, D), lambda b, c: (b, 0, 0)),
        ],
        out_shape=[
            jax.ShapeDtypeStruct((BATCH, SEQ, D), BF16),
            jax.ShapeDtypeStruct((BATCH, ML_H, ML_DK, ML_DV), F32),
            jax.ShapeDtypeStruct((BATCH, ML_H, ML_DK), F32),
            jax.ShapeDtypeStruct((BATCH, 1, LANES), F32),
            jax.ShapeDtypeStruct((BATCH, CONV_W - 1, D), F32),
        ],
        scratch_shapes=[
            pltpu.VMEM((sp, ML_H, ML_DK, ML_DV), F32),
            pltpu.VMEM((sp, 8, ML_DK), F32),
            pltpu.VMEM((sp, 1, LANES), F32),
            pltpu.VMEM((sp, CS + 8, D), F32),
        ],
        compiler_params=_params(("parallel", "arbitrary")),
        name="odd_prompt",
    )(*([z] * sp), *([zg] * sp), bg, cw, cb, hnw, tri)


def _odd_sample_kernel(zr_ref, zg_ref, ut_ref, conv_ref, convt_ref, bg_ref, cw_ref, cwt_ref, cb_ref, cbt_ref,
                       hnw_ref, c_ref, n_ref, m_ref,
                       y_ref, c_out, n_out, m_out, conv_out, h_scr):
    zr = zr_ref[...]
    u = zr[:, 0:D]
    v = zr[:, D:2 * D]
    o_pre = zr[:, 2 * D:3 * D]
    uc = cb_ref[...] + u * cw_ref[CONV_W - 1:CONV_W, :]
    uc_t = cbt_ref[...] + ut_ref[0] * cwt_ref[:, CONV_W - 1:CONV_W]
    for j in range(CONV_W - 1):
        uc = uc + conv_ref[:, j * D:(j + 1) * D] * cw_ref[j:j + 1, :]
        uc_t = uc_t + convt_ref[0, j] * cwt_ref[:, j:j + 1]
        conv_out[:, j * D:(j + 1) * D] = conv_ref[:, (j + 1) * D:(j + 2) * D] if j + 1 < CONV_W - 1 else u
    act = _silu(uc)
    k_row = act[:, 512:1024]
    act_t = _silu(uc_t)
    q_t = act_t[0:512] * ML_DK ** -0.5
    k_t = act_t[512:1024]
    q_row = act[:, 0:512] * ML_DK ** -0.5

    gates = zg_ref[...] + bg_ref[...]
    lf = _log_sigmoid(gates)
    m_in = m_ref[...]
    m_out[...] = m_in

    for j in range(SG):
        for h in range(ML_H):
            ks = slice(h * ML_DK, (h + 1) * ML_DK)
            vs = slice(h * ML_DV, (h + 1) * ML_DV)
            ig = gates[j:j + 1, h:h + 1]
            log_prev = lf[j:j + 1, 4 + h:5 + h] + m_in[j:j + 1, h:h + 1]
            m_t = jnp.maximum(ig, log_prev)
            d = jnp.exp(ig - m_t)
            w_prev = jnp.exp(log_prev - m_t)
            c_new = w_prev * c_ref[j, h] + (d * k_t[ks, j:j + 1]) * v[j:j + 1, vs]
            n_new = w_prev * n_ref[j, h:h + 1, :] + d * k_row[j:j + 1, ks]
            c_out[j, h] = c_new
            n_out[j, h:h + 1, :] = n_new
            m_out[j:j + 1, h:h + 1] = m_t
            num = jnp.sum(q_t[ks, j:j + 1] * c_new, axis=0, keepdims=True)
            den = jnp.sum(q_row[j:j + 1, ks] * n_new, axis=-1, keepdims=True)
            h_scr[j:j + 1, vs] = num / jnp.maximum(jnp.abs(den), jnp.exp(-m_t))

    hh = h_scr[...]
    for h in range(ML_H):
        vs = slice(h * ML_DV, (h + 1) * ML_DV)
        y_ref[:, vs] = _mh_norm_gate(hh[:, vs], o_pre[:, vs], hnw_ref[:, vs]).astype(BF16)


def _odd_sample(z, zg, ut3, conv, convt, bg, cw, cwt, cb, cbt, hnw, c_in, n_in, m_in):
    c2 = lambda g: (0, 0)
    return pl.pallas_call(
        _odd_sample_kernel,
        grid=(NS // SG,),
        in_specs=[
            pl.BlockSpec((SG, ODD_MAIN), lambda g: (NP // SG + g, 0)),
            pl.BlockSpec((SG, LANES), lambda g: (NP // SG + g, 0)),
            pl.BlockSpec((1, D, SG), lambda g: (g, 0, 0)),
            pl.BlockSpec((SG, (CONV_W - 1) * D), lambda g: (g, 0)),
            pl.BlockSpec((1, CONV_W - 1, D, SG), lambda g: (g, 0, 0, 0)),
            pl.BlockSpec((1, LANES), c2),
            pl.BlockSpec((CONV_W, D), c2),
            pl.BlockSpec((D, CONV_W), c2),
            pl.BlockSpec((1, D), c2),
            pl.BlockSpec((D, 1), c2),
            pl.BlockSpec((1, D), c2),
            pl.BlockSpec((SG, ML_H, ML_DK, ML_DV), lambda g: (g, 0, 0, 0)),
            pl.BlockSpec((SG, ML_H, ML_DK), lambda g: (g, 0, 0)),
            pl.BlockSpec((SG, LANES), lambda g: (g, 0)),
        ],
        out_specs=[
            pl.BlockSpec((SG, D), lambda g: (g, 0)),
            pl.BlockSpec((SG, ML_H, ML_DK, ML_DV), lambda g: (g, 0, 0, 0)),
            pl.BlockSpec((SG, ML_H, ML_DK), lambda g: (g, 0, 0)),
            pl.BlockSpec((SG, LANES), lambda g: (g, 0)),
            pl.BlockSpec((SG, (CONV_W - 1) * D), lambda g: (g, 0)),
        ],
        out_shape=[
            jax.ShapeDtypeStruct((NS, D), BF16),
            jax.ShapeDtypeStruct((NS, ML_H, ML_DK, ML_DV), F32),
            jax.ShapeDtypeStruct((NS, ML_H, ML_DK), F32),
            jax.ShapeDtypeStruct((NS, LANES), F32),
            jax.ShapeDtypeStruct((NS, (CONV_W - 1) * D), F32),
        ],
        scratch_shapes=[pltpu.VMEM((SG, D), F32)],
        compiler_params=_params(("parallel",)),
        name="odd_sample",
    )(z, zg, ut3, conv, convt, bg, cw, cwt, cb, cbt, hnw, c_in, n_in, m_in)


def _out_ln_router_kernel(x_ref, y_ref, w_ref, g_ref, b_ref, wr_ref, tri_ref,
                          o_ref, meta_ref, cnt_ref, carry):
    i = pl.program_id(0)

    @pl.when(i == 0)
    def _():
        carry[...] = jnp.zeros_like(carry)

    r = ALPHA * x_ref[...] + _dot(y_ref[...], w_ref[...])
    x3 = _layernorm(r, g_ref[...], b_ref[...])
    o_ref[...] = x3

    lane = lax.broadcasted_iota(jnp.int32, (TM, LANES), 1).astype(F32)
    logits = jnp.where(lane < N_EXPERTS, _dot(x3.astype(BF16), wr_ref[...]), -jnp.inf)
    m1 = jnp.max(logits, axis=-1, keepdims=True)
    i1 = jnp.min(jnp.where(logits == m1, lane, float(LANES)), axis=-1, keepdims=True)
    rest = jnp.where(lane == i1, -jnp.inf, logits)
    m2 = jnp.max(rest, axis=-1, keepdims=True)
    i2 = jnp.min(jnp.where(rest == m2, lane, float(LANES)), axis=-1, keepdims=True)
    e2 = jnp.exp(m2 - m1)
    tot = 1.0 + e2
    w1 = 1.0 / tot
    w2 = e2 / tot

    sel1 = lane == i1
    sel2 = lane == i2
    onehot = jnp.where(sel1 | sel2, 1.0, 0.0)
    before = _dot(tri_ref[...], onehot.astype(BF16)) + carry[...]
    r1 = jnp.sum(jnp.where(sel1, before, 0.0), axis=-1, keepdims=True)
    r2 = jnp.sum(jnp.where(sel2, before, 0.0), axis=-1, keepdims=True)
    carry[...] = carry[...] + jnp.sum(onehot, axis=0, keepdims=True)
    cnt_ref[...] = carry[...]

    meta = jnp.where(lane == 0.0, i1, 0.0)
    meta = jnp.where(lane == 1.0, i2, meta)
    meta = jnp.where(lane == 2.0, w1, meta)
    meta = jnp.where(lane == 3.0, w2, meta)
    meta = jnp.where(lane == 4.0, r1, meta)
    meta = jnp.where(lane == 5.0, r2, meta)
    meta_ref[...] = meta


def _out_ln_router(x, y, w, g, b, wr, tri):
    c2 = lambda i: (0, 0)
    return pl.pallas_call(
        _out_ln_router_kernel,
        grid=(NT // TM,),
        in_specs=[
            pl.BlockSpec((TM, D), lambda i: (i, 0)),
            pl.BlockSpec((TM, D), lambda i: (i, 0)),
            pl.BlockSpec((D, D), c2),
            pl.BlockSpec((1, D), c2),
            pl.BlockSpec((1, D), c2),
            pl.BlockSpec((D, LANES), c2),
            pl.BlockSpec((TM, TM), c2),
        ],
        out_specs=[
            pl.BlockSpec((TM, D), lambda i: (i, 0)),
            pl.BlockSpec((TM, LANES), lambda i: (i, 0)),
            pl.BlockSpec((1, LANES), c2),
        ],
        out_shape=[
            jax.ShapeDtypeStruct((NT, D), F32),
            jax.ShapeDtypeStruct((NT, LANES), F32),
            jax.ShapeDtypeStruct((1, LANES), F32),
        ],
        scratch_shapes=[pltpu.VMEM((1, LANES), F32)],
        compiler_params=_params(("arbitrary",)),
        name="out_ln_router",
    )(x, y, w, g, b, wr, tri)


def _moe_ffn_kernel(te_ref, nu_ref, gnext_ref, gcur_ref, sprev_ref, scur_ref, x_hbm, w1_ref, w3_ref, w2_ref,
                    out_hbm, xg, yacc, xb_scr, sem_g, sem_s):
    i = pl.program_id(0)
    j = pl.program_id(1)
    used = i < nu_ref[0]
    slot = i % 2
    other = 1 - slot

    def gather(tok, buf, r):
        return pltpu.make_async_copy(x_hbm.at[pl.ds(tok, 1)], xg.at[buf, pl.ds(r, 1)], sem_g)

    def scatter(buf, r, dst):
        return pltpu.make_async_copy(yacc.at[buf, pl.ds(r, 1)], out_hbm.at[pl.ds(dst, 1)], sem_s)

    def wait_rows(kind, n):
        for _ in range(n):
            (gather(0, 0, 0) if kind == "g" else scatter(0, 0, 0)).wait()

    def issue_neighbours():
        for r in range(MOE_ROWS_PER_STEP):
            rr = j * MOE_ROWS_PER_STEP + r
            gather(gnext_ref[rr], other, rr).start(priority=r % 2)
            scatter(other, rr, sprev_ref[rr]).start(priority=r % 2)

    @pl.when(j == 0)
    def _():
        @pl.when(i == 0)
        def _():
            yacc[1] = jnp.zeros((TMM, D), F32)
            for r in range(TMM):
                gather(gcur_ref[r], 0, r).start()

        wait_rows("g", TMM)

        @pl.when(i > 0)
        def _():
            wait_rows("s", TMM)

        xb_scr[...] = xg[slot].astype(BF16)
        yacc[slot] = jnp.zeros((TMM, D), F32)

    @pl.when(used)
    def _():
        issue_neighbours()
        xb = xb_scr[...]
        hmid = _silu(_dot(xb, w1_ref[...])) * _dot(xb, w3_ref[...])
        yacc[slot] += _dot(hmid.astype(BF16), w2_ref[...])

    @pl.when(jnp.logical_not(used))
    def _():
        issue_neighbours()

    @pl.when((i == N_MOE_TILES - 1) & (j == MOE_NFF - 1))
    def _():
        for r in range(TMM):
            scatter(slot, r, scur_ref[r]).start()
        wait_rows("s", 2 * TMM)
        wait_rows("g", TMM)


def _moe_ffn(tile_expert, n_used, gsrc, sdst, x, w1, w3, w2):
    nff = MOE_NFF

    def wcol(i, j, te, nu):
        return (te[i], 0, jnp.where(i < nu[0], j, nff - 1))

    def wrow(i, j, te, nu):
        return (te[i], jnp.where(i < nu[0], j, nff - 1), 0)

    smem = functools.partial(pl.BlockSpec, (TMM,), memory_space=pltpu.SMEM)
    grid_spec = pltpu.PrefetchScalarGridSpec(
        num_scalar_prefetch=2,
        grid=(N_MOE_TILES, nff),
        in_specs=[
            smem(lambda i, j, te, nu: (i + 1,)),
            smem(lambda i, j, te, nu: (i,)),
            smem(lambda i, j, te, nu: (i,)),
            smem(lambda i, j, te, nu: (i + 1,)),
            pl.BlockSpec(memory_space=pl.ANY),
            pl.BlockSpec((None, D, TFF), wcol),
            pl.BlockSpec((None, D, TFF), wcol),
            pl.BlockSpec((None, TFF, D), wrow),
        ],
        out_specs=pl.BlockSpec(memory_space=pl.ANY),
        scratch_shapes=[
            pltpu.VMEM((2, TMM, D), F32),
            pltpu.VMEM((2, TMM, D), F32),
            pltpu.VMEM((TMM, D), BF16),
            pltpu.SemaphoreType.DMA(()),
            pltpu.SemaphoreType.DMA(()),
        ],
    )
    return pl.pallas_call(
        _moe_ffn_kernel,
        grid_spec=grid_spec,
        out_shape=jax.ShapeDtypeStruct((MOE_OUT_ROWS, D), F32),
        compiler_params=_params(("arbitrary", "arbitrary")),
        name="moe_ffn",
    )(tile_expert, n_used, gsrc, gsrc, sdst, sdst, x, w1, w3, w2)


def _combine_kernel(x_ref, meta_ref, y0_ref, y1_ref, g_ref, b_ref, o_ref):
    meta = meta_ref[...]
    moe = meta[:, 2:3] * y0_ref[...] + meta[:, 3:4] * y1_ref[...]
    o_ref[...] = _layernorm(ALPHA * x_ref[...] + moe, g_ref[...], b_ref[...])


def _combine(x, meta, ys, g, b):
    c2 = lambda i: (0, 0)
    return pl.pallas_call(
        _combine_kernel,
        grid=(NT // TM,),
        in_specs=[
            pl.BlockSpec((TM, D), lambda i: (i, 0)),
            pl.BlockSpec((TM, LANES), lambda i: (i, 0)),
            pl.BlockSpec((TM, D), lambda i: (i, 0)),
            pl.BlockSpec((TM, D), lambda i: (i + NT // TM, 0)),
            pl.BlockSpec((1, D), c2),
            pl.BlockSpec((1, D), c2),
        ],
        out_specs=pl.BlockSpec((TM, D), lambda i: (i, 0)),
        out_shape=jax.ShapeDtypeStruct((NT, D), F32),
        compiler_params=_params(("parallel",)),
        name="moe_combine",
    )(x, meta, ys, ys, g, b)


def _pad_cols(w, n):
    return jnp.pad(w, ((0, 0), (0, n - w.shape[1])))


def kernel(x_prompt, x_sample, state_hgrn, state_gla, state_mlstm_C, state_mlstm_n, state_mlstm_m,
           state_mlstm_conv, w_in_even, hg_lower_bounds, w_gk, b_gk, gn_hg, gn_gla, w_out_even,
           w1_dense, w3_dense, w2_dense, w_in_odd, b_gate_odd, conv_w, conv_b, hn_w, w_out_odd,
           w_router, w1_moe, w3_moe, w2_moe, ln1_g, ln1_b, ln2_g, ln2_b):
    assert x_prompt.shape == (BATCH, SEQ, D) and x_sample.shape == (NS, 1, D)
    assert w_in_even.shape[0] == 1 and w_in_odd.shape[0] == 1 and hg_lower_bounds.shape[0] == 2
    sel_np, masks_np = _gla_matrices()
    sel = jnp.asarray(sel_np, BF16)
    masks = jnp.asarray(masks_np, F32)
    tri_cs = jnp.asarray(_tri(CS, False), BF16)
    tri_tm = jnp.asarray(_tri(TM, True), BF16)
    row = lambda a: a.reshape(1, -1)

    x0 = jnp.concatenate([x_prompt.reshape(NP, D), x_sample.reshape(NS, D)], axis=0)

    w_even = w_in_even[0].astype(BF16)
    z, zgr = _proj(x0, w_even[:, :EVEN_MAIN], _pad_cols(w_even[:, EVEN_MAIN:], LANES))
    wgk = jnp.pad(w_gk[0].astype(BF16), ((0, LANES - GLA_RANK), (0, 0)))
    lbp = hg_lower_bounds
    y_p, hg_p, gla_p = _even_prompt(z, zgr, lbp, wgk, row(b_gk[0]), row(gn_hg[0]), row(gn_gla[0]), sel, masks)

    zs = z[NP:].reshape(NS // SG, SG, EVEN_MAIN).transpose(0, 2, 1)
    grs = zgr[NP:].reshape(NS // SG, SG, LANES).transpose(0, 2, 1)
    y_s, hg_s, gla_s = _even_sample(z, zs, grs, lbp.T, wgk.T, b_gk[0].reshape(-1, 1),
                                    row(gn_hg[0]), row(gn_gla[0]), state_hgrn[0], state_gla[0])
    y = jnp.concatenate([y_p.reshape(NP, D), y_s], axis=0)
    x1 = _out_ln(x0, y, w_out_even[0].astype(BF16), row(ln1_g[0]), row(ln1_b[0]))
    x2 = _ffn(x1, w1_dense[0].astype(BF16), w3_dense[0].astype(BF16), w2_dense[0].astype(BF16),
              row(ln2_g[0]), row(ln2_b[0]))

    w_odd = w_in_odd[0].astype(BF16)
    zo, zog = _proj(x2, w_odd[:, :ODD_MAIN], _pad_cols(w_odd[:, ODD_MAIN:], LANES))
    bg = jnp.pad(b_gate_odd[0], (0, LANES - 2 * ML_H)).reshape(1, LANES)
    yo_p, c_p, n_p, m_p, conv_p = _odd_prompt(zo, zog, bg, conv_w[0], row(conv_b[0]), row(hn_w[0]), tri_cs)

    ut = zo[NP:, :D].reshape(NS // SG, SG, D).transpose(0, 2, 1)
    conv_in = state_mlstm_conv[0]
    conv_t = conv_in.reshape(NS // SG, SG, CONV_W - 1, D).transpose(0, 2, 3, 1)
    m_in = jnp.pad(state_mlstm_m[0], ((0, 0), (0, LANES - ML_H)))
    yo_s, c_s, n_s, m_s, conv_s = _odd_sample(
        zo, zog, ut, conv_in.reshape(NS, (CONV_W - 1) * D), conv_t, bg, conv_w[0], conv_w[0].T, row(conv_b[0]), conv_b[0].reshape(-1, 1),
        row(hn_w[0]), state_mlstm_C[0], state_mlstm_n[0], m_in)
    yo = jnp.concatenate([yo_p.reshape(NP, D), yo_s], axis=0)

    wr = _pad_cols(w_router[0].astype(BF16), LANES)
    x3, meta, cnt = _out_ln_router(x2, yo, w_out_odd[0].astype(BF16), row(ln1_g[1]), row(ln1_b[1]), wr, tri_tm)

    counts = cnt[0, :N_EXPERTS].astype(jnp.int32)
    padded = ((counts + TMM - 1) // TMM) * TMM
    ends = jnp.cumsum(padded)
    offsets = ends - padded
    idx = meta[:, 0:2].astype(jnp.int32)
    pos = offsets[idx] + meta[:, 4:6].astype(jnp.int32)
    tile_start = jnp.arange(N_MOE_TILES, dtype=jnp.int32) * TMM
    tile_expert = jnp.minimum(jnp.sum(tile_start[:, None] >= ends[None, :], axis=1), N_EXPERTS - 1).astype(jnp.int32)
    n_used = (ends[-1] // TMM).astype(jnp.int32).reshape(1)

    slot = jnp.arange(MOE_SLOTS, dtype=jnp.int32)
    e_slot = jnp.sum(slot[:, None] >= ends[None, :], axis=1)
    off_x = jnp.concatenate([offsets, ends[-1:]])
    cnt_x = jnp.concatenate([counts, jnp.zeros((1,), jnp.int32)])
    real_before_x = jnp.concatenate([jnp.cumsum(counts) - counts, jnp.full((1,), 2 * NT, jnp.int32)])
    local = slot - off_x[e_slot]
    real_before = real_before_x[e_slot] + jnp.minimum(local, cnt_x[e_slot])
    spill_row = 2 * NT + TMM + (slot - real_before)
    token = jnp.arange(NT, dtype=jnp.int32)
    dst_rows = jnp.stack([token, NT + token], axis=1)
    dst = spill_row.at[pos.reshape(-1)].set(dst_rows.reshape(-1), unique_indices=True)
    sdst = jnp.concatenate([2 * NT + jnp.arange(TMM, dtype=jnp.int32), dst])
    src_tok = jnp.where(dst < NT, dst, jnp.where(dst < 2 * NT, dst - NT, 0))
    gsrc = jnp.concatenate([src_tok, jnp.zeros((TMM,), jnp.int32)])

    ys = _moe_ffn(tile_expert, n_used, gsrc, sdst, x3,
                  w1_moe[0].astype(BF16), w3_moe[0].astype(BF16), w2_moe[0].astype(BF16))
    out = _combine(x3, meta, ys, row(ln2_g[1]), row(ln2_b[1]))

    y_prompt = out[:NP].reshape(BATCH, SEQ, D)
    y_sample = out[NP:].reshape(NS, 1, D)
    return (y_prompt, y_sample,
            hg_p.reshape(1, BATCH, HG_H, HG_DK, HG_DV), gla_p.reshape(1, BATCH, GLA_H, GLA_DK, GLA_DV),
            c_p[None], n_p[None], m_p[:, 0, :ML_H][None], conv_p[None],
            hg_s[None], gla_s[None], c_s[None], n_s[None], m_s[:, :ML_H][None], conv_s.reshape(1, NS, CONV_W - 1, D))
```

```python
import functools
import math

import jax
import jax.numpy as jnp
import numpy as np
from jax import lax
from jax.experimental import pallas as pl
from jax.experimental.pallas import tpu as pltpu

F32 = jnp.float32
BF16 = jnp.bfloat16

D = 1024
BATCH = 8
SEQ = 2048
DEC_BATCH = 128
NP = BATCH * SEQ
NS = DEC_BATCH
NT = NP + NS
HG_H, HG_DK, HG_DV = 4, 128, 128
GLA_H, GLA_DK, GLA_DV = 4, 64, 128
GLA_RANK = 16
GLA_GATE_NORM = 16.0
ML_H, ML_DK, ML_DV = 4, 128, 256
CONV_W = 4
D_FF_DENSE = 2816
D_FF_EXPERT = 3584
N_EXPERTS = 8
EPS = 1e-5
DEPTH = 2
ALPHA = (2.0 * DEPTH) ** 0.25
EVEN_MAIN = 3584
ODD_MAIN = 3072

LANES = 128
SUBLANES = 8
VMEM_LIMIT = 56 * 1024 * 1024

TM = 384
CS = 128
NCHUNK = SEQ // CS
SEQ_PER_STEP = 1
SG = 16
TMM = 512
TFF = 896
MOE_NFF = D_FF_EXPERT // TFF
MOE_ROWS_PER_STEP = TMM // MOE_NFF
N_MOE_TILES = -(-(2 * NT + N_EXPERTS * (TMM - 1)) // TMM)
MOE_SLOTS = N_MOE_TILES * TMM
MOE_OUT_ROWS = MOE_SLOTS + TMM
N_LEVELS = int(math.log2(CS))

assert NT % TM == 0 and NP % CS == 0 and NS % SG == 0 and D_FF_EXPERT % TFF == 0 and TMM % MOE_NFF == 0


def _params(sem, limit=VMEM_LIMIT):
    return pltpu.CompilerParams(dimension_semantics=sem, vmem_limit_bytes=limit)


def _dot(a, b):
    return jnp.dot(a, b, preferred_element_type=F32)


def _dot_nt(a, b):
    return lax.dot_general(a, b, (((1,), (1,)), ((), ())), preferred_element_type=F32)


def _dot_tn(a, b):
    return lax.dot_general(a, b, (((0,), (0,)), ((), ())), preferred_element_type=F32)


def _split3(x):
    hi = x.astype(BF16)
    r1 = x - hi.astype(F32)
    mid = r1.astype(BF16)
    lo = (r1 - mid.astype(F32)).astype(BF16)
    return hi, mid, lo


def _dot_sel(sel, x):
    hi, mid, lo = _split3(x)
    return _dot(sel, hi) + _dot(sel, mid) + _dot(sel, lo)


def _sigmoid(x):
    return jax.nn.sigmoid(x)


def _silu(x):
    return x * jax.nn.sigmoid(x)


def _log_sigmoid(x):
    return jnp.minimum(x, 0.0) - jnp.log(1.0 + jnp.exp(-jnp.abs(x)))


def _layernorm(r, g, b):
    mu = jnp.mean(r, axis=-1, keepdims=True)
    c = r - mu
    var = jnp.mean(c * c, axis=-1, keepdims=True)
    return c * lax.rsqrt(var + EPS) * g + b


def _gla_matrices():
    sel = np.zeros(((2 + N_LEVELS) * CS, CS), np.float32)
    masks = np.zeros((N_LEVELS + 1, CS, CS), np.float32)
    for t in range(CS):
        sel[t, : t + 1] = 1.0
        sel[CS + t, t + 1:] = 1.0
        for l in range(N_LEVELS):
            half = 1 << l
            start = (t // (2 * half)) * (2 * half)
            mid = start + half
            row = (2 + l) * CS + t
            if t >= mid:
                sel[row, mid: t + 1] = 1.0
                masks[l, t, start:mid] = 1.0
            else:
                sel[row, t + 1: mid] = 1.0
        masks[N_LEVELS, t, t] = 1.0
    return sel, masks


def _tri(n, strict):
    return np.tril(np.ones((n, n), np.float32), -1 if strict else 0)


def _proj_kernel(x_ref, wa_ref, wb_ref, oa_ref, ob_ref):
    xb = x_ref[...].astype(BF16)
    oa_ref[...] = _dot(xb, wa_ref[...])
    ob_ref[...] = _dot(xb, wb_ref[...])


def _proj(x, wa, wb):
    na, nb = wa.shape[1], wb.shape[1]
    return pl.pallas_call(
        _proj_kernel,
        grid=(NT // TM,),
        in_specs=[
            pl.BlockSpec((TM, D), lambda i: (i, 0)),
            pl.BlockSpec((D, na), lambda i: (0, 0)),
            pl.BlockSpec((D, nb), lambda i: (0, 0)),
        ],
        out_specs=[
            pl.BlockSpec((TM, na), lambda i: (i, 0)),
            pl.BlockSpec((TM, nb), lambda i: (i, 0)),
        ],
        out_shape=[jax.ShapeDtypeStruct((NT, na), F32), jax.ShapeDtypeStruct((NT, nb), F32)],
        compiler_params=_params(("parallel",)),
        name="proj",
    )(x, wa, wb)


def _rms_gate(o, gate, w):
    o = o * lax.rsqrt(jnp.mean(o * o, axis=-1, keepdims=True) + EPS) * w
    return o * _silu(gate)


def _gla_chunk(q, k, v, g, st_ref, sel, masks_ref, heads, dk, dv):
    e = _dot_sel(sel, g)
    zf = jnp.exp(e)
    z_cum = zf[0:CS]
    z_end = zf[CS:2 * CS]
    st = st_ref[...]
    outs = []
    for h in range(heads):
        ks = slice(h * dk, (h + 1) * dk)
        vs = slice(h * dv, (h + 1) * dv)
        qh, kh = q[:, ks], k[:, ks]
        vh = v[:, vs].astype(BF16)
        scores = _dot_nt(qh.astype(BF16), kh.astype(BF16)) * masks_ref[N_LEVELS]
        for l in range(N_LEVELS):
            zl = zf[(2 + l) * CS:(3 + l) * CS, ks]
            scores = scores + _dot_nt((qh * zl).astype(BF16), (kh * zl).astype(BF16)) * masks_ref[l]
        o = _dot(scores.astype(BF16), vh)
        o = o + _dot_nt((qh * z_cum[:, ks]).astype(BF16), st[:, ks].astype(BF16))
        outs.append(o)
        upd = _dot_tn(vh, (kh * z_end[:, ks]).astype(BF16))
        st_ref[:, ks] = st[:, ks] * z_cum[CS - 1:CS, ks] + upd
    return outs


def _even_prompt_kernel(*refs):
    z_refs = refs[0:SEQ_PER_STEP]
    zgr_refs = refs[SEQ_PER_STEP:2 * SEQ_PER_STEP]
    (lbp_ref, wgk_ref, bgk_ref, gnh_ref, gng_ref, sel_ref, masks_ref,
     y_ref, shg_ref, sgla_ref, st_hg, st_gla) = refs[2 * SEQ_PER_STEP:]
    c = pl.program_id(1)

    @pl.when(c == 0)
    def _():
        st_hg[...] = jnp.zeros_like(st_hg)
        st_gla[...] = jnp.zeros_like(st_gla)

    sel = sel_ref[...]
    p = lbp_ref[...]
    pe = jnp.exp(p - jnp.max(p, axis=0, keepdims=True))
    lb = pe[0:1] / jnp.sum(pe, axis=0, keepdims=True)

    for s in range(SEQ_PER_STEP):
        z = z_refs[s][...]
        hq, hf, hi, hg = z[:, 0:512], z[:, 512:1024], z[:, 1024:1536], z[:, 1536:2048]
        gq, gk, gv, gg = z[:, 2048:2304], z[:, 2304:2560], z[:, 2560:3072], z[:, 3072:3584]
        f = lb + (1.0 - lb) * _sigmoid(hf)
        k_hg = (1.0 - lb) * _sigmoid(-hf)
        o_hg = _gla_chunk(_silu(hq), k_hg, hi, jnp.log(f), st_hg.at[s], sel, masks_ref, HG_H, HG_DK, HG_DV)

        la = _log_sigmoid(_dot(zgr_refs[s][...].astype(BF16), wgk_ref[...]) + bgk_ref[...]) / GLA_GATE_NORM
        o_gla = _gla_chunk(gq * GLA_DK ** -0.5, gk, gv, la, st_gla.at[s], sel, masks_ref, GLA_H, GLA_DK, GLA_DV)

        for h in range(HG_H):
            cs = slice(h * 128, (h + 1) * 128)
            y_ref[s, :, cs] = _rms_gate(o_hg[h], hg[:, cs], gnh_ref[...]).astype(BF16)
        for h in range(GLA_H):
            cs = slice(h * 128, (h + 1) * 128)
            y_ref[s, :, 512 + h * 128:512 + (h + 1) * 128] = _rms_gate(o_gla[h], gg[:, cs], gng_ref[...]).astype(BF16)

    @pl.when(c == NCHUNK - 1)
    def _():
        for s in range(SEQ_PER_STEP):
            shg_ref[s] = st_hg[s].T
            sgla_ref[s] = st_gla[s].T


def _seq_row_specs(width):
    return [pl.BlockSpec((CS, width), functools.partial(lambda b, c, s: ((SEQ_PER_STEP * b + s) * NCHUNK + c, 0), s=s))
            for s in range(SEQ_PER_STEP)]


def _even_prompt(z, zgr, lbp, wgk, bgk, gnh, gng, sel, masks):
    const2 = lambda b, c: (0, 0)
    sp = SEQ_PER_STEP
    return pl.pallas_call(
        _even_prompt_kernel,
        grid=(BATCH // sp, NCHUNK),
        in_specs=_seq_row_specs(EVEN_MAIN) + _seq_row_specs(LANES) + [
            pl.BlockSpec(lbp.shape, const2),
            pl.BlockSpec(wgk.shape, const2),
            pl.BlockSpec(bgk.shape, const2),
            pl.BlockSpec(gnh.shape, const2),
            pl.BlockSpec(gng.shape, const2),
            pl.BlockSpec(sel.shape, const2),
            pl.BlockSpec(masks.shape, lambda b, c: (0, 0, 0)),
        ],
        out_specs=[
            pl.BlockSpec((sp, CS, D), lambda b, c: (b, c, 0)),
            pl.BlockSpec((sp, HG_H * HG_DK, HG_DV), lambda b, c: (b, 0, 0)),
            pl.BlockSpec((sp, GLA_H * GLA_DK, GLA_DV), lambda b, c: (b, 0, 0)),
        ],
        out_shape=[
            jax.ShapeDtypeStruct((BATCH, SEQ, D), BF16),
            jax.ShapeDtypeStruct((BATCH, HG_H * HG_DK, HG_DV), F32),
            jax.ShapeDtypeStruct((BATCH, GLA_H * GLA_DK, GLA_DV), F32),
        ],
        scratch_shapes=[pltpu.VMEM((sp, HG_DV, HG_H * HG_DK), F32), pltpu.VMEM((sp, GLA_DV, GLA_H * GLA_DK), F32)],
        compiler_params=_params(("parallel", "arbitrary")),
        name="even_prompt",
    )(*([z] * sp), *([zgr] * sp), lbp, wgk, bgk, gnh, gng, sel, masks)


def _even_sample_kernel(zr_ref, zt_ref, grt_ref, lbpt_ref, wgkt_ref, bgkt_ref, gnh_ref, gng_ref,
                        shg_ref, sgla_ref, y_ref, shg_out, sgla_out, o_scr):
    zt = zt_ref[0]
    hq_t, hf_t = zt[0:512], zt[512:1024]
    gq_t, gk_t = zt[2048:2304], zt[2304:2560]
    pt = lbpt_ref[...]
    pe = jnp.exp(pt - jnp.max(pt, axis=1, keepdims=True))
    lb = pe[:, 0:1] / jnp.sum(pe, axis=1, keepdims=True)
    a_hg = jnp.exp(jnp.log(lb + (1.0 - lb) * _sigmoid(hf_t)))
    k_hg = (1.0 - lb) * _sigmoid(-hf_t)
    q_hg = _silu(hq_t)
    la = _log_sigmoid(_dot(wgkt_ref[...], grt_ref[0].astype(BF16)) + bgkt_ref[...]) / GLA_GATE_NORM
    a_gla = jnp.exp(la)
    q_gla = gq_t * GLA_DK ** -0.5
    zr = zr_ref[...]
    hi, hg = zr[:, 1024:1536], zr[:, 1536:2048]
    gv, gg = zr[:, 2560:3072], zr[:, 3072:3584]

    for j in range(SG):
        for h in range(HG_H):
            ks = slice(h * HG_DK, (h + 1) * HG_DK)
            s_new = a_hg[ks, j:j + 1] * shg_ref[j, h] + k_hg[ks, j:j + 1] * hi[j:j + 1, h * 128:(h + 1) * 128]
            shg_out[j, h] = s_new
            o_scr[j:j + 1, h * 128:(h + 1) * 128] = jnp.sum(q_hg[ks, j:j + 1] * s_new, axis=0, keepdims=True)
        for h in range(GLA_H):
            ks = slice(h * GLA_DK, (h + 1) * GLA_DK)
            s_new = a_gla[ks, j:j + 1] * sgla_ref[j, h] + gk_t[ks, j:j + 1] * gv[j:j + 1, h * 128:(h + 1) * 128]
            sgla_out[j, h] = s_new
            o_scr[j:j + 1, 512 + h * 128:512 + (h + 1) * 128] = jnp.sum(
                q_gla[ks, j:j + 1] * s_new, axis=0, keepdims=True)

    o = o_scr[...]
    for h in range(HG_H):
        cs = slice(h * 128, (h + 1) * 128)
        y_ref[:, cs] = _rms_gate(o[:, cs], hg[:, cs], gnh_ref[...]).astype(BF16)
    for h in range(GLA_H):
        cs = slice(512 + h * 128, 512 + (h + 1) * 128)
        y_ref[:, cs] = _rms_gate(o[:, cs], gg[:, h * 128:(h + 1) * 128], gng_ref[...]).astype(BF16)


def _even_sample(z, zt3, grt3, lbpt, wgkt, bgkt, gnh, gng, s_hg, s_gla):
    c2 = lambda g: (0, 0)
    return pl.pallas_call(
        _even_sample_kernel,
        grid=(NS // SG,),
        in_specs=[
            pl.BlockSpec((SG, EVEN_MAIN), lambda g: (NP // SG + g, 0)),
            pl.BlockSpec((1, EVEN_MAIN, SG), lambda g: (g, 0, 0)),
            pl.BlockSpec((1, LANES, SG), lambda g: (g, 0, 0)),
            pl.BlockSpec(lbpt.shape, c2),
            pl.BlockSpec(wgkt.shape, c2),
            pl.BlockSpec(bgkt.shape, c2),
            pl.BlockSpec(gnh.shape, c2),
            pl.BlockSpec(gng.shape, c2),
            pl.BlockSpec((SG, HG_H, HG_DK, HG_DV), lambda g: (g, 0, 0, 0)),
            pl.BlockSpec((SG, GLA_H, GLA_DK, GLA_DV), lambda g: (g, 0, 0, 0)),
        ],
        out_specs=[
            pl.BlockSpec((SG, D), lambda g: (g, 0)),
            pl.BlockSpec((SG, HG_H, HG_DK, HG_DV), lambda g: (g, 0, 0, 0)),
            pl.BlockSpec((SG, GLA_H, GLA_DK, GLA_DV), lambda g: (g, 0, 0, 0)),
        ],
        out_shape=[
            jax.ShapeDtypeStruct((NS, D), BF16),
            jax.ShapeDtypeStruct((NS, HG_H, HG_DK, HG_DV), F32),
            jax.ShapeDtypeStruct((NS, GLA_H, GLA_DK, GLA_DV), F32),
        ],
        scratch_shapes=[pltpu.VMEM((SG, D), F32)],
        compiler_params=_params(("parallel",)),
        name="even_sample",
    )(z, zt3, grt3, lbpt, wgkt, bgkt, gnh, gng, s_hg, s_gla)


def _out_ln_kernel(x_ref, y_ref, w_ref, g_ref, b_ref, o_ref):
    r = ALPHA * x_ref[...] + _dot(y_ref[...], w_ref[...])
    o_ref[...] = _layernorm(r, g_ref[...], b_ref[...])


def _out_ln(x, y, w, g, b):
    c2 = lambda i: (0, 0)
    return pl.pallas_call(
        _out_ln_kernel,
        grid=(NT // TM,),
        in_specs=[
            pl.BlockSpec((TM, D), lambda i: (i, 0)),
            pl.BlockSpec((TM, D), lambda i: (i, 0)),
            pl.BlockSpec((D, D), c2),
            pl.BlockSpec((1, D), c2),
            pl.BlockSpec((1, D), c2),
        ],
        out_specs=pl.BlockSpec((TM, D), lambda i: (i, 0)),
        out_shape=jax.ShapeDtypeStruct((NT, D), F32),
        compiler_params=_params(("parallel",)),
        name="out_ln",
    )(x, y, w, g, b)


FF_SPLIT = 2


def _ffn_kernel(x_ref, w1_ref, w3_ref, w2_ref, g_ref, b_ref, o_ref):
    x = x_ref[...]
    xb = x.astype(BF16)
    step = D_FF_DENSE // FF_SPLIT
    acc = ALPHA * x
    for s in range(FF_SPLIT):
        cs = slice(s * step, (s + 1) * step)
        hmid = _silu(_dot(xb, w1_ref[:, cs])) * _dot(xb, w3_ref[:, cs])
        acc = acc + _dot(hmid.astype(BF16), w2_ref[cs, :])
    o_ref[...] = _layernorm(acc, g_ref[...], b_ref[...])


def _ffn(x, w1, w3, w2, g, b):
    c2 = lambda i: (0, 0)
    one = pl.Buffered(1)
    return pl.pallas_call(
        _ffn_kernel,
        grid=(NT // TM,),
        in_specs=[
            pl.BlockSpec((TM, D), lambda i: (i, 0)),
            pl.BlockSpec((D, D_FF_DENSE), c2, pipeline_mode=one),
            pl.BlockSpec((D, D_FF_DENSE), c2, pipeline_mode=one),
            pl.BlockSpec((D_FF_DENSE, D), c2, pipeline_mode=one),
            pl.BlockSpec((1, D), c2),
            pl.BlockSpec((1, D), c2),
        ],
        out_specs=pl.BlockSpec((TM, D), lambda i: (i, 0)),
        out_shape=jax.ShapeDtypeStruct((NT, D), F32),
        compiler_params=_params(("parallel",)),
        name="ffn_dense",
    )(x, w1, w3, w2, g, b)


def _mh_norm_gate(hh, o_pre, w):
    mu = jnp.mean(hh, axis=-1, keepdims=True)
    c = hh - mu
    var = jnp.mean(c * c, axis=-1, keepdims=True)
    return _sigmoid(o_pre) * (c * lax.rsqrt(var + EPS) * w)


def _odd_prompt_kernel(*refs):
    z_refs = refs[0:SEQ_PER_STEP]
    zg_refs = refs[SEQ_PER_STEP:2 * SEQ_PER_STEP]
    (bg_ref, cw_ref, cb_ref, hnw_ref, tri_ref,
     y_ref, c_out, n_out, m_out, conv_out,
     c_scr, n_scr, m_scr, u_scr) = refs[2 * SEQ_PER_STEP:]
    c = pl.program_id(1)

    @pl.when(c == 0)
    def _():
        c_scr[...] = jnp.zeros_like(c_scr)
        n_scr[...] = jnp.zeros_like(n_scr)
        m_scr[...] = jnp.zeros_like(m_scr)
        for s in range(SEQ_PER_STEP):
            u_scr[s, 0:8, :] = jnp.zeros((8, D), F32)

    row = lax.broadcasted_iota(jnp.int32, (CS, CS), 0)
    col = lax.broadcasted_iota(jnp.int32, (CS, CS), 1)
    causal = col <= row
    tails = []

    for s in range(SEQ_PER_STEP):
        z_ref = z_refs[s]
        u_scr[s, 8:8 + CS, :] = z_ref[:, 0:D]
        uc = cb_ref[...]
        for j in range(CONV_W):
            uc = uc + u_scr[s, 5 + j:5 + j + CS, :] * cw_ref[j:j + 1, :]
        tail = u_scr[s, CS:CS + 8, :]
        u_scr[s, 0:8, :] = tail
        tails.append(tail)
        act = _silu(uc)
        q = act[:, 0:512] * ML_DK ** -0.5
        k = act[:, 512:1024]
        v = z_ref[:, D:2 * D]
        o_pre = z_ref[:, 2 * D:3 * D]

        gates = zg_refs[s][...] + bg_ref[...]
        lf = _log_sigmoid(gates)
        bcum = _dot_sel(tri_ref[...], lf)
        bcum_t = bcum.T
        gates_t = gates.T
        m_all = m_scr[s]

        for h in range(ML_H):
            ks = slice(h * ML_DK, (h + 1) * ML_DK)
            vs = slice(h * ML_DV, (h + 1) * ML_DV)
            qh, kh = q[:, ks], k[:, ks]
            vh = v[:, vs].astype(BF16)
            b_col = bcum[:, 4 + h:5 + h]
            b_row = bcum_t[4 + h:5 + h, :]
            i_col = gates[:, h:h + 1]
            i_row = gates_t[h:h + 1, :]
            m_prev = m_all[:, h:h + 1]
            log_d = jnp.where(causal, b_col - b_row + i_row, -jnp.inf)
            log_prev = b_col + m_prev
            m_t = jnp.maximum(jnp.max(log_d, axis=-1, keepdims=True), log_prev)
            d = jnp.exp(log_d - m_t)
            w_prev = jnp.exp(log_prev - m_t)
            scores = _dot_nt(qh.astype(BF16), kh.astype(BF16)) * d
            c_h = c_scr[s, h]
            n_h = n_scr[s, h:h + 1, :]
            num = _dot(scores.astype(BF16), vh) + w_prev * _dot(qh.astype(BF16), c_h.astype(BF16))
            den = jnp.sum(scores, axis=-1, keepdims=True) + w_prev * jnp.sum(qh * n_h, axis=-1, keepdims=True)
            hh = num / jnp.maximum(jnp.abs(den), jnp.exp(-m_t))
            m_new = m_t[CS - 1:CS, :]
            b_last = b_col[CS - 1:CS, :]
            w_c = jnp.exp(b_last + m_prev - m_new)
            w_s = jnp.exp(b_last - b_col + i_col - m_new)
            kw = kh * w_s
            c_scr[s, h] = w_c * c_h + _dot_tn(kw.astype(BF16), vh)
            n_scr[s, h:h + 1, :] = w_c * n_h + jnp.sum(kw, axis=0, keepdims=True)
            m_scr[s, :, h:h + 1] = m_new
            y_ref[s, :, vs] = _mh_norm_gate(hh, o_pre[:, vs], hnw_ref[:, vs]).astype(BF16)

    @pl.when(c == NCHUNK - 1)
    def _():
        c_out[...] = c_scr[...]
        for s in range(SEQ_PER_STEP):
            n_out[s] = n_scr[s, 0:ML_H, :]
            m_out[s] = m_scr[s]
            conv_out[s] = tails[s][8 - (CONV_W - 1):8, :]


def _odd_prompt(z, zg, bg, cw, cb, hnw, tri):
    c2 = lambda b, c: (0, 0)
    sp = SEQ_PER_STEP
    return pl.pallas_call(
        _odd_prompt_kernel,
        grid=(BATCH // sp, NCHUNK),
        in_specs=_seq_row_specs(ODD_MAIN) + _seq_row_specs(LANES) + [
            pl.BlockSpec((1, LANES), c2),
            pl.BlockSpec((CONV_W, D), c2),
            pl.BlockSpec((1, D), c2),
            pl.BlockSpec((1, D), c2),
            pl.BlockSpec((CS, CS), c2),
        ],
        out_specs=[
            pl.BlockSpec((sp, CS, D), lambda b, c: (b, c, 0)),
            pl.BlockSpec((sp, ML_H, ML_DK, ML_DV), lambda b, c: (b, 0, 0, 0)),
            pl.BlockSpec((sp, ML_H, ML_DK), lambda b, c: (b, 0, 0)),
            pl.BlockSpec((sp, 1, LANES), lambda b, c: (b, 0, 0)),
            pl.BlockSpec((sp, CONV_W - 1, D), lambda b, c: (b, 0, 0)),
        ],
        out_shape=[
            jax.ShapeDtypeStruct((BATCH, SEQ, D), BF16),
            jax.ShapeDtypeStruct((BATCH, ML_H, ML_DK, ML_DV), F32),
            jax.ShapeDtypeStruct((BATCH, ML_H, ML_DK), F32),
            jax.ShapeDtypeStruct((BATCH, 1, LANES), F32),
            jax.ShapeDtypeStruct((BATCH, CONV_W - 1, D), F32),
        ],
        scratch_shapes=[
            pltpu.VMEM((sp, ML_H, ML_DK, ML_DV), F32),
            pltpu.VMEM((sp, 8, ML_DK), F32),
            pltpu.VMEM((sp, 1, LANES), F32),
            pltpu.VMEM((sp, CS + 8, D), F32),
        ],
        compiler_params=_params(("parallel", "arbitrary")),
        name="odd_prompt",
    )(*([z] * sp), *([zg] * sp), bg, cw, cb, hnw, tri)


def _odd_sample_kernel(zr_ref, zg_ref, ut_ref, conv_ref, convt_ref, bg_ref, cw_ref, cwt_ref, cb_ref, cbt_ref,
                       hnw_ref, c_ref, n_ref, m_ref,
                       y_ref, c_out, n_out, m_out, conv_out, h_scr):
    zr = zr_ref[...]
    u = zr[:, 0:D]
    v = zr[:, D:2 * D]
    o_pre = zr[:, 2 * D:3 * D]
    uc = cb_ref[...] + u * cw_ref[CONV_W - 1:CONV_W, :]
    uc_t = cbt_ref[...] + ut_ref[0] * cwt_ref[:, CONV_W - 1:CONV_W]
    for j in range(CONV_W - 1):
        uc = uc + conv_ref[:, j * D:(j + 1) * D] * cw_ref[j:j + 1, :]
        uc_t = uc_t + convt_ref[0, j] * cwt_ref[:, j:j + 1]
        conv_out[:, j * D:(j + 1) * D] = conv_ref[:, (j + 1) * D:(j + 2) * D] if j + 1 < CONV_W - 1 else u
    act = _silu(uc)
    k_row = act[:, 512:1024]
    act_t = _silu(uc_t)
    q_t = act_t[0:512] * ML_DK ** -0.5
    k_t = act_t[512:1024]
    q_row = act[:, 0:512] * ML_DK ** -0.5

    gates = zg_ref[...] + bg_ref[...]
    lf = _log_sigmoid(gates)
    m_in = m_ref[...]
    m_out[...] = m_in

    for j in range(SG):
        for h in range(ML_H):
            ks = slice(h * ML_DK, (h + 1) * ML_DK)
            vs = slice(h * ML_DV, (h + 1) * ML_DV)
            ig = gates[j:j + 1, h:h + 1]
            log_prev = lf[j:j + 1, 4 + h:5 + h] + m_in[j:j + 1, h:h + 1]
            m_t = jnp.maximum(ig, log_prev)
            d = jnp.exp(ig - m_t)
            w_prev = jnp.exp(log_prev - m_t)
            c_new = w_prev * c_ref[j, h] + (d * k_t[ks, j:j + 1]) * v[j:j + 1, vs]
            n_new = w_prev * n_ref[j, h:h + 1, :] + d * k_row[j:j + 1, ks]
            c_out[j, h] = c_new
            n_out[j, h:h + 1, :] = n_new
            m_out[j:j + 1, h:h + 1] = m_t
            num = jnp.sum(q_t[ks, j:j + 1] * c_new, axis=0, keepdims=True)
            den = jnp.sum(q_row[j:j + 1, ks] * n_new, axis=-1, keepdims=True)
            h_scr[j:j + 1, vs] = num / jnp.maximum(jnp.abs(den), jnp.exp(-m_t))

    hh = h_scr[...]
    for h in range(ML_H):
        vs = slice(h * ML_DV, (h + 1) * ML_DV)
        y_ref[:, vs] = _mh_norm_gate(hh[:, vs], o_pre[:, vs], hnw_ref[:, vs]).astype(BF16)


def _odd_sample(z, zg, ut3, conv, convt, bg, cw, cwt, cb, cbt, hnw, c_in, n_in, m_in):
    c2 = lambda g: (0, 0)
    return pl.pallas_call(
        _odd_sample_kernel,
        grid=(NS // SG,),
        in_specs=[
            pl.BlockSpec((SG, ODD_MAIN), lambda g: (NP // SG + g, 0)),
            pl.BlockSpec((SG, LANES), lambda g: (NP // SG + g, 0)),
            pl.BlockSpec((1, D, SG), lambda g: (g, 0, 0)),
            pl.BlockSpec((SG, (CONV_W - 1) * D), lambda g: (g, 0)),
            pl.BlockSpec((1, CONV_W - 1, D, SG), lambda g: (g, 0, 0, 0)),
            pl.BlockSpec((1, LANES), c2),
            pl.BlockSpec((CONV_W, D), c2),
            pl.BlockSpec((D, CONV_W), c2),
            pl.BlockSpec((1, D), c2),
            pl.BlockSpec((D, 1), c2),
            pl.BlockSpec((1, D), c2),
            pl.BlockSpec((SG, ML_H, ML_DK, ML_DV), lambda g: (g, 0, 0, 0)),
            pl.BlockSpec((SG, ML_H, ML_DK), lambda g: (g, 0, 0)),
            pl.BlockSpec((SG, LANES), lambda g: (g, 0)),
        ],
        out_specs=[
            pl.BlockSpec((SG, D), lambda g: (g, 0)),
            pl.BlockSpec((SG, ML_H, ML_DK, ML_DV), lambda g: (g, 0, 0, 0)),
            pl.BlockSpec((SG, ML_H, ML_DK), lambda g: (g, 0, 0)),
            pl.BlockSpec((SG, LANES), lambda g: (g, 0)),
            pl.BlockSpec((SG, (CONV_W - 1) * D), lambda g: (g, 0)),
        ],
        out_shape=[
            jax.ShapeDtypeStruct((NS, D), BF16),
            jax.ShapeDtypeStruct((NS, ML_H, ML_DK, ML_DV), F32),
            jax.ShapeDtypeStruct((NS, ML_H, ML_DK), F32),
            jax.ShapeDtypeStruct((NS, LANES), F32),
            jax.ShapeDtypeStruct((NS, (CONV_W - 1) * D), F32),
        ],
        scratch_shapes=[pltpu.VMEM((SG, D), F32)],
        compiler_params=_params(("parallel",)),
        name="odd_sample",
    )(z, zg, ut3, conv, convt, bg, cw, cwt, cb, cbt, hnw, c_in, n_in, m_in)


def _out_ln_router_kernel(x_ref, y_ref, w_ref, g_ref, b_ref, wr_ref, tri_ref,
                          o_ref, meta_ref, cnt_ref, carry):
    i = pl.program_id(0)

    @pl.when(i == 0)
    def _():
        carry[...] = jnp.zeros_like(carry)

    r = ALPHA * x_ref[...] + _dot(y_ref[...], w_ref[...])
    x3 = _layernorm(r, g_ref[...], b_ref[...])
    o_ref[...] = x3.reshape(TM, SUBLANES, LANES)

    lane = lax.broadcasted_iota(jnp.int32, (TM, LANES), 1).astype(F32)
    logits = jnp.where(lane < N_EXPERTS, _dot(x3.astype(BF16), wr_ref[...]), -jnp.inf)
    m1 = jnp.max(logits, axis=-1, keepdims=True)
    i1 = jnp.min(jnp.where(logits == m1, lane, float(LANES)), axis=-1, keepdims=True)
    rest = jnp.where(lane == i1, -jnp.inf, logits)
    m2 = jnp.max(rest, axis=-1, keepdims=True)
    i2 = jnp.min(jnp.where(rest == m2, lane, float(LANES)), axis=-1, keepdims=True)
    e2 = jnp.exp(m2 - m1)
    tot = 1.0 + e2
    w1 = 1.0 / tot
    w2 = e2 / tot

    sel1 = lane == i1
    sel2 = lane == i2
    onehot = jnp.where(sel1 | sel2, 1.0, 0.0)
    before = _dot(tri_ref[...], onehot.astype(BF16)) + carry[...]
    r1 = jnp.sum(jnp.where(sel1, before, 0.0), axis=-1, keepdims=True)
    r2 = jnp.sum(jnp.where(sel2, before, 0.0), axis=-1, keepdims=True)
    carry[...] = carry[...] + jnp.sum(onehot, axis=0, keepdims=True)
    cnt_ref[...] = carry[...]

    meta = jnp.where(lane == 0.0, i1, 0.0)
    meta = jnp.where(lane == 1.0, i2, meta)
    meta = jnp.where(lane == 2.0, w1, meta)
    meta = jnp.where(lane == 3.0, w2, meta)
    meta = jnp.where(lane == 4.0, r1, meta)
    meta = jnp.where(lane == 5.0, r2, meta)
    meta_ref[...] = meta


def _out_ln_router(x, y, w, g, b, wr, tri):
    c2 = lambda i: (0, 0)
    return pl.pallas_call(
        _out_ln_router_kernel,
        grid=(NT // TM,),
        in_specs=[
            pl.BlockSpec((TM, D), lambda i: (i, 0)),
            pl.BlockSpec((TM, D), lambda i: (i, 0)),
            pl.BlockSpec((D, D), c2),
            pl.BlockSpec((1, D), c2),
            pl.BlockSpec((1, D), c2),
            pl.BlockSpec((D, LANES), c2),
            pl.BlockSpec((TM, TM), c2),
        ],
        out_specs=[
            pl.BlockSpec((TM, SUBLANES, LANES), lambda i: (i, 0, 0)),
            pl.BlockSpec((TM, LANES), lambda i: (i, 0)),
            pl.BlockSpec((1, LANES), c2),
        ],
        out_shape=[
            jax.ShapeDtypeStruct((NT, SUBLANES, LANES), F32),
            jax.ShapeDtypeStruct((NT, LANES), F32),
            jax.ShapeDtypeStruct((1, LANES), F32),
        ],
        scratch_shapes=[pltpu.VMEM((1, LANES), F32)],
        compiler_params=_params(("arbitrary",)),
        name="out_ln_router",
    )(x, y, w, g, b, wr, tri)


def _moe_ffn_kernel(te_ref, nu_ref, gnext_ref, gcur_ref, sprev_ref, scur_ref, x_hbm, w1_ref, w3_ref, w2_ref,
                    out_hbm, xg, yacc, xb_scr, sem_g, sem_s):
    i = pl.program_id(0)
    j = pl.program_id(1)
    used = i < nu_ref[0]
    slot = i % 2
    other = 1 - slot

    def gather(tok, buf, r):
        return pltpu.make_async_copy(x_hbm.at[tok], xg.at[buf, r], sem_g)

    def scatter(buf, r, dst):
        return pltpu.make_async_copy(yacc.at[buf, pl.ds(r, 1)], out_hbm.at[pl.ds(dst, 1)], sem_s)

    def wait_rows(kind, n):
        for _ in range(n):
            (gather(0, 0, 0) if kind == "g" else scatter(0, 0, 0)).wait()

    def issue_neighbours():
        for r in range(MOE_ROWS_PER_STEP):
            rr = j * MOE_ROWS_PER_STEP + r
            gather(gnext_ref[rr], other, rr).start(priority=r % 2)
            scatter(other, rr, sprev_ref[rr]).start(priority=r % 2)

    @pl.when(j == 0)
    def _():
        @pl.when(i == 0)
        def _():
            yacc[1] = jnp.zeros((TMM, D), F32)
            for r in range(TMM):
                gather(gcur_ref[r], 0, r).start()

        wait_rows("g", TMM)

        @pl.when(i > 0)
        def _():
            wait_rows("s", TMM)

        xb_scr[...] = xg[slot].reshape(TMM, D).astype(BF16)
        yacc[slot] = jnp.zeros((TMM, D), F32)

    @pl.when(used)
    def _():
        issue_neighbours()
        xb = xb_scr[...]
        hmid = _silu(_dot(xb, w1_ref[...])) * _dot(xb, w3_ref[...])
        yacc[slot] += _dot(hmid.astype(BF16), w2_ref[...])

    @pl.when(jnp.logical_not(used))
    def _():
        issue_neighbours()

    @pl.when((i == N_MOE_TILES - 1) & (j == MOE_NFF - 1))
    def _():
        for r in range(TMM):
            scatter(slot, r, scur_ref[r]).start()
        wait_rows("s", 2 * TMM)
        wait_rows("g", TMM)


def _moe_ffn(tile_expert, n_used, gsrc, sdst, x, w1, w3, w2):
    nff = MOE_NFF

    def wcol(i, j, te, nu):
        return (te[i], jnp.where(i < nu[0], j, nff - 1), 0, 0)

    def wrow(i, j, te, nu):
        return (te[i], jnp.where(i < nu[0], j, nff - 1), 0)

    smem = functools.partial(pl.BlockSpec, (TMM,), memory_space=pltpu.SMEM)
    grid_spec = pltpu.PrefetchScalarGridSpec(
        num_scalar_prefetch=2,
        grid=(N_MOE_TILES, nff),
        in_specs=[
            smem(lambda i, j, te, nu: (i + 1,)),
            smem(lambda i, j, te, nu: (i,)),
            smem(lambda i, j, te, nu: (i,)),
            smem(lambda i, j, te, nu: (i + 1,)),
            pl.BlockSpec(memory_space=pl.ANY),
            pl.BlockSpec((None, None, D, TFF), wcol),
            pl.BlockSpec((None, None, D, TFF), wcol),
            pl.BlockSpec((None, TFF, D), wrow),
        ],
        out_specs=pl.BlockSpec(memory_space=pl.ANY),
        scratch_shapes=[
            pltpu.VMEM((2, TMM, SUBLANES, LANES), F32),
            pltpu.VMEM((2, TMM, D), F32),
            pltpu.VMEM((TMM, D), BF16),
            pltpu.SemaphoreType.DMA(()),
            pltpu.SemaphoreType.DMA(()),
        ],
    )
    return pl.pallas_call(
        _moe_ffn_kernel,
        grid_spec=grid_spec,
        out_shape=jax.ShapeDtypeStruct((MOE_OUT_ROWS, D), F32),
        compiler_params=_params(("arbitrary", "arbitrary")),
        name="moe_ffn",
    )(tile_expert, n_used, gsrc, gsrc, sdst, sdst, x, w1, w3, w2)


def _combine_kernel(x_ref, meta_ref, y0_ref, y1_ref, g_ref, b_ref, o_ref):
    meta = meta_ref[...]
    moe = meta[:, 2:3] * y0_ref[...] + meta[:, 3:4] * y1_ref[...]
    o_ref[...] = _layernorm(ALPHA * x_ref[...].reshape(TM, D) + moe, g_ref[...], b_ref[...])


def _combine(x, meta, ys, g, b):
    c2 = lambda i: (0, 0)
    return pl.pallas_call(
        _combine_kernel,
        grid=(NT // TM,),
        in_specs=[
            pl.BlockSpec((TM, SUBLANES, LANES), lambda i: (i, 0, 0)),
            pl.BlockSpec((TM, LANES), lambda i: (i, 0)),
            pl.BlockSpec((TM, D), lambda i: (i, 0)),
            pl.BlockSpec((TM, D), lambda i: (i + NT // TM, 0)),
            pl.BlockSpec((1, D), c2),
            pl.BlockSpec((1, D), c2),
        ],
        out_specs=pl.BlockSpec((TM, D), lambda i: (i, 0)),
        out_shape=jax.ShapeDtypeStruct((NT, D), F32),
        compiler_params=_params(("parallel",)),
        name="moe_combine",
    )(x, meta, ys, ys, g, b)


def _pad_cols(w, n):
    return jnp.pad(w, ((0, 0), (0, n - w.shape[1])))


def kernel(x_prompt, x_sample, state_hgrn, state_gla, state_mlstm_C, state_mlstm_n, state_mlstm_m,
           state_mlstm_conv, w_in_even, hg_lower_bounds, w_gk, b_gk, gn_hg, gn_gla, w_out_even,
           w1_dense, w3_dense, w2_dense, w_in_odd, b_gate_odd, conv_w, conv_b, hn_w, w_out_odd,
           w_router, w1_moe, w3_moe, w2_moe, ln1_g, ln1_b, ln2_g, ln2_b):
    assert x_prompt.shape == (BATCH, SEQ, D) and x_sample.shape == (NS, 1, D)
    assert w_in_even.shape[0] == 1 and w_in_odd.shape[0] == 1 and hg_lower_bounds.shape[0] == 2
    sel_np, masks_np = _gla_matrices()
    sel = jnp.asarray(sel_np, BF16)
    masks = jnp.asarray(masks_np, F32)
    tri_cs = jnp.asarray(_tri(CS, False), BF16)
    tri_tm = jnp.asarray(_tri(TM, True), BF16)
    row = lambda a: a.reshape(1, -1)

    x0 = jnp.concatenate([x_prompt.reshape(NP, D), x_sample.reshape(NS, D)], axis=0)

    w_even = w_in_even[0].astype(BF16)
    z, zgr = _proj(x0, w_even[:, :EVEN_MAIN], _pad_cols(w_even[:, EVEN_MAIN:], LANES))
    wgk = jnp.pad(w_gk[0].astype(BF16), ((0, LANES - GLA_RANK), (0, 0)))
    lbp = hg_lower_bounds
    y_p, hg_p, gla_p = _even_prompt(z, zgr, lbp, wgk, row(b_gk[0]), row(gn_hg[0]), row(gn_gla[0]), sel, masks)

    zs = z[NP:].reshape(NS // SG, SG, EVEN_MAIN).transpose(0, 2, 1)
    grs = zgr[NP:].reshape(NS // SG, SG, LANES).transpose(0, 2, 1)
    y_s, hg_s, gla_s = _even_sample(z, zs, grs, lbp.T, wgk.T, b_gk[0].reshape(-1, 1),
                                    row(gn_hg[0]), row(gn_gla[0]), state_hgrn[0], state_gla[0])
    y = jnp.concatenate([y_p.reshape(NP, D), y_s], axis=0)
    x1 = _out_ln(x0, y, w_out_even[0].astype(BF16), row(ln1_g[0]), row(ln1_b[0]))
    x2 = _ffn(x1, w1_dense[0].astype(BF16), w3_dense[0].astype(BF16), w2_dense[0].astype(BF16),
              row(ln2_g[0]), row(ln2_b[0]))

    w_odd = w_in_odd[0].astype(BF16)
    zo, zog = _proj(x2, w_odd[:, :ODD_MAIN], _pad_cols(w_odd[:, ODD_MAIN:], LANES))
    bg = jnp.pad(b_gate_odd[0], (0, LANES - 2 * ML_H)).reshape(1, LANES)
    yo_p, c_p, n_p, m_p, conv_p = _odd_prompt(zo, zog, bg, conv_w[0], row(conv_b[0]), row(hn_w[0]), tri_cs)

    ut = zo[NP:, :D].reshape(NS // SG, SG, D).transpose(0, 2, 1)
    conv_in = state_mlstm_conv[0]
    conv_t = conv_in.reshape(NS // SG, SG, CONV_W - 1, D).transpose(0, 2, 3, 1)
    m_in = jnp.pad(state_mlstm_m[0], ((0, 0), (0, LANES - ML_H)))
    yo_s, c_s, n_s, m_s, conv_s = _odd_sample(
        zo, zog, ut, conv_in.reshape(NS, (CONV_W - 1) * D), conv_t, bg, conv_w[0], conv_w[0].T, row(conv_b[0]), conv_b[0].reshape(-1, 1),
        row(hn_w[0]), state_mlstm_C[0], state_mlstm_n[0], m_in)
    yo = jnp.concatenate([yo_p.reshape(NP, D), yo_s], axis=0)

    wr = _pad_cols(w_router[0].astype(BF16), LANES)
    x3, meta, cnt = _out_ln_router(x2, yo, w_out_odd[0].astype(BF16), row(ln1_g[1]), row(ln1_b[1]), wr, tri_tm)

    counts = cnt[0, :N_EXPERTS].astype(jnp.int32)
    padded = ((counts + TMM - 1) // TMM) * TMM
    ends = jnp.cumsum(padded)
    offsets = ends - padded
    idx = meta[:, 0:2].astype(jnp.int32)
    pos = offsets[idx] + meta[:, 4:6].astype(jnp.int32)
    tile_start = jnp.arange(N_MOE_TILES, dtype=jnp.int32) * TMM
    tile_expert = jnp.minimum(jnp.sum(tile_start[:, None] >= ends[None, :], axis=1), N_EXPERTS - 1).astype(jnp.int32)
    n_used = (ends[-1] // TMM).astype(jnp.int32).reshape(1)

    slot = jnp.arange(MOE_SLOTS, dtype=jnp.int32)
    e_slot = jnp.sum(slot[:, None] >= ends[None, :], axis=1)
    off_x = jnp.concatenate([offsets, ends[-1:]])
    cnt_x = jnp.concatenate([counts, jnp.zeros((1,), jnp.int32)])
    real_before_x = jnp.concatenate([jnp.cumsum(counts) - counts, jnp.full((1,), 2 * NT, jnp.int32)])
    local = slot - off_x[e_slot]
    real_before = real_before_x[e_slot] + jnp.minimum(local, cnt_x[e_slot])
    spill_row = 2 * NT + TMM + (slot - real_before)
    token = jnp.arange(NT, dtype=jnp.int32)
    dst_rows = jnp.stack([token, NT + token], axis=1)
    dst = spill_row.at[pos.reshape(-1)].set(dst_rows.reshape(-1), unique_indices=True)
    sdst = jnp.concatenate([2 * NT + jnp.arange(TMM, dtype=jnp.int32), dst])
    src_tok = jnp.where(dst < NT, dst, jnp.where(dst < 2 * NT, dst - NT, 0))
    gsrc = jnp.concatenate([src_tok, jnp.zeros((TMM,), jnp.int32)])

    ff_tiled = lambda w: w.astype(BF16).reshape(N_EXPERTS, D, MOE_NFF, TFF).transpose(0, 2, 1, 3)
    ys = _moe_ffn(tile_expert, n_used, gsrc, sdst, x3,
                  ff_tiled(w1_moe[0]), ff_tiled(w3_moe[0]), w2_moe[0].astype(BF16))
    out = _combine(x3, meta, ys, row(ln2_g[1]), row(ln2_b[1]))

    y_prompt = out[:NP].reshape(BATCH, SEQ, D)
    y_sample = out[NP:].reshape(NS, 1, D)
    return (y_prompt, y_sample,
            hg_p.reshape(1, BATCH, HG_H, HG_DK, HG_DV), gla_p.reshape(1, BATCH, GLA_H, GLA_DK, GLA_DV),
            c_p[None], n_p[None], m_p[:, 0, :ML_H][None], conv_p[None],
            hg_s[None], gla_s[None], c_s[None], n_s[None], m_s[:, :ML_H][None], conv_s.reshape(1, NS, CONV_W - 1, D))
```

```python
import functools
import math

import jax
import jax.numpy as jnp
import numpy as np
from jax import lax
from jax.experimental import pallas as pl
from jax.experimental.pallas import tpu as pltpu

F32 = jnp.float32
BF16 = jnp.bfloat16
U32 = jnp.uint32

D = 1024
BATCH = 8
SEQ = 2048
DEC_BATCH = 128
NP = BATCH * SEQ
NS = DEC_BATCH
NT = NP + NS
HG_H, HG_DK, HG_DV = 4, 128, 128
GLA_H, GLA_DK, GLA_DV = 4, 64, 128
GLA_RANK = 16
GLA_GATE_NORM = 16.0
ML_H, ML_DK, ML_DV = 4, 128, 256
CONV_W = 4
D_FF_DENSE = 2816
D_FF_EXPERT = 3584
N_EXPERTS = 8
EPS = 1e-5
DEPTH = 2
ALPHA = (2.0 * DEPTH) ** 0.25
EVEN_MAIN = 3584
ODD_MAIN = 3072

LANES = 128
SUBLANES = 8
VMEM_LIMIT = 56 * 1024 * 1024

TM = 384
CS = 128
NCHUNK = SEQ // CS
SEQ_PER_STEP = 1
SG = 16
TMM = 448
TFF = 512
MOE_TAB = 512
MOE_NFF = D_FF_EXPERT // TFF
MOE_ROWS_PER_STEP = TMM // MOE_NFF
N_MOE_TILES = -(-(2 * NT + N_EXPERTS * (TMM - 1)) // TMM)
MOE_SLOTS = N_MOE_TILES * TMM
MOE_OUT_ROWS = MOE_SLOTS + TMM
N_LEVELS = int(math.log2(CS))

assert NT % TM == 0 and NP % CS == 0 and NS % SG == 0 and D_FF_EXPERT % TFF == 0 and TMM % MOE_NFF == 0


def _params(sem, limit=VMEM_LIMIT):
    return pltpu.CompilerParams(dimension_semantics=sem, vmem_limit_bytes=limit)


def _dot(a, b):
    return jnp.dot(a, b, preferred_element_type=F32)


def _dot_nt(a, b):
    return lax.dot_general(a, b, (((1,), (1,)), ((), ())), preferred_element_type=F32)


def _dot_tn(a, b):
    return lax.dot_general(a, b, (((0,), (0,)), ((), ())), preferred_element_type=F32)


def _split3(x):
    hi = x.astype(BF16)
    r1 = x - hi.astype(F32)
    mid = r1.astype(BF16)
    lo = (r1 - mid.astype(F32)).astype(BF16)
    return hi, mid, lo


def _dot_sel(sel, x):
    hi, mid, lo = _split3(x)
    return _dot(sel, hi) + _dot(sel, mid) + _dot(sel, lo)


def _sigmoid(x):
    return jax.nn.sigmoid(x)


def _silu(x):
    return x * jax.nn.sigmoid(x)


def _log_sigmoid(x):
    return jnp.minimum(x, 0.0) - jnp.log(1.0 + jnp.exp(-jnp.abs(x)))


def _layernorm(r, g, b):
    mu = jnp.mean(r, axis=-1, keepdims=True)
    c = r - mu
    var = jnp.mean(c * c, axis=-1, keepdims=True)
    return c * lax.rsqrt(var + EPS) * g + b


def _gla_matrices():
    sel = np.zeros(((2 + N_LEVELS) * CS, CS), np.float32)
    masks = np.zeros((N_LEVELS + 1, CS, CS), np.float32)
    for t in range(CS):
        sel[t, : t + 1] = 1.0
        sel[CS + t, t + 1:] = 1.0
        for l in range(N_LEVELS):
            half = 1 << l
            start = (t // (2 * half)) * (2 * half)
            mid = start + half
            row = (2 + l) * CS + t
            if t >= mid:
                sel[row, mid: t + 1] = 1.0
                masks[l, t, start:mid] = 1.0
            else:
                sel[row, t + 1: mid] = 1.0
        masks[N_LEVELS, t, t] = 1.0
    return sel, masks


def _tri(n, strict):
    return np.tril(np.ones((n, n), np.float32), -1 if strict else 0)


def _proj_kernel(x_ref, wa_ref, wb_ref, oa_ref, ob_ref):
    xb = x_ref[...].astype(BF16)
    oa_ref[...] = _dot(xb, wa_ref[...])
    ob_ref[...] = _dot(xb, wb_ref[...])


def _proj(x, wa, wb):
    na, nb = wa.shape[1], wb.shape[1]
    return pl.pallas_call(
        _proj_kernel,
        grid=(NT // TM,),
        in_specs=[
            pl.BlockSpec((TM, D), lambda i: (i, 0)),
            pl.BlockSpec((D, na), lambda i: (0, 0)),
            pl.BlockSpec((D, nb), lambda i: (0, 0)),
        ],
        out_specs=[
            pl.BlockSpec((TM, na), lambda i: (i, 0)),
            pl.BlockSpec((TM, nb), lambda i: (i, 0)),
        ],
        out_shape=[jax.ShapeDtypeStruct((NT, na), F32), jax.ShapeDtypeStruct((NT, nb), F32)],
        compiler_params=_params(("parallel",)),
        name="proj",
    )(x, wa, wb)


def _rms_gate(o, gate, w):
    o = o * lax.rsqrt(jnp.mean(o * o, axis=-1, keepdims=True) + EPS) * w
    return o * _silu(gate)


def _gla_chunk(q, k, v, g, st_ref, sel, masks_ref, heads, dk, dv):
    e = _dot_sel(sel, g)
    zf = jnp.exp(e)
    z_cum = zf[0:CS]
    z_end = zf[CS:2 * CS]
    st = st_ref[...]
    outs = []
    for h in range(heads):
        ks = slice(h * dk, (h + 1) * dk)
        vs = slice(h * dv, (h + 1) * dv)
        qh, kh = q[:, ks], k[:, ks]
        vh = v[:, vs].astype(BF16)
        scores = _dot_nt(qh.astype(BF16), kh.astype(BF16)) * masks_ref[N_LEVELS]
        for l in range(N_LEVELS):
            zl = zf[(2 + l) * CS:(3 + l) * CS, ks]
            scores = scores + _dot_nt((qh * zl).astype(BF16), (kh * zl).astype(BF16)) * masks_ref[l]
        o = _dot(scores.astype(BF16), vh)
        o = o + _dot_nt((qh * z_cum[:, ks]).astype(BF16), st[:, ks].astype(BF16))
        outs.append(o)
        upd = _dot_tn(vh, (kh * z_end[:, ks]).astype(BF16))
        st_ref[:, ks] = st[:, ks] * z_cum[CS - 1:CS, ks] + upd
    return outs


def _even_prompt_kernel(*refs):
    z_refs = refs[0:SEQ_PER_STEP]
    zgr_refs = refs[SEQ_PER_STEP:2 * SEQ_PER_STEP]
    (lbp_ref, wgk_ref, bgk_ref, gnh_ref, gng_ref, sel_ref, masks_ref,
     y_ref, shg_ref, sgla_ref, st_hg, st_gla) = refs[2 * SEQ_PER_STEP:]
    c = pl.program_id(1)

    @pl.when(c == 0)
    def _():
        st_hg[...] = jnp.zeros_like(st_hg)
        st_gla[...] = jnp.zeros_like(st_gla)

    sel = sel_ref[...]
    p = lbp_ref[...]
    pe = jnp.exp(p - jnp.max(p, axis=0, keepdims=True))
    lb = pe[0:1] / jnp.sum(pe, axis=0, keepdims=True)

    for s in range(SEQ_PER_STEP):
        z = z_refs[s][...]
        hq, hf, hi, hg = z[:, 0:512], z[:, 512:1024], z[:, 1024:1536], z[:, 1536:2048]
        gq, gk, gv, gg = z[:, 2048:2304], z[:, 2304:2560], z[:, 2560:3072], z[:, 3072:3584]
        f = lb + (1.0 - lb) * _sigmoid(hf)
        k_hg = (1.0 - lb) * _sigmoid(-hf)
        o_hg = _gla_chunk(_silu(hq), k_hg, hi, jnp.log(f), st_hg.at[s], sel, masks_ref, HG_H, HG_DK, HG_DV)

        la = _log_sigmoid(_dot(zgr_refs[s][...].astype(BF16), wgk_ref[...]) + bgk_ref[...]) / GLA_GATE_NORM
        o_gla = _gla_chunk(gq * GLA_DK ** -0.5, gk, gv, la, st_gla.at[s], sel, masks_ref, GLA_H, GLA_DK, GLA_DV)

        for h in range(HG_H):
            cs = slice(h * 128, (h + 1) * 128)
            y_ref[s, :, cs] = _rms_gate(o_hg[h], hg[:, cs], gnh_ref[...]).astype(BF16)
        for h in range(GLA_H):
            cs = slice(h * 128, (h + 1) * 128)
            y_ref[s, :, 512 + h * 128:512 + (h + 1) * 128] = _rms_gate(o_gla[h], gg[:, cs], gng_ref[...]).astype(BF16)

    @pl.when(c == NCHUNK - 1)
    def _():
        for s in range(SEQ_PER_STEP):
            shg_ref[s] = st_hg[s].T
            sgla_ref[s] = st_gla[s].T


def _seq_row_specs(width):
    return [pl.BlockSpec((CS, width), functools.partial(lambda b, c, s: ((SEQ_PER_STEP * b + s) * NCHUNK + c, 0), s=s))
            for s in range(SEQ_PER_STEP)]


def _even_prompt(z, zgr, lbp, wgk, bgk, gnh, gng, sel, masks):
    const2 = lambda b, c: (0, 0)
    sp = SEQ_PER_STEP
    return pl.pallas_call(
        _even_prompt_kernel,
        grid=(BATCH // sp, NCHUNK),
        in_specs=_seq_row_specs(EVEN_MAIN) + _seq_row_specs(LANES) + [
            pl.BlockSpec(lbp.shape, const2),
            pl.BlockSpec(wgk.shape, const2),
            pl.BlockSpec(bgk.shape, const2),
            pl.BlockSpec(gnh.shape, const2),
            pl.BlockSpec(gng.shape, const2),
            pl.BlockSpec(sel.shape, const2),
            pl.BlockSpec(masks.shape, lambda b, c: (0, 0, 0)),
        ],
        out_specs=[
            pl.BlockSpec((sp, CS, D), lambda b, c: (b, c, 0)),
            pl.BlockSpec((sp, HG_H * HG_DK, HG_DV), lambda b, c: (b, 0, 0)),
            pl.BlockSpec((sp, GLA_H * GLA_DK, GLA_DV), lambda b, c: (b, 0, 0)),
        ],
        out_shape=[
            jax.ShapeDtypeStruct((BATCH, SEQ, D), BF16),
            jax.ShapeDtypeStruct((BATCH, HG_H * HG_DK, HG_DV), F32),
            jax.ShapeDtypeStruct((BATCH, GLA_H * GLA_DK, GLA_DV), F32),
        ],
        scratch_shapes=[pltpu.VMEM((sp, HG_DV, HG_H * HG_DK), F32), pltpu.VMEM((sp, GLA_DV, GLA_H * GLA_DK), F32)],
        compiler_params=_params(("parallel", "arbitrary")),
        name="even_prompt",
    )(*([z] * sp), *([zgr] * sp), lbp, wgk, bgk, gnh, gng, sel, masks)


def _even_sample_kernel(zr_ref, zt_ref, grt_ref, lbpt_ref, wgkt_ref, bgkt_ref, gnh_ref, gng_ref,
                        shg_ref, sgla_ref, y_ref, shg_out, sgla_out, o_scr):
    zt = zt_ref[0]
    hq_t, hf_t = zt[0:512], zt[512:1024]
    gq_t, gk_t = zt[2048:2304], zt[2304:2560]
    pt = lbpt_ref[...]
    pe = jnp.exp(pt - jnp.max(pt, axis=1, keepdims=True))
    lb = pe[:, 0:1] / jnp.sum(pe, axis=1, keepdims=True)
    a_hg = jnp.exp(jnp.log(lb + (1.0 - lb) * _sigmoid(hf_t)))
    k_hg = (1.0 - lb) * _sigmoid(-hf_t)
    q_hg = _silu(hq_t)
    la = _log_sigmoid(_dot(wgkt_ref[...], grt_ref[0].astype(BF16)) + bgkt_ref[...]) / GLA_GATE_NORM
    a_gla = jnp.exp(la)
    q_gla = gq_t * GLA_DK ** -0.5
    zr = zr_ref[...]
    hi, hg = zr[:, 1024:1536], zr[:, 1536:2048]
    gv, gg = zr[:, 2560:3072], zr[:, 3072:3584]

    for j in range(SG):
        for h in range(HG_H):
            ks = slice(h * HG_DK, (h + 1) * HG_DK)
            s_new = a_hg[ks, j:j + 1] * shg_ref[j, h] + k_hg[ks, j:j + 1] * hi[j:j + 1, h * 128:(h + 1) * 128]
            shg_out[j, h] = s_new
            o_scr[j:j + 1, h * 128:(h + 1) * 128] = jnp.sum(q_hg[ks, j:j + 1] * s_new, axis=0, keepdims=True)
        for h in range(GLA_H):
            ks = slice(h * GLA_DK, (h + 1) * GLA_DK)
            s_new = a_gla[ks, j:j + 1] * sgla_ref[j, h] + gk_t[ks, j:j + 1] * gv[j:j + 1, h * 128:(h + 1) * 128]
            sgla_out[j, h] = s_new
            o_scr[j:j + 1, 512 + h * 128:512 + (h + 1) * 128] = jnp.sum(
                q_gla[ks, j:j + 1] * s_new, axis=0, keepdims=True)

    o = o_scr[...]
    for h in range(HG_H):
        cs = slice(h * 128, (h + 1) * 128)
        y_ref[:, cs] = _rms_gate(o[:, cs], hg[:, cs], gnh_ref[...]).astype(BF16)
    for h in range(GLA_H):
        cs = slice(512 + h * 128, 512 + (h + 1) * 128)
        y_ref[:, cs] = _rms_gate(o[:, cs], gg[:, h * 128:(h + 1) * 128], gng_ref[...]).astype(BF16)


def _even_sample(z, zt3, grt3, lbpt, wgkt, bgkt, gnh, gng, s_hg, s_gla):
    c2 = lambda g: (0, 0)
    return pl.pallas_call(
        _even_sample_kernel,
        grid=(NS // SG,),
        in_specs=[
            pl.BlockSpec((SG, EVEN_MAIN), lambda g: (NP // SG + g, 0)),
            pl.BlockSpec((1, EVEN_MAIN, SG), lambda g: (g, 0, 0)),
            pl.BlockSpec((1, LANES, SG), lambda g: (g, 0, 0)),
            pl.BlockSpec(lbpt.shape, c2),
            pl.BlockSpec(wgkt.shape, c2),
            pl.BlockSpec(bgkt.shape, c2),
            pl.BlockSpec(gnh.shape, c2),
            pl.BlockSpec(gng.shape, c2),
            pl.BlockSpec((SG, HG_H, HG_DK, HG_DV), lambda g: (g, 0, 0, 0)),
            pl.BlockSpec((SG, GLA_H, GLA_DK, GLA_DV), lambda g: (g, 0, 0, 0)),
        ],
        out_specs=[
            pl.BlockSpec((SG, D), lambda g: (g, 0)),
            pl.BlockSpec((SG, HG_H, HG_DK, HG_DV), lambda g: (g, 0, 0, 0)),
            pl.BlockSpec((SG, GLA_H, GLA_DK, GLA_DV), lambda g: (g, 0, 0, 0)),
        ],
        out_shape=[
            jax.ShapeDtypeStruct((NS, D), BF16),
            jax.ShapeDtypeStruct((NS, HG_H, HG_DK, HG_DV), F32),
            jax.ShapeDtypeStruct((NS, GLA_H, GLA_DK, GLA_DV), F32),
        ],
        scratch_shapes=[pltpu.VMEM((SG, D), F32)],
        compiler_params=_params(("parallel",)),
        name="even_sample",
    )(z, zt3, grt3, lbpt, wgkt, bgkt, gnh, gng, s_hg, s_gla)


def _out_ln_kernel(x_ref, y_ref, w_ref, g_ref, b_ref, o_ref):
    r = ALPHA * x_ref[...] + _dot(y_ref[...], w_ref[...])
    o_ref[...] = _layernorm(r, g_ref[...], b_ref[...])


def _out_ln(x, y, w, g, b):
    c2 = lambda i: (0, 0)
    return pl.pallas_call(
        _out_ln_kernel,
        grid=(NT // TM,),
        in_specs=[
            pl.BlockSpec((TM, D), lambda i: (i, 0)),
            pl.BlockSpec((TM, D), lambda i: (i, 0)),
            pl.BlockSpec((D, D), c2),
            pl.BlockSpec((1, D), c2),
            pl.BlockSpec((1, D), c2),
        ],
        out_specs=pl.BlockSpec((TM, D), lambda i: (i, 0)),
        out_shape=jax.ShapeDtypeStruct((NT, D), F32),
        compiler_params=_params(("parallel",)),
        name="out_ln",
    )(x, y, w, g, b)


FF_SPLIT = 2


def _ffn_kernel(x_ref, w1_ref, w3_ref, w2_ref, g_ref, b_ref, o_ref):
    x = x_ref[...]
    xb = x.astype(BF16)
    step = D_FF_DENSE // FF_SPLIT
    acc = ALPHA * x
    for s in range(FF_SPLIT):
        cs = slice(s * step, (s + 1) * step)
        hmid = _silu(_dot(xb, w1_ref[:, cs])) * _dot(xb, w3_ref[:, cs])
        acc = acc + _dot(hmid.astype(BF16), w2_ref[cs, :])
    o_ref[...] = _layernorm(acc, g_ref[...], b_ref[...])


def _ffn(x, w1, w3, w2, g, b):
    c2 = lambda i: (0, 0)
    one = pl.Buffered(1)
    return pl.pallas_call(
        _ffn_kernel,
        grid=(NT // TM,),
        in_specs=[
            pl.BlockSpec((TM, D), lambda i: (i, 0)),
            pl.BlockSpec((D, D_FF_DENSE), c2, pipeline_mode=one),
            pl.BlockSpec((D, D_FF_DENSE), c2, pipeline_mode=one),
            pl.BlockSpec((D_FF_DENSE, D), c2, pipeline_mode=one),
            pl.BlockSpec((1, D), c2),
            pl.BlockSpec((1, D), c2),
        ],
        out_specs=pl.BlockSpec((TM, D), lambda i: (i, 0)),
        out_shape=jax.ShapeDtypeStruct((NT, D), F32),
        compiler_params=_params(("parallel",)),
        name="ffn_dense",
    )(x, w1, w3, w2, g, b)


def _mh_norm_gate(hh, o_pre, w):
    mu = jnp.mean(hh, axis=-1, keepdims=True)
    c = hh - mu
    var = jnp.mean(c * c, axis=-1, keepdims=True)
    return _sigmoid(o_pre) * (c * lax.rsqrt(var + EPS) * w)


def _odd_prompt_kernel(*refs):
    z_refs = refs[0:SEQ_PER_STEP]
    zg_refs = refs[SEQ_PER_STEP:2 * SEQ_PER_STEP]
    (bg_ref, cw_ref, cb_ref, hnw_ref, tri_ref,
     y_ref, c_out, n_out, m_out, conv_out,
     c_scr, n_scr, m_scr, u_scr) = refs[2 * SEQ_PER_STEP:]
    c = pl.program_id(1)

    @pl.when(c == 0)
    def _():
        c_scr[...] = jnp.zeros_like(c_scr)
        n_scr[...] = jnp.zeros_like(n_scr)
        m_scr[...] = jnp.zeros_like(m_scr)
        for s in range(SEQ_PER_STEP):
            u_scr[s, 0:8, :] = jnp.zeros((8, D), F32)

    row = lax.broadcasted_iota(jnp.int32, (CS, CS), 0)
    col = lax.broadcasted_iota(jnp.int32, (CS, CS), 1)
    causal = col <= row
    tails = []

    for s in range(SEQ_PER_STEP):
        z_ref = z_refs[s]
        u_scr[s, 8:8 + CS, :] = z_ref[:, 0:D]
        uc = cb_ref[...]
        for j in range(CONV_W):
            uc = uc + u_scr[s, 5 + j:5 + j + CS, :] * cw_ref[j:j + 1, :]
        tail = u_scr[s, CS:CS + 8, :]
        u_scr[s, 0:8, :] = tail
        tails.append(tail)
        act = _silu(uc)
        q = act[:, 0:512] * ML_DK ** -0.5
        k = act[:, 512:1024]
        v = z_ref[:, D:2 * D]
        o_pre = z_ref[:, 2 * D:3 * D]

        gates = zg_refs[s][...] + bg_ref[...]
        lf = _log_sigmoid(gates)
        bcum = _dot_sel(tri_ref[...], lf)
        bcum_t = bcum.T
        gates_t = gates.T
        m_all = m_scr[s]

        for h in range(ML_H):
            ks = slice(h * ML_DK, (h + 1) * ML_DK)
            vs = slice(h * ML_DV, (h + 1) * ML_DV)
            qh, kh = q[:, ks], k[:, ks]
            vh = v[:, vs].astype(BF16)
            b_col = bcum[:, 4 + h:5 + h]
            b_row = bcum_t[4 + h:5 + h, :]
            i_col = gates[:, h:h + 1]
            i_row = gates_t[h:h + 1, :]
            m_prev = m_all[:, h:h + 1]
            log_d = jnp.where(causal, b_col - b_row + i_row, -jnp.inf)
            log_prev = b_col + m_prev
            m_t = jnp.maximum(jnp.max(log_d, axis=-1, keepdims=True), log_prev)
            d = jnp.exp(log_d - m_t)
            w_prev = jnp.exp(log_prev - m_t)
            scores = _dot_nt(qh.astype(BF16), kh.astype(BF16)) * d
            c_h = c_scr[s, h]
            n_h = n_scr[s, h:h + 1, :]
            num = _dot(scores.astype(BF16), vh) + w_prev * _dot(qh.astype(BF16), c_h.astype(BF16))
            den = jnp.sum(scores, axis=-1, keepdims=True) + w_prev * jnp.sum(qh * n_h, axis=-1, keepdims=True)
            hh = num / jnp.maximum(jnp.abs(den), jnp.exp(-m_t))
            m_new = m_t[CS - 1:CS, :]
            b_last = b_col[CS - 1:CS, :]
            w_c = jnp.exp(b_last + m_prev - m_new)
            w_s = jnp.exp(b_last - b_col + i_col - m_new)
            kw = kh * w_s
            c_scr[s, h] = w_c * c_h + _dot_tn(kw.astype(BF16), vh)
            n_scr[s, h:h + 1, :] = w_c * n_h + jnp.sum(kw, axis=0, keepdims=True)
            m_scr[s, :, h:h + 1] = m_new
            y_ref[s, :, vs] = _mh_norm_gate(hh, o_pre[:, vs], hnw_ref[:, vs]).astype(BF16)

    @pl.when(c == NCHUNK - 1)
    def _():
        c_out[...] = c_scr[...]
        for s in range(SEQ_PER_STEP):
            n_out[s] = n_scr[s, 0:ML_H, :]
            m_out[s] = m_scr[s]
            conv_out[s] = tails[s][8 - (CONV_W - 1):8, :]


def _odd_prompt(z, zg, bg, cw, cb, hnw, tri):
    c2 = lambda b, c: (0, 0)
    sp = SEQ_PER_STEP
    return pl.pallas_call(
        _odd_prompt_kernel,
        grid=(BATCH // sp, NCHUNK),
        in_specs=_seq_row_specs(ODD_MAIN) + _seq_row_specs(LANES) + [
            pl.BlockSpec((1, LANES), c2),
            pl.BlockSpec((CONV_W, D), c2),
            pl.BlockSpec((1, D), c2),
            pl.BlockSpec((1, D), c2),
            pl.BlockSpec((CS, CS), c2),
        ],
        out_specs=[
            pl.BlockSpec((sp, CS, D), lambda b, c: (b, c, 0)),
            pl.BlockSpec((sp, ML_H, ML_DK, ML_DV), lambda b, c: (b, 0, 0, 0)),
            pl.BlockSpec((sp, ML_H, ML_DK), lambda b, c: (b, 0, 0)),
            pl.BlockSpec((sp, 1, LANES), lambda b, c: (b, 0, 0)),
            pl.BlockSpec((sp, CONV_W - 1, D), lambda b, c: (b, 0, 0)),
        ],
        out_shape=[
            jax.ShapeDtypeStruct((BATCH, SEQ, D), BF16),
            jax.ShapeDtypeStruct((BATCH, ML_H, ML_DK, ML_DV), F32),
            jax.ShapeDtypeStruct((BATCH, ML_H, ML_DK), F32),
            jax.ShapeDtypeStruct((BATCH, 1, LANES), F32),
            jax.ShapeDtypeStruct((BATCH, CONV_W - 1, D), F32),
        ],
        scratch_shapes=[
            pltpu.VMEM((sp, ML_H, ML_DK, ML_DV), F32),
            pltpu.VMEM((sp, 8, ML_DK), F32),
            pltpu.VMEM((sp, 1, LANES), F32),
            pltpu.VMEM((sp, CS + 8, D), F32),
        ],
        compiler_params=_params(("parallel", "arbitrary")),
        name="odd_prompt",
    )(*([z] * sp), *([zg] * sp), bg, cw, cb, hnw, tri)


def _odd_sample_kernel(zr_ref, zg_ref, ut_ref, conv_ref, convt_ref, bg_ref, cw_ref, cwt_ref, cb_ref, cbt_ref,
                       hnw_ref, c_ref, n_ref, m_ref,
                       y_ref, c_out, n_out, m_out, conv_out, h_scr):
    zr = zr_ref[...]
    u = zr[:, 0:D]
    v = zr[:, D:2 * D]
    o_pre = zr[:, 2 * D:3 * D]
    uc = cb_ref[...] + u * cw_ref[CONV_W - 1:CONV_W, :]
    uc_t = cbt_ref[...] + ut_ref[0] * cwt_ref[:, CONV_W - 1:CONV_W]
    for j in range(CONV_W - 1):
        uc = uc + conv_ref[:, j * D:(j + 1) * D] * cw_ref[j:j + 1, :]
        uc_t = uc_t + convt_ref[0, j] * cwt_ref[:, j:j + 1]
        conv_out[:, j * D:(j + 1) * D] = conv_ref[:, (j + 1) * D:(j + 2) * D] if j + 1 < CONV_W - 1 else u
    act = _silu(uc)
    k_row = act[:, 512:1024]
    act_t = _silu(uc_t)
    q_t = act_t[0:512] * ML_DK ** -0.5
    k_t = act_t[512:1024]
    q_row = act[:, 0:512] * ML_DK ** -0.5

    gates = zg_ref[...] + bg_ref[...]
    lf = _log_sigmoid(gates)
    m_in = m_ref[...]
    m_out[...] = m_in

    for j in range(SG):
        for h in range(ML_H):
            ks = slice(h * ML_DK, (h + 1) * ML_DK)
            vs = slice(h * ML_DV, (h + 1) * ML_DV)
            ig = gates[j:j + 1, h:h + 1]
            log_prev = lf[j:j + 1, 4 + h:5 + h] + m_in[j:j + 1, h:h + 1]
            m_t = jnp.maximum(ig, log_prev)
            d = jnp.exp(ig - m_t)
            w_prev = jnp.exp(log_prev - m_t)
            c_new = w_prev * c_ref[j, h] + (d * k_t[ks, j:j + 1]) * v[j:j + 1, vs]
            n_new = w_prev * n_ref[j, h:h + 1, :] + d * k_row[j:j + 1, ks]
            c_out[j, h] = c_new
            n_out[j, h:h + 1, :] = n_new
            m_out[j:j + 1, h:h + 1] = m_t
            num = jnp.sum(q_t[ks, j:j + 1] * c_new, axis=0, keepdims=True)
            den = jnp.sum(q_row[j:j + 1, ks] * n_new, axis=-1, keepdims=True)
            h_scr[j:j + 1, vs] = num / jnp.maximum(jnp.abs(den), jnp.exp(-m_t))

    hh = h_scr[...]
    for h in range(ML_H):
        vs = slice(h * ML_DV, (h + 1) * ML_DV)
        y_ref[:, vs] = _mh_norm_gate(hh[:, vs], o_pre[:, vs], hnw_ref[:, vs]).astype(BF16)


def _odd_sample(z, zg, ut3, conv, convt, bg, cw, cwt, cb, cbt, hnw, c_in, n_in, m_in):
    c2 = lambda g: (0, 0)
    return pl.pallas_call(
        _odd_sample_kernel,
        grid=(NS // SG,),
        in_specs=[
            pl.BlockSpec((SG, ODD_MAIN), lambda g: (NP // SG + g, 0)),
            pl.BlockSpec((SG, LANES), lambda g: (NP // SG + g, 0)),
            pl.BlockSpec((1, D, SG), lambda g: (g, 0, 0)),
            pl.BlockSpec((SG, (CONV_W - 1) * D), lambda g: (g, 0)),
            pl.BlockSpec((1, CONV_W - 1, D, SG), lambda g: (g, 0, 0, 0)),
            pl.BlockSpec((1, LANES), c2),
            pl.BlockSpec((CONV_W, D), c2),
            pl.BlockSpec((D, CONV_W), c2),
            pl.BlockSpec((1, D), c2),
            pl.BlockSpec((D, 1), c2),
            pl.BlockSpec((1, D), c2),
            pl.BlockSpec((SG, ML_H, ML_DK, ML_DV), lambda g: (g, 0, 0, 0)),
            pl.BlockSpec((SG, ML_H, ML_DK), lambda g: (g, 0, 0)),
            pl.BlockSpec((SG, LANES), lambda g: (g, 0)),
        ],
        out_specs=[
            pl.BlockSpec((SG, D), lambda g: (g, 0)),
            pl.BlockSpec((SG, ML_H, ML_DK, ML_DV), lambda g: (g, 0, 0, 0)),
            pl.BlockSpec((SG, ML_H, ML_DK), lambda g: (g, 0, 0)),
            pl.BlockSpec((SG, LANES), lambda g: (g, 0)),
            pl.BlockSpec((SG, (CONV_W - 1) * D), lambda g: (g, 0)),
        ],
        out_shape=[
            jax.ShapeDtypeStruct((NS, D), BF16),
            jax.ShapeDtypeStruct((NS, ML_H, ML_DK, ML_DV), F32),
            jax.ShapeDtypeStruct((NS, ML_H, ML_DK), F32),
            jax.ShapeDtypeStruct((NS, LANES), F32),
            jax.ShapeDtypeStruct((NS, (CONV_W - 1) * D), F32),
        ],
        scratch_shapes=[pltpu.VMEM((SG, D), F32)],
        compiler_params=_params(("parallel",)),
        name="odd_sample",
    )(z, zg, ut3, conv, convt, bg, cw, cwt, cb, cbt, hnw, c_in, n_in, m_in)


def _out_ln_router_kernel(x_ref, y_ref, w_ref, g_ref, b_ref, wr_ref, tri_ref,
                          o_ref, op_ref, meta_ref, cnt_ref, carry):
    i = pl.program_id(0)

    @pl.when(i == 0)
    def _():
        carry[...] = jnp.zeros_like(carry)

    r = ALPHA * x_ref[...] + _dot(y_ref[...], w_ref[...])
    x3 = _layernorm(r, g_ref[...], b_ref[...])
    o_ref[...] = x3
    x3b = x3.astype(BF16)
    halves = pltpu.bitcast(x3b.astype(F32), U32)
    op_ref[...] = halves[:, D // 2:] | (halves[:, :D // 2] >> 16)

    lane = lax.broadcasted_iota(jnp.int32, (TM, LANES), 1).astype(F32)
    logits = jnp.where(lane < N_EXPERTS, _dot(x3b, wr_ref[...]), -jnp.inf)
    m1 = jnp.max(logits, axis=-1, keepdims=True)
    i1 = jnp.min(jnp.where(logits == m1, lane, float(LANES)), axis=-1, keepdims=True)
    rest = jnp.where(lane == i1, -jnp.inf, logits)
    m2 = jnp.max(rest, axis=-1, keepdims=True)
    i2 = jnp.min(jnp.where(rest == m2, lane, float(LANES)), axis=-1, keepdims=True)
    e2 = jnp.exp(m2 - m1)
    tot = 1.0 + e2
    w1 = 1.0 / tot
    w2 = e2 / tot

    sel1 = lane == i1
    sel2 = lane == i2
    onehot = jnp.where(sel1 | sel2, 1.0, 0.0)
    before = _dot(tri_ref[...], onehot.astype(BF16)) + carry[...]
    r1 = jnp.sum(jnp.where(sel1, before, 0.0), axis=-1, keepdims=True)
    r2 = jnp.sum(jnp.where(sel2, before, 0.0), axis=-1, keepdims=True)
    carry[...] = carry[...] + jnp.sum(onehot, axis=0, keepdims=True)
    cnt_ref[...] = carry[...]

    meta = jnp.where(lane == 0.0, i1, 0.0)
    meta = jnp.where(lane == 1.0, i2, meta)
    meta = jnp.where(lane == 2.0, w1, meta)
    meta = jnp.where(lane == 3.0, w2, meta)
    meta = jnp.where(lane == 4.0, r1, meta)
    meta = jnp.where(lane == 5.0, r2, meta)
    meta_ref[...] = meta


def _out_ln_router(x, y, w, g, b, wr, tri):
    c2 = lambda i: (0, 0)
    return pl.pallas_call(
        _out_ln_router_kernel,
        grid=(NT // TM,),
        in_specs=[
            pl.BlockSpec((TM, D), lambda i: (i, 0)),
            pl.BlockSpec((TM, D), lambda i: (i, 0)),
            pl.BlockSpec((D, D), c2),
            pl.BlockSpec((1, D), c2),
            pl.BlockSpec((1, D), c2),
            pl.BlockSpec((D, LANES), c2),
            pl.BlockSpec((TM, TM), c2),
        ],
        out_specs=[
            pl.BlockSpec((TM, D), lambda i: (i, 0)),
            pl.BlockSpec((TM, D // 2), lambda i: (i, 0)),
            pl.BlockSpec((TM, LANES), lambda i: (i, 0)),
            pl.BlockSpec((1, LANES), c2),
        ],
        out_shape=[
            jax.ShapeDtypeStruct((NT, D), F32),
            jax.ShapeDtypeStruct((NT, D // 2), U32),
            jax.ShapeDtypeStruct((NT, LANES), F32),
            jax.ShapeDtypeStruct((1, LANES), F32),
        ],
        scratch_shapes=[pltpu.VMEM((1, LANES), F32)],
        compiler_params=_params(("arbitrary",)),
        name="out_ln_router",
    )(x, y, w, g, b, wr, tri)


def _moe_ffn_kernel(te_ref, nu_ref, gnext_ref, gcur_ref, sprev_ref, scur_ref, xp_ref, w1_ref, w3_ref, w2_ref,
                    out_hbm, stage, yacc, xb_scr, sem_s):
    i = pl.program_id(0)
    j = pl.program_id(1)
    used = i < nu_ref[0]
    slot = i % 2
    other = 1 - slot
    rps = MOE_ROWS_PER_STEP

    def gather_rows(tab_ref, buf, part):
        for r in range(rps):
            stage[buf, part, pl.ds(r, 1), :] = xp_ref[pl.ds(tab_ref[part * rps + r], 1), :]

    def scatter(buf, r, dst):
        return pltpu.make_async_copy(yacc.at[buf, pl.ds(r, 1)], out_hbm.at[pl.ds(dst, 1)], sem_s)

    def wait_scatters(n):
        for _ in range(n):
            scatter(0, 0, 0).wait()

    def issue_neighbours():
        gather_rows(gnext_ref, other, j)
        for r in range(rps):
            rr = j * rps + r
            scatter(other, rr, sprev_ref[rr]).start()

    @pl.when(j == 0)
    def _():
        @pl.when(i == 0)
        def _():
            yacc[1] = jnp.zeros((TMM, D), F32)
            for part in range(MOE_NFF):
                gather_rows(gcur_ref, 0, part)

        @pl.when(i > 0)
        def _():
            wait_scatters(TMM)

        words = stage[slot].reshape(TMM, D // 2)
        low = pltpu.bitcast(words << 16, F32).astype(BF16)
        high = pltpu.bitcast(words & jnp.uint32(0xFFFF0000), F32).astype(BF16)
        xb_scr[:, 0:D // 2] = low
        xb_scr[:, D // 2:D] = high
        yacc[slot] = jnp.zeros((TMM, D), F32)

    @pl.when(used)
    def _():
        issue_neighbours()
        xb = xb_scr[...]
        hmid = _silu(_dot(xb, w1_ref[...])) * _dot(xb, w3_ref[...])
        yacc[slot] += _dot(hmid.astype(BF16), w2_ref[...])

    @pl.when(jnp.logical_not(used))
    def _():
        issue_neighbours()

    @pl.when((i == N_MOE_TILES - 1) & (j == MOE_NFF - 1))
    def _():
        for r in range(TMM):
            scatter(slot, r, scur_ref[r]).start()
        wait_scatters(2 * TMM)


def _moe_ffn(tile_expert, n_used, gsrc, sdst, xp, w1, w3, w2):
    nff = MOE_NFF

    def wcol(i, j, te, nu):
        return (te[i], 0, jnp.where(i < nu[0], j, nff - 1))

    def wrow(i, j, te, nu):
        return (te[i], jnp.where(i < nu[0], j, nff - 1), 0)

    smem = functools.partial(pl.BlockSpec, (MOE_TAB,), memory_space=pltpu.SMEM)
    grid_spec = pltpu.PrefetchScalarGridSpec(
        num_scalar_prefetch=2,
        grid=(N_MOE_TILES, nff),
        in_specs=[
            smem(lambda i, j, te, nu: (i + 1,)),
            smem(lambda i, j, te, nu: (i,)),
            smem(lambda i, j, te, nu: (i,)),
            smem(lambda i, j, te, nu: (i + 1,)),
            pl.BlockSpec((NT, D // 2), lambda i, j, te, nu: (0, 0), pipeline_mode=pl.Buffered(1)),
            pl.BlockSpec((None, D, TFF), wcol),
            pl.BlockSpec((None, D, TFF), wcol),
            pl.BlockSpec((None, TFF, D), wrow),
        ],
        out_specs=pl.BlockSpec(memory_space=pl.ANY),
        scratch_shapes=[
            pltpu.VMEM((2, MOE_NFF, MOE_ROWS_PER_STEP, D // 2), U32),
            pltpu.VMEM((2, TMM, D), F32),
            pltpu.VMEM((TMM, D), BF16),
            pltpu.SemaphoreType.DMA(()),
        ],
    )
    return pl.pallas_call(
        _moe_ffn_kernel,
        grid_spec=grid_spec,
        out_shape=jax.ShapeDtypeStruct((MOE_OUT_ROWS, D), F32),
        compiler_params=_params(("arbitrary", "arbitrary")),
        name="moe_ffn",
    )(tile_expert, n_used, gsrc, gsrc, sdst, sdst, xp, w1, w3, w2)


def _combine_kernel(x_ref, meta_ref, y0_ref, y1_ref, g_ref, b_ref, o_ref):
    meta = meta_ref[...]
    moe = meta[:, 2:3] * y0_ref[...] + meta[:, 3:4] * y1_ref[...]
    o_ref[...] = _layernorm(ALPHA * x_ref[...] + moe, g_ref[...], b_ref[...])


def _combine(x, meta, ys, g, b):
    c2 = lambda i: (0, 0)
    return pl.pallas_call(
        _combine_kernel,
        grid=(NT // TM,),
        in_specs=[
            pl.BlockSpec((TM, D), lambda i: (i, 0)),
            pl.BlockSpec((TM, LANES), lambda i: (i, 0)),
            pl.BlockSpec((TM, D), lambda i: (i, 0)),
            pl.BlockSpec((TM, D), lambda i: (i + NT // TM, 0)),
            pl.BlockSpec((1, D), c2),
            pl.BlockSpec((1, D), c2),
        ],
        out_specs=pl.BlockSpec((TM, D), lambda i: (i, 0)),
        out_shape=jax.ShapeDtypeStruct((NT, D), F32),
        compiler_params=_params(("parallel",)),
        name="moe_combine",
    )(x, meta, ys, ys, g, b)


def _pad_cols(w, n):
    return jnp.pad(w, ((0, 0), (0, n - w.shape[1])))


def kernel(x_prompt, x_sample, state_hgrn, state_gla, state_mlstm_C, state_mlstm_n, state_mlstm_m,
           state_mlstm_conv, w_in_even, hg_lower_bounds, w_gk, b_gk, gn_hg, gn_gla, w_out_even,
           w1_dense, w3_dense, w2_dense, w_in_odd, b_gate_odd, conv_w, conv_b, hn_w, w_out_odd,
           w_router, w1_moe, w3_moe, w2_moe, ln1_g, ln1_b, ln2_g, ln2_b):
    assert x_prompt.shape == (BATCH, SEQ, D) and x_sample.shape == (NS, 1, D)
    assert w_in_even.shape[0] == 1 and w_in_odd.shape[0] == 1 and hg_lower_bounds.shape[0] == 2
    sel_np, masks_np = _gla_matrices()
    sel = jnp.asarray(sel_np, BF16)
    masks = jnp.asarray(masks_np, F32)
    tri_cs = jnp.asarray(_tri(CS, False), BF16)
    tri_tm = jnp.asarray(_tri(TM, True), BF16)
    row = lambda a: a.reshape(1, -1)

    x0 = jnp.concatenate([x_prompt.reshape(NP, D), x_sample.reshape(NS, D)], axis=0)

    w_even = w_in_even[0].astype(BF16)
    z, zgr = _proj(x0, w_even[:, :EVEN_MAIN], _pad_cols(w_even[:, EVEN_MAIN:], LANES))
    wgk = jnp.pad(w_gk[0].astype(BF16), ((0, LANES - GLA_RANK), (0, 0)))
    lbp = hg_lower_bounds
    y_p, hg_p, gla_p = _even_prompt(z, zgr, lbp, wgk, row(b_gk[0]), row(gn_hg[0]), row(gn_gla[0]), sel, masks)

    zs = z[NP:].reshape(NS // SG, SG, EVEN_MAIN).transpose(0, 2, 1)
    grs = zgr[NP:].reshape(NS // SG, SG, LANES).transpose(0, 2, 1)
    y_s, hg_s, gla_s = _even_sample(z, zs, grs, lbp.T, wgk.T, b_gk[0].reshape(-1, 1),
                                    row(gn_hg[0]), row(gn_gla[0]), state_hgrn[0], state_gla[0])
    y = jnp.concatenate([y_p.reshape(NP, D), y_s], axis=0)
    x1 = _out_ln(x0, y, w_out_even[0].astype(BF16), row(ln1_g[0]), row(ln1_b[0]))
    x2 = _ffn(x1, w1_dense[0].astype(BF16), w3_dense[0].astype(BF16), w2_dense[0].astype(BF16),
              row(ln2_g[0]), row(ln2_b[0]))

    w_odd = w_in_odd[0].astype(BF16)
    zo, zog = _proj(x2, w_odd[:, :ODD_MAIN], _pad_cols(w_odd[:, ODD_MAIN:], LANES))
    bg = jnp.pad(b_gate_odd[0], (0, LANES - 2 * ML_H)).reshape(1, LANES)
    yo_p, c_p, n_p, m_p, conv_p = _odd_prompt(zo, zog, bg, conv_w[0], row(conv_b[0]), row(hn_w[0]), tri_cs)

    ut = zo[NP:, :D].reshape(NS // SG, SG, D).transpose(0, 2, 1)
    conv_in = state_mlstm_conv[0]
    conv_t = conv_in.reshape(NS // SG, SG, CONV_W - 1, D).transpose(0, 2, 3, 1)
    m_in = jnp.pad(state_mlstm_m[0], ((0, 0), (0, LANES - ML_H)))
    yo_s, c_s, n_s, m_s, conv_s = _odd_sample(
        zo, zog, ut, conv_in.reshape(NS, (CONV_W - 1) * D), conv_t, bg, conv_w[0], conv_w[0].T, row(conv_b[0]), conv_b[0].reshape(-1, 1),
        row(hn_w[0]), state_mlstm_C[0], state_mlstm_n[0], m_in)
    yo = jnp.concatenate([yo_p.reshape(NP, D), yo_s], axis=0)

    wr = _pad_cols(w_router[0].astype(BF16), LANES)
    x3, x3p, meta, cnt = _out_ln_router(x2, yo, w_out_odd[0].astype(BF16), row(ln1_g[1]), row(ln1_b[1]), wr, tri_tm)

    counts = cnt[0, :N_EXPERTS].astype(jnp.int32)
    padded = ((counts + TMM - 1) // TMM) * TMM
    ends = jnp.cumsum(padded)
    offsets = ends - padded
    idx = meta[:, 0:2].astype(jnp.int32)
    pos = offsets[idx] + meta[:, 4:6].astype(jnp.int32)
    tile_start = jnp.arange(N_MOE_TILES, dtype=jnp.int32) * TMM
    tile_expert = jnp.minimum(jnp.sum(tile_start[:, None] >= ends[None, :], axis=1), N_EXPERTS - 1).astype(jnp.int32)
    n_used = (ends[-1] // TMM).astype(jnp.int32).reshape(1)

    slot = jnp.arange(MOE_SLOTS, dtype=jnp.int32)
    e_slot = jnp.sum(slot[:, None] >= ends[None, :], axis=1)
    off_x = jnp.concatenate([offsets, ends[-1:]])
    cnt_x = jnp.concatenate([counts, jnp.zeros((1,), jnp.int32)])
    real_before_x = jnp.concatenate([jnp.cumsum(counts) - counts, jnp.full((1,), 2 * NT, jnp.int32)])
    local = slot - off_x[e_slot]
    real_before = real_before_x[e_slot] + jnp.minimum(local, cnt_x[e_slot])
    spill_row = 2 * NT + TMM + (slot - real_before)
    token = jnp.arange(NT, dtype=jnp.int32)
    dst_rows = jnp.stack([token, NT + token], axis=1)
    dst = spill_row.at[pos.reshape(-1)].set(dst_rows.reshape(-1), unique_indices=True)
    per_tile = lambda t: jnp.pad(t.reshape(-1, TMM), ((0, 0), (0, MOE_TAB - TMM))).reshape(-1)
    sdst = per_tile(jnp.concatenate([2 * NT + jnp.arange(TMM, dtype=jnp.int32), dst]))
    src_tok = jnp.where(dst < NT, dst, jnp.where(dst < 2 * NT, dst - NT, 0))
    gsrc = per_tile(jnp.concatenate([src_tok, jnp.zeros((TMM,), jnp.int32)]))

    ys = _moe_ffn(tile_expert, n_used, gsrc, sdst, x3p,
                  w1_moe[0].astype(BF16), w3_moe[0].astype(BF16), w2_moe[0].astype(BF16))
    out = _combine(x3, meta, ys, row(ln2_g[1]), row(ln2_b[1]))

    y_prompt = out[:NP].reshape(BATCH, SEQ, D)
    y_sample = out[NP:].reshape(NS, 1, D)
    return (y_prompt, y_sample,
            hg_p.reshape(1, BATCH, HG_H, HG_DK, HG_DV), gla_p.reshape(1, BATCH, GLA_H, GLA_DK, GLA_DV),
            c_p[None], n_p[None], m_p[:, 0, :ML_H][None], conv_p[None],
            hg_s[None], gla_s[None], c_s[None], n_s[None], m_s[:, :ML_H][None], conv_s.reshape(1, NS, CONV_W - 1, D))
```

```python
import functools
import math

import jax
import jax.numpy as jnp
import numpy as np
from jax import lax
from jax.experimental import pallas as pl
from jax.experimental.pallas import tpu as pltpu

F32 = jnp.float32
BF16 = jnp.bfloat16
U32 = jnp.uint32

D = 1024
BATCH = 8
SEQ = 2048
DEC_BATCH = 128
NP = BATCH * SEQ
NS = DEC_BATCH
NT = NP + NS
HG_H, HG_DK, HG_DV = 4, 128, 128
GLA_H, GLA_DK, GLA_DV = 4, 64, 128
GLA_RANK = 16
GLA_GATE_NORM = 16.0
ML_H, ML_DK, ML_DV = 4, 128, 256
CONV_W = 4
D_FF_DENSE = 2816
D_FF_EXPERT = 3584
N_EXPERTS = 8
EPS = 1e-5
DEPTH = 2
ALPHA = (2.0 * DEPTH) ** 0.25
EVEN_MAIN = 3584
ODD_MAIN = 3072

LANES = 128
SUBLANES = 8
VMEM_LIMIT = 56 * 1024 * 1024

TM = 384
CS = 128
NCHUNK = SEQ // CS
SEQ_PER_STEP = 1
SG = 16
TMM = 512
TFF = 896
MOE_TAB = 512
MOE_NFF = D_FF_EXPERT // TFF
MOE_ROWS_PER_STEP = TMM // MOE_NFF
N_MOE_TILES = -(-(2 * NT + N_EXPERTS * (TMM - 1)) // TMM)
MOE_SLOTS = N_MOE_TILES * TMM
MOE_OUT_ROWS = MOE_SLOTS + TMM
N_LEVELS = int(math.log2(CS))

assert NT % TM == 0 and NP % CS == 0 and NS % SG == 0 and D_FF_EXPERT % TFF == 0 and TMM % MOE_NFF == 0


def _params(sem, limit=VMEM_LIMIT):
    return pltpu.CompilerParams(dimension_semantics=sem, vmem_limit_bytes=limit)


def _dot(a, b):
    return jnp.dot(a, b, preferred_element_type=F32)


def _dot_nt(a, b):
    return lax.dot_general(a, b, (((1,), (1,)), ((), ())), preferred_element_type=F32)


def _dot_tn(a, b):
    return lax.dot_general(a, b, (((0,), (0,)), ((), ())), preferred_element_type=F32)


def _split3(x):
    hi = x.astype(BF16)
    r1 = x - hi.astype(F32)
    mid = r1.astype(BF16)
    lo = (r1 - mid.astype(F32)).astype(BF16)
    return hi, mid, lo


def _dot_sel(sel, x):
    hi, mid, lo = _split3(x)
    return _dot(sel, hi) + _dot(sel, mid) + _dot(sel, lo)


def _sigmoid(x):
    return jax.nn.sigmoid(x)


def _silu(x):
    return x * jax.nn.sigmoid(x)


def _log_sigmoid(x):
    return jnp.minimum(x, 0.0) - jnp.log(1.0 + jnp.exp(-jnp.abs(x)))


def _layernorm(r, g, b):
    mu = jnp.mean(r, axis=-1, keepdims=True)
    c = r - mu
    var = jnp.mean(c * c, axis=-1, keepdims=True)
    return c * lax.rsqrt(var + EPS) * g + b


def _gla_matrices():
    sel = np.zeros(((2 + N_LEVELS) * CS, CS), np.float32)
    masks = np.zeros((N_LEVELS + 1, CS, CS), np.float32)
    for t in range(CS):
        sel[t, : t + 1] = 1.0
        sel[CS + t, t + 1:] = 1.0
        for l in range(N_LEVELS):
            half = 1 << l
            start = (t // (2 * half)) * (2 * half)
            mid = start + half
            row = (2 + l) * CS + t
            if t >= mid:
                sel[row, mid: t + 1] = 1.0
                masks[l, t, start:mid] = 1.0
            else:
                sel[row, t + 1: mid] = 1.0
        masks[N_LEVELS, t, t] = 1.0
    return sel, masks


def _tri(n, strict):
    return np.tril(np.ones((n, n), np.float32), -1 if strict else 0)


def _proj_kernel(x_ref, wa_ref, wb_ref, oa_ref, ob_ref):
    xb = x_ref[...].astype(BF16)
    oa_ref[...] = _dot(xb, wa_ref[...])
    ob_ref[...] = _dot(xb, wb_ref[...])


def _proj(x, wa, wb):
    na, nb = wa.shape[1], wb.shape[1]
    return pl.pallas_call(
        _proj_kernel,
        grid=(NT // TM,),
        in_specs=[
            pl.BlockSpec((TM, D), lambda i: (i, 0)),
            pl.BlockSpec((D, na), lambda i: (0, 0)),
            pl.BlockSpec((D, nb), lambda i: (0, 0)),
        ],
        out_specs=[
            pl.BlockSpec((TM, na), lambda i: (i, 0)),
            pl.BlockSpec((TM, nb), lambda i: (i, 0)),
        ],
        out_shape=[jax.ShapeDtypeStruct((NT, na), F32), jax.ShapeDtypeStruct((NT, nb), F32)],
        compiler_params=_params(("parallel",)),
        name="proj",
    )(x, wa, wb)


def _rms_gate(o, gate, w):
    o = o * lax.rsqrt(jnp.mean(o * o, axis=-1, keepdims=True) + EPS) * w
    return o * _silu(gate)


def _gla_chunk(q, k, v, g, st_ref, sel, masks_ref, heads, dk, dv):
    e = _dot_sel(sel, g)
    zf = jnp.exp(e)
    z_cum = zf[0:CS]
    z_end = zf[CS:2 * CS]
    st = st_ref[...]
    outs = []
    for h in range(heads):
        ks = slice(h * dk, (h + 1) * dk)
        vs = slice(h * dv, (h + 1) * dv)
        qh, kh = q[:, ks], k[:, ks]
        vh = v[:, vs].astype(BF16)
        scores = _dot_nt(qh.astype(BF16), kh.astype(BF16)) * masks_ref[N_LEVELS]
        for l in range(N_LEVELS):
            zl = zf[(2 + l) * CS:(3 + l) * CS, ks]
            scores = scores + _dot_nt((qh * zl).astype(BF16), (kh * zl).astype(BF16)) * masks_ref[l]
        o = _dot(scores.astype(BF16), vh)
        o = o + _dot_nt((qh * z_cum[:, ks]).astype(BF16), st[:, ks].astype(BF16))
        outs.append(o)
        upd = _dot_tn(vh, (kh * z_end[:, ks]).astype(BF16))
        st_ref[:, ks] = st[:, ks] * z_cum[CS - 1:CS, ks] + upd
    return outs


def _even_prompt_kernel(*refs):
    z_refs = refs[0:SEQ_PER_STEP]
    zgr_refs = refs[SEQ_PER_STEP:2 * SEQ_PER_STEP]
    (lbp_ref, wgk_ref, bgk_ref, gnh_ref, gng_ref, sel_ref, masks_ref,
     y_ref, shg_ref, sgla_ref, st_hg, st_gla) = refs[2 * SEQ_PER_STEP:]
    c = pl.program_id(1)

    @pl.when(c == 0)
    def _():
        st_hg[...] = jnp.zeros_like(st_hg)
        st_gla[...] = jnp.zeros_like(st_gla)

    sel = sel_ref[...]
    p = lbp_ref[...]
    pe = jnp.exp(p - jnp.max(p, axis=0, keepdims=True))
    lb = pe[0:1] / jnp.sum(pe, axis=0, keepdims=True)

    for s in range(SEQ_PER_STEP):
        z = z_refs[s][...]
        hq, hf, hi, hg = z[:, 0:512], z[:, 512:1024], z[:, 1024:1536], z[:, 1536:2048]
        gq, gk, gv, gg = z[:, 2048:2304], z[:, 2304:2560], z[:, 2560:3072], z[:, 3072:3584]
        f = lb + (1.0 - lb) * _sigmoid(hf)
        k_hg = (1.0 - lb) * _sigmoid(-hf)
        o_hg = _gla_chunk(_silu(hq), k_hg, hi, jnp.log(f), st_hg.at[s], sel, masks_ref, HG_H, HG_DK, HG_DV)

        la = _log_sigmoid(_dot(zgr_refs[s][...].astype(BF16), wgk_ref[...]) + bgk_ref[...]) / GLA_GATE_NORM
        o_gla = _gla_chunk(gq * GLA_DK ** -0.5, gk, gv, la, st_gla.at[s], sel, masks_ref, GLA_H, GLA_DK, GLA_DV)

        for h in range(HG_H):
            cs = slice(h * 128, (h + 1) * 128)
            y_ref[s, :, cs] = _rms_gate(o_hg[h], hg[:, cs], gnh_ref[...]).astype(BF16)
        for h in range(GLA_H):
            cs = slice(h * 128, (h + 1) * 128)
            y_ref[s, :, 512 + h * 128:512 + (h + 1) * 128] = _rms_gate(o_gla[h], gg[:, cs], gng_ref[...]).astype(BF16)

    @pl.when(c == NCHUNK - 1)
    def _():
        for s in range(SEQ_PER_STEP):
            shg_ref[s] = st_hg[s].T
            sgla_ref[s] = st_gla[s].T


def _seq_row_specs(width):
    return [pl.BlockSpec((CS, width), functools.partial(lambda b, c, s: ((SEQ_PER_STEP * b + s) * NCHUNK + c, 0), s=s))
            for s in range(SEQ_PER_STEP)]


def _even_prompt(z, zgr, lbp, wgk, bgk, gnh, gng, sel, masks):
    const2 = lambda b, c: (0, 0)
    sp = SEQ_PER_STEP
    return pl.pallas_call(
        _even_prompt_kernel,
        grid=(BATCH // sp, NCHUNK),
        in_specs=_seq_row_specs(EVEN_MAIN) + _seq_row_specs(LANES) + [
            pl.BlockSpec(lbp.shape, const2),
            pl.BlockSpec(wgk.shape, const2),
            pl.BlockSpec(bgk.shape, const2),
            pl.BlockSpec(gnh.shape, const2),
            pl.BlockSpec(gng.shape, const2),
            pl.BlockSpec(sel.shape, const2),
            pl.BlockSpec(masks.shape, lambda b, c: (0, 0, 0)),
        ],
        out_specs=[
            pl.BlockSpec((sp, CS, D), lambda b, c: (b, c, 0)),
            pl.BlockSpec((sp, HG_H * HG_DK, HG_DV), lambda b, c: (b, 0, 0)),
            pl.BlockSpec((sp, GLA_H * GLA_DK, GLA_DV), lambda b, c: (b, 0, 0)),
        ],
        out_shape=[
            jax.ShapeDtypeStruct((BATCH, SEQ, D), BF16),
            jax.ShapeDtypeStruct((BATCH, HG_H * HG_DK, HG_DV), F32),
            jax.ShapeDtypeStruct((BATCH, GLA_H * GLA_DK, GLA_DV), F32),
        ],
        scratch_shapes=[pltpu.VMEM((sp, HG_DV, HG_H * HG_DK), F32), pltpu.VMEM((sp, GLA_DV, GLA_H * GLA_DK), F32)],
        compiler_params=_params(("parallel", "arbitrary")),
        name="even_prompt",
    )(*([z] * sp), *([zgr] * sp), lbp, wgk, bgk, gnh, gng, sel, masks)


def _even_sample_kernel(zr_ref, zt_ref, grt_ref, lbpt_ref, wgkt_ref, bgkt_ref, gnh_ref, gng_ref,
                        shg_ref, sgla_ref, y_ref, shg_out, sgla_out, o_scr):
    zt = zt_ref[0]
    hq_t, hf_t = zt[0:512], zt[512:1024]
    gq_t, gk_t = zt[2048:2304], zt[2304:2560]
    pt = lbpt_ref[...]
    pe = jnp.exp(pt - jnp.max(pt, axis=1, keepdims=True))
    lb = pe[:, 0:1] / jnp.sum(pe, axis=1, keepdims=True)
    a_hg = jnp.exp(jnp.log(lb + (1.0 - lb) * _sigmoid(hf_t)))
    k_hg = (1.0 - lb) * _sigmoid(-hf_t)
    q_hg = _silu(hq_t)
    la = _log_sigmoid(_dot(wgkt_ref[...], grt_ref[0].astype(BF16)) + bgkt_ref[...]) / GLA_GATE_NORM
    a_gla = jnp.exp(la)
    q_gla = gq_t * GLA_DK ** -0.5
    zr = zr_ref[...]
    hi, hg = zr[:, 1024:1536], zr[:, 1536:2048]
    gv, gg = zr[:, 2560:3072], zr[:, 3072:3584]

    for j in range(SG):
        for h in range(HG_H):
            ks = slice(h * HG_DK, (h + 1) * HG_DK)
            s_new = a_hg[ks, j:j + 1] * shg_ref[j, h] + k_hg[ks, j:j + 1] * hi[j:j + 1, h * 128:(h + 1) * 128]
            shg_out[j, h] = s_new
            o_scr[j:j + 1, h * 128:(h + 1) * 128] = jnp.sum(q_hg[ks, j:j + 1] * s_new, axis=0, keepdims=True)
        for h in range(GLA_H):
            ks = slice(h * GLA_DK, (h + 1) * GLA_DK)
            s_new = a_gla[ks, j:j + 1] * sgla_ref[j, h] + gk_t[ks, j:j + 1] * gv[j:j + 1, h * 128:(h + 1) * 128]
            sgla_out[j, h] = s_new
            o_scr[j:j + 1, 512 + h * 128:512 + (h + 1) * 128] = jnp.sum(
                q_gla[ks, j:j + 1] * s_new, axis=0, keepdims=True)

    o = o_scr[...]
    for h in range(HG_H):
        cs = slice(h * 128, (h + 1) * 128)
        y_ref[:, cs] = _rms_gate(o[:, cs], hg[:, cs], gnh_ref[...]).astype(BF16)
    for h in range(GLA_H):
        cs = slice(512 + h * 128, 512 + (h + 1) * 128)
        y_ref[:, cs] = _rms_gate(o[:, cs], gg[:, h * 128:(h + 1) * 128], gng_ref[...]).astype(BF16)


def _even_sample(z, zt3, grt3, lbpt, wgkt, bgkt, gnh, gng, s_hg, s_gla):
    c2 = lambda g: (0, 0)
    return pl.pallas_call(
        _even_sample_kernel,
        grid=(NS // SG,),
        in_specs=[
            pl.BlockSpec((SG, EVEN_MAIN), lambda g: (NP // SG + g, 0)),
            pl.BlockSpec((1, EVEN_MAIN, SG), lambda g: (g, 0, 0)),
            pl.BlockSpec((1, LANES, SG), lambda g: (g, 0, 0)),
            pl.BlockSpec(lbpt.shape, c2),
            pl.BlockSpec(wgkt.shape, c2),
            pl.BlockSpec(bgkt.shape, c2),
            pl.BlockSpec(gnh.shape, c2),
            pl.BlockSpec(gng.shape, c2),
            pl.BlockSpec((SG, HG_H, HG_DK, HG_DV), lambda g: (g, 0, 0, 0)),
            pl.BlockSpec((SG, GLA_H, GLA_DK, GLA_DV), lambda g: (g, 0, 0, 0)),
        ],
        out_specs=[
            pl.BlockSpec((SG, D), lambda g: (g, 0)),
            pl.BlockSpec((SG, HG_H, HG_DK, HG_DV), lambda g: (g, 0, 0, 0)),
            pl.BlockSpec((SG, GLA_H, GLA_DK, GLA_DV), lambda g: (g, 0, 0, 0)),
        ],
        out_shape=[
            jax.ShapeDtypeStruct((NS, D), BF16),
            jax.ShapeDtypeStruct((NS, HG_H, HG_DK, HG_DV), F32),
            jax.ShapeDtypeStruct((NS, GLA_H, GLA_DK, GLA_DV), F32),
        ],
        scratch_shapes=[pltpu.VMEM((SG, D), F32)],
        compiler_params=_params(("parallel",)),
        name="even_sample",
    )(z, zt3, grt3, lbpt, wgkt, bgkt, gnh, gng, s_hg, s_gla)


def _out_ln_kernel(x_ref, y_ref, w_ref, g_ref, b_ref, o_ref):
    r = ALPHA * x_ref[...] + _dot(y_ref[...], w_ref[...])
    o_ref[...] = _layernorm(r, g_ref[...], b_ref[...])


def _out_ln(x, y, w, g, b):
    c2 = lambda i: (0, 0)
    return pl.pallas_call(
        _out_ln_kernel,
        grid=(NT // TM,),
        in_specs=[
            pl.BlockSpec((TM, D), lambda i: (i, 0)),
            pl.BlockSpec((TM, D), lambda i: (i, 0)),
            pl.BlockSpec((D, D), c2),
            pl.BlockSpec((1, D), c2),
            pl.BlockSpec((1, D), c2),
        ],
        out_specs=pl.BlockSpec((TM, D), lambda i: (i, 0)),
        out_shape=jax.ShapeDtypeStruct((NT, D), F32),
        compiler_params=_params(("parallel",)),
        name="out_ln",
    )(x, y, w, g, b)


FF_SPLIT = 2


def _ffn_kernel(x_ref, w1_ref, w3_ref, w2_ref, g_ref, b_ref, o_ref):
    x = x_ref[...]
    xb = x.astype(BF16)
    step = D_FF_DENSE // FF_SPLIT
    acc = ALPHA * x
    for s in range(FF_SPLIT):
        cs = slice(s * step, (s + 1) * step)
        hmid = _silu(_dot(xb, w1_ref[:, cs])) * _dot(xb, w3_ref[:, cs])
        acc = acc + _dot(hmid.astype(BF16), w2_ref[cs, :])
    o_ref[...] = _layernorm(acc, g_ref[...], b_ref[...])


def _ffn(x, w1, w3, w2, g, b):
    c2 = lambda i: (0, 0)
    one = pl.Buffered(1)
    return pl.pallas_call(
        _ffn_kernel,
        grid=(NT // TM,),
        in_specs=[
            pl.BlockSpec((TM, D), lambda i: (i, 0)),
            pl.BlockSpec((D, D_FF_DENSE), c2, pipeline_mode=one),
            pl.BlockSpec((D, D_FF_DENSE), c2, pipeline_mode=one),
            pl.BlockSpec((D_FF_DENSE, D), c2, pipeline_mode=one),
            pl.BlockSpec((1, D), c2),
            pl.BlockSpec((1, D), c2),
        ],
        out_specs=pl.BlockSpec((TM, D), lambda i: (i, 0)),
        out_shape=jax.ShapeDtypeStruct((NT, D), F32),
        compiler_params=_params(("parallel",)),
        name="ffn_dense",
    )(x, w1, w3, w2, g, b)


def _mh_norm_gate(hh, o_pre, w):
    mu = jnp.mean(hh, axis=-1, keepdims=True)
    c = hh - mu
    var = jnp.mean(c * c, axis=-1, keepdims=True)
    return _sigmoid(o_pre) * (c * lax.rsqrt(var + EPS) * w)


def _odd_prompt_kernel(*refs):
    z_refs = refs[0:SEQ_PER_STEP]
    zg_refs = refs[SEQ_PER_STEP:2 * SEQ_PER_STEP]
    (bg_ref, cw_ref, cb_ref, hnw_ref, tri_ref,
     y_ref, c_out, n_out, m_out, conv_out,
     c_scr, n_scr, m_scr, u_scr) = refs[2 * SEQ_PER_STEP:]
    c = pl.program_id(1)

    @pl.when(c == 0)
    def _():
        c_scr[...] = jnp.zeros_like(c_scr)
        n_scr[...] = jnp.zeros_like(n_scr)
        m_scr[...] = jnp.zeros_like(m_scr)
        for s in range(SEQ_PER_STEP):
            u_scr[s, 0:8, :] = jnp.zeros((8, D), F32)

    row = lax.broadcasted_iota(jnp.int32, (CS, CS), 0)
    col = lax.broadcasted_iota(jnp.int32, (CS, CS), 1)
    causal = col <= row
    tails = []

    for s in range(SEQ_PER_STEP):
        z_ref = z_refs[s]
        u_scr[s, 8:8 + CS, :] = z_ref[:, 0:D]
        uc = cb_ref[...]
        for j in range(CONV_W):
            uc = uc + u_scr[s, 5 + j:5 + j + CS, :] * cw_ref[j:j + 1, :]
        tail = u_scr[s, CS:CS + 8, :]
        u_scr[s, 0:8, :] = tail
        tails.append(tail)
        act = _silu(uc)
        q = act[:, 0:512] * ML_DK ** -0.5
        k = act[:, 512:1024]
        v = z_ref[:, D:2 * D]
        o_pre = z_ref[:, 2 * D:3 * D]

        gates = zg_refs[s][...] + bg_ref[...]
        lf = _log_sigmoid(gates)
        bcum = _dot_sel(tri_ref[...], lf)
        bcum_t = bcum.T
        gates_t = gates.T
        m_all = m_scr[s]

        for h in range(ML_H):
            ks = slice(h * ML_DK, (h + 1) * ML_DK)
            vs = slice(h * ML_DV, (h + 1) * ML_DV)
            qh, kh = q[:, ks], k[:, ks]
            vh = v[:, vs].astype(BF16)
            b_col = bcum[:, 4 + h:5 + h]
            b_row = bcum_t[4 + h:5 + h, :]
            i_col = gates[:, h:h + 1]
            i_row = gates_t[h:h + 1, :]
            m_prev = m_all[:, h:h + 1]
            log_d = jnp.where(causal, b_col - b_row + i_row, -jnp.inf)
            log_prev = b_col + m_prev
            m_t = jnp.maximum(jnp.max(log_d, axis=-1, keepdims=True), log_prev)
            d = jnp.exp(log_d - m_t)
            w_prev = jnp.exp(log_prev - m_t)
            scores = _dot_nt(qh.astype(BF16), kh.astype(BF16)) * d
            c_h = c_scr[s, h]
            n_h = n_scr[s, h:h + 1, :]
            num = _dot(scores.astype(BF16), vh) + w_prev * _dot(qh.astype(BF16), c_h.astype(BF16))
            den = jnp.sum(scores, axis=-1, keepdims=True) + w_prev * jnp.sum(qh * n_h, axis=-1, keepdims=True)
            hh = num / jnp.maximum(jnp.abs(den), jnp.exp(-m_t))
            m_new = m_t[CS - 1:CS, :]
            b_last = b_col[CS - 1:CS, :]
            w_c = jnp.exp(b_last + m_prev - m_new)
            w_s = jnp.exp(b_last - b_col + i_col - m_new)
            kw = kh * w_s
            c_scr[s, h] = w_c * c_h + _dot_tn(kw.astype(BF16), vh)
            n_scr[s, h:h + 1, :] = w_c * n_h + jnp.sum(kw, axis=0, keepdims=True)
            m_scr[s, :, h:h + 1] = m_new
            y_ref[s, :, vs] = _mh_norm_gate(hh, o_pre[:, vs], hnw_ref[:, vs]).astype(BF16)

    @pl.when(c == NCHUNK - 1)
    def _():
        c_out[...] = c_scr[...]
        for s in range(SEQ_PER_STEP):
            n_out[s] = n_scr[s, 0:ML_H, :]
            m_out[s] = m_scr[s]
            conv_out[s] = tails[s][8 - (CONV_W - 1):8, :]


def _odd_prompt(z, zg, bg, cw, cb, hnw, tri):
    c2 = lambda b, c: (0, 0)
    sp = SEQ_PER_STEP
    return pl.pallas_call(
        _odd_prompt_kernel,
        grid=(BATCH // sp, NCHUNK),
        in_specs=_seq_row_specs(ODD_MAIN) + _seq_row_specs(LANES) + [
            pl.BlockSpec((1, LANES), c2),
            pl.BlockSpec((CONV_W, D), c2),
            pl.BlockSpec((1, D), c2),
            pl.BlockSpec((1, D), c2),
            pl.BlockSpec((CS, CS), c2),
        ],
        out_specs=[
            pl.BlockSpec((sp, CS, D), lambda b, c: (b, c, 0)),
            pl.BlockSpec((sp, ML_H, ML_DK, ML_DV), lambda b, c: (b, 0, 0, 0)),
            pl.BlockSpec((sp, ML_H, ML_DK), lambda b, c: (b, 0, 0)),
            pl.BlockSpec((sp, 1, LANES), lambda b, c: (b, 0, 0)),
            pl.BlockSpec((sp, CONV_W - 1, D), lambda b, c: (b, 0, 0)),
        ],
        out_shape=[
            jax.ShapeDtypeStruct((BATCH, SEQ, D), BF16),
            jax.ShapeDtypeStruct((BATCH, ML_H, ML_DK, ML_DV), F32),
            jax.ShapeDtypeStruct((BATCH, ML_H, ML_DK), F32),
            jax.ShapeDtypeStruct((BATCH, 1, LANES), F32),
            jax.ShapeDtypeStruct((BATCH, CONV_W - 1, D), F32),
        ],
        scratch_shapes=[
            pltpu.VMEM((sp, ML_H, ML_DK, ML_DV), F32),
            pltpu.VMEM((sp, 8, ML_DK), F32),
            pltpu.VMEM((sp, 1, LANES), F32),
            pltpu.VMEM((sp, CS + 8, D), F32),
        ],
        compiler_params=_params(("parallel", "arbitrary")),
        name="odd_prompt",
    )(*([z] * sp), *([zg] * sp), bg, cw, cb, hnw, tri)


def _odd_sample_kernel(zr_ref, zg_ref, ut_ref, conv_ref, convt_ref, bg_ref, cw_ref, cwt_ref, cb_ref, cbt_ref,
                       hnw_ref, c_ref, n_ref, m_ref,
                       y_ref, c_out, n_out, m_out, conv_out, h_scr):
    zr = zr_ref[...]
    u = zr[:, 0:D]
    v = zr[:, D:2 * D]
    o_pre = zr[:, 2 * D:3 * D]
    uc = cb_ref[...] + u * cw_ref[CONV_W - 1:CONV_W, :]
    uc_t = cbt_ref[...] + ut_ref[0] * cwt_ref[:, CONV_W - 1:CONV_W]
    for j in range(CONV_W - 1):
        uc = uc + conv_ref[:, j * D:(j + 1) * D] * cw_ref[j:j + 1, :]
        uc_t = uc_t + convt_ref[0, j] * cwt_ref[:, j:j + 1]
        conv_out[:, j * D:(j + 1) * D] = conv_ref[:, (j + 1) * D:(j + 2) * D] if j + 1 < CONV_W - 1 else u
    act = _silu(uc)
    k_row = act[:, 512:1024]
    act_t = _silu(uc_t)
    q_t = act_t[0:512] * ML_DK ** -0.5
    k_t = act_t[512:1024]
    q_row = act[:, 0:512] * ML_DK ** -0.5

    gates = zg_ref[...] + bg_ref[...]
    lf = _log_sigmoid(gates)
    m_in = m_ref[...]
    m_out[...] = m_in

    for j in range(SG):
        for h in range(ML_H):
            ks = slice(h * ML_DK, (h + 1) * ML_DK)
            vs = slice(h * ML_DV, (h + 1) * ML_DV)
            ig = gates[j:j + 1, h:h + 1]
            log_prev = lf[j:j + 1, 4 + h:5 + h] + m_in[j:j + 1, h:h + 1]
            m_t = jnp.maximum(ig, log_prev)
            d = jnp.exp(ig - m_t)
            w_prev = jnp.exp(log_prev - m_t)
            c_new = w_prev * c_ref[j, h] + (d * k_t[ks, j:j + 1]) * v[j:j + 1, vs]
            n_new = w_prev * n_ref[j, h:h + 1, :] + d * k_row[j:j + 1, ks]
            c_out[j, h] = c_new
            n_out[j, h:h + 1, :] = n_new
            m_out[j:j + 1, h:h + 1] = m_t
            num = jnp.sum(q_t[ks, j:j + 1] * c_new, axis=0, keepdims=True)
            den = jnp.sum(q_row[j:j + 1, ks] * n_new, axis=-1, keepdims=True)
            h_scr[j:j + 1, vs] = num / jnp.maximum(jnp.abs(den), jnp.exp(-m_t))

    hh = h_scr[...]
    for h in range(ML_H):
        vs = slice(h * ML_DV, (h + 1) * ML_DV)
        y_ref[:, vs] = _mh_norm_gate(hh[:, vs], o_pre[:, vs], hnw_ref[:, vs]).astype(BF16)


def _odd_sample(z, zg, ut3, conv, convt, bg, cw, cwt, cb, cbt, hnw, c_in, n_in, m_in):
    c2 = lambda g: (0, 0)
    return pl.pallas_call(
        _odd_sample_kernel,
        grid=(NS // SG,),
        in_specs=[
            pl.BlockSpec((SG, ODD_MAIN), lambda g: (NP // SG + g, 0)),
            pl.BlockSpec((SG, LANES), lambda g: (NP // SG + g, 0)),
            pl.BlockSpec((1, D, SG), lambda g: (g, 0, 0)),
            pl.BlockSpec((SG, (CONV_W - 1) * D), lambda g: (g, 0)),
            pl.BlockSpec((1, CONV_W - 1, D, SG), lambda g: (g, 0, 0, 0)),
            pl.BlockSpec((1, LANES), c2),
            pl.BlockSpec((CONV_W, D), c2),
            pl.BlockSpec((D, CONV_W), c2),
            pl.BlockSpec((1, D), c2),
            pl.BlockSpec((D, 1), c2),
            pl.BlockSpec((1, D), c2),
            pl.BlockSpec((SG, ML_H, ML_DK, ML_DV), lambda g: (g, 0, 0, 0)),
            pl.BlockSpec((SG, ML_H, ML_DK), lambda g: (g, 0, 0)),
            pl.BlockSpec((SG, LANES), lambda g: (g, 0)),
        ],
        out_specs=[
            pl.BlockSpec((SG, D), lambda g: (g, 0)),
            pl.BlockSpec((SG, ML_H, ML_DK, ML_DV), lambda g: (g, 0, 0, 0)),
            pl.BlockSpec((SG, ML_H, ML_DK), lambda g: (g, 0, 0)),
            pl.BlockSpec((SG, LANES), lambda g: (g, 0)),
            pl.BlockSpec((SG, (CONV_W - 1) * D), lambda g: (g, 0)),
        ],
        out_shape=[
            jax.ShapeDtypeStruct((NS, D), BF16),
            jax.ShapeDtypeStruct((NS, ML_H, ML_DK, ML_DV), F32),
            jax.ShapeDtypeStruct((NS, ML_H, ML_DK), F32),
            jax.ShapeDtypeStruct((NS, LANES), F32),
            jax.ShapeDtypeStruct((NS, (CONV_W - 1) * D), F32),
        ],
        scratch_shapes=[pltpu.VMEM((SG, D), F32)],
        compiler_params=_params(("parallel",)),
        name="odd_sample",
    )(z, zg, ut3, conv, convt, bg, cw, cwt, cb, cbt, hnw, c_in, n_in, m_in)


def _out_ln_router_kernel(x_ref, y_ref, w_ref, g_ref, b_ref, wr_ref, tri_ref,
                          o_ref, op_ref, meta_ref, cnt_ref, carry):
    i = pl.program_id(0)

    @pl.when(i == 0)
    def _():
        carry[...] = jnp.zeros_like(carry)

    r = ALPHA * x_ref[...] + _dot(y_ref[...], w_ref[...])
    x3 = _layernorm(r, g_ref[...], b_ref[...])
    o_ref[...] = x3
    op_ref[...] = pltpu.pack_elementwise([x3[:, :D // 2], x3[:, D // 2:]], packed_dtype=BF16)

    lane = lax.broadcasted_iota(jnp.int32, (TM, LANES), 1).astype(F32)
    logits = jnp.where(lane < N_EXPERTS, _dot(x3.astype(BF16), wr_ref[...]), -jnp.inf)
    m1 = jnp.max(logits, axis=-1, keepdims=True)
    i1 = jnp.min(jnp.where(logits == m1, lane, float(LANES)), axis=-1, keepdims=True)
    rest = jnp.where(lane == i1, -jnp.inf, logits)
    m2 = jnp.max(rest, axis=-1, keepdims=True)
    i2 = jnp.min(jnp.where(rest == m2, lane, float(LANES)), axis=-1, keepdims=True)
    e2 = jnp.exp(m2 - m1)
    tot = 1.0 + e2
    w1 = 1.0 / tot
    w2 = e2 / tot

    sel1 = lane == i1
    sel2 = lane == i2
    onehot = jnp.where(sel1 | sel2, 1.0, 0.0)
    before = _dot(tri_ref[...], onehot.astype(BF16)) + carry[...]
    r1 = jnp.sum(jnp.where(sel1, before, 0.0), axis=-1, keepdims=True)
    r2 = jnp.sum(jnp.where(sel2, before, 0.0), axis=-1, keepdims=True)
    carry[...] = carry[...] + jnp.sum(onehot, axis=0, keepdims=True)
    cnt_ref[...] = carry[...]

    meta = jnp.where(lane == 0.0, i1, 0.0)
    meta = jnp.where(lane == 1.0, i2, meta)
    meta = jnp.where(lane == 2.0, w1, meta)
    meta = jnp.where(lane == 3.0, w2, meta)
    meta = jnp.where(lane == 4.0, r1, meta)
    meta = jnp.where(lane == 5.0, r2, meta)
    meta_ref[...] = meta


def _out_ln_router(x, y, w, g, b, wr, tri):
    c2 = lambda i: (0, 0)
    return pl.pallas_call(
        _out_ln_router_kernel,
        grid=(NT // TM,),
        in_specs=[
            pl.BlockSpec((TM, D), lambda i: (i, 0)),
            pl.BlockSpec((TM, D), lambda i: (i, 0)),
            pl.BlockSpec((D, D), c2),
            pl.BlockSpec((1, D), c2),
            pl.BlockSpec((1, D), c2),
            pl.BlockSpec((D, LANES), c2),
            pl.BlockSpec((TM, TM), c2),
        ],
        out_specs=[
            pl.BlockSpec((TM, D), lambda i: (i, 0)),
            pl.BlockSpec((TM, D // 2), lambda i: (i, 0)),
            pl.BlockSpec((TM, LANES), lambda i: (i, 0)),
            pl.BlockSpec((1, LANES), c2),
        ],
        out_shape=[
            jax.ShapeDtypeStruct((NT, D), F32),
            jax.ShapeDtypeStruct((NT, D // 2), U32),
            jax.ShapeDtypeStruct((NT, LANES), F32),
            jax.ShapeDtypeStruct((1, LANES), F32),
        ],
        scratch_shapes=[pltpu.VMEM((1, LANES), F32)],
        compiler_params=_params(("arbitrary",)),
        name="out_ln_router",
    )(x, y, w, g, b, wr, tri)


def _moe_ffn_kernel(te_ref, nu_ref, gnext_ref, gcur_ref, sprev_ref, scur_ref, xp_ref, w1_ref, w3_ref, w2_ref,
                    out_hbm, stage, yacc, xb_scr, sem_s):
    i = pl.program_id(0)
    j = pl.program_id(1)
    used = i < nu_ref[0]
    slot = i % 2
    other = 1 - slot
    rps = MOE_ROWS_PER_STEP

    def gather_rows(tab_ref, buf, part):
        for r in range(rps):
            stage[buf, part, pl.ds(r, 1), :] = xp_ref[pl.ds(tab_ref[part * rps + r], 1), :]

    def scatter(buf, r, dst):
        return pltpu.make_async_copy(yacc.at[buf, pl.ds(r, 1)], out_hbm.at[pl.ds(dst, 1)], sem_s)

    def wait_scatters(n):
        for _ in range(n):
            scatter(0, 0, 0).wait()

    def issue_neighbours():
        gather_rows(gnext_ref, other, j)
        for r in range(rps):
            rr = j * rps + r
            scatter(other, rr, sprev_ref[rr]).start()

    @pl.when(j == 0)
    def _():
        @pl.when(i == 0)
        def _():
            yacc[1] = jnp.zeros((TMM, D), F32)
            for part in range(MOE_NFF):
                gather_rows(gcur_ref, 0, part)

        @pl.when(i > 0)
        def _():
            wait_scatters(TMM)

        words = stage[slot].reshape(TMM, D // 2)
        for half in range(2):
            xb_scr[:, half * (D // 2):(half + 1) * (D // 2)] = pltpu.unpack_elementwise(
                words, index=half, packed_dtype=BF16, unpacked_dtype=F32).astype(BF16)
        yacc[slot] = jnp.zeros((TMM, D), F32)

    @pl.when(used)
    def _():
        issue_neighbours()
        xb = xb_scr[...]
        hmid = _silu(_dot(xb, w1_ref[...])) * _dot(xb, w3_ref[...])
        yacc[slot] += _dot(hmid.astype(BF16), w2_ref[...])

    @pl.when(jnp.logical_not(used))
    def _():
        issue_neighbours()

    @pl.when((i == N_MOE_TILES - 1) & (j == MOE_NFF - 1))
    def _():
        for r in range(TMM):
            scatter(slot, r, scur_ref[r]).start()
        wait_scatters(2 * TMM)


def _moe_ffn(tile_expert, n_used, gsrc, sdst, xp, w1, w3, w2):
    nff = MOE_NFF

    def wcol(i, j, te, nu):
        return (te[i], 0, jnp.where(i < nu[0], j, nff - 1))

    def wrow(i, j, te, nu):
        return (te[i], jnp.where(i < nu[0], j, nff - 1), 0)

    smem = functools.partial(pl.BlockSpec, (MOE_TAB,), memory_space=pltpu.SMEM)
    grid_spec = pltpu.PrefetchScalarGridSpec(
        num_scalar_prefetch=2,
        grid=(N_MOE_TILES, nff),
        in_specs=[
            smem(lambda i, j, te, nu: (i + 1,)),
            smem(lambda i, j, te, nu: (i,)),
            smem(lambda i, j, te, nu: (i,)),
            smem(lambda i, j, te, nu: (i + 1,)),
            pl.BlockSpec((NT, D // 2), lambda i, j, te, nu: (0, 0), pipeline_mode=pl.Buffered(1)),
            pl.BlockSpec((None, D, TFF), wcol),
            pl.BlockSpec((None, D, TFF), wcol),
            pl.BlockSpec((None, TFF, D), wrow),
        ],
        out_specs=pl.BlockSpec(memory_space=pl.ANY),
        scratch_shapes=[
            pltpu.VMEM((2, MOE_NFF, MOE_ROWS_PER_STEP, D // 2), U32),
            pltpu.VMEM((2, TMM, D), F32),
            pltpu.VMEM((TMM, D), BF16),
            pltpu.SemaphoreType.DMA(()),
        ],
    )
    return pl.pallas_call(
        _moe_ffn_kernel,
        grid_spec=grid_spec,
        out_shape=jax.ShapeDtypeStruct((MOE_OUT_ROWS, D), F32),
        compiler_params=_params(("arbitrary", "arbitrary")),
        name="moe_ffn",
    )(tile_expert, n_used, gsrc, gsrc, sdst, sdst, xp, w1, w3, w2)


def _combine_kernel(x_ref, meta_ref, y0_ref, y1_ref, g_ref, b_ref, o_ref):
    meta = meta_ref[...]
    moe = meta[:, 2:3] * y0_ref[...] + meta[:, 3:4] * y1_ref[...]
    o_ref[...] = _layernorm(ALPHA * x_ref[...] + moe, g_ref[...], b_ref[...])


def _combine(x, meta, ys, g, b):
    c2 = lambda i: (0, 0)
    return pl.pallas_call(
        _combine_kernel,
        grid=(NT // TM,),
        in_specs=[
            pl.BlockSpec((TM, D), lambda i: (i, 0)),
            pl.BlockSpec((TM, LANES), lambda i: (i, 0)),
            pl.BlockSpec((TM, D), lambda i: (i, 0)),
            pl.BlockSpec((TM, D), lambda i: (i + NT // TM, 0)),
            pl.BlockSpec((1, D), c2),
            pl.BlockSpec((1, D), c2),
        ],
        out_specs=pl.BlockSpec((TM, D), lambda i: (i, 0)),
        out_shape=jax.ShapeDtypeStruct((NT, D), F32),
        compiler_params=_params(("parallel",)),
        name="moe_combine",
    )(x, meta, ys, ys, g, b)


def _pad_cols(w, n):
    return jnp.pad(w, ((0, 0), (0, n - w.shape[1])))


def kernel(x_prompt, x_sample, state_hgrn, state_gla, state_mlstm_C, state_mlstm_n, state_mlstm_m,
           state_mlstm_conv, w_in_even, hg_lower_bounds, w_gk, b_gk, gn_hg, gn_gla, w_out_even,
           w1_dense, w3_dense, w2_dense, w_in_odd, b_gate_odd, conv_w, conv_b, hn_w, w_out_odd,
           w_router, w1_moe, w3_moe, w2_moe, ln1_g, ln1_b, ln2_g, ln2_b):
    assert x_prompt.shape == (BATCH, SEQ, D) and x_sample.shape == (NS, 1, D)
    assert w_in_even.shape[0] == 1 and w_in_odd.shape[0] == 1 and hg_lower_bounds.shape[0] == 2
    sel_np, masks_np = _gla_matrices()
    sel = jnp.asarray(sel_np, BF16)
    masks = jnp.asarray(masks_np, F32)
    tri_cs = jnp.asarray(_tri(CS, False), BF16)
    tri_tm = jnp.asarray(_tri(TM, True), BF16)
    row = lambda a: a.reshape(1, -1)

    x0 = jnp.concatenate([x_prompt.reshape(NP, D), x_sample.reshape(NS, D)], axis=0)

    w_even = w_in_even[0].astype(BF16)
    z, zgr = _proj(x0, w_even[:, :EVEN_MAIN], _pad_cols(w_even[:, EVEN_MAIN:], LANES))
    wgk = jnp.pad(w_gk[0].astype(BF16), ((0, LANES - GLA_RANK), (0, 0)))
    lbp = hg_lower_bounds
    y_p, hg_p, gla_p = _even_prompt(z, zgr, lbp, wgk, row(b_gk[0]), row(gn_hg[0]), row(gn_gla[0]), sel, masks)

    zs = z[NP:].reshape(NS // SG, SG, EVEN_MAIN).transpose(0, 2, 1)
    grs = zgr[NP:].reshape(NS // SG, SG, LANES).transpose(0, 2, 1)
    y_s, hg_s, gla_s = _even_sample(z, zs, grs, lbp.T, wgk.T, b_gk[0].reshape(-1, 1),
                                    row(gn_hg[0]), row(gn_gla[0]), state_hgrn[0], state_gla[0])
    y = jnp.concatenate([y_p.reshape(NP, D), y_s], axis=0)
    x1 = _out_ln(x0, y, w_out_even[0].astype(BF16), row(ln1_g[0]), row(ln1_b[0]))
    x2 = _ffn(x1, w1_dense[0].astype(BF16), w3_dense[0].astype(BF16), w2_dense[0].astype(BF16),
              row(ln2_g[0]), row(ln2_b[0]))

    w_odd = w_in_odd[0].astype(BF16)
    zo, zog = _proj(x2, w_odd[:, :ODD_MAIN], _pad_cols(w_odd[:, ODD_MAIN:], LANES))
    bg = jnp.pad(b_gate_odd[0], (0, LANES - 2 * ML_H)).reshape(1, LANES)
    yo_p, c_p, n_p, m_p, conv_p = _odd_prompt(zo, zog, bg, conv_w[0], row(conv_b[0]), row(hn_w[0]), tri_cs)

    ut = zo[NP:, :D].reshape(NS // SG, SG, D).transpose(0, 2, 1)
    conv_in = state_mlstm_conv[0]
    conv_t = conv_in.reshape(NS // SG, SG, CONV_W - 1, D).transpose(0, 2, 3, 1)
    m_in = jnp.pad(state_mlstm_m[0], ((0, 0), (0, LANES - ML_H)))
    yo_s, c_s, n_s, m_s, conv_s = _odd_sample(
        zo, zog, ut, conv_in.reshape(NS, (CONV_W - 1) * D), conv_t, bg, conv_w[0], conv_w[0].T, row(conv_b[0]), conv_b[0].reshape(-1, 1),
        row(hn_w[0]), state_mlstm_C[0], state_mlstm_n[0], m_in)
    yo = jnp.concatenate([yo_p.reshape(NP, D), yo_s], axis=0)

    wr = _pad_cols(w_router[0].astype(BF16), LANES)
    x3, x3p, meta, cnt = _out_ln_router(x2, yo, w_out_odd[0].astype(BF16), row(ln1_g[1]), row(ln1_b[1]), wr, tri_tm)

    counts = cnt[0, :N_EXPERTS].astype(jnp.int32)
    padded = ((counts + TMM - 1) // TMM) * TMM
    ends = jnp.cumsum(padded)
    offsets = ends - padded
    idx = meta[:, 0:2].astype(jnp.int32)
    pos = offsets[idx] + meta[:, 4:6].astype(jnp.int32)
    tile_start = jnp.arange(N_MOE_TILES, dtype=jnp.int32) * TMM
    tile_expert = jnp.minimum(jnp.sum(tile_start[:, None] >= ends[None, :], axis=1), N_EXPERTS - 1).astype(jnp.int32)
    n_used = (ends[-1] // TMM).astype(jnp.int32).reshape(1)

    slot = jnp.arange(MOE_SLOTS, dtype=jnp.int32)
    e_slot = jnp.sum(slot[:, None] >= ends[None, :], axis=1)
    off_x = jnp.concatenate([offsets, ends[-1:]])
    cnt_x = jnp.concatenate([counts, jnp.zeros((1,), jnp.int32)])
    real_before_x = jnp.concatenate([jnp.cumsum(counts) - counts, jnp.full((1,), 2 * NT, jnp.int32)])
    local = slot - off_x[e_slot]
    real_before = real_before_x[e_slot] + jnp.minimum(local, cnt_x[e_slot])
    spill_row = 2 * NT + TMM + (slot - real_before)
    token = jnp.arange(NT, dtype=jnp.int32)
    dst_rows = jnp.stack([token, NT + token], axis=1)
    dst = spill_row.at[pos.reshape(-1)].set(dst_rows.reshape(-1), unique_indices=True)
    per_tile = lambda t: jnp.pad(t.reshape(-1, TMM), ((0, 0), (0, MOE_TAB - TMM))).reshape(-1)
    sdst = per_tile(jnp.concatenate([2 * NT + jnp.arange(TMM, dtype=jnp.int32), dst]))
    src_tok = jnp.where(dst < NT, dst, jnp.where(dst < 2 * NT, dst - NT, 0))
    gsrc = per_tile(jnp.concatenate([src_tok, jnp.zeros((TMM,), jnp.int32)]))

    ys = _moe_ffn(tile_expert, n_used, gsrc, sdst, x3p,
                  w1_moe[0].astype(BF16), w3_moe[0].astype(BF16), w2_moe[0].astype(BF16))
    out = _combine(x3, meta, ys, row(ln2_g[1]), row(ln2_b[1]))

    y_prompt = out[:NP].reshape(BATCH, SEQ, D)
    y_sample = out[NP:].reshape(NS, 1, D)
    return (y_prompt, y_sample,
            hg_p.reshape(1, BATCH, HG_H, HG_DK, HG_DV), gla_p.reshape(1, BATCH, GLA_H, GLA_DK, GLA_DV),
            c_p[None], n_p[None], m_p[:, 0, :ML_H][None], conv_p[None],
            hg_s[None], gla_s[None], c_s[None], n_s[None], m_s[:, :ML_H][None], conv_s.reshape(1, NS, CONV_W - 1, D))
```

```python
import functools
import math

import jax
import jax.numpy as jnp
import numpy as np
from jax import lax
from jax.experimental import pallas as pl
from jax.experimental.pallas import tpu as pltpu

F32 = jnp.float32
BF16 = jnp.bfloat16
U32 = jnp.uint32

D = 1024
BATCH = 8
SEQ = 2048
DEC_BATCH = 128
NP = BATCH * SEQ
NS = DEC_BATCH
NT = NP + NS
HG_H, HG_DK, HG_DV = 4, 128, 128
GLA_H, GLA_DK, GLA_DV = 4, 64, 128
GLA_RANK = 16
GLA_GATE_NORM = 16.0
ML_H, ML_DK, ML_DV = 4, 128, 256
CONV_W = 4
D_FF_DENSE = 2816
D_FF_EXPERT = 3584
N_EXPERTS = 8
EPS = 1e-5
DEPTH = 2
ALPHA = (2.0 * DEPTH) ** 0.25
EVEN_MAIN = 3584
ODD_MAIN = 3072

LANES = 128
SUBLANES = 8
VMEM_LIMIT = 56 * 1024 * 1024

TM = 384
CS = 128
NCHUNK = SEQ // CS
SEQ_PER_STEP = 1
SG = 16
TMM = 512
TFF = 896
MOE_TAB = 512
MOE_NFF = D_FF_EXPERT // TFF
MOE_ROWS_PER_STEP = TMM // MOE_NFF
N_MOE_TILES = -(-(2 * NT + N_EXPERTS * (TMM - 1)) // TMM)
MOE_SLOTS = N_MOE_TILES * TMM
MOE_OUT_ROWS = MOE_SLOTS + TMM
N_LEVELS = int(math.log2(CS))

assert NT % TM == 0 and NP % CS == 0 and NS % SG == 0 and D_FF_EXPERT % TFF == 0 and TMM % MOE_NFF == 0


def _params(sem, limit=VMEM_LIMIT):
    return pltpu.CompilerParams(dimension_semantics=sem, vmem_limit_bytes=limit)


def _dot(a, b):
    return jnp.dot(a, b, preferred_element_type=F32)


def _dot_nt(a, b):
    return lax.dot_general(a, b, (((1,), (1,)), ((), ())), preferred_element_type=F32)


def _dot_tn(a, b):
    return lax.dot_general(a, b, (((0,), (0,)), ((), ())), preferred_element_type=F32)


def _split3(x):
    hi = x.astype(BF16)
    r1 = x - hi.astype(F32)
    mid = r1.astype(BF16)
    lo = (r1 - mid.astype(F32)).astype(BF16)
    return hi, mid, lo


def _dot_sel(sel, x):
    hi, mid, lo = _split3(x)
    return _dot(sel, hi) + _dot(sel, mid) + _dot(sel, lo)


def _sigmoid(x):
    return jax.nn.sigmoid(x)


def _silu(x):
    return x * jax.nn.sigmoid(x)


def _log_sigmoid(x):
    return jnp.minimum(x, 0.0) - jnp.log(1.0 + jnp.exp(-jnp.abs(x)))


def _layernorm(r, g, b):
    mu = jnp.mean(r, axis=-1, keepdims=True)
    c = r - mu
    var = jnp.mean(c * c, axis=-1, keepdims=True)
    return c * lax.rsqrt(var + EPS) * g + b


def _gla_masks():
    masks = np.zeros((N_LEVELS + 1, CS, CS), np.float32)
    for t in range(CS):
        for l in range(N_LEVELS):
            half = 1 << l
            start = (t // (2 * half)) * (2 * half)
            mid = start + half
            if t >= mid:
                masks[l, t, start:mid] = 1.0
        masks[N_LEVELS, t, t] = 1.0
    return masks


def _tri(n, strict):
    return np.tril(np.ones((n, n), np.float32), -1 if strict else 0)


def _proj_kernel(x_ref, wa_ref, wb_ref, oa_ref, ob_ref):
    xb = x_ref[...].astype(BF16)
    oa_ref[...] = _dot(xb, wa_ref[...])
    ob_ref[...] = _dot(xb, wb_ref[...])


def _proj(x, wa, wb):
    na, nb = wa.shape[1], wb.shape[1]
    return pl.pallas_call(
        _proj_kernel,
        grid=(NT // TM,),
        in_specs=[
            pl.BlockSpec((TM, D), lambda i: (i, 0)),
            pl.BlockSpec((D, na), lambda i: (0, 0)),
            pl.BlockSpec((D, nb), lambda i: (0, 0)),
        ],
        out_specs=[
            pl.BlockSpec((TM, na), lambda i: (i, 0)),
            pl.BlockSpec((TM, nb), lambda i: (i, 0)),
        ],
        out_shape=[jax.ShapeDtypeStruct((NT, na), F32), jax.ShapeDtypeStruct((NT, nb), F32)],
        compiler_params=_params(("parallel",)),
        name="proj",
    )(x, wa, wb)


def _rms_gate(o, gate, w):
    o = o * lax.rsqrt(jnp.mean(o * o, axis=-1, keepdims=True) + EPS) * w
    return o * _silu(gate)


def _level_decays(g, bc):
    width = g.shape[1]
    ng = CS // SUBLANES
    shape3 = (ng, SUBLANES, width)
    bc3 = bc.reshape(shape3)
    sub = lax.broadcasted_iota(jnp.int32, shape3, 1)

    def group_row(s):
        return jnp.broadcast_to(bc3[:, s:s + 1, :], shape3)

    last = group_row(SUBLANES - 1)
    refs = [None,
            jnp.where(sub < 4, group_row(1), group_row(5)),
            group_row(3)]
    for l in range(3, N_LEVELS):
        per_block = 1 << (l - 2)
        grouped = last.reshape(ng // per_block, per_block, SUBLANES, width)
        ref = jnp.broadcast_to(grouped[:, per_block // 2 - 1:per_block // 2], grouped.shape)
        refs.append(ref.reshape(shape3))
    decays = [jnp.exp(jnp.where((sub & 1) == 1, g.reshape(shape3), 0.0))]
    decays += [jnp.exp(-jnp.abs(bc3 - ref)) for ref in refs[1:]]
    to_end = jnp.exp(jnp.broadcast_to(last[ng - 1:ng], shape3) - bc3)
    return [d.reshape(CS, width) for d in decays], to_end.reshape(CS, width)


def _gla_chunk(q, k, v, g, st_ref, tri, masks_ref, heads, dk, dv):
    bc = _dot_sel(tri, g)
    z_levels, z_end = _level_decays(g, bc)
    z_cum = jnp.exp(bc)
    st = st_ref[...]
    outs = []
    for h in range(heads):
        ks = slice(h * dk, (h + 1) * dk)
        vs = slice(h * dv, (h + 1) * dv)
        qh, kh = q[:, ks], k[:, ks]
        vh = v[:, vs].astype(BF16)
        scores = _dot_nt(qh.astype(BF16), kh.astype(BF16)) * masks_ref[N_LEVELS]
        for l in range(N_LEVELS):
            zl = z_levels[l][:, ks]
            scores = scores + _dot_nt((qh * zl).astype(BF16), (kh * zl).astype(BF16)) * masks_ref[l]
        o = _dot(scores.astype(BF16), vh)
        o = o + _dot_nt((qh * z_cum[:, ks]).astype(BF16), st[:, ks].astype(BF16))
        outs.append(o)
        upd = _dot_tn(vh, (kh * z_end[:, ks]).astype(BF16))
        st_ref[:, ks] = st[:, ks] * z_cum[CS - 1:CS, ks] + upd
    return outs


def _even_prompt_kernel(*refs):
    z_refs = refs[0:SEQ_PER_STEP]
    zgr_refs = refs[SEQ_PER_STEP:2 * SEQ_PER_STEP]
    (lbp_ref, wgk_ref, bgk_ref, gnh_ref, gng_ref, tri_ref, masks_ref,
     y_ref, shg_ref, sgla_ref, st_hg, st_gla) = refs[2 * SEQ_PER_STEP:]
    c = pl.program_id(1)

    @pl.when(c == 0)
    def _():
        st_hg[...] = jnp.zeros_like(st_hg)
        st_gla[...] = jnp.zeros_like(st_gla)

    tri = tri_ref[...]
    p = lbp_ref[...]
    pe = jnp.exp(p - jnp.max(p, axis=0, keepdims=True))
    lb = pe[0:1] / jnp.sum(pe, axis=0, keepdims=True)

    for s in range(SEQ_PER_STEP):
        z = z_refs[s][...]
        hq, hf, hi, hg = z[:, 0:512], z[:, 512:1024], z[:, 1024:1536], z[:, 1536:2048]
        gq, gk, gv, gg = z[:, 2048:2304], z[:, 2304:2560], z[:, 2560:3072], z[:, 3072:3584]
        f = lb + (1.0 - lb) * _sigmoid(hf)
        k_hg = (1.0 - lb) * _sigmoid(-hf)
        o_hg = _gla_chunk(_silu(hq), k_hg, hi, jnp.log(f), st_hg.at[s], tri, masks_ref, HG_H, HG_DK, HG_DV)

        la = _log_sigmoid(_dot(zgr_refs[s][...].astype(BF16), wgk_ref[...]) + bgk_ref[...]) / GLA_GATE_NORM
        o_gla = _gla_chunk(gq * GLA_DK ** -0.5, gk, gv, la, st_gla.at[s], tri, masks_ref, GLA_H, GLA_DK, GLA_DV)

        for h in range(HG_H):
            cs = slice(h * 128, (h + 1) * 128)
            y_ref[s, :, cs] = _rms_gate(o_hg[h], hg[:, cs], gnh_ref[...]).astype(BF16)
        for h in range(GLA_H):
            cs = slice(h * 128, (h + 1) * 128)
            y_ref[s, :, 512 + h * 128:512 + (h + 1) * 128] = _rms_gate(o_gla[h], gg[:, cs], gng_ref[...]).astype(BF16)

    @pl.when(c == NCHUNK - 1)
    def _():
        for s in range(SEQ_PER_STEP):
            shg_ref[s] = st_hg[s].T
            sgla_ref[s] = st_gla[s].T


def _seq_row_specs(width):
    return [pl.BlockSpec((CS, width), functools.partial(lambda b, c, s: ((SEQ_PER_STEP * b + s) * NCHUNK + c, 0), s=s))
            for s in range(SEQ_PER_STEP)]


def _even_prompt(z, zgr, lbp, wgk, bgk, gnh, gng, tri, masks):
    const2 = lambda b, c: (0, 0)
    sp = SEQ_PER_STEP
    return pl.pallas_call(
        _even_prompt_kernel,
        grid=(BATCH // sp, NCHUNK),
        in_specs=_seq_row_specs(EVEN_MAIN) + _seq_row_specs(LANES) + [
            pl.BlockSpec(lbp.shape, const2),
            pl.BlockSpec(wgk.shape, const2),
            pl.BlockSpec(bgk.shape, const2),
            pl.BlockSpec(gnh.shape, const2),
            pl.BlockSpec(gng.shape, const2),
            pl.BlockSpec(tri.shape, const2),
            pl.BlockSpec(masks.shape, lambda b, c: (0, 0, 0)),
        ],
        out_specs=[
            pl.BlockSpec((sp, CS, D), lambda b, c: (b, c, 0)),
            pl.BlockSpec((sp, HG_H * HG_DK, HG_DV), lambda b, c: (b, 0, 0)),
            pl.BlockSpec((sp, GLA_H * GLA_DK, GLA_DV), lambda b, c: (b, 0, 0)),
        ],
        out_shape=[
            jax.ShapeDtypeStruct((BATCH, SEQ, D), BF16),
            jax.ShapeDtypeStruct((BATCH, HG_H * HG_DK, HG_DV), F32),
            jax.ShapeDtypeStruct((BATCH, GLA_H * GLA_DK, GLA_DV), F32),
        ],
        scratch_shapes=[pltpu.VMEM((sp, HG_DV, HG_H * HG_DK), F32), pltpu.VMEM((sp, GLA_DV, GLA_H * GLA_DK), F32)],
        compiler_params=_params(("parallel", "arbitrary")),
        name="even_prompt",
    )(*([z] * sp), *([zgr] * sp), lbp, wgk, bgk, gnh, gng, tri, masks)


def _even_sample_kernel(zr_ref, zt_ref, grt_ref, lbpt_ref, wgkt_ref, bgkt_ref, gnh_ref, gng_ref,
                        shg_ref, sgla_ref, y_ref, shg_out, sgla_out, o_scr):
    zt = zt_ref[0]
    hq_t, hf_t = zt[0:512], zt[512:1024]
    gq_t, gk_t = zt[2048:2304], zt[2304:2560]
    pt = lbpt_ref[...]
    pe = jnp.exp(pt - jnp.max(pt, axis=1, keepdims=True))
    lb = pe[:, 0:1] / jnp.sum(pe, axis=1, keepdims=True)
    a_hg = jnp.exp(jnp.log(lb + (1.0 - lb) * _sigmoid(hf_t)))
    k_hg = (1.0 - lb) * _sigmoid(-hf_t)
    q_hg = _silu(hq_t)
    la = _log_sigmoid(_dot(wgkt_ref[...], grt_ref[0].astype(BF16)) + bgkt_ref[...]) / GLA_GATE_NORM
    a_gla = jnp.exp(la)
    q_gla = gq_t * GLA_DK ** -0.5
    zr = zr_ref[...]
    hi, hg = zr[:, 1024:1536], zr[:, 1536:2048]
    gv, gg = zr[:, 2560:3072], zr[:, 3072:3584]

    for j in range(SG):
        for h in range(HG_H):
            ks = slice(h * HG_DK, (h + 1) * HG_DK)
            s_new = a_hg[ks, j:j + 1] * shg_ref[j, h] + k_hg[ks, j:j + 1] * hi[j:j + 1, h * 128:(h + 1) * 128]
            shg_out[j, h] = s_new
            o_scr[j:j + 1, h * 128:(h + 1) * 128] = jnp.sum(q_hg[ks, j:j + 1] * s_new, axis=0, keepdims=True)
        for h in range(GLA_H):
            ks = slice(h * GLA_DK, (h + 1) * GLA_DK)
            s_new = a_gla[ks, j:j + 1] * sgla_ref[j, h] + gk_t[ks, j:j + 1] * gv[j:j + 1, h * 128:(h + 1) * 128]
            sgla_out[j, h] = s_new
            o_scr[j:j + 1, 512 + h * 128:512 + (h + 1) * 128] = jnp.sum(
                q_gla[ks, j:j + 1] * s_new, axis=0, keepdims=True)

    o = o_scr[...]
    for h in range(HG_H):
        cs = slice(h * 128, (h + 1) * 128)
        y_ref[:, cs] = _rms_gate(o[:, cs], hg[:, cs], gnh_ref[...]).astype(BF16)
    for h in range(GLA_H):
        cs = slice(512 + h * 128, 512 + (h + 1) * 128)
        y_ref[:, cs] = _rms_gate(o[:, cs], gg[:, h * 128:(h + 1) * 128], gng_ref[...]).astype(BF16)


def _even_sample(z, zt3, grt3, lbpt, wgkt, bgkt, gnh, gng, s_hg, s_gla):
    c2 = lambda g: (0, 0)
    return pl.pallas_call(
        _even_sample_kernel,
        grid=(NS // SG,),
        in_specs=[
            pl.BlockSpec((SG, EVEN_MAIN), lambda g: (NP // SG + g, 0)),
            pl.BlockSpec((1, EVEN_MAIN, SG), lambda g: (g, 0, 0)),
            pl.BlockSpec((1, LANES, SG), lambda g: (g, 0, 0)),
            pl.BlockSpec(lbpt.shape, c2),
            pl.BlockSpec(wgkt.shape, c2),
            pl.BlockSpec(bgkt.shape, c2),
            pl.BlockSpec(gnh.shape, c2),
            pl.BlockSpec(gng.shape, c2),
            pl.BlockSpec((SG, HG_H, HG_DK, HG_DV), lambda g: (g, 0, 0, 0)),
            pl.BlockSpec((SG, GLA_H, GLA_DK, GLA_DV), lambda g: (g, 0, 0, 0)),
        ],
        out_specs=[
            pl.BlockSpec((SG, D), lambda g: (g, 0)),
            pl.BlockSpec((SG, HG_H, HG_DK, HG_DV), lambda g: (g, 0, 0, 0)),
            pl.BlockSpec((SG, GLA_H, GLA_DK, GLA_DV), lambda g: (g, 0, 0, 0)),
        ],
        out_shape=[
            jax.ShapeDtypeStruct((NS, D), BF16),
            jax.ShapeDtypeStruct((NS, HG_H, HG_DK, HG_DV), F32),
            jax.ShapeDtypeStruct((NS, GLA_H, GLA_DK, GLA_DV), F32),
        ],
        scratch_shapes=[pltpu.VMEM((SG, D), F32)],
        compiler_params=_params(("parallel",)),
        name="even_sample",
    )(z, zt3, grt3, lbpt, wgkt, bgkt, gnh, gng, s_hg, s_gla)


def _out_ln_kernel(x_ref, y_ref, w_ref, g_ref, b_ref, o_ref):
    r = ALPHA * x_ref[...] + _dot(y_ref[...], w_ref[...])
    o_ref[...] = _layernorm(r, g_ref[...], b_ref[...])


def _out_ln(x, y, w, g, b):
    c2 = lambda i: (0, 0)
    return pl.pallas_call(
        _out_ln_kernel,
        grid=(NT // TM,),
        in_specs=[
            pl.BlockSpec((TM, D), lambda i: (i, 0)),
            pl.BlockSpec((TM, D), lambda i: (i, 0)),
            pl.BlockSpec((D, D), c2),
            pl.BlockSpec((1, D), c2),
            pl.BlockSpec((1, D), c2),
        ],
        out_specs=pl.BlockSpec((TM, D), lambda i: (i, 0)),
        out_shape=jax.ShapeDtypeStruct((NT, D), F32),
        compiler_params=_params(("parallel",)),
        name="out_ln",
    )(x, y, w, g, b)


FF_SPLIT = 2


def _ffn_kernel(x_ref, w1_ref, w3_ref, w2_ref, g_ref, b_ref, o_ref):
    x = x_ref[...]
    xb = x.astype(BF16)
    step = D_FF_DENSE // FF_SPLIT
    acc = ALPHA * x
    for s in range(FF_SPLIT):
        cs = slice(s * step, (s + 1) * step)
        hmid = _silu(_dot(xb, w1_ref[:, cs])) * _dot(xb, w3_ref[:, cs])
        acc = acc + _dot(hmid.astype(BF16), w2_ref[cs, :])
    o_ref[...] = _layernorm(acc, g_ref[...], b_ref[...])


def _ffn(x, w1, w3, w2, g, b):
    c2 = lambda i: (0, 0)
    one = pl.Buffered(1)
    return pl.pallas_call(
        _ffn_kernel,
        grid=(NT // TM,),
        in_specs=[
            pl.BlockSpec((TM, D), lambda i: (i, 0)),
            pl.BlockSpec((D, D_FF_DENSE), c2, pipeline_mode=one),
            pl.BlockSpec((D, D_FF_DENSE), c2, pipeline_mode=one),
            pl.BlockSpec((D_FF_DENSE, D), c2, pipeline_mode=one),
            pl.BlockSpec((1, D), c2),
            pl.BlockSpec((1, D), c2),
        ],
        out_specs=pl.BlockSpec((TM, D), lambda i: (i, 0)),
        out_shape=jax.ShapeDtypeStruct((NT, D), F32),
        compiler_params=_params(("parallel",)),
        name="ffn_dense",
    )(x, w1, w3, w2, g, b)


def _mh_norm_gate(hh, o_pre, w):
    mu = jnp.mean(hh, axis=-1, keepdims=True)
    c = hh - mu
    var = jnp.mean(c * c, axis=-1, keepdims=True)
    return _sigmoid(o_pre) * (c * lax.rsqrt(var + EPS) * w)


def _odd_prompt_kernel(*refs):
    z_refs = refs[0:SEQ_PER_STEP]
    zg_refs = refs[SEQ_PER_STEP:2 * SEQ_PER_STEP]
    (bg_ref, cw_ref, cb_ref, hnw_ref, tri_ref,
     y_ref, c_out, n_out, m_out, conv_out,
     c_scr, n_scr, m_scr, u_scr) = refs[2 * SEQ_PER_STEP:]
    c = pl.program_id(1)

    @pl.when(c == 0)
    def _():
        c_scr[...] = jnp.zeros_like(c_scr)
        n_scr[...] = jnp.zeros_like(n_scr)
        m_scr[...] = jnp.zeros_like(m_scr)
        for s in range(SEQ_PER_STEP):
            u_scr[s, 0:8, :] = jnp.zeros((8, D), F32)

    row = lax.broadcasted_iota(jnp.int32, (CS, CS), 0)
    col = lax.broadcasted_iota(jnp.int32, (CS, CS), 1)
    causal = col <= row
    tails = []

    for s in range(SEQ_PER_STEP):
        z_ref = z_refs[s]
        u_scr[s, 8:8 + CS, :] = z_ref[:, 0:D]
        uc = cb_ref[...]
        for j in range(CONV_W):
            uc = uc + u_scr[s, 5 + j:5 + j + CS, :] * cw_ref[j:j + 1, :]
        tail = u_scr[s, CS:CS + 8, :]
        u_scr[s, 0:8, :] = tail
        tails.append(tail)
        act = _silu(uc)
        q = act[:, 0:512] * ML_DK ** -0.5
        k = act[:, 512:1024]
        v = z_ref[:, D:2 * D]
        o_pre = z_ref[:, 2 * D:3 * D]

        gates = zg_refs[s][...] + bg_ref[...]
        lf = _log_sigmoid(gates)
        bcum = _dot_sel(tri_ref[...], lf)
        bcum_t = bcum.T
        gates_t = gates.T
        m_all = m_scr[s]

        for h in range(ML_H):
            ks = slice(h * ML_DK, (h + 1) * ML_DK)
            vs = slice(h * ML_DV, (h + 1) * ML_DV)
            qh, kh = q[:, ks], k[:, ks]
            vh = v[:, vs].astype(BF16)
            b_col = bcum[:, 4 + h:5 + h]
            b_row = bcum_t[4 + h:5 + h, :]
            i_col = gates[:, h:h + 1]
            i_row = gates_t[h:h + 1, :]
            m_prev = m_all[:, h:h + 1]
            log_d = jnp.where(causal, b_col - b_row + i_row, -jnp.inf)
            log_prev = b_col + m_prev
            m_t = jnp.maximum(jnp.max(log_d, axis=-1, keepdims=True), log_prev)
            d = jnp.exp(log_d - m_t)
            w_prev = jnp.exp(log_prev - m_t)
            scores = _dot_nt(qh.astype(BF16), kh.astype(BF16)) * d
            c_h = c_scr[s, h]
            n_h = n_scr[s, h:h + 1, :]
            num = _dot(scores.astype(BF16), vh) + w_prev * _dot(qh.astype(BF16), c_h.astype(BF16))
            den = jnp.sum(scores, axis=-1, keepdims=True) + w_prev * jnp.sum(qh * n_h, axis=-1, keepdims=True)
            hh = num / jnp.maximum(jnp.abs(den), jnp.exp(-m_t))
            m_new = m_t[CS - 1:CS, :]
            b_last = b_col[CS - 1:CS, :]
            w_c = jnp.exp(b_last + m_prev - m_new)
            w_s = jnp.exp(b_last - b_col + i_col - m_new)
            kw = kh * w_s
            c_scr[s, h] = w_c * c_h + _dot_tn(kw.astype(BF16), vh)
            n_scr[s, h:h + 1, :] = w_c * n_h + jnp.sum(kw, axis=0, keepdims=True)
            m_scr[s, :, h:h + 1] = m_new
            y_ref[s, :, vs] = _mh_norm_gate(hh, o_pre[:, vs], hnw_ref[:, vs]).astype(BF16)

    @pl.when(c == NCHUNK - 1)
    def _():
        c_out[...] = c_scr[...]
        for s in range(SEQ_PER_STEP):
            n_out[s] = n_scr[s, 0:ML_H, :]
            m_out[s] = m_scr[s]
            conv_out[s] = tails[s][8 - (CONV_W - 1):8, :]


def _odd_prompt(z, zg, bg, cw, cb, hnw, tri):
    c2 = lambda b, c: (0, 0)
    sp = SEQ_PER_STEP
    return pl.pallas_call(
        _odd_prompt_kernel,
        grid=(BATCH // sp, NCHUNK),
        in_specs=_seq_row_specs(ODD_MAIN) + _seq_row_specs(LANES) + [
            pl.BlockSpec((1, LANES), c2),
            pl.BlockSpec((CONV_W, D), c2),
            pl.BlockSpec((1, D), c2),
            pl.BlockSpec((1, D), c2),
            pl.BlockSpec((CS, CS), c2),
        ],
        out_specs=[
            pl.BlockSpec((sp, CS, D), lambda b, c: (b, c, 0)),
            pl.BlockSpec((sp, ML_H, ML_DK, ML_DV), lambda b, c: (b, 0, 0, 0)),
            pl.BlockSpec((sp, ML_H, ML_DK), lambda b, c: (b, 0, 0)),
            pl.BlockSpec((sp, 1, LANES), lambda b, c: (b, 0, 0)),
            pl.BlockSpec((sp, CONV_W - 1, D), lambda b, c: (b, 0, 0)),
        ],
        out_shape=[
            jax.ShapeDtypeStruct((BATCH, SEQ, D), BF16),
            jax.ShapeDtypeStruct((BATCH, ML_H, ML_DK, ML_DV), F32),
            jax.ShapeDtypeStruct((BATCH, ML_H, ML_DK), F32),
            jax.ShapeDtypeStruct((BATCH, 1, LANES), F32),
            jax.ShapeDtypeStruct((BATCH, CONV_W - 1, D), F32),
        ],
        scratch_shapes=[
            pltpu.VMEM((sp, ML_H, ML_DK, ML_DV), F32),
            pltpu.VMEM((sp, 8, ML_DK), F32),
            pltpu.VMEM((sp, 1, LANES), F32),
            pltpu.VMEM((sp, CS + 8, D), F32),
        ],
        compiler_params=_params(("parallel", "arbitrary")),
        name="odd_prompt",
    )(*([z] * sp), *([zg] * sp), bg, cw, cb, hnw, tri)


def _odd_sample_kernel(zr_ref, zg_ref, ut_ref, conv_ref, convt_ref, bg_ref, cw_ref, cwt_ref, cb_ref, cbt_ref,
                       hnw_ref, c_ref, n_ref, m_ref,
                       y_ref, c_out, n_out, m_out, conv_out, h_scr):
    zr = zr_ref[...]
    u = zr[:, 0:D]
    v = zr[:, D:2 * D]
    o_pre = zr[:, 2 * D:3 * D]
    uc = cb_ref[...] + u * cw_ref[CONV_W - 1:CONV_W, :]
    uc_t = cbt_ref[...] + ut_ref[0] * cwt_ref[:, CONV_W - 1:CONV_W]
    for j in range(CONV_W - 1):
        uc = uc + conv_ref[:, j * D:(j + 1) * D] * cw_ref[j:j + 1, :]
        uc_t = uc_t + convt_ref[0, j] * cwt_ref[:, j:j + 1]
        conv_out[:, j * D:(j + 1) * D] = conv_ref[:, (j + 1) * D:(j + 2) * D] if j + 1 < CONV_W - 1 else u
    act = _silu(uc)
    k_row = act[:, 512:1024]
    act_t = _silu(uc_t)
    q_t = act_t[0:512] * ML_DK ** -0.5
    k_t = act_t[512:1024]
    q_row = act[:, 0:512] * ML_DK ** -0.5

    gates = zg_ref[...] + bg_ref[...]
    lf = _log_sigmoid(gates)
    m_in = m_ref[...]
    m_out[...] = m_in

    for j in range(SG):
        for h in range(ML_H):
            ks = slice(h * ML_DK, (h + 1) * ML_DK)
            vs = slice(h * ML_DV, (h + 1) * ML_DV)
            ig = gates[j:j + 1, h:h + 1]
            log_prev = lf[j:j + 1, 4 + h:5 + h] + m_in[j:j + 1, h:h + 1]
            m_t = jnp.maximum(ig, log_prev)
            d = jnp.exp(ig - m_t)
            w_prev = jnp.exp(log_prev - m_t)
            c_new = w_prev * c_ref[j, h] + (d * k_t[ks, j:j + 1]) * v[j:j + 1, vs]
            n_new = w_prev * n_ref[j, h:h + 1, :] + d * k_row[j:j + 1, ks]
            c_out[j, h] = c_new
            n_out[j, h:h + 1, :] = n_new
            m_out[j:j + 1, h:h + 1] = m_t
            num = jnp.sum(q_t[ks, j:j + 1] * c_new, axis=0, keepdims=True)
            den = jnp.sum(q_row[j:j + 1, ks] * n_new, axis=-1, keepdims=True)
            h_scr[j:j + 1, vs] = num / jnp.maximum(jnp.abs(den), jnp.exp(-m_t))

    hh = h_scr[...]
    for h in range(ML_H):
        vs = slice(h * ML_DV, (h + 1) * ML_DV)
        y_ref[:, vs] = _mh_norm_gate(hh[:, vs], o_pre[:, vs], hnw_ref[:, vs]).astype(BF16)


def _odd_sample(z, zg, ut3, conv, convt, bg, cw, cwt, cb, cbt, hnw, c_in, n_in, m_in):
    c2 = lambda g: (0, 0)
    return pl.pallas_call(
        _odd_sample_kernel,
        grid=(NS // SG,),
        in_specs=[
            pl.BlockSpec((SG, ODD_MAIN), lambda g: (NP // SG + g, 0)),
            pl.BlockSpec((SG, LANES), lambda g: (NP // SG + g, 0)),
            pl.BlockSpec((1, D, SG), lambda g: (g, 0, 0)),
            pl.BlockSpec((SG, (CONV_W - 1) * D), lambda g: (g, 0)),
            pl.BlockSpec((1, CONV_W - 1, D, SG), lambda g: (g, 0, 0, 0)),
            pl.BlockSpec((1, LANES), c2),
            pl.BlockSpec((CONV_W, D), c2),
            pl.BlockSpec((D, CONV_W), c2),
            pl.BlockSpec((1, D), c2),
            pl.BlockSpec((D, 1), c2),
            pl.BlockSpec((1, D), c2),
            pl.BlockSpec((SG, ML_H, ML_DK, ML_DV), lambda g: (g, 0, 0, 0)),
            pl.BlockSpec((SG, ML_H, ML_DK), lambda g: (g, 0, 0)),
            pl.BlockSpec((SG, LANES), lambda g: (g, 0)),
        ],
        out_specs=[
            pl.BlockSpec((SG, D), lambda g: (g, 0)),
            pl.BlockSpec((SG, ML_H, ML_DK, ML_DV), lambda g: (g, 0, 0, 0)),
            pl.BlockSpec((SG, ML_H, ML_DK), lambda g: (g, 0, 0)),
            pl.BlockSpec((SG, LANES), lambda g: (g, 0)),
            pl.BlockSpec((SG, (CONV_W - 1) * D), lambda g: (g, 0)),
        ],
        out_shape=[
            jax.ShapeDtypeStruct((NS, D), BF16),
            jax.ShapeDtypeStruct((NS, ML_H, ML_DK, ML_DV), F32),
            jax.ShapeDtypeStruct((NS, ML_H, ML_DK), F32),
            jax.ShapeDtypeStruct((NS, LANES), F32),
            jax.ShapeDtypeStruct((NS, (CONV_W - 1) * D), F32),
        ],
        scratch_shapes=[pltpu.VMEM((SG, D), F32)],
        compiler_params=_params(("parallel",)),
        name="odd_sample",
    )(z, zg, ut3, conv, convt, bg, cw, cwt, cb, cbt, hnw, c_in, n_in, m_in)


def _out_ln_router_kernel(x_ref, y_ref, w_ref, g_ref, b_ref, wr_ref, tri_ref,
                          o_ref, op_ref, meta_ref, cnt_ref, carry):
    i = pl.program_id(0)

    @pl.when(i == 0)
    def _():
        carry[...] = jnp.zeros_like(carry)

    r = ALPHA * x_ref[...] + _dot(y_ref[...], w_ref[...])
    x3 = _layernorm(r, g_ref[...], b_ref[...])
    o_ref[...] = x3
    op_ref[...] = pltpu.pack_elementwise([x3[:, :D // 2], x3[:, D // 2:]], packed_dtype=BF16)

    lane = lax.broadcasted_iota(jnp.int32, (TM, LANES), 1).astype(F32)
    logits = jnp.where(lane < N_EXPERTS, _dot(x3.astype(BF16), wr_ref[...]), -jnp.inf)
    m1 = jnp.max(logits, axis=-1, keepdims=True)
    i1 = jnp.min(jnp.where(logits == m1, lane, float(LANES)), axis=-1, keepdims=True)
    rest = jnp.where(lane == i1, -jnp.inf, logits)
    m2 = jnp.max(rest, axis=-1, keepdims=True)
    i2 = jnp.min(jnp.where(rest == m2, lane, float(LANES)), axis=-1, keepdims=True)
    e2 = jnp.exp(m2 - m1)
    tot = 1.0 + e2
    w1 = 1.0 / tot
    w2 = e2 / tot

    sel1 = lane == i1
    sel2 = lane == i2
    onehot = jnp.where(sel1 | sel2, 1.0, 0.0)
    before = _dot(tri_ref[...], onehot.astype(BF16)) + carry[...]
    r1 = jnp.sum(jnp.where(sel1, before, 0.0), axis=-1, keepdims=True)
    r2 = jnp.sum(jnp.where(sel2, before, 0.0), axis=-1, keepdims=True)
    carry[...] = carry[...] + jnp.sum(onehot, axis=0, keepdims=True)
    cnt_ref[...] = carry[...]

    meta = jnp.where(lane == 0.0, i1, 0.0)
    meta = jnp.where(lane == 1.0, i2, meta)
    meta = jnp.where(lane == 2.0, w1, meta)
    meta = jnp.where(lane == 3.0, w2, meta)
    meta = jnp.where(lane == 4.0, r1, meta)
    meta = jnp.where(lane == 5.0, r2, meta)
    meta_ref[...] = meta


def _out_ln_router(x, y, w, g, b, wr, tri):
    c2 = lambda i: (0, 0)
    return pl.pallas_call(
        _out_ln_router_kernel,
        grid=(NT // TM,),
        in_specs=[
            pl.BlockSpec((TM, D), lambda i: (i, 0)),
            pl.BlockSpec((TM, D), lambda i: (i, 0)),
            pl.BlockSpec((D, D), c2),
            pl.BlockSpec((1, D), c2),
            pl.BlockSpec((1, D), c2),
            pl.BlockSpec((D, LANES), c2),
            pl.BlockSpec((TM, TM), c2),
        ],
        out_specs=[
            pl.BlockSpec((TM, D), lambda i: (i, 0)),
            pl.BlockSpec((TM, D // 2), lambda i: (i, 0)),
            pl.BlockSpec((TM, LANES), lambda i: (i, 0)),
            pl.BlockSpec((1, LANES), c2),
        ],
        out_shape=[
            jax.ShapeDtypeStruct((NT, D), F32),
            jax.ShapeDtypeStruct((NT, D // 2), U32),
            jax.ShapeDtypeStruct((NT, LANES), F32),
            jax.ShapeDtypeStruct((1, LANES), F32),
        ],
        scratch_shapes=[pltpu.VMEM((1, LANES), F32)],
        compiler_params=_params(("arbitrary",)),
        name="out_ln_router",
    )(x, y, w, g, b, wr, tri)


def _moe_ffn_kernel(te_ref, nu_ref, gnext_ref, gcur_ref, sprev_ref, scur_ref, xp_ref, w1_ref, w3_ref, w2_ref,
                    out_hbm, stage, yacc, xb_scr, sem_s):
    i = pl.program_id(0)
    j = pl.program_id(1)
    used = i < nu_ref[0]
    slot = i % 2
    other = 1 - slot
    rps = MOE_ROWS_PER_STEP

    def gather_rows(tab_ref, buf, part):
        for r in range(rps):
            stage[buf, part, pl.ds(r, 1), :] = xp_ref[pl.ds(tab_ref[part * rps + r], 1), :]

    def scatter(buf, r, dst):
        return pltpu.make_async_copy(yacc.at[buf, pl.ds(r, 1)], out_hbm.at[pl.ds(dst, 1)], sem_s)

    def wait_scatters(n):
        for _ in range(n):
            scatter(0, 0, 0).wait()

    def issue_neighbours():
        gather_rows(gnext_ref, other, j)
        for r in range(rps):
            rr = j * rps + r
            scatter(other, rr, sprev_ref[rr]).start()

    @pl.when(j == 0)
    def _():
        @pl.when(i == 0)
        def _():
            yacc[1] = jnp.zeros((TMM, D), F32)
            for part in range(MOE_NFF):
                gather_rows(gcur_ref, 0, part)

        @pl.when(i > 0)
        def _():
            wait_scatters(TMM)

        words = stage[slot].reshape(TMM, D // 2)
        for half in range(2):
            xb_scr[:, half * (D // 2):(half + 1) * (D // 2)] = pltpu.unpack_elementwise(
                words, index=half, packed_dtype=BF16, unpacked_dtype=F32).astype(BF16)
        yacc[slot] = jnp.zeros((TMM, D), F32)

    @pl.when(used)
    def _():
        issue_neighbours()
        xb = xb_scr[...]
        hmid = _silu(_dot(xb, w1_ref[...])) * _dot(xb, w3_ref[...])
        yacc[slot] += _dot(hmid.astype(BF16), w2_ref[...])

    @pl.when(jnp.logical_not(used))
    def _():
        issue_neighbours()

    @pl.when((i == N_MOE_TILES - 1) & (j == MOE_NFF - 1))
    def _():
        for r in range(TMM):
            scatter(slot, r, scur_ref[r]).start()
        wait_scatters(2 * TMM)


def _moe_ffn(tile_expert, n_used, gsrc, sdst, xp, w1, w3, w2):
    nff = MOE_NFF

    def wcol(i, j, te, nu):
        return (te[i], 0, jnp.where(i < nu[0], j, nff - 1))

    def wrow(i, j, te, nu):
        return (te[i], jnp.where(i < nu[0], j, nff - 1), 0)

    smem = functools.partial(pl.BlockSpec, (MOE_TAB,), memory_space=pltpu.SMEM)
    grid_spec = pltpu.PrefetchScalarGridSpec(
        num_scalar_prefetch=2,
        grid=(N_MOE_TILES, nff),
        in_specs=[
            smem(lambda i, j, te, nu: (i + 1,)),
            smem(lambda i, j, te, nu: (i,)),
            smem(lambda i, j, te, nu: (i,)),
            smem(lambda i, j, te, nu: (i + 1,)),
            pl.BlockSpec((NT, D // 2), lambda i, j, te, nu: (0, 0), pipeline_mode=pl.Buffered(1)),
            pl.BlockSpec((None, D, TFF), wcol),
            pl.BlockSpec((None, D, TFF), wcol),
            pl.BlockSpec((None, TFF, D), wrow),
        ],
        out_specs=pl.BlockSpec(memory_space=pl.ANY),
        scratch_shapes=[
            pltpu.VMEM((2, MOE_NFF, MOE_ROWS_PER_STEP, D // 2), U32),
            pltpu.VMEM((2, TMM, D), F32),
            pltpu.VMEM((TMM, D), BF16),
            pltpu.SemaphoreType.DMA(()),
        ],
    )
    return pl.pallas_call(
        _moe_ffn_kernel,
        grid_spec=grid_spec,
        out_shape=jax.ShapeDtypeStruct((MOE_OUT_ROWS, D), F32),
        compiler_params=_params(("arbitrary", "arbitrary")),
        name="moe_ffn",
    )(tile_expert, n_used, gsrc, gsrc, sdst, sdst, xp, w1, w3, w2)


def _combine_kernel(x_ref, meta_ref, y0_ref, y1_ref, g_ref, b_ref, o_ref):
    meta = meta_ref[...]
    moe = meta[:, 2:3] * y0_ref[...] + meta[:, 3:4] * y1_ref[...]
    o_ref[...] = _layernorm(ALPHA * x_ref[...] + moe, g_ref[...], b_ref[...])


def _combine(x, meta, ys, g, b):
    c2 = lambda i: (0, 0)
    return pl.pallas_call(
        _combine_kernel,
        grid=(NT // TM,),
        in_specs=[
            pl.BlockSpec((TM, D), lambda i: (i, 0)),
            pl.BlockSpec((TM, LANES), lambda i: (i, 0)),
            pl.BlockSpec((TM, D), lambda i: (i, 0)),
            pl.BlockSpec((TM, D), lambda i: (i + NT // TM, 0)),
            pl.BlockSpec((1, D), c2),
            pl.BlockSpec((1, D), c2),
        ],
        out_specs=pl.BlockSpec((TM, D), lambda i: (i, 0)),
        out_shape=jax.ShapeDtypeStruct((NT, D), F32),
        compiler_params=_params(("parallel",)),
        name="moe_combine",
    )(x, meta, ys, ys, g, b)


def _pad_cols(w, n):
    return jnp.pad(w, ((0, 0), (0, n - w.shape[1])))


def kernel(x_prompt, x_sample, state_hgrn, state_gla, state_mlstm_C, state_mlstm_n, state_mlstm_m,
           state_mlstm_conv, w_in_even, hg_lower_bounds, w_gk, b_gk, gn_hg, gn_gla, w_out_even,
           w1_dense, w3_dense, w2_dense, w_in_odd, b_gate_odd, conv_w, conv_b, hn_w, w_out_odd,
           w_router, w1_moe, w3_moe, w2_moe, ln1_g, ln1_b, ln2_g, ln2_b):
    assert x_prompt.shape == (BATCH, SEQ, D) and x_sample.shape == (NS, 1, D)
    assert w_in_even.shape[0] == 1 and w_in_odd.shape[0] == 1 and hg_lower_bounds.shape[0] == 2
    masks = jnp.asarray(_gla_masks(), F32)
    tri_cs = jnp.asarray(_tri(CS, False), BF16)
    tri_tm = jnp.asarray(_tri(TM, True), BF16)
    row = lambda a: a.reshape(1, -1)

    x0 = jnp.concatenate([x_prompt.reshape(NP, D), x_sample.reshape(NS, D)], axis=0)

    w_even = w_in_even[0].astype(BF16)
    z, zgr = _proj(x0, w_even[:, :EVEN_MAIN], _pad_cols(w_even[:, EVEN_MAIN:], LANES))
    wgk = jnp.pad(w_gk[0].astype(BF16), ((0, LANES - GLA_RANK), (0, 0)))
    lbp = hg_lower_bounds
    y_p, hg_p, gla_p = _even_prompt(z, zgr, lbp, wgk, row(b_gk[0]), row(gn_hg[0]), row(gn_gla[0]), tri_cs, masks)

    zs = z[NP:].reshape(NS // SG, SG, EVEN_MAIN).transpose(0, 2, 1)
    grs = zgr[NP:].reshape(NS // SG, SG, LANES).transpose(0, 2, 1)
    y_s, hg_s, gla_s = _even_sample(z, zs, grs, lbp.T, wgk.T, b_gk[0].reshape(-1, 1),
                                    row(gn_hg[0]), row(gn_gla[0]), state_hgrn[0], state_gla[0])
    y = jnp.concatenate([y_p.reshape(NP, D), y_s], axis=0)
    x1 = _out_ln(x0, y, w_out_even[0].astype(BF16), row(ln1_g[0]), row(ln1_b[0]))
    x2 = _ffn(x1, w1_dense[0].astype(BF16), w3_dense[0].astype(BF16), w2_dense[0].astype(BF16),
              row(ln2_g[0]), row(ln2_b[0]))

    w_odd = w_in_odd[0].astype(BF16)
    zo, zog = _proj(x2, w_odd[:, :ODD_MAIN], _pad_cols(w_odd[:, ODD_MAIN:], LANES))
    bg = jnp.pad(b_gate_odd[0], (0, LANES - 2 * ML_H)).reshape(1, LANES)
    yo_p, c_p, n_p, m_p, conv_p = _odd_prompt(zo, zog, bg, conv_w[0], row(conv_b[0]), row(hn_w[0]), tri_cs)

    ut = zo[NP:, :D].reshape(NS // SG, SG, D).transpose(0, 2, 1)
    conv_in = state_mlstm_conv[0]
    conv_t = conv_in.reshape(NS // SG, SG, CONV_W - 1, D).transpose(0, 2, 3, 1)
    m_in = jnp.pad(state_mlstm_m[0], ((0, 0), (0, LANES - ML_H)))
    yo_s, c_s, n_s, m_s, conv_s = _odd_sample(
        zo, zog, ut, conv_in.reshape(NS, (CONV_W - 1) * D), conv_t, bg, conv_w[0], conv_w[0].T, row(conv_b[0]), conv_b[0].reshape(-1, 1),
        row(hn_w[0]), state_mlstm_C[0], state_mlstm_n[0], m_in)
    yo = jnp.concatenate([yo_p.reshape(NP, D), yo_s], axis=0)

    wr = _pad_cols(w_router[0].astype(BF16), LANES)
    x3, x3p, meta, cnt = _out_ln_router(x2, yo, w_out_odd[0].astype(BF16), row(ln1_g[1]), row(ln1_b[1]), wr, tri_tm)

    counts = cnt[0, :N_EXPERTS].astype(jnp.int32)
    padded = ((counts + TMM - 1) // TMM) * TMM
    ends = jnp.cumsum(padded)
    offsets = ends - padded
    idx = meta[:, 0:2].astype(jnp.int32)
    pos = offsets[idx] + meta[:, 4:6].astype(jnp.int32)
    tile_start = jnp.arange(N_MOE_TILES, dtype=jnp.int32) * TMM
    tile_expert = jnp.minimum(jnp.sum(tile_start[:, None] >= ends[None, :], axis=1), N_EXPERTS - 1).astype(jnp.int32)
    n_used = (ends[-1] // TMM).astype(jnp.int32).reshape(1)

    slot = jnp.arange(MOE_SLOTS, dtype=jnp.int32)
    e_slot = jnp.sum(slot[:, None] >= ends[None, :], axis=1)
    off_x = jnp.concatenate([offsets, ends[-1:]])
    cnt_x = jnp.concatenate([counts, jnp.zeros((1,), jnp.int32)])
    real_before_x = jnp.concatenate([jnp.cumsum(counts) - counts, jnp.full((1,), 2 * NT, jnp.int32)])
    local = slot - off_x[e_slot]
    real_before = real_before_x[e_slot] + jnp.minimum(local, cnt_x[e_slot])
    spill_row = 2 * NT + TMM + (slot - real_before)
    token = jnp.arange(NT, dtype=jnp.int32)
    dst_rows = jnp.stack([token, NT + token], axis=1)
    dst = spill_row.at[pos.reshape(-1)].set(dst_rows.reshape(-1), unique_indices=True)
    per_tile = lambda t: jnp.pad(t.reshape(-1, TMM), ((0, 0), (0, MOE_TAB - TMM))).reshape(-1)
    sdst = per_tile(jnp.concatenate([2 * NT + jnp.arange(TMM, dtype=jnp.int32), dst]))
    src_tok = jnp.where(dst < NT, dst, jnp.where(dst < 2 * NT, dst - NT, 0))
    gsrc = per_tile(jnp.concatenate([src_tok, jnp.zeros((TMM,), jnp.int32)]))

    ys = _moe_ffn(tile_expert, n_used, gsrc, sdst, x3p,
                  w1_moe[0].astype(BF16), w3_moe[0].astype(BF16), w2_moe[0].astype(BF16))
    out = _combine(x3, meta, ys, row(ln2_g[1]), row(ln2_b[1]))

    y_prompt = out[:NP].reshape(BATCH, SEQ, D)
    y_sample = out[NP:].reshape(NS, 1, D)
    return (y_prompt, y_sample,
            hg_p.reshape(1, BATCH, HG_H, HG_DK, HG_DV), gla_p.reshape(1, BATCH, GLA_H, GLA_DK, GLA_DV),
            c_p[None], n_p[None], m_p[:, 0, :ML_H][None], conv_p[None],
            hg_s[None], gla_s[None], c_s[None], n_s[None], m_s[:, :ML_H][None], conv_s.reshape(1, NS, CONV_W - 1, D))
```

```python
import functools
import math

import jax
import jax.numpy as jnp
import numpy as np
from jax import lax
from jax.experimental import pallas as pl
from jax.experimental.pallas import tpu as pltpu

F32 = jnp.float32
BF16 = jnp.bfloat16
U32 = jnp.uint32

D = 1024
BATCH = 8
SEQ = 2048
DEC_BATCH = 128
NP = BATCH * SEQ
NS = DEC_BATCH
NT = NP + NS
HG_H, HG_DK, HG_DV = 4, 128, 128
GLA_H, GLA_DK, GLA_DV = 4, 64, 128
GLA_RANK = 16
GLA_GATE_NORM = 16.0
ML_H, ML_DK, ML_DV = 4, 128, 256
CONV_W = 4
D_FF_DENSE = 2816
D_FF_EXPERT = 3584
N_EXPERTS = 8
EPS = 1e-5
DEPTH = 2
ALPHA = (2.0 * DEPTH) ** 0.25
EVEN_MAIN = 3584
ODD_MAIN = 3072

LANES = 128
SUBLANES = 8
VMEM_LIMIT = 56 * 1024 * 1024

TM = 384
CS = 128
NCHUNK = SEQ // CS
SEQ_PER_STEP = 1
SG = 16
TMM = 512
TFF = 896
MOE_TAB = 512
MOE_CAP = -(-NT // MOE_TAB) * MOE_TAB
MOE_NFF = D_FF_EXPERT // TFF
MOE_ROWS_PER_STEP = TMM // MOE_NFF
N_MOE_TILES = -(-(2 * NT + N_EXPERTS * (TMM - 1)) // TMM)
MOE_SLOTS = N_MOE_TILES * TMM
MOE_OUT_ROWS = MOE_SLOTS + TMM
N_LEVELS = int(math.log2(CS))

assert NT % TM == 0 and NP % CS == 0 and NS % SG == 0 and D_FF_EXPERT % TFF == 0 and TMM % MOE_NFF == 0
assert TMM == MOE_TAB


def _params(sem, limit=VMEM_LIMIT):
    return pltpu.CompilerParams(dimension_semantics=sem, vmem_limit_bytes=limit)


def _dot(a, b):
    return jnp.dot(a, b, preferred_element_type=F32)


def _dot_nt(a, b):
    return lax.dot_general(a, b, (((1,), (1,)), ((), ())), preferred_element_type=F32)


def _dot_tn(a, b):
    return lax.dot_general(a, b, (((0,), (0,)), ((), ())), preferred_element_type=F32)


def _split3(x):
    hi = x.astype(BF16)
    r1 = x - hi.astype(F32)
    mid = r1.astype(BF16)
    lo = (r1 - mid.astype(F32)).astype(BF16)
    return hi, mid, lo


def _dot_sel(sel, x):
    hi, mid, lo = _split3(x)
    return _dot(sel, hi) + _dot(sel, mid) + _dot(sel, lo)


def _sigmoid(x):
    return jax.nn.sigmoid(x)


def _silu(x):
    return x * jax.nn.sigmoid(x)


def _log_sigmoid(x):
    return jnp.minimum(x, 0.0) - jnp.log(1.0 + jnp.exp(-jnp.abs(x)))


def _layernorm(r, g, b):
    mu = jnp.mean(r, axis=-1, keepdims=True)
    c = r - mu
    var = jnp.mean(c * c, axis=-1, keepdims=True)
    return c * lax.rsqrt(var + EPS) * g + b


def _gla_masks():
    masks = np.zeros((N_LEVELS + 1, CS, CS), np.float32)
    for t in range(CS):
        for l in range(N_LEVELS):
            half = 1 << l
            start = (t // (2 * half)) * (2 * half)
            mid = start + half
            if t >= mid:
                masks[l, t, start:mid] = 1.0
        masks[N_LEVELS, t, t] = 1.0
    return masks


def _tri(n, strict):
    return np.tril(np.ones((n, n), np.float32), -1 if strict else 0)


def _proj_kernel(x_ref, wa_ref, wb_ref, oa_ref, ob_ref):
    xb = x_ref[...].astype(BF16)
    oa_ref[...] = _dot(xb, wa_ref[...])
    ob_ref[...] = _dot(xb, wb_ref[...])


def _proj(x, wa, wb):
    na, nb = wa.shape[1], wb.shape[1]
    return pl.pallas_call(
        _proj_kernel,
        grid=(NT // TM,),
        in_specs=[
            pl.BlockSpec((TM, D), lambda i: (i, 0)),
            pl.BlockSpec((D, na), lambda i: (0, 0)),
            pl.BlockSpec((D, nb), lambda i: (0, 0)),
        ],
        out_specs=[
            pl.BlockSpec((TM, na), lambda i: (i, 0)),
            pl.BlockSpec((TM, nb), lambda i: (i, 0)),
        ],
        out_shape=[jax.ShapeDtypeStruct((NT, na), F32), jax.ShapeDtypeStruct((NT, nb), F32)],
        compiler_params=_params(("parallel",)),
        name="proj",
    )(x, wa, wb)


def _rms_gate(o, gate, w):
    o = o * lax.rsqrt(jnp.mean(o * o, axis=-1, keepdims=True) + EPS) * w
    return o * _silu(gate)


def _level_decays(g, bc):
    width = g.shape[1]
    ng = CS // SUBLANES
    shape3 = (ng, SUBLANES, width)
    bc3 = bc.reshape(shape3)
    sub = lax.broadcasted_iota(jnp.int32, shape3, 1)

    def group_row(s):
        return jnp.broadcast_to(bc3[:, s:s + 1, :], shape3)

    last = group_row(SUBLANES - 1)
    refs = [None,
            jnp.where(sub < 4, group_row(1), group_row(5)),
            group_row(3)]
    for l in range(3, N_LEVELS):
        per_block = 1 << (l - 2)
        grouped = last.reshape(ng // per_block, per_block, SUBLANES, width)
        ref = jnp.broadcast_to(grouped[:, per_block // 2 - 1:per_block // 2], grouped.shape)
        refs.append(ref.reshape(shape3))
    decays = [jnp.exp(jnp.where((sub & 1) == 1, g.reshape(shape3), 0.0))]
    decays += [jnp.exp(-jnp.abs(bc3 - ref)) for ref in refs[1:]]
    to_end = jnp.exp(jnp.broadcast_to(last[ng - 1:ng], shape3) - bc3)
    return [d.reshape(CS, width) for d in decays], to_end.reshape(CS, width)


def _gla_chunk(q, k, v, g, st_ref, tri, masks_ref, heads, dk, dv):
    bc = _dot_sel(tri, g)
    z_levels, z_end = _level_decays(g, bc)
    z_cum = jnp.exp(bc)
    st = st_ref[...]
    outs = []
    for h in range(heads):
        ks = slice(h * dk, (h + 1) * dk)
        vs = slice(h * dv, (h + 1) * dv)
        qh, kh = q[:, ks], k[:, ks]
        vh = v[:, vs].astype(BF16)
        scores = _dot_nt(qh.astype(BF16), kh.astype(BF16)) * masks_ref[N_LEVELS]
        for l in range(N_LEVELS):
            zl = z_levels[l][:, ks]
            scores = scores + _dot_nt((qh * zl).astype(BF16), (kh * zl).astype(BF16)) * masks_ref[l]
        o = _dot(scores.astype(BF16), vh)
        o = o + _dot_nt((qh * z_cum[:, ks]).astype(BF16), st[:, ks].astype(BF16))
        outs.append(o)
        upd = _dot_tn(vh, (kh * z_end[:, ks]).astype(BF16))
        st_ref[:, ks] = st[:, ks] * z_cum[CS - 1:CS, ks] + upd
    return outs


def _even_prompt_kernel(*refs):
    z_refs = refs[0:SEQ_PER_STEP]
    zgr_refs = refs[SEQ_PER_STEP:2 * SEQ_PER_STEP]
    (lbp_ref, wgk_ref, bgk_ref, gnh_ref, gng_ref, tri_ref, masks_ref,
     y_ref, shg_ref, sgla_ref, st_hg, st_gla) = refs[2 * SEQ_PER_STEP:]
    c = pl.program_id(1)

    @pl.when(c == 0)
    def _():
        st_hg[...] = jnp.zeros_like(st_hg)
        st_gla[...] = jnp.zeros_like(st_gla)

    tri = tri_ref[...]
    p = lbp_ref[...]
    pe = jnp.exp(p - jnp.max(p, axis=0, keepdims=True))
    lb = pe[0:1] / jnp.sum(pe, axis=0, keepdims=True)

    for s in range(SEQ_PER_STEP):
        z = z_refs[s][...]
        hq, hf, hi, hg = z[:, 0:512], z[:, 512:1024], z[:, 1024:1536], z[:, 1536:2048]
        gq, gk, gv, gg = z[:, 2048:2304], z[:, 2304:2560], z[:, 2560:3072], z[:, 3072:3584]
        f = lb + (1.0 - lb) * _sigmoid(hf)
        k_hg = (1.0 - lb) * _sigmoid(-hf)
        o_hg = _gla_chunk(_silu(hq), k_hg, hi, jnp.log(f), st_hg.at[s], tri, masks_ref, HG_H, HG_DK, HG_DV)

        la = _log_sigmoid(_dot(zgr_refs[s][...].astype(BF16), wgk_ref[...]) + bgk_ref[...]) / GLA_GATE_NORM
        o_gla = _gla_chunk(gq * GLA_DK ** -0.5, gk, gv, la, st_gla.at[s], tri, masks_ref, GLA_H, GLA_DK, GLA_DV)

        for h in range(HG_H):
            cs = slice(h * 128, (h + 1) * 128)
            y_ref[s, :, cs] = _rms_gate(o_hg[h], hg[:, cs], gnh_ref[...]).astype(BF16)
        for h in range(GLA_H):
            cs = slice(h * 128, (h + 1) * 128)
            y_ref[s, :, 512 + h * 128:512 + (h + 1) * 128] = _rms_gate(o_gla[h], gg[:, cs], gng_ref[...]).astype(BF16)

    @pl.when(c == NCHUNK - 1)
    def _():
        for s in range(SEQ_PER_STEP):
            shg_ref[s] = st_hg[s].T
            sgla_ref[s] = st_gla[s].T


def _seq_row_specs(width):
    return [pl.BlockSpec((CS, width), functools.partial(lambda b, c, s: ((SEQ_PER_STEP * b + s) * NCHUNK + c, 0), s=s))
            for s in range(SEQ_PER_STEP)]


def _even_prompt(z, zgr, lbp, wgk, bgk, gnh, gng, tri, masks):
    const2 = lambda b, c: (0, 0)
    sp = SEQ_PER_STEP
    return pl.pallas_call(
        _even_prompt_kernel,
        grid=(BATCH // sp, NCHUNK),
        in_specs=_seq_row_specs(EVEN_MAIN) + _seq_row_specs(LANES) + [
            pl.BlockSpec(lbp.shape, const2),
            pl.BlockSpec(wgk.shape, const2),
            pl.BlockSpec(bgk.shape, const2),
            pl.BlockSpec(gnh.shape, const2),
            pl.BlockSpec(gng.shape, const2),
            pl.BlockSpec(tri.shape, const2),
            pl.BlockSpec(masks.shape, lambda b, c: (0, 0, 0)),
        ],
        out_specs=[
            pl.BlockSpec((sp, CS, D), lambda b, c: (b, c, 0)),
            pl.BlockSpec((sp, HG_H * HG_DK, HG_DV), lambda b, c: (b, 0, 0)),
            pl.BlockSpec((sp, GLA_H * GLA_DK, GLA_DV), lambda b, c: (b, 0, 0)),
        ],
        out_shape=[
            jax.ShapeDtypeStruct((BATCH, SEQ, D), BF16),
            jax.ShapeDtypeStruct((BATCH, HG_H * HG_DK, HG_DV), F32),
            jax.ShapeDtypeStruct((BATCH, GLA_H * GLA_DK, GLA_DV), F32),
        ],
        scratch_shapes=[pltpu.VMEM((sp, HG_DV, HG_H * HG_DK), F32), pltpu.VMEM((sp, GLA_DV, GLA_H * GLA_DK), F32)],
        compiler_params=_params(("parallel", "arbitrary")),
        name="even_prompt",
    )(*([z] * sp), *([zgr] * sp), lbp, wgk, bgk, gnh, gng, tri, masks)


def _even_sample_kernel(zr_ref, zt_ref, grt_ref, lbpt_ref, wgkt_ref, bgkt_ref, gnh_ref, gng_ref,
                        shg_ref, sgla_ref, y_ref, shg_out, sgla_out, o_scr):
    zt = zt_ref[0]
    hq_t, hf_t = zt[0:512], zt[512:1024]
    gq_t, gk_t = zt[2048:2304], zt[2304:2560]
    pt = lbpt_ref[...]
    pe = jnp.exp(pt - jnp.max(pt, axis=1, keepdims=True))
    lb = pe[:, 0:1] / jnp.sum(pe, axis=1, keepdims=True)
    a_hg = jnp.exp(jnp.log(lb + (1.0 - lb) * _sigmoid(hf_t)))
    k_hg = (1.0 - lb) * _sigmoid(-hf_t)
    q_hg = _silu(hq_t)
    la = _log_sigmoid(_dot(wgkt_ref[...], grt_ref[0].astype(BF16)) + bgkt_ref[...]) / GLA_GATE_NORM
    a_gla = jnp.exp(la)
    q_gla = gq_t * GLA_DK ** -0.5
    zr = zr_ref[...]
    hi, hg = zr[:, 1024:1536], zr[:, 1536:2048]
    gv, gg = zr[:, 2560:3072], zr[:, 3072:3584]

    for j in range(SG):
        for h in range(HG_H):
            ks = slice(h * HG_DK, (h + 1) * HG_DK)
            s_new = a_hg[ks, j:j + 1] * shg_ref[j, h] + k_hg[ks, j:j + 1] * hi[j:j + 1, h * 128:(h + 1) * 128]
            shg_out[j, h] = s_new
            o_scr[j:j + 1, h * 128:(h + 1) * 128] = jnp.sum(q_hg[ks, j:j + 1] * s_new, axis=0, keepdims=True)
        for h in range(GLA_H):
            ks = slice(h * GLA_DK, (h + 1) * GLA_DK)
            s_new = a_gla[ks, j:j + 1] * sgla_ref[j, h] + gk_t[ks, j:j + 1] * gv[j:j + 1, h * 128:(h + 1) * 128]
            sgla_out[j, h] = s_new
            o_scr[j:j + 1, 512 + h * 128:512 + (h + 1) * 128] = jnp.sum(
                q_gla[ks, j:j + 1] * s_new, axis=0, keepdims=True)

    o = o_scr[...]
    for h in range(HG_H):
        cs = slice(h * 128, (h + 1) * 128)
        y_ref[:, cs] = _rms_gate(o[:, cs], hg[:, cs], gnh_ref[...]).astype(BF16)
    for h in range(GLA_H):
        cs = slice(512 + h * 128, 512 + (h + 1) * 128)
        y_ref[:, cs] = _rms_gate(o[:, cs], gg[:, h * 128:(h + 1) * 128], gng_ref[...]).astype(BF16)


def _even_sample(z, zt3, grt3, lbpt, wgkt, bgkt, gnh, gng, s_hg, s_gla):
    c2 = lambda g: (0, 0)
    return pl.pallas_call(
        _even_sample_kernel,
        grid=(NS // SG,),
        in_specs=[
            pl.BlockSpec((SG, EVEN_MAIN), lambda g: (NP // SG + g, 0)),
            pl.BlockSpec((1, EVEN_MAIN, SG), lambda g: (g, 0, 0)),
            pl.BlockSpec((1, LANES, SG), lambda g: (g, 0, 0)),
            pl.BlockSpec(lbpt.shape, c2),
            pl.BlockSpec(wgkt.shape, c2),
            pl.BlockSpec(bgkt.shape, c2),
            pl.BlockSpec(gnh.shape, c2),
            pl.BlockSpec(gng.shape, c2),
            pl.BlockSpec((SG, HG_H, HG_DK, HG_DV), lambda g: (g, 0, 0, 0)),
            pl.BlockSpec((SG, GLA_H, GLA_DK, GLA_DV), lambda g: (g, 0, 0, 0)),
        ],
        out_specs=[
            pl.BlockSpec((SG, D), lambda g: (g, 0)),
            pl.BlockSpec((SG, HG_H, HG_DK, HG_DV), lambda g: (g, 0, 0, 0)),
            pl.BlockSpec((SG, GLA_H, GLA_DK, GLA_DV), lambda g: (g, 0, 0, 0)),
        ],
        out_shape=[
            jax.ShapeDtypeStruct((NS, D), BF16),
            jax.ShapeDtypeStruct((NS, HG_H, HG_DK, HG_DV), F32),
            jax.ShapeDtypeStruct((NS, GLA_H, GLA_DK, GLA_DV), F32),
        ],
        scratch_shapes=[pltpu.VMEM((SG, D), F32)],
        compiler_params=_params(("parallel",)),
        name="even_sample",
    )(z, zt3, grt3, lbpt, wgkt, bgkt, gnh, gng, s_hg, s_gla)


def _out_ln_kernel(x_ref, y_ref, w_ref, g_ref, b_ref, o_ref):
    r = ALPHA * x_ref[...] + _dot(y_ref[...], w_ref[...])
    o_ref[...] = _layernorm(r, g_ref[...], b_ref[...])


def _out_ln(x, y, w, g, b):
    c2 = lambda i: (0, 0)
    return pl.pallas_call(
        _out_ln_kernel,
        grid=(NT // TM,),
        in_specs=[
            pl.BlockSpec((TM, D), lambda i: (i, 0)),
            pl.BlockSpec((TM, D), lambda i: (i, 0)),
            pl.BlockSpec((D, D), c2),
            pl.BlockSpec((1, D), c2),
            pl.BlockSpec((1, D), c2),
        ],
        out_specs=pl.BlockSpec((TM, D), lambda i: (i, 0)),
        out_shape=jax.ShapeDtypeStruct((NT, D), F32),
        compiler_params=_params(("parallel",)),
        name="out_ln",
    )(x, y, w, g, b)


FF_SPLIT = 2


def _ffn_kernel(x_ref, w1_ref, w3_ref, w2_ref, g_ref, b_ref, o_ref):
    x = x_ref[...]
    xb = x.astype(BF16)
    step = D_FF_DENSE // FF_SPLIT
    acc = ALPHA * x
    for s in range(FF_SPLIT):
        cs = slice(s * step, (s + 1) * step)
        hmid = _silu(_dot(xb, w1_ref[:, cs])) * _dot(xb, w3_ref[:, cs])
        acc = acc + _dot(hmid.astype(BF16), w2_ref[cs, :])
    o_ref[...] = _layernorm(acc, g_ref[...], b_ref[...])


def _ffn(x, w1, w3, w2, g, b):
    c2 = lambda i: (0, 0)
    one = pl.Buffered(1)
    return pl.pallas_call(
        _ffn_kernel,
        grid=(NT // TM,),
        in_specs=[
            pl.BlockSpec((TM, D), lambda i: (i, 0)),
            pl.BlockSpec((D, D_FF_DENSE), c2, pipeline_mode=one),
            pl.BlockSpec((D, D_FF_DENSE), c2, pipeline_mode=one),
            pl.BlockSpec((D_FF_DENSE, D), c2, pipeline_mode=one),
            pl.BlockSpec((1, D), c2),
            pl.BlockSpec((1, D), c2),
        ],
        out_specs=pl.BlockSpec((TM, D), lambda i: (i, 0)),
        out_shape=jax.ShapeDtypeStruct((NT, D), F32),
        compiler_params=_params(("parallel",)),
        name="ffn_dense",
    )(x, w1, w3, w2, g, b)


def _mh_norm_gate(hh, o_pre, w):
    mu = jnp.mean(hh, axis=-1, keepdims=True)
    c = hh - mu
    var = jnp.mean(c * c, axis=-1, keepdims=True)
    return _sigmoid(o_pre) * (c * lax.rsqrt(var + EPS) * w)


def _odd_prompt_kernel(*refs):
    z_refs = refs[0:SEQ_PER_STEP]
    zg_refs = refs[SEQ_PER_STEP:2 * SEQ_PER_STEP]
    (bg_ref, cw_ref, cb_ref, hnw_ref, tri_ref,
     y_ref, c_out, n_out, m_out, conv_out,
     c_scr, n_scr, m_scr, u_scr) = refs[2 * SEQ_PER_STEP:]
    c = pl.program_id(1)

    @pl.when(c == 0)
    def _():
        c_scr[...] = jnp.zeros_like(c_scr)
        n_scr[...] = jnp.zeros_like(n_scr)
        m_scr[...] = jnp.zeros_like(m_scr)
        for s in range(SEQ_PER_STEP):
            u_scr[s, 0:8, :] = jnp.zeros((8, D), F32)

    row = lax.broadcasted_iota(jnp.int32, (CS, CS), 0)
    col = lax.broadcasted_iota(jnp.int32, (CS, CS), 1)
    causal = col <= row
    tails = []

    for s in range(SEQ_PER_STEP):
        z_ref = z_refs[s]
        u_scr[s, 8:8 + CS, :] = z_ref[:, 0:D]
        uc = cb_ref[...]
        for j in range(CONV_W):
            uc = uc + u_scr[s, 5 + j:5 + j + CS, :] * cw_ref[j:j + 1, :]
        tail = u_scr[s, CS:CS + 8, :]
        u_scr[s, 0:8, :] = tail
        tails.append(tail)
        act = _silu(uc)
        q = act[:, 0:512] * ML_DK ** -0.5
        k = act[:, 512:1024]
        v = z_ref[:, D:2 * D]
        o_pre = z_ref[:, 2 * D:3 * D]

        gates = zg_refs[s][...] + bg_ref[...]
        lf = _log_sigmoid(gates)
        bcum = _dot_sel(tri_ref[...], lf)
        bcum_t = bcum.T
        gates_t = gates.T
        m_all = m_scr[s]

        for h in range(ML_H):
            ks = slice(h * ML_DK, (h + 1) * ML_DK)
            vs = slice(h * ML_DV, (h + 1) * ML_DV)
            qh, kh = q[:, ks], k[:, ks]
            vh = v[:, vs].astype(BF16)
            b_col = bcum[:, 4 + h:5 + h]
            b_row = bcum_t[4 + h:5 + h, :]
            i_col = gates[:, h:h + 1]
            i_row = gates_t[h:h + 1, :]
            m_prev = m_all[:, h:h + 1]
            log_d = jnp.where(causal, b_col - b_row + i_row, -jnp.inf)
            log_prev = b_col + m_prev
            m_t = jnp.maximum(jnp.max(log_d, axis=-1, keepdims=True), log_prev)
            d = jnp.exp(log_d - m_t)
            w_prev = jnp.exp(log_prev - m_t)
            scores = _dot_nt(qh.astype(BF16), kh.astype(BF16)) * d
            c_h = c_scr[s, h]
            n_h = n_scr[s, h:h + 1, :]
            num = _dot(scores.astype(BF16), vh) + w_prev * _dot(qh.astype(BF16), c_h.astype(BF16))
            den = jnp.sum(scores, axis=-1, keepdims=True) + w_prev * jnp.sum(qh * n_h, axis=-1, keepdims=True)
            hh = num / jnp.maximum(jnp.abs(den), jnp.exp(-m_t))
            m_new = m_t[CS - 1:CS, :]
            b_last = b_col[CS - 1:CS, :]
            w_c = jnp.exp(b_last + m_prev - m_new)
            w_s = jnp.exp(b_last - b_col + i_col - m_new)
            kw = kh * w_s
            c_scr[s, h] = w_c * c_h + _dot_tn(kw.astype(BF16), vh)
            n_scr[s, h:h + 1, :] = w_c * n_h + jnp.sum(kw, axis=0, keepdims=True)
            m_scr[s, :, h:h + 1] = m_new
            y_ref[s, :, vs] = _mh_norm_gate(hh, o_pre[:, vs], hnw_ref[:, vs]).astype(BF16)

    @pl.when(c == NCHUNK - 1)
    def _():
        c_out[...] = c_scr[...]
        for s in range(SEQ_PER_STEP):
            n_out[s] = n_scr[s, 0:ML_H, :]
            m_out[s] = m_scr[s]
            conv_out[s] = tails[s][8 - (CONV_W - 1):8, :]


def _odd_prompt(z, zg, bg, cw, cb, hnw, tri):
    c2 = lambda b, c: (0, 0)
    sp = SEQ_PER_STEP
    return pl.pallas_call(
        _odd_prompt_kernel,
        grid=(BATCH // sp, NCHUNK),
        in_specs=_seq_row_specs(ODD_MAIN) + _seq_row_specs(LANES) + [
            pl.BlockSpec((1, LANES), c2),
            pl.BlockSpec((CONV_W, D), c2),
            pl.BlockSpec((1, D), c2),
            pl.BlockSpec((1, D), c2),
            pl.BlockSpec((CS, CS), c2),
        ],
        out_specs=[
            pl.BlockSpec((sp, CS, D), lambda b, c: (b, c, 0)),
            pl.BlockSpec((sp, ML_H, ML_DK, ML_DV), lambda b, c: (b, 0, 0, 0)),
            pl.BlockSpec((sp, ML_H, ML_DK), lambda b, c: (b, 0, 0)),
            pl.BlockSpec((sp, 1, LANES), lambda b, c: (b, 0, 0)),
            pl.BlockSpec((sp, CONV_W - 1, D), lambda b, c: (b, 0, 0)),
        ],
        out_shape=[
            jax.ShapeDtypeStruct((BATCH, SEQ, D), BF16),
            jax.ShapeDtypeStruct((BATCH, ML_H, ML_DK, ML_DV), F32),
            jax.ShapeDtypeStruct((BATCH, ML_H, ML_DK), F32),
            jax.ShapeDtypeStruct((BATCH, 1, LANES), F32),
            jax.ShapeDtypeStruct((BATCH, CONV_W - 1, D), F32),
        ],
        scratch_shapes=[
            pltpu.VMEM((sp, ML_H, ML_DK, ML_DV), F32),
            pltpu.VMEM((sp, 8, ML_DK), F32),
            pltpu.VMEM((sp, 1, LANES), F32),
            pltpu.VMEM((sp, CS + 8, D), F32),
        ],
        compiler_params=_params(("parallel", "arbitrary")),
        name="odd_prompt",
    )(*([z] * sp), *([zg] * sp), bg, cw, cb, hnw, tri)


def _odd_sample_kernel(zr_ref, zg_ref, ut_ref, conv_ref, convt_ref, bg_ref, cw_ref, cwt_ref, cb_ref, cbt_ref,
                       hnw_ref, c_ref, n_ref, m_ref,
                       y_ref, c_out, n_out, m_out, conv_out, h_scr):
    zr = zr_ref[...]
    u = zr[:, 0:D]
    v = zr[:, D:2 * D]
    o_pre = zr[:, 2 * D:3 * D]
    uc = cb_ref[...] + u * cw_ref[CONV_W - 1:CONV_W, :]
    uc_t = cbt_ref[...] + ut_ref[0] * cwt_ref[:, CONV_W - 1:CONV_W]
    for j in range(CONV_W - 1):
        uc = uc + conv_ref[:, j * D:(j + 1) * D] * cw_ref[j:j + 1, :]
        uc_t = uc_t + convt_ref[0, j] * cwt_ref[:, j:j + 1]
        conv_out[:, j * D:(j + 1) * D] = conv_ref[:, (j + 1) * D:(j + 2) * D] if j + 1 < CONV_W - 1 else u
    act = _silu(uc)
    k_row = act[:, 512:1024]
    act_t = _silu(uc_t)
    q_t = act_t[0:512] * ML_DK ** -0.5
    k_t = act_t[512:1024]
    q_row = act[:, 0:512] * ML_DK ** -0.5

    gates = zg_ref[...] + bg_ref[...]
    lf = _log_sigmoid(gates)
    m_in = m_ref[...]
    m_out[...] = m_in

    for j in range(SG):
        for h in range(ML_H):
            ks = slice(h * ML_DK, (h + 1) * ML_DK)
            vs = slice(h * ML_DV, (h + 1) * ML_DV)
            ig = gates[j:j + 1, h:h + 1]
            log_prev = lf[j:j + 1, 4 + h:5 + h] + m_in[j:j + 1, h:h + 1]
            m_t = jnp.maximum(ig, log_prev)
            d = jnp.exp(ig - m_t)
            w_prev = jnp.exp(log_prev - m_t)
            c_new = w_prev * c_ref[j, h] + (d * k_t[ks, j:j + 1]) * v[j:j + 1, vs]
            n_new = w_prev * n_ref[j, h:h + 1, :] + d * k_row[j:j + 1, ks]
            c_out[j, h] = c_new
            n_out[j, h:h + 1, :] = n_new
            m_out[j:j + 1, h:h + 1] = m_t
            num = jnp.sum(q_t[ks, j:j + 1] * c_new, axis=0, keepdims=True)
            den = jnp.sum(q_row[j:j + 1, ks] * n_new, axis=-1, keepdims=True)
            h_scr[j:j + 1, vs] = num / jnp.maximum(jnp.abs(den), jnp.exp(-m_t))

    hh = h_scr[...]
    for h in range(ML_H):
        vs = slice(h * ML_DV, (h + 1) * ML_DV)
        y_ref[:, vs] = _mh_norm_gate(hh[:, vs], o_pre[:, vs], hnw_ref[:, vs]).astype(BF16)


def _odd_sample(z, zg, ut3, conv, convt, bg, cw, cwt, cb, cbt, hnw, c_in, n_in, m_in):
    c2 = lambda g: (0, 0)
    return pl.pallas_call(
        _odd_sample_kernel,
        grid=(NS // SG,),
        in_specs=[
            pl.BlockSpec((SG, ODD_MAIN), lambda g: (NP // SG + g, 0)),
            pl.BlockSpec((SG, LANES), lambda g: (NP // SG + g, 0)),
            pl.BlockSpec((1, D, SG), lambda g: (g, 0, 0)),
            pl.BlockSpec((SG, (CONV_W - 1) * D), lambda g: (g, 0)),
            pl.BlockSpec((1, CONV_W - 1, D, SG), lambda g: (g, 0, 0, 0)),
            pl.BlockSpec((1, LANES), c2),
            pl.BlockSpec((CONV_W, D), c2),
            pl.BlockSpec((D, CONV_W), c2),
            pl.BlockSpec((1, D), c2),
            pl.BlockSpec((D, 1), c2),
            pl.BlockSpec((1, D), c2),
            pl.BlockSpec((SG, ML_H, ML_DK, ML_DV), lambda g: (g, 0, 0, 0)),
            pl.BlockSpec((SG, ML_H, ML_DK), lambda g: (g, 0, 0)),
            pl.BlockSpec((SG, LANES), lambda g: (g, 0)),
        ],
        out_specs=[
            pl.BlockSpec((SG, D), lambda g: (g, 0)),
            pl.BlockSpec((SG, ML_H, ML_DK, ML_DV), lambda g: (g, 0, 0, 0)),
            pl.BlockSpec((SG, ML_H, ML_DK), lambda g: (g, 0, 0)),
            pl.BlockSpec((SG, LANES), lambda g: (g, 0)),
            pl.BlockSpec((SG, (CONV_W - 1) * D), lambda g: (g, 0)),
        ],
        out_shape=[
            jax.ShapeDtypeStruct((NS, D), BF16),
            jax.ShapeDtypeStruct((NS, ML_H, ML_DK, ML_DV), F32),
            jax.ShapeDtypeStruct((NS, ML_H, ML_DK), F32),
            jax.ShapeDtypeStruct((NS, LANES), F32),
            jax.ShapeDtypeStruct((NS, (CONV_W - 1) * D), F32),
        ],
        scratch_shapes=[pltpu.VMEM((SG, D), F32)],
        compiler_params=_params(("parallel",)),
        name="odd_sample",
    )(z, zg, ut3, conv, convt, bg, cw, cwt, cb, cbt, hnw, c_in, n_in, m_in)


def _out_ln_router_kernel(x_ref, y_ref, w_ref, g_ref, b_ref, wr_ref, tri_ref,
                          o_ref, op_ref, meta_ref, cnt_ref, tab_ref, carry, filled):
    i = pl.program_id(0)

    @pl.when(i == 0)
    def _():
        carry[...] = jnp.zeros_like(carry)
        tab_ref[...] = jnp.zeros_like(tab_ref)
        for e in range(N_EXPERTS):
            filled[e] = 0

    r = ALPHA * x_ref[...] + _dot(y_ref[...], w_ref[...])
    x3 = _layernorm(r, g_ref[...], b_ref[...])
    o_ref[...] = x3
    op_ref[...] = pltpu.pack_elementwise([x3[:, :D // 2], x3[:, D // 2:]], packed_dtype=BF16)

    lane = lax.broadcasted_iota(jnp.int32, (TM, LANES), 1).astype(F32)
    logits = jnp.where(lane < N_EXPERTS, _dot(x3.astype(BF16), wr_ref[...]), -jnp.inf)
    m1 = jnp.max(logits, axis=-1, keepdims=True)
    i1 = jnp.min(jnp.where(logits == m1, lane, float(LANES)), axis=-1, keepdims=True)
    rest = jnp.where(lane == i1, -jnp.inf, logits)
    m2 = jnp.max(rest, axis=-1, keepdims=True)
    i2 = jnp.min(jnp.where(rest == m2, lane, float(LANES)), axis=-1, keepdims=True)
    e2 = jnp.exp(m2 - m1)
    tot = 1.0 + e2
    w1 = 1.0 / tot
    w2 = e2 / tot

    sel1 = lane == i1
    sel2 = lane == i2
    onehot = jnp.where(sel1 | sel2, 1.0, 0.0)
    in_tile = _dot(tri_ref[...], onehot.astype(BF16))
    carry[...] = carry[...] + jnp.sum(onehot, axis=0, keepdims=True)
    cnt_ref[...] = carry[...]

    meta = jnp.where(lane == 0.0, i1, 0.0)
    meta = jnp.where(lane == 1.0, i2, meta)
    meta = jnp.where(lane == 2.0, w1, meta)
    meta = jnp.where(lane == 3.0, w2, meta)
    meta_ref[...] = meta

    token = (i * TM + lax.broadcasted_iota(jnp.int32, (TM, 1), 0)).astype(F32)
    digit_hi = jnp.floor(token * (1.0 / 256.0))
    digits = jnp.where(lane == 0.0, digit_hi, jnp.where(lane == 1.0, token - 256.0 * digit_hi, 0.0))
    place = lax.broadcasted_iota(jnp.int32, (TM, TM), 0).astype(F32)
    rank_rows = in_tile.T
    chosen_rows = onehot.T
    for e in range(N_EXPERTS):
        hit = jnp.where((chosen_rows[e:e + 1, :] > 0.0) & (rank_rows[e:e + 1, :] == place), 1.0, 0.0)
        rhs = jnp.where(lane == 2.0, jnp.where(i2 == float(e), 1.0, 0.0), digits)
        got = _dot(hit.astype(BF16), rhs.astype(BF16))
        entry = 256.0 * got[:, 0:1] + got[:, 1:2] + float(NT) * got[:, 2:3]
        start = filled[e]
        count = jnp.sum(chosen_rows[e:e + 1, :]).astype(jnp.int32)
        filled[e] = start + count
        tab_ref[pl.ds(start, LANES), e:e + 1] = entry[0:LANES]
        for part in range(1, TM // LANES):
            @pl.when(count > part * LANES)
            def _():
                tab_ref[pl.ds(start + part * LANES, LANES), e:e + 1] = entry[part * LANES:(part + 1) * LANES]


def _out_ln_router(x, y, w, g, b, wr, tri):
    c2 = lambda i: (0, 0)
    return pl.pallas_call(
        _out_ln_router_kernel,
        grid=(NT // TM,),
        in_specs=[
            pl.BlockSpec((TM, D), lambda i: (i, 0)),
            pl.BlockSpec((TM, D), lambda i: (i, 0)),
            pl.BlockSpec((D, D), c2),
            pl.BlockSpec((1, D), c2),
            pl.BlockSpec((1, D), c2),
            pl.BlockSpec((D, LANES), c2),
            pl.BlockSpec((TM, TM), c2),
        ],
        out_specs=[
            pl.BlockSpec((TM, D), lambda i: (i, 0)),
            pl.BlockSpec((TM, D // 2), lambda i: (i, 0)),
            pl.BlockSpec((TM, LANES), lambda i: (i, 0)),
            pl.BlockSpec((1, LANES), c2),
            pl.BlockSpec((MOE_CAP + TM, LANES), c2),
        ],
        out_shape=[
            jax.ShapeDtypeStruct((NT, D), F32),
            jax.ShapeDtypeStruct((NT, D // 2), U32),
            jax.ShapeDtypeStruct((NT, LANES), F32),
            jax.ShapeDtypeStruct((1, LANES), F32),
            jax.ShapeDtypeStruct((MOE_CAP + TM, LANES), F32),
        ],
        scratch_shapes=[pltpu.VMEM((1, LANES), F32), pltpu.SMEM((N_EXPERTS,), jnp.int32)],
        compiler_params=_params(("arbitrary",)),
        name="out_ln_router",
    )(x, y, w, g, b, wr, tri)


def _moe_ffn_kernel(te_ref, nu_ref, tb_ref, gnext_ref, gcur_ref, sprev_ref, scur_ref, xp_ref,
                    w1_ref, w3_ref, w2_ref, out_hbm, stage, yacc, xb_scr, sem_s):
    del tb_ref
    i = pl.program_id(0)
    j = pl.program_id(1)
    used = i < nu_ref[0]
    slot = i % 2
    other = 1 - slot
    rps = MOE_ROWS_PER_STEP

    def gather_rows(tab_ref, buf, part):
        for r in range(rps):
            stage[buf, part, pl.ds(r, 1), :] = xp_ref[pl.ds(tab_ref[part * rps + r], 1), :]

    def scatter(buf, r, dst):
        return pltpu.make_async_copy(yacc.at[buf, pl.ds(r, 1)], out_hbm.at[pl.ds(dst, 1)], sem_s)

    def wait_scatters(n):
        for _ in range(n):
            scatter(0, 0, 0).wait()

    def issue_neighbours():
        gather_rows(gnext_ref, other, j)
        for r in range(rps):
            rr = j * rps + r
            scatter(other, rr, sprev_ref[rr]).start()

    @pl.when(j == 0)
    def _():
        @pl.when(i == 0)
        def _():
            yacc[1] = jnp.zeros((TMM, D), F32)
            for part in range(MOE_NFF):
                gather_rows(gcur_ref, 0, part)

        @pl.when(i > 0)
        def _():
            wait_scatters(TMM)

        words = stage[slot].reshape(TMM, D // 2)
        for half in range(2):
            xb_scr[:, half * (D // 2):(half + 1) * (D // 2)] = pltpu.unpack_elementwise(
                words, index=half, packed_dtype=BF16, unpacked_dtype=F32).astype(BF16)
        yacc[slot] = jnp.zeros((TMM, D), F32)

    @pl.when(used)
    def _():
        issue_neighbours()
        xb = xb_scr[...]
        hmid = _silu(_dot(xb, w1_ref[...])) * _dot(xb, w3_ref[...])
        yacc[slot] += _dot(hmid.astype(BF16), w2_ref[...])

    @pl.when(jnp.logical_not(used))
    def _():
        issue_neighbours()

    @pl.when((i == N_MOE_TILES - 1) & (j == MOE_NFF - 1))
    def _():
        for r in range(TMM):
            scatter(slot, r, scur_ref[r]).start()
        wait_scatters(2 * TMM)


def _moe_ffn(tile_expert, n_used, tab_block, gsrc, sdst, xp, w1, w3, w2):
    nff = MOE_NFF

    def wcol(i, j, te, nu, tb):
        return (te[i], 0, jnp.where(i < nu[0], j, nff - 1))

    def wrow(i, j, te, nu, tb):
        return (te[i], jnp.where(i < nu[0], j, nff - 1), 0)

    smem = functools.partial(pl.BlockSpec, (MOE_TAB,), memory_space=pltpu.SMEM)
    grid_spec = pltpu.PrefetchScalarGridSpec(
        num_scalar_prefetch=3,
        grid=(N_MOE_TILES, nff),
        in_specs=[
            smem(lambda i, j, te, nu, tb: (tb[i + 2],)),
            smem(lambda i, j, te, nu, tb: (tb[i + 1],)),
            smem(lambda i, j, te, nu, tb: (tb[i],)),
            smem(lambda i, j, te, nu, tb: (tb[i + 1],)),
            pl.BlockSpec((NT, D // 2), lambda i, j, te, nu, tb: (0, 0), pipeline_mode=pl.Buffered(1)),
            pl.BlockSpec((None, D, TFF), wcol),
            pl.BlockSpec((None, D, TFF), wcol),
            pl.BlockSpec((None, TFF, D), wrow),
        ],
        out_specs=pl.BlockSpec(memory_space=pl.ANY),
        scratch_shapes=[
            pltpu.VMEM((2, MOE_NFF, MOE_ROWS_PER_STEP, D // 2), U32),
            pltpu.VMEM((2, TMM, D), F32),
            pltpu.VMEM((TMM, D), BF16),
            pltpu.SemaphoreType.DMA(()),
        ],
    )
    return pl.pallas_call(
        _moe_ffn_kernel,
        grid_spec=grid_spec,
        out_shape=jax.ShapeDtypeStruct((MOE_OUT_ROWS, D), F32),
        compiler_params=_params(("arbitrary", "arbitrary")),
        name="moe_ffn",
    )(tile_expert, n_used, tab_block, gsrc, gsrc, sdst, sdst, xp, w1, w3, w2)


def _combine_kernel(x_ref, meta_ref, y0_ref, y1_ref, g_ref, b_ref, o_ref):
    meta = meta_ref[...]
    moe = meta[:, 2:3] * y0_ref[...] + meta[:, 3:4] * y1_ref[...]
    o_ref[...] = _layernorm(ALPHA * x_ref[...] + moe, g_ref[...], b_ref[...])


def _combine(x, meta, ys, g, b):
    c2 = lambda i: (0, 0)
    return pl.pallas_call(
        _combine_kernel,
        grid=(NT // TM,),
        in_specs=[
            pl.BlockSpec((TM, D), lambda i: (i, 0)),
            pl.BlockSpec((TM, LANES), lambda i: (i, 0)),
            pl.BlockSpec((TM, D), lambda i: (i, 0)),
            pl.BlockSpec((TM, D), lambda i: (i + NT // TM, 0)),
            pl.BlockSpec((1, D), c2),
            pl.BlockSpec((1, D), c2),
        ],
        out_specs=pl.BlockSpec((TM, D), lambda i: (i, 0)),
        out_shape=jax.ShapeDtypeStruct((NT, D), F32),
        compiler_params=_params(("parallel",)),
        name="moe_combine",
    )(x, meta, ys, ys, g, b)


def _pad_cols(w, n):
    return jnp.pad(w, ((0, 0), (0, n - w.shape[1])))


def kernel(x_prompt, x_sample, state_hgrn, state_gla, state_mlstm_C, state_mlstm_n, state_mlstm_m,
           state_mlstm_conv, w_in_even, hg_lower_bounds, w_gk, b_gk, gn_hg, gn_gla, w_out_even,
           w1_dense, w3_dense, w2_dense, w_in_odd, b_gate_odd, conv_w, conv_b, hn_w, w_out_odd,
           w_router, w1_moe, w3_moe, w2_moe, ln1_g, ln1_b, ln2_g, ln2_b):
    assert x_prompt.shape == (BATCH, SEQ, D) and x_sample.shape == (NS, 1, D)
    assert w_in_even.shape[0] == 1 and w_in_odd.shape[0] == 1 and hg_lower_bounds.shape[0] == 2
    masks = jnp.asarray(_gla_masks(), F32)
    tri_cs = jnp.asarray(_tri(CS, False), BF16)
    tri_tm = jnp.asarray(_tri(TM, True), BF16)
    row = lambda a: a.reshape(1, -1)

    x0 = jnp.concatenate([x_prompt.reshape(NP, D), x_sample.reshape(NS, D)], axis=0)

    w_even = w_in_even[0].astype(BF16)
    z, zgr = _proj(x0, w_even[:, :EVEN_MAIN], _pad_cols(w_even[:, EVEN_MAIN:], LANES))
    wgk = jnp.pad(w_gk[0].astype(BF16), ((0, LANES - GLA_RANK), (0, 0)))
    lbp = hg_lower_bounds
    y_p, hg_p, gla_p = _even_prompt(z, zgr, lbp, wgk, row(b_gk[0]), row(gn_hg[0]), row(gn_gla[0]), tri_cs, masks)

    zs = z[NP:].reshape(NS // SG, SG, EVEN_MAIN).transpose(0, 2, 1)
    grs = zgr[NP:].reshape(NS // SG, SG, LANES).transpose(0, 2, 1)
    y_s, hg_s, gla_s = _even_sample(z, zs, grs, lbp.T, wgk.T, b_gk[0].reshape(-1, 1),
                                    row(gn_hg[0]), row(gn_gla[0]), state_hgrn[0], state_gla[0])
    y = jnp.concatenate([y_p.reshape(NP, D), y_s], axis=0)
    x1 = _out_ln(x0, y, w_out_even[0].astype(BF16), row(ln1_g[0]), row(ln1_b[0]))
    x2 = _ffn(x1, w1_dense[0].astype(BF16), w3_dense[0].astype(BF16), w2_dense[0].astype(BF16),
              row(ln2_g[0]), row(ln2_b[0]))

    w_odd = w_in_odd[0].astype(BF16)
    zo, zog = _proj(x2, w_odd[:, :ODD_MAIN], _pad_cols(w_odd[:, ODD_MAIN:], LANES))
    bg = jnp.pad(b_gate_odd[0], (0, LANES - 2 * ML_H)).reshape(1, LANES)
    yo_p, c_p, n_p, m_p, conv_p = _odd_prompt(zo, zog, bg, conv_w[0], row(conv_b[0]), row(hn_w[0]), tri_cs)

    ut = zo[NP:, :D].reshape(NS // SG, SG, D).transpose(0, 2, 1)
    conv_in = state_mlstm_conv[0]
    conv_t = conv_in.reshape(NS // SG, SG, CONV_W - 1, D).transpose(0, 2, 3, 1)
    m_in = jnp.pad(state_mlstm_m[0], ((0, 0), (0, LANES - ML_H)))
    yo_s, c_s, n_s, m_s, conv_s = _odd_sample(
        zo, zog, ut, conv_in.reshape(NS, (CONV_W - 1) * D), conv_t, bg, conv_w[0], conv_w[0].T, row(conv_b[0]), conv_b[0].reshape(-1, 1),
        row(hn_w[0]), state_mlstm_C[0], state_mlstm_n[0], m_in)
    yo = jnp.concatenate([yo_p.reshape(NP, D), yo_s], axis=0)

    wr = _pad_cols(w_router[0].astype(BF16), LANES)
    x3, x3p, meta, cnt, tab = _out_ln_router(
        x2, yo, w_out_odd[0].astype(BF16), row(ln1_g[1]), row(ln1_b[1]), wr, tri_tm)

    counts = cnt[0, :N_EXPERTS].astype(jnp.int32)
    padded = ((counts + TMM - 1) // TMM) * TMM
    ends = jnp.cumsum(padded)
    offsets = ends - padded
    tile = jnp.arange(N_MOE_TILES, dtype=jnp.int32)
    tile_expert = jnp.minimum(jnp.sum((tile * TMM)[:, None] >= ends[None, :], axis=1), N_EXPERTS - 1).astype(jnp.int32)
    n_tiles_used = ends[-1] // TMM
    in_use = tile < n_tiles_used
    local_tile = (tile * TMM - offsets[tile_expert]) // TMM
    blocks_per_expert = MOE_CAP // MOE_TAB
    spill_block = N_EXPERTS * blocks_per_expert
    n_spare = N_MOE_TILES - (2 * NT) // TMM
    entry = tab[:MOE_CAP, :N_EXPERTS].T.astype(jnp.int32)
    local = jnp.arange(MOE_CAP, dtype=jnp.int32)[None, :]
    valid = local < counts[:, None]
    padding = padded - counts
    pad_before = (jnp.cumsum(padding) - padding)[:, None]
    row_in_block = jnp.arange(MOE_TAB, dtype=jnp.int32)
    spare_rows = (2 * NT + TMM + jnp.sum(padding)
                  + jnp.arange(n_spare, dtype=jnp.int32)[:, None] * TMM + row_in_block[None, :])
    sdst = jnp.concatenate([
        jnp.where(valid, entry, 2 * NT + TMM + pad_before + local - counts[:, None]).reshape(-1),
        2 * NT + row_in_block, spare_rows.reshape(-1)])
    gsrc = jnp.concatenate([
        jnp.where(valid, entry - jnp.where(entry >= NT, NT, 0), 0).reshape(-1),
        jnp.zeros(((1 + n_spare) * MOE_TAB,), jnp.int32)])
    own_block = jnp.where(in_use, tile_expert * blocks_per_expert + local_tile, spill_block + 1 + tile - n_tiles_used)
    tab_block = jnp.concatenate([jnp.full((1,), spill_block, jnp.int32), own_block.astype(jnp.int32),
                                 jnp.full((1,), spill_block, jnp.int32)])

    ys = _moe_ffn(tile_expert, n_tiles_used.reshape(1), tab_block, gsrc, sdst, x3p,
                  w1_moe[0].astype(BF16), w3_moe[0].astype(BF16), w2_moe[0].astype(BF16))
    out = _combine(x3, meta, ys, row(ln2_g[1]), row(ln2_b[1]))

    y_prompt = out[:NP].reshape(BATCH, SEQ, D)
    y_sample = out[NP:].reshape(NS, 1, D)
    return (y_prompt, y_sample,
            hg_p.reshape(1, BATCH, HG_H, HG_DK, HG_DV), gla_p.reshape(1, BATCH, GLA_H, GLA_DK, GLA_DV),
            c_p[None], n_p[None], m_p[:, 0, :ML_H][None], conv_p[None],
            hg_s[None], gla_s[None], c_s[None], n_s[None], m_s[:, :ML_H][None], conv_s.reshape(1, NS, CONV_W - 1, D))
```

```python
import functools
import math

import jax
import jax.numpy as jnp
import numpy as np
from jax import lax
from jax.experimental import pallas as pl
from jax.experimental.pallas import tpu as pltpu

F32 = jnp.float32
BF16 = jnp.bfloat16
U32 = jnp.uint32

D = 1024
BATCH = 8
SEQ = 2048
DEC_BATCH = 128
NP = BATCH * SEQ
NS = DEC_BATCH
NT = NP + NS
HG_H, HG_DK, HG_DV = 4, 128, 128
GLA_H, GLA_DK, GLA_DV = 4, 64, 128
GLA_RANK = 16
GLA_GATE_NORM = 16.0
ML_H, ML_DK, ML_DV = 4, 128, 256
CONV_W = 4
D_FF_DENSE = 2816
D_FF_EXPERT = 3584
N_EXPERTS = 8
EPS = 1e-5
DEPTH = 2
ALPHA = (2.0 * DEPTH) ** 0.25
EVEN_MAIN = 3584
ODD_MAIN = 3072

LANES = 128
SUBLANES = 8
VMEM_LIMIT = 56 * 1024 * 1024

TM = 384
CS = 128
NCHUNK = SEQ // CS
SEQ_PER_STEP = 1
SG = 16
TMM = 512
TFF = 896
MOE_TAB = 512
MOE_CAP = -(-NT // MOE_TAB) * MOE_TAB
MOE_NFF = D_FF_EXPERT // TFF
MOE_ROWS_PER_STEP = TMM // MOE_NFF
N_MOE_TILES = -(-(2 * NT + N_EXPERTS * (TMM - 1)) // TMM)
MOE_SLOTS = N_MOE_TILES * TMM
MOE_OUT_ROWS = MOE_SLOTS + TMM
N_LEVELS = int(math.log2(CS))

assert NT % TM == 0 and NP % CS == 0 and NS % SG == 0 and D_FF_EXPERT % TFF == 0 and TMM % MOE_NFF == 0
assert TMM == MOE_TAB


def _params(sem, limit=VMEM_LIMIT):
    return pltpu.CompilerParams(dimension_semantics=sem, vmem_limit_bytes=limit)


def _dot(a, b):
    return jnp.dot(a, b, preferred_element_type=F32)


def _dot_nt(a, b):
    return lax.dot_general(a, b, (((1,), (1,)), ((), ())), preferred_element_type=F32)


def _dot_tn(a, b):
    return lax.dot_general(a, b, (((0,), (0,)), ((), ())), preferred_element_type=F32)


def _split3(x):
    hi = x.astype(BF16)
    r1 = x - hi.astype(F32)
    mid = r1.astype(BF16)
    lo = (r1 - mid.astype(F32)).astype(BF16)
    return hi, mid, lo


def _dot_sel(sel, x):
    hi, mid, lo = _split3(x)
    return _dot(sel, hi) + _dot(sel, mid) + _dot(sel, lo)


def _sigmoid(x):
    return jax.nn.sigmoid(x)


def _silu(x):
    return x * jax.nn.sigmoid(x)


def _log_sigmoid(x):
    return jnp.minimum(x, 0.0) - jnp.log(1.0 + jnp.exp(-jnp.abs(x)))


def _layernorm(r, g, b):
    mu = jnp.mean(r, axis=-1, keepdims=True)
    c = r - mu
    var = jnp.mean(c * c, axis=-1, keepdims=True)
    return c * lax.rsqrt(var + EPS) * g + b


def _gla_masks():
    masks = np.zeros((N_LEVELS + 1, CS, CS), np.float32)
    for t in range(CS):
        for l in range(N_LEVELS):
            half = 1 << l
            start = (t // (2 * half)) * (2 * half)
            mid = start + half
            if t >= mid:
                masks[l, t, start:mid] = 1.0
        masks[N_LEVELS, t, t] = 1.0
    return masks


def _tri(n, strict):
    return np.tril(np.ones((n, n), np.float32), -1 if strict else 0)


def _proj_kernel(x_ref, wa_ref, wb_ref, oa_ref, ob_ref):
    xb = x_ref[...].astype(BF16)
    oa_ref[...] = _dot(xb, wa_ref[...])
    ob_ref[...] = _dot(xb, wb_ref[...])


def _proj(x, wa, wb):
    na, nb = wa.shape[1], wb.shape[1]
    return pl.pallas_call(
        _proj_kernel,
        grid=(NT // TM,),
        in_specs=[
            pl.BlockSpec((TM, D), lambda i: (i, 0)),
            pl.BlockSpec((D, na), lambda i: (0, 0)),
            pl.BlockSpec((D, nb), lambda i: (0, 0)),
        ],
        out_specs=[
            pl.BlockSpec((TM, na), lambda i: (i, 0)),
            pl.BlockSpec((TM, nb), lambda i: (i, 0)),
        ],
        out_shape=[jax.ShapeDtypeStruct((NT, na), F32), jax.ShapeDtypeStruct((NT, nb), F32)],
        compiler_params=_params(("parallel",)),
        name="proj",
    )(x, wa, wb)


def _rms_gate(o, gate, w):
    o = o * lax.rsqrt(jnp.mean(o * o, axis=-1, keepdims=True) + EPS) * w
    return o * _silu(gate)


def _level_decays(g, bc):
    width = g.shape[1]
    ng = CS // SUBLANES
    shape3 = (ng, SUBLANES, width)
    bc3 = bc.reshape(shape3)
    sub = lax.broadcasted_iota(jnp.int32, shape3, 1)

    def group_row(s):
        return jnp.broadcast_to(bc3[:, s:s + 1, :], shape3)

    last = group_row(SUBLANES - 1)
    refs = [None,
            jnp.where(sub < 4, group_row(1), group_row(5)),
            group_row(3)]
    for l in range(3, N_LEVELS):
        per_block = 1 << (l - 2)
        grouped = last.reshape(ng // per_block, per_block, SUBLANES, width)
        ref = jnp.broadcast_to(grouped[:, per_block // 2 - 1:per_block // 2], grouped.shape)
        refs.append(ref.reshape(shape3))
    decays = [jnp.exp(jnp.where((sub & 1) == 1, g.reshape(shape3), 0.0))]
    decays += [jnp.exp(-jnp.abs(bc3 - ref)) for ref in refs[1:]]
    to_end = jnp.exp(jnp.broadcast_to(last[ng - 1:ng], shape3) - bc3)
    return [d.reshape(CS, width) for d in decays], to_end.reshape(CS, width)


def _gla_chunk(q, k, v, g, st_ref, tri, masks_ref, heads, dk, dv):
    bc = _dot_sel(tri, g)
    z_levels, z_end = _level_decays(g, bc)
    z_cum = jnp.exp(bc)
    st = st_ref[...]
    outs = []
    for h in range(heads):
        ks = slice(h * dk, (h + 1) * dk)
        vs = slice(h * dv, (h + 1) * dv)
        qh, kh = q[:, ks], k[:, ks]
        vh = v[:, vs].astype(BF16)
        scores = _dot_nt(qh.astype(BF16), kh.astype(BF16)) * masks_ref[N_LEVELS]
        for l in range(N_LEVELS):
            zl = z_levels[l][:, ks]
            scores = scores + _dot_nt((qh * zl).astype(BF16), (kh * zl).astype(BF16)) * masks_ref[l]
        o = _dot(scores.astype(BF16), vh)
        o = o + _dot_nt((qh * z_cum[:, ks]).astype(BF16), st[:, ks].astype(BF16))
        outs.append(o)
        upd = _dot_tn(vh, (kh * z_end[:, ks]).astype(BF16))
        st_ref[:, ks] = st[:, ks] * z_cum[CS - 1:CS, ks] + upd
    return outs


def _even_prompt_kernel(*refs):
    z_refs = refs[0:SEQ_PER_STEP]
    zgr_refs = refs[SEQ_PER_STEP:2 * SEQ_PER_STEP]
    (lbp_ref, wgk_ref, bgk_ref, gnh_ref, gng_ref, tri_ref, masks_ref,
     y_ref, shg_ref, sgla_ref, st_hg, st_gla) = refs[2 * SEQ_PER_STEP:]
    c = pl.program_id(1)

    @pl.when(c == 0)
    def _():
        st_hg[...] = jnp.zeros_like(st_hg)
        st_gla[...] = jnp.zeros_like(st_gla)

    tri = tri_ref[...]
    p = lbp_ref[...]
    pe = jnp.exp(p - jnp.max(p, axis=0, keepdims=True))
    lb = pe[0:1] / jnp.sum(pe, axis=0, keepdims=True)

    for s in range(SEQ_PER_STEP):
        z = z_refs[s][...]
        hq, hf, hi, hg = z[:, 0:512], z[:, 512:1024], z[:, 1024:1536], z[:, 1536:2048]
        gq, gk, gv, gg = z[:, 2048:2304], z[:, 2304:2560], z[:, 2560:3072], z[:, 3072:3584]
        f = lb + (1.0 - lb) * _sigmoid(hf)
        k_hg = (1.0 - lb) * _sigmoid(-hf)
        o_hg = _gla_chunk(_silu(hq), k_hg, hi, jnp.log(f), st_hg.at[s], tri, masks_ref, HG_H, HG_DK, HG_DV)

        la = _log_sigmoid(_dot(zgr_refs[s][...].astype(BF16), wgk_ref[...]) + bgk_ref[...]) / GLA_GATE_NORM
        o_gla = _gla_chunk(gq * GLA_DK ** -0.5, gk, gv, la, st_gla.at[s], tri, masks_ref, GLA_H, GLA_DK, GLA_DV)

        for h in range(HG_H):
            cs = slice(h * 128, (h + 1) * 128)
            y_ref[s, :, cs] = _rms_gate(o_hg[h], hg[:, cs], gnh_ref[...]).astype(BF16)
        for h in range(GLA_H):
            cs = slice(h * 128, (h + 1) * 128)
            y_ref[s, :, 512 + h * 128:512 + (h + 1) * 128] = _rms_gate(o_gla[h], gg[:, cs], gng_ref[...]).astype(BF16)

    @pl.when(c == NCHUNK - 1)
    def _():
        for s in range(SEQ_PER_STEP):
            shg_ref[s] = st_hg[s].T
            sgla_ref[s] = st_gla[s].T


def _seq_row_specs(width):
    return [pl.BlockSpec((CS, width), functools.partial(lambda b, c, s: ((SEQ_PER_STEP * b + s) * NCHUNK + c, 0), s=s))
            for s in range(SEQ_PER_STEP)]


def _even_prompt(z, zgr, lbp, wgk, bgk, gnh, gng, tri, masks):
    const2 = lambda b, c: (0, 0)
    sp = SEQ_PER_STEP
    return pl.pallas_call(
        _even_prompt_kernel,
        grid=(BATCH // sp, NCHUNK),
        in_specs=_seq_row_specs(EVEN_MAIN) + _seq_row_specs(LANES) + [
            pl.BlockSpec(lbp.shape, const2),
            pl.BlockSpec(wgk.shape, const2),
            pl.BlockSpec(bgk.shape, const2),
            pl.BlockSpec(gnh.shape, const2),
            pl.BlockSpec(gng.shape, const2),
            pl.BlockSpec(tri.shape, const2),
            pl.BlockSpec(masks.shape, lambda b, c: (0, 0, 0)),
        ],
        out_specs=[
            pl.BlockSpec((sp, CS, D), lambda b, c: (b, c, 0)),
            pl.BlockSpec((sp, HG_H * HG_DK, HG_DV), lambda b, c: (b, 0, 0)),
            pl.BlockSpec((sp, GLA_H * GLA_DK, GLA_DV), lambda b, c: (b, 0, 0)),
        ],
        out_shape=[
            jax.ShapeDtypeStruct((BATCH, SEQ, D), BF16),
            jax.ShapeDtypeStruct((BATCH, HG_H * HG_DK, HG_DV), F32),
            jax.ShapeDtypeStruct((BATCH, GLA_H * GLA_DK, GLA_DV), F32),
        ],
        scratch_shapes=[pltpu.VMEM((sp, HG_DV, HG_H * HG_DK), F32), pltpu.VMEM((sp, GLA_DV, GLA_H * GLA_DK), F32)],
        compiler_params=_params(("parallel", "arbitrary")),
        name="even_prompt",
    )(*([z] * sp), *([zgr] * sp), lbp, wgk, bgk, gnh, gng, tri, masks)


def _even_sample_kernel(zr_ref, zt_ref, grt_ref, lbpt_ref, wgkt_ref, bgkt_ref, gnh_ref, gng_ref,
                        shg_ref, sgla_ref, y_ref, shg_out, sgla_out, o_scr):
    zt = zt_ref[0]
    hq_t, hf_t = zt[0:512], zt[512:1024]
    gq_t, gk_t = zt[2048:2304], zt[2304:2560]
    pt = lbpt_ref[...]
    pe = jnp.exp(pt - jnp.max(pt, axis=1, keepdims=True))
    lb = pe[:, 0:1] / jnp.sum(pe, axis=1, keepdims=True)
    a_hg = jnp.exp(jnp.log(lb + (1.0 - lb) * _sigmoid(hf_t)))
    k_hg = (1.0 - lb) * _sigmoid(-hf_t)
    q_hg = _silu(hq_t)
    la = _log_sigmoid(_dot(wgkt_ref[...], grt_ref[0].astype(BF16)) + bgkt_ref[...]) / GLA_GATE_NORM
    a_gla = jnp.exp(la)
    q_gla = gq_t * GLA_DK ** -0.5
    zr = zr_ref[...]
    hi, hg = zr[:, 1024:1536], zr[:, 1536:2048]
    gv, gg = zr[:, 2560:3072], zr[:, 3072:3584]

    for j in range(SG):
        for h in range(HG_H):
            ks = slice(h * HG_DK, (h + 1) * HG_DK)
            s_new = a_hg[ks, j:j + 1] * shg_ref[j, h] + k_hg[ks, j:j + 1] * hi[j:j + 1, h * 128:(h + 1) * 128]
            shg_out[j, h] = s_new
            o_scr[j:j + 1, h * 128:(h + 1) * 128] = jnp.sum(q_hg[ks, j:j + 1] * s_new, axis=0, keepdims=True)
        for h in range(GLA_H):
            ks = slice(h * GLA_DK, (h + 1) * GLA_DK)
            s_new = a_gla[ks, j:j + 1] * sgla_ref[j, h] + gk_t[ks, j:j + 1] * gv[j:j + 1, h * 128:(h + 1) * 128]
            sgla_out[j, h] = s_new
            o_scr[j:j + 1, 512 + h * 128:512 + (h + 1) * 128] = jnp.sum(
                q_gla[ks, j:j + 1] * s_new, axis=0, keepdims=True)

    o = o_scr[...]
    for h in range(HG_H):
        cs = slice(h * 128, (h + 1) * 128)
        y_ref[:, cs] = _rms_gate(o[:, cs], hg[:, cs], gnh_ref[...]).astype(BF16)
    for h in range(GLA_H):
        cs = slice(512 + h * 128, 512 + (h + 1) * 128)
        y_ref[:, cs] = _rms_gate(o[:, cs], gg[:, h * 128:(h + 1) * 128], gng_ref[...]).astype(BF16)


def _even_sample(z, zt3, grt3, lbpt, wgkt, bgkt, gnh, gng, s_hg, s_gla):
    c2 = lambda g: (0, 0)
    return pl.pallas_call(
        _even_sample_kernel,
        grid=(NS // SG,),
        in_specs=[
            pl.BlockSpec((SG, EVEN_MAIN), lambda g: (NP // SG + g, 0)),
            pl.BlockSpec((1, EVEN_MAIN, SG), lambda g: (g, 0, 0)),
            pl.BlockSpec((1, LANES, SG), lambda g: (g, 0, 0)),
            pl.BlockSpec(lbpt.shape, c2),
            pl.BlockSpec(wgkt.shape, c2),
            pl.BlockSpec(bgkt.shape, c2),
            pl.BlockSpec(gnh.shape, c2),
            pl.BlockSpec(gng.shape, c2),
            pl.BlockSpec((SG, HG_H, HG_DK, HG_DV), lambda g: (g, 0, 0, 0)),
            pl.BlockSpec((SG, GLA_H, GLA_DK, GLA_DV), lambda g: (g, 0, 0, 0)),
        ],
        out_specs=[
            pl.BlockSpec((SG, D), lambda g: (g, 0)),
            pl.BlockSpec((SG, HG_H, HG_DK, HG_DV), lambda g: (g, 0, 0, 0)),
            pl.BlockSpec((SG, GLA_H, GLA_DK, GLA_DV), lambda g: (g, 0, 0, 0)),
        ],
        out_shape=[
            jax.ShapeDtypeStruct((NS, D), BF16),
            jax.ShapeDtypeStruct((NS, HG_H, HG_DK, HG_DV), F32),
            jax.ShapeDtypeStruct((NS, GLA_H, GLA_DK, GLA_DV), F32),
        ],
        scratch_shapes=[pltpu.VMEM((SG, D), F32)],
        compiler_params=_params(("parallel",)),
        name="even_sample",
    )(z, zt3, grt3, lbpt, wgkt, bgkt, gnh, gng, s_hg, s_gla)


def _out_ln_kernel(x_ref, y_ref, w_ref, g_ref, b_ref, o_ref):
    r = ALPHA * x_ref[...] + _dot(y_ref[...], w_ref[...])
    o_ref[...] = _layernorm(r, g_ref[...], b_ref[...])


def _out_ln(x, y, w, g, b):
    c2 = lambda i: (0, 0)
    return pl.pallas_call(
        _out_ln_kernel,
        grid=(NT // TM,),
        in_specs=[
            pl.BlockSpec((TM, D), lambda i: (i, 0)),
            pl.BlockSpec((TM, D), lambda i: (i, 0)),
            pl.BlockSpec((D, D), c2),
            pl.BlockSpec((1, D), c2),
            pl.BlockSpec((1, D), c2),
        ],
        out_specs=pl.BlockSpec((TM, D), lambda i: (i, 0)),
        out_shape=jax.ShapeDtypeStruct((NT, D), F32),
        compiler_params=_params(("parallel",)),
        name="out_ln",
    )(x, y, w, g, b)


FF_SPLIT = 2


def _ffn_kernel(x_ref, w1_ref, w3_ref, w2_ref, g_ref, b_ref, o_ref):
    x = x_ref[...]
    xb = x.astype(BF16)
    step = D_FF_DENSE // FF_SPLIT
    acc = ALPHA * x
    for s in range(FF_SPLIT):
        cs = slice(s * step, (s + 1) * step)
        hmid = _silu(_dot(xb, w1_ref[:, cs])) * _dot(xb, w3_ref[:, cs])
        acc = acc + _dot(hmid.astype(BF16), w2_ref[cs, :])
    o_ref[...] = _layernorm(acc, g_ref[...], b_ref[...])


def _ffn(x, w1, w3, w2, g, b):
    c2 = lambda i: (0, 0)
    one = pl.Buffered(1)
    return pl.pallas_call(
        _ffn_kernel,
        grid=(NT // TM,),
        in_specs=[
            pl.BlockSpec((TM, D), lambda i: (i, 0)),
            pl.BlockSpec((D, D_FF_DENSE), c2, pipeline_mode=one),
            pl.BlockSpec((D, D_FF_DENSE), c2, pipeline_mode=one),
            pl.BlockSpec((D_FF_DENSE, D), c2, pipeline_mode=one),
            pl.BlockSpec((1, D), c2),
            pl.BlockSpec((1, D), c2),
        ],
        out_specs=pl.BlockSpec((TM, D), lambda i: (i, 0)),
        out_shape=jax.ShapeDtypeStruct((NT, D), F32),
        compiler_params=_params(("parallel",)),
        name="ffn_dense",
    )(x, w1, w3, w2, g, b)


def _mh_norm_gate(hh, o_pre, w):
    mu = jnp.mean(hh, axis=-1, keepdims=True)
    c = hh - mu
    var = jnp.mean(c * c, axis=-1, keepdims=True)
    return _sigmoid(o_pre) * (c * lax.rsqrt(var + EPS) * w)


def _odd_prompt_kernel(*refs):
    z_refs = refs[0:SEQ_PER_STEP]
    zg_refs = refs[SEQ_PER_STEP:2 * SEQ_PER_STEP]
    (bg_ref, cw_ref, cb_ref, hnw_ref, tri_ref,
     y_ref, c_out, n_out, m_out, conv_out,
     c_scr, n_scr, m_scr, u_scr) = refs[2 * SEQ_PER_STEP:]
    c = pl.program_id(1)

    @pl.when(c == 0)
    def _():
        c_scr[...] = jnp.zeros_like(c_scr)
        n_scr[...] = jnp.zeros_like(n_scr)
        m_scr[...] = jnp.zeros_like(m_scr)
        for s in range(SEQ_PER_STEP):
            u_scr[s, 0:8, :] = jnp.zeros((8, D), F32)

    row = lax.broadcasted_iota(jnp.int32, (CS, CS), 0)
    col = lax.broadcasted_iota(jnp.int32, (CS, CS), 1)
    causal = col <= row
    tails = []

    for s in range(SEQ_PER_STEP):
        z_ref = z_refs[s]
        u_scr[s, 8:8 + CS, :] = z_ref[:, 0:D]
        uc = cb_ref[...]
        for j in range(CONV_W):
            uc = uc + u_scr[s, 5 + j:5 + j + CS, :] * cw_ref[j:j + 1, :]
        tail = u_scr[s, CS:CS + 8, :]
        u_scr[s, 0:8, :] = tail
        tails.append(tail)
        act = _silu(uc)
        q = act[:, 0:512] * ML_DK ** -0.5
        k = act[:, 512:1024]
        v = z_ref[:, D:2 * D]
        o_pre = z_ref[:, 2 * D:3 * D]

        gates = zg_refs[s][...] + bg_ref[...]
        lf = _log_sigmoid(gates)
        bcum = _dot_sel(tri_ref[...], lf)
        bcum_t = bcum.T
        gates_t = gates.T
        m_all = m_scr[s]

        for h in range(ML_H):
            ks = slice(h * ML_DK, (h + 1) * ML_DK)
            vs = slice(h * ML_DV, (h + 1) * ML_DV)
            qh, kh = q[:, ks], k[:, ks]
            vh = v[:, vs].astype(BF16)
            b_col = bcum[:, 4 + h:5 + h]
            b_row = bcum_t[4 + h:5 + h, :]
            i_col = gates[:, h:h + 1]
            i_row = gates_t[h:h + 1, :]
            m_prev = m_all[:, h:h + 1]
            log_d = jnp.where(causal, b_col - b_row + i_row, -jnp.inf)
            log_prev = b_col + m_prev
            m_t = jnp.maximum(jnp.max(log_d, axis=-1, keepdims=True), log_prev)
            d = jnp.exp(log_d - m_t)
            w_prev = jnp.exp(log_prev - m_t)
            scores = _dot_nt(qh.astype(BF16), kh.astype(BF16)) * d
            c_h = c_scr[s, h]
            n_h = n_scr[s, h:h + 1, :]
            num = _dot(scores.astype(BF16), vh) + w_prev * _dot(qh.astype(BF16), c_h.astype(BF16))
            den = jnp.sum(scores, axis=-1, keepdims=True) + w_prev * jnp.sum(qh * n_h, axis=-1, keepdims=True)
            hh = num / jnp.maximum(jnp.abs(den), jnp.exp(-m_t))
            m_new = m_t[CS - 1:CS, :]
            b_last = b_col[CS - 1:CS, :]
            w_c = jnp.exp(b_last + m_prev - m_new)
            w_s = jnp.exp(b_last - b_col + i_col - m_new)
            kw = kh * w_s
            c_scr[s, h] = w_c * c_h + _dot_tn(kw.astype(BF16), vh)
            n_scr[s, h:h + 1, :] = w_c * n_h + jnp.sum(kw, axis=0, keepdims=True)
            m_scr[s, :, h:h + 1] = m_new
            y_ref[s, :, vs] = _mh_norm_gate(hh, o_pre[:, vs], hnw_ref[:, vs]).astype(BF16)

    @pl.when(c == NCHUNK - 1)
    def _():
        c_out[...] = c_scr[...]
        for s in range(SEQ_PER_STEP):
            n_out[s] = n_scr[s, 0:ML_H, :]
            m_out[s] = m_scr[s]
            conv_out[s] = tails[s][8 - (CONV_W - 1):8, :]


def _odd_prompt(z, zg, bg, cw, cb, hnw, tri):
    c2 = lambda b, c: (0, 0)
    sp = SEQ_PER_STEP
    return pl.pallas_call(
        _odd_prompt_kernel,
        grid=(BATCH // sp, NCHUNK),
        in_specs=_seq_row_specs(ODD_MAIN) + _seq_row_specs(LANES) + [
            pl.BlockSpec((1, LANES), c2),
            pl.BlockSpec((CONV_W, D), c2),
            pl.BlockSpec((1, D), c2),
            pl.BlockSpec((1, D), c2),
            pl.BlockSpec((CS, CS), c2),
        ],
        out_specs=[
            pl.BlockSpec((sp, CS, D), lambda b, c: (b, c, 0)),
            pl.BlockSpec((sp, ML_H, ML_DK, ML_DV), lambda b, c: (b, 0, 0, 0)),
            pl.BlockSpec((sp, ML_H, ML_DK), lambda b, c: (b, 0, 0)),
            pl.BlockSpec((sp, 1, LANES), lambda b, c: (b, 0, 0)),
            pl.BlockSpec((sp, CONV_W - 1, D), lambda b, c: (b, 0, 0)),
        ],
        out_shape=[
            jax.ShapeDtypeStruct((BATCH, SEQ, D), BF16),
            jax.ShapeDtypeStruct((BATCH, ML_H, ML_DK, ML_DV), F32),
            jax.ShapeDtypeStruct((BATCH, ML_H, ML_DK), F32),
            jax.ShapeDtypeStruct((BATCH, 1, LANES), F32),
            jax.ShapeDtypeStruct((BATCH, CONV_W - 1, D), F32),
        ],
        scratch_shapes=[
            pltpu.VMEM((sp, ML_H, ML_DK, ML_DV), F32),
            pltpu.VMEM((sp, 8, ML_DK), F32),
            pltpu.VMEM((sp, 1, LANES), F32),
            pltpu.VMEM((sp, CS + 8, D), F32),
        ],
        compiler_params=_params(("parallel", "arbitrary")),
        name="odd_prompt",
    )(*([z] * sp), *([zg] * sp), bg, cw, cb, hnw, tri)


def _odd_sample_kernel(zr_ref, zg_ref, ut_ref, conv_ref, convt_ref, bg_ref, cw_ref, cwt_ref, cb_ref, cbt_ref,
                       hnw_ref, c_ref, n_ref, m_ref,
                       y_ref, c_out, n_out, m_out, conv_out, h_scr):
    zr = zr_ref[...]
    u = zr[:, 0:D]
    v = zr[:, D:2 * D]
    o_pre = zr[:, 2 * D:3 * D]
    uc = cb_ref[...] + u * cw_ref[CONV_W - 1:CONV_W, :]
    uc_t = cbt_ref[...] + ut_ref[0] * cwt_ref[:, CONV_W - 1:CONV_W]
    for j in range(CONV_W - 1):
        uc = uc + conv_ref[:, j * D:(j + 1) * D] * cw_ref[j:j + 1, :]
        uc_t = uc_t + convt_ref[0, j] * cwt_ref[:, j:j + 1]
        conv_out[:, j * D:(j + 1) * D] = conv_ref[:, (j + 1) * D:(j + 2) * D] if j + 1 < CONV_W - 1 else u
    act = _silu(uc)
    k_row = act[:, 512:1024]
    act_t = _silu(uc_t)
    q_t = act_t[0:512] * ML_DK ** -0.5
    k_t = act_t[512:1024]
    q_row = act[:, 0:512] * ML_DK ** -0.5

    gates = zg_ref[...] + bg_ref[...]
    lf = _log_sigmoid(gates)
    m_in = m_ref[...]
    m_out[...] = m_in

    for j in range(SG):
        for h in range(ML_H):
            ks = slice(h * ML_DK, (h + 1) * ML_DK)
            vs = slice(h * ML_DV, (h + 1) * ML_DV)
            ig = gates[j:j + 1, h:h + 1]
            log_prev = lf[j:j + 1, 4 + h:5 + h] + m_in[j:j + 1, h:h + 1]
            m_t = jnp.maximum(ig, log_prev)
            d = jnp.exp(ig - m_t)
            w_prev = jnp.exp(log_prev - m_t)
            c_new = w_prev * c_ref[j, h] + (d * k_t[ks, j:j + 1]) * v[j:j + 1, vs]
            n_new = w_prev * n_ref[j, h:h + 1, :] + d * k_row[j:j + 1, ks]
            c_out[j, h] = c_new
            n_out[j, h:h + 1, :] = n_new
            m_out[j:j + 1, h:h + 1] = m_t
            num = jnp.sum(q_t[ks, j:j + 1] * c_new, axis=0, keepdims=True)
            den = jnp.sum(q_row[j:j + 1, ks] * n_new, axis=-1, keepdims=True)
            h_scr[j:j + 1, vs] = num / jnp.maximum(jnp.abs(den), jnp.exp(-m_t))

    hh = h_scr[...]
    for h in range(ML_H):
        vs = slice(h * ML_DV, (h + 1) * ML_DV)
        y_ref[:, vs] = _mh_norm_gate(hh[:, vs], o_pre[:, vs], hnw_ref[:, vs]).astype(BF16)


def _odd_sample(z, zg, ut3, conv, convt, bg, cw, cwt, cb, cbt, hnw, c_in, n_in, m_in):
    c2 = lambda g: (0, 0)
    return pl.pallas_call(
        _odd_sample_kernel,
        grid=(NS // SG,),
        in_specs=[
            pl.BlockSpec((SG, ODD_MAIN), lambda g: (NP // SG + g, 0)),
            pl.BlockSpec((SG, LANES), lambda g: (NP // SG + g, 0)),
            pl.BlockSpec((1, D, SG), lambda g: (g, 0, 0)),
            pl.BlockSpec((SG, (CONV_W - 1) * D), lambda g: (g, 0)),
            pl.BlockSpec((1, CONV_W - 1, D, SG), lambda g: (g, 0, 0, 0)),
            pl.BlockSpec((1, LANES), c2),
            pl.BlockSpec((CONV_W, D), c2),
            pl.BlockSpec((D, CONV_W), c2),
            pl.BlockSpec((1, D), c2),
            pl.BlockSpec((D, 1), c2),
            pl.BlockSpec((1, D), c2),
            pl.BlockSpec((SG, ML_H, ML_DK, ML_DV), lambda g: (g, 0, 0, 0)),
            pl.BlockSpec((SG, ML_H, ML_DK), lambda g: (g, 0, 0)),
            pl.BlockSpec((SG, LANES), lambda g: (g, 0)),
        ],
        out_specs=[
            pl.BlockSpec((SG, D), lambda g: (g, 0)),
            pl.BlockSpec((SG, ML_H, ML_DK, ML_DV), lambda g: (g, 0, 0, 0)),
            pl.BlockSpec((SG, ML_H, ML_DK), lambda g: (g, 0, 0)),
            pl.BlockSpec((SG, LANES), lambda g: (g, 0)),
            pl.BlockSpec((SG, (CONV_W - 1) * D), lambda g: (g, 0)),
        ],
        out_shape=[
            jax.ShapeDtypeStruct((NS, D), BF16),
            jax.ShapeDtypeStruct((NS, ML_H, ML_DK, ML_DV), F32),
            jax.ShapeDtypeStruct((NS, ML_H, ML_DK), F32),
            jax.ShapeDtypeStruct((NS, LANES), F32),
            jax.ShapeDtypeStruct((NS, (CONV_W - 1) * D), F32),
        ],
        scratch_shapes=[pltpu.VMEM((SG, D), F32)],
        compiler_params=_params(("parallel",)),
        name="odd_sample",
    )(z, zg, ut3, conv, convt, bg, cw, cwt, cb, cbt, hnw, c_in, n_in, m_in)


def _out_ln_router_kernel(x_ref, y_ref, w_ref, g_ref, b_ref, wr_ref, tri_ref,
                          o_ref, op_ref, meta_ref, cnt_ref, tab_ref, carry, filled):
    i = pl.program_id(0)

    @pl.when(i == 0)
    def _():
        carry[...] = jnp.zeros_like(carry)
        tab_ref[...] = jnp.zeros_like(tab_ref)
        for e in range(N_EXPERTS):
            filled[e] = 0

    r = ALPHA * x_ref[...] + _dot(y_ref[...], w_ref[...])
    x3 = _layernorm(r, g_ref[...], b_ref[...])
    o_ref[...] = x3
    op_ref[...] = pltpu.pack_elementwise([x3[:, :D // 2], x3[:, D // 2:]], packed_dtype=BF16)

    lane = lax.broadcasted_iota(jnp.int32, (TM, LANES), 1).astype(F32)
    logits = jnp.where(lane < N_EXPERTS, _dot(x3.astype(BF16), wr_ref[...]), -jnp.inf)
    m1 = jnp.max(logits, axis=-1, keepdims=True)
    i1 = jnp.min(jnp.where(logits == m1, lane, float(LANES)), axis=-1, keepdims=True)
    rest = jnp.where(lane == i1, -jnp.inf, logits)
    m2 = jnp.max(rest, axis=-1, keepdims=True)
    i2 = jnp.min(jnp.where(rest == m2, lane, float(LANES)), axis=-1, keepdims=True)
    e2 = jnp.exp(m2 - m1)
    tot = 1.0 + e2
    w1 = 1.0 / tot
    w2 = e2 / tot

    sel1 = lane == i1
    sel2 = lane == i2
    onehot = jnp.where(sel1 | sel2, 1.0, 0.0)
    in_tile = _dot(tri_ref[...], onehot.astype(BF16))
    carry[...] = carry[...] + jnp.sum(onehot, axis=0, keepdims=True)
    cnt_ref[...] = carry[...]

    meta = jnp.where(lane == 0.0, i1, 0.0)
    meta = jnp.where(lane == 1.0, i2, meta)
    meta = jnp.where(lane == 2.0, w1, meta)
    meta = jnp.where(lane == 3.0, w2, meta)
    meta_ref[...] = meta

    token = (i * TM + lax.broadcasted_iota(jnp.int32, (TM, 1), 0)).astype(F32)
    digit_hi = jnp.floor(token * (1.0 / 256.0))
    rhs = jnp.where(lane == 0.0, digit_hi, jnp.where(lane == 1.0, token - 256.0 * digit_hi, 0.0))
    rhs = jnp.where(lane == i2 + float(SUBLANES), 1.0, rhs).astype(BF16)
    place = lax.broadcasted_iota(jnp.int32, (TM, TM), 0).astype(F32)
    rank_rows = jnp.where(onehot > 0.0, in_tile, -1.0).T
    entries = []
    for e in range(N_EXPERTS):
        hit = jnp.where(rank_rows[e:e + 1, :] == place, 1.0, 0.0).astype(BF16)
        got = _dot(hit, rhs)
        entries.append(256.0 * got[:, 0:1] + got[:, 1:2]
                       + float(NT) * got[:, SUBLANES + e:SUBLANES + e + 1])
    tile_counts = jnp.sum(onehot, axis=0, keepdims=True)
    counts = [jnp.sum(tile_counts[:, e:e + 1]).astype(jnp.int32) for e in range(N_EXPERTS)]
    for e in range(N_EXPERTS):
        start = filled[e]
        filled[e] = start + counts[e]
        tab_ref[pl.ds(start, LANES), e:e + 1] = entries[e][0:LANES]
        for part in range(1, TM // LANES):
            @pl.when(counts[e] > part * LANES)
            def _():
                tab_ref[pl.ds(start + part * LANES, LANES), e:e + 1] = entries[e][part * LANES:(part + 1) * LANES]


def _out_ln_router(x, y, w, g, b, wr, tri):
    c2 = lambda i: (0, 0)
    return pl.pallas_call(
        _out_ln_router_kernel,
        grid=(NT // TM,),
        in_specs=[
            pl.BlockSpec((TM, D), lambda i: (i, 0)),
            pl.BlockSpec((TM, D), lambda i: (i, 0)),
            pl.BlockSpec((D, D), c2),
            pl.BlockSpec((1, D), c2),
            pl.BlockSpec((1, D), c2),
            pl.BlockSpec((D, LANES), c2),
            pl.BlockSpec((TM, TM), c2),
        ],
        out_specs=[
            pl.BlockSpec((TM, D), lambda i: (i, 0)),
            pl.BlockSpec((TM, D // 2), lambda i: (i, 0)),
            pl.BlockSpec((TM, LANES), lambda i: (i, 0)),
            pl.BlockSpec((1, LANES), c2),
            pl.BlockSpec((MOE_CAP + TM, LANES), c2),
        ],
        out_shape=[
            jax.ShapeDtypeStruct((NT, D), F32),
            jax.ShapeDtypeStruct((NT, D // 2), U32),
            jax.ShapeDtypeStruct((NT, LANES), F32),
            jax.ShapeDtypeStruct((1, LANES), F32),
            jax.ShapeDtypeStruct((MOE_CAP + TM, LANES), F32),
        ],
        scratch_shapes=[pltpu.VMEM((1, LANES), F32), pltpu.SMEM((N_EXPERTS,), jnp.int32)],
        compiler_params=_params(("arbitrary",)),
        name="out_ln_router",
    )(x, y, w, g, b, wr, tri)


def _moe_ffn_kernel(te_ref, nu_ref, tb_ref, gnext_ref, gcur_ref, sprev_ref, scur_ref, xp_ref,
                    w1_ref, w3_ref, w2_ref, out_hbm, stage, yacc, xb_scr, sem_s):
    del tb_ref
    i = pl.program_id(0)
    j = pl.program_id(1)
    used = i < nu_ref[0]
    slot = i % 2
    other = 1 - slot
    rps = MOE_ROWS_PER_STEP

    def gather_rows(tab_ref, buf, part):
        for r in range(rps):
            stage[buf, part, pl.ds(r, 1), :] = xp_ref[pl.ds(tab_ref[part * rps + r], 1), :]

    def scatter(buf, r, dst):
        return pltpu.make_async_copy(yacc.at[buf, pl.ds(r, 1)], out_hbm.at[pl.ds(dst, 1)], sem_s)

    def wait_scatters(n):
        for _ in range(n):
            scatter(0, 0, 0).wait()

    def issue_neighbours():
        gather_rows(gnext_ref, other, j)
        for r in range(rps):
            rr = j * rps + r
            scatter(other, rr, sprev_ref[rr]).start()

    @pl.when(j == 0)
    def _():
        @pl.when(i == 0)
        def _():
            yacc[1] = jnp.zeros((TMM, D), F32)
            for part in range(MOE_NFF):
                gather_rows(gcur_ref, 0, part)

        @pl.when(i > 0)
        def _():
            wait_scatters(TMM)

        words = stage[slot].reshape(TMM, D // 2)
        for half in range(2):
            xb_scr[:, half * (D // 2):(half + 1) * (D // 2)] = pltpu.unpack_elementwise(
                words, index=half, packed_dtype=BF16, unpacked_dtype=F32).astype(BF16)
        yacc[slot] = jnp.zeros((TMM, D), F32)

    @pl.when(used)
    def _():
        issue_neighbours()
        xb = xb_scr[...]
        hmid = _silu(_dot(xb, w1_ref[...])) * _dot(xb, w3_ref[...])
        yacc[slot] += _dot(hmid.astype(BF16), w2_ref[...])

    @pl.when(jnp.logical_not(used))
    def _():
        issue_neighbours()

    @pl.when((i == N_MOE_TILES - 1) & (j == MOE_NFF - 1))
    def _():
        for r in range(TMM):
            scatter(slot, r, scur_ref[r]).start()
        wait_scatters(2 * TMM)


def _moe_ffn(tile_expert, n_used, tab_block, gsrc, sdst, xp, w1, w3, w2):
    nff = MOE_NFF

    def wcol(i, j, te, nu, tb):
        return (te[i], 0, jnp.where(i < nu[0], j, nff - 1))

    def wrow(i, j, te, nu, tb):
        return (te[i], jnp.where(i < nu[0], j, nff - 1), 0)

    smem = functools.partial(pl.BlockSpec, (MOE_TAB,), memory_space=pltpu.SMEM)
    grid_spec = pltpu.PrefetchScalarGridSpec(
        num_scalar_prefetch=3,
        grid=(N_MOE_TILES, nff),
        in_specs=[
            smem(lambda i, j, te, nu, tb: (tb[i + 2],)),
            smem(lambda i, j, te, nu, tb: (tb[i + 1],)),
            smem(lambda i, j, te, nu, tb: (tb[i],)),
            smem(lambda i, j, te, nu, tb: (tb[i + 1],)),
            pl.BlockSpec((NT, D // 2), lambda i, j, te, nu, tb: (0, 0), pipeline_mode=pl.Buffered(1)),
            pl.BlockSpec((None, D, TFF), wcol),
            pl.BlockSpec((None, D, TFF), wcol),
            pl.BlockSpec((None, TFF, D), wrow),
        ],
        out_specs=pl.BlockSpec(memory_space=pl.ANY),
        scratch_shapes=[
            pltpu.VMEM((2, MOE_NFF, MOE_ROWS_PER_STEP, D // 2), U32),
            pltpu.VMEM((2, TMM, D), F32),
            pltpu.VMEM((TMM, D), BF16),
            pltpu.SemaphoreType.DMA(()),
        ],
    )
    return pl.pallas_call(
        _moe_ffn_kernel,
        grid_spec=grid_spec,
        out_shape=jax.ShapeDtypeStruct((MOE_OUT_ROWS, D), F32),
        compiler_params=_params(("arbitrary", "arbitrary")),
        name="moe_ffn",
    )(tile_expert, n_used, tab_block, gsrc, gsrc, sdst, sdst, xp, w1, w3, w2)


def _combine_kernel(x_ref, meta_ref, y0_ref, y1_ref, g_ref, b_ref, o_ref):
    meta = meta_ref[...]
    moe = meta[:, 2:3] * y0_ref[...] + meta[:, 3:4] * y1_ref[...]
    o_ref[...] = _layernorm(ALPHA * x_ref[...] + moe, g_ref[...], b_ref[...])


def _combine(x, meta, ys, g, b):
    c2 = lambda i: (0, 0)
    return pl.pallas_call(
        _combine_kernel,
        grid=(NT // TM,),
        in_specs=[
            pl.BlockSpec((TM, D), lambda i: (i, 0)),
            pl.BlockSpec((TM, LANES), lambda i: (i, 0)),
            pl.BlockSpec((TM, D), lambda i: (i, 0)),
            pl.BlockSpec((TM, D), lambda i: (i + NT // TM, 0)),
            pl.BlockSpec((1, D), c2),
            pl.BlockSpec((1, D), c2),
        ],
        out_specs=pl.BlockSpec((TM, D), lambda i: (i, 0)),
        out_shape=jax.ShapeDtypeStruct((NT, D), F32),
        compiler_params=_params(("parallel",)),
        name="moe_combine",
    )(x, meta, ys, ys, g, b)


def _pad_cols(w, n):
    return jnp.pad(w, ((0, 0), (0, n - w.shape[1])))


def kernel(x_prompt, x_sample, state_hgrn, state_gla, state_mlstm_C, state_mlstm_n, state_mlstm_m,
           state_mlstm_conv, w_in_even, hg_lower_bounds, w_gk, b_gk, gn_hg, gn_gla, w_out_even,
           w1_dense, w3_dense, w2_dense, w_in_odd, b_gate_odd, conv_w, conv_b, hn_w, w_out_odd,
           w_router, w1_moe, w3_moe, w2_moe, ln1_g, ln1_b, ln2_g, ln2_b):
    assert x_prompt.shape == (BATCH, SEQ, D) and x_sample.shape == (NS, 1, D)
    assert w_in_even.shape[0] == 1 and w_in_odd.shape[0] == 1 and hg_lower_bounds.shape[0] == 2
    masks = jnp.asarray(_gla_masks(), F32)
    tri_cs = jnp.asarray(_tri(CS, False), BF16)
    tri_tm = jnp.asarray(_tri(TM, True), BF16)
    row = lambda a: a.reshape(1, -1)

    x0 = jnp.concatenate([x_prompt.reshape(NP, D), x_sample.reshape(NS, D)], axis=0)

    w_even = w_in_even[0].astype(BF16)
    z, zgr = _proj(x0, w_even[:, :EVEN_MAIN], _pad_cols(w_even[:, EVEN_MAIN:], LANES))
    wgk = jnp.pad(w_gk[0].astype(BF16), ((0, LANES - GLA_RANK), (0, 0)))
    lbp = hg_lower_bounds
    y_p, hg_p, gla_p = _even_prompt(z, zgr, lbp, wgk, row(b_gk[0]), row(gn_hg[0]), row(gn_gla[0]), tri_cs, masks)

    zs = z[NP:].reshape(NS // SG, SG, EVEN_MAIN).transpose(0, 2, 1)
    grs = zgr[NP:].reshape(NS // SG, SG, LANES).transpose(0, 2, 1)
    y_s, hg_s, gla_s = _even_sample(z, zs, grs, lbp.T, wgk.T, b_gk[0].reshape(-1, 1),
                                    row(gn_hg[0]), row(gn_gla[0]), state_hgrn[0], state_gla[0])
    y = jnp.concatenate([y_p.reshape(NP, D), y_s], axis=0)
    x1 = _out_ln(x0, y, w_out_even[0].astype(BF16), row(ln1_g[0]), row(ln1_b[0]))
    x2 = _ffn(x1, w1_dense[0].astype(BF16), w3_dense[0].astype(BF16), w2_dense[0].astype(BF16),
              row(ln2_g[0]), row(ln2_b[0]))

    w_odd = w_in_odd[0].astype(BF16)
    zo, zog = _proj(x2, w_odd[:, :ODD_MAIN], _pad_cols(w_odd[:, ODD_MAIN:], LANES))
    bg = jnp.pad(b_gate_odd[0], (0, LANES - 2 * ML_H)).reshape(1, LANES)
    yo_p, c_p, n_p, m_p, conv_p = _odd_prompt(zo, zog, bg, conv_w[0], row(conv_b[0]), row(hn_w[0]), tri_cs)

    ut = zo[NP:, :D].reshape(NS // SG, SG, D).transpose(0, 2, 1)
    conv_in = state_mlstm_conv[0]
    conv_t = conv_in.reshape(NS // SG, SG, CONV_W - 1, D).transpose(0, 2, 3, 1)
    m_in = jnp.pad(state_mlstm_m[0], ((0, 0), (0, LANES - ML_H)))
    yo_s, c_s, n_s, m_s, conv_s = _odd_sample(
        zo, zog, ut, conv_in.reshape(NS, (CONV_W - 1) * D), conv_t, bg, conv_w[0], conv_w[0].T, row(conv_b[0]), conv_b[0].reshape(-1, 1),
        row(hn_w[0]), state_mlstm_C[0], state_mlstm_n[0], m_in)
    yo = jnp.concatenate([yo_p.reshape(NP, D), yo_s], axis=0)

    wr = _pad_cols(w_router[0].astype(BF16), LANES)
    x3, x3p, meta, cnt, tab = _out_ln_router(
        x2, yo, w_out_odd[0].astype(BF16), row(ln1_g[1]), row(ln1_b[1]), wr, tri_tm)

    counts = cnt[0, :N_EXPERTS].astype(jnp.int32)
    padded = ((counts + TMM - 1) // TMM) * TMM
    ends = jnp.cumsum(padded)
    offsets = ends - padded
    tile = jnp.arange(N_MOE_TILES, dtype=jnp.int32)
    tile_expert = jnp.minimum(jnp.sum((tile * TMM)[:, None] >= ends[None, :], axis=1), N_EXPERTS - 1).astype(jnp.int32)
    n_tiles_used = ends[-1] // TMM
    in_use = tile < n_tiles_used
    local_tile = (tile * TMM - offsets[tile_expert]) // TMM
    blocks_per_expert = MOE_CAP // MOE_TAB
    spill_block = N_EXPERTS * blocks_per_expert
    n_spare = N_MOE_TILES - (2 * NT) // TMM
    entry = tab[:MOE_CAP, :N_EXPERTS].T.astype(jnp.int32)
    local = jnp.arange(MOE_CAP, dtype=jnp.int32)[None, :]
    valid = local < counts[:, None]
    padding = padded - counts
    pad_before = (jnp.cumsum(padding) - padding)[:, None]
    row_in_block = jnp.arange(MOE_TAB, dtype=jnp.int32)
    spare_rows = (2 * NT + TMM + jnp.sum(padding)
                  + jnp.arange(n_spare, dtype=jnp.int32)[:, None] * TMM + row_in_block[None, :])
    sdst = jnp.concatenate([
        jnp.where(valid, entry, 2 * NT + TMM + pad_before + local - counts[:, None]).reshape(-1),
        2 * NT + row_in_block, spare_rows.reshape(-1)])
    gsrc = jnp.concatenate([
        jnp.where(valid, entry - jnp.where(entry >= NT, NT, 0), 0).reshape(-1),
        jnp.zeros(((1 + n_spare) * MOE_TAB,), jnp.int32)])
    own_block = jnp.where(in_use, tile_expert * blocks_per_expert + local_tile, spill_block + 1 + tile - n_tiles_used)
    tab_block = jnp.concatenate([jnp.full((1,), spill_block, jnp.int32), own_block.astype(jnp.int32),
                                 jnp.full((1,), spill_block, jnp.int32)])

    ys = _moe_ffn(tile_expert, n_tiles_used.reshape(1), tab_block, gsrc, sdst, x3p,
                  w1_moe[0].astype(BF16), w3_moe[0].astype(BF16), w2_moe[0].astype(BF16))
    out = _combine(x3, meta, ys, row(ln2_g[1]), row(ln2_b[1]))

    y_prompt = out[:NP].reshape(BATCH, SEQ, D)
    y_sample = out[NP:].reshape(NS, 1, D)
    return (y_prompt, y_sample,
            hg_p.reshape(1, BATCH, HG_H, HG_DK, HG_DV), gla_p.reshape(1, BATCH, GLA_H, GLA_DK, GLA_DV),
            c_p[None], n_p[None], m_p[:, 0, :ML_H][None], conv_p[None],
            hg_s[None], gla_s[None], c_s[None], n_s[None], m_s[:, :ML_H][None], conv_s.reshape(1, NS, CONV_W - 1, D))
```

```python
import functools
import math

import jax
import jax.numpy as jnp
import numpy as np
from jax import lax
from jax.experimental import pallas as pl
from jax.experimental.pallas import tpu as pltpu

F32 = jnp.float32
BF16 = jnp.bfloat16
U32 = jnp.uint32

D = 1024
BATCH = 8
SEQ = 2048
DEC_BATCH = 128
NP = BATCH * SEQ
NS = DEC_BATCH
NT = NP + NS
HG_H, HG_DK, HG_DV = 4, 128, 128
GLA_H, GLA_DK, GLA_DV = 4, 64, 128
GLA_RANK = 16
GLA_GATE_NORM = 16.0
ML_H, ML_DK, ML_DV = 4, 128, 256
CONV_W = 4
D_FF_DENSE = 2816
D_FF_EXPERT = 3584
N_EXPERTS = 8
EPS = 1e-5
DEPTH = 2
ALPHA = (2.0 * DEPTH) ** 0.25
EVEN_MAIN = 3584
ODD_MAIN = 3072

LANES = 128
SUBLANES = 8
VMEM_LIMIT = 56 * 1024 * 1024

TM = 384
CS = 128
NCHUNK = SEQ // CS
SG = 16
TMM = 512
TFF = 896
MOE_TAB = 512
MOE_CAP = -(-NT // MOE_TAB) * MOE_TAB
MOE_NFF = D_FF_EXPERT // TFF
MOE_ROWS_PER_STEP = TMM // MOE_NFF
N_MOE_TILES = -(-(2 * NT + N_EXPERTS * (TMM - 1)) // TMM)
MOE_SLOTS = N_MOE_TILES * TMM
MOE_OUT_ROWS = MOE_SLOTS + TMM
N_LEVELS = int(math.log2(CS))

assert NT % TM == 0 and NP % CS == 0 and NS % SG == 0 and D_FF_EXPERT % TFF == 0 and TMM % MOE_NFF == 0
assert TMM == MOE_TAB and NS == CS


def _params(sem, limit=VMEM_LIMIT):
    return pltpu.CompilerParams(dimension_semantics=sem, vmem_limit_bytes=limit)


def _dot(a, b):
    return jnp.dot(a, b, preferred_element_type=F32)


def _dot_nt(a, b):
    return lax.dot_general(a, b, (((1,), (1,)), ((), ())), preferred_element_type=F32)


def _dot_tn(a, b):
    return lax.dot_general(a, b, (((0,), (0,)), ((), ())), preferred_element_type=F32)


def _split3(x):
    hi = x.astype(BF16)
    r1 = x - hi.astype(F32)
    mid = r1.astype(BF16)
    lo = (r1 - mid.astype(F32)).astype(BF16)
    return hi, mid, lo


def _dot_sel(sel, x):
    hi, mid, lo = _split3(x)
    return _dot(sel, hi) + _dot(sel, mid) + _dot(sel, lo)


def _sigmoid(x):
    return jax.nn.sigmoid(x)


def _silu(x):
    return x * jax.nn.sigmoid(x)


def _log_sigmoid(x):
    return jnp.minimum(x, 0.0) - jnp.log(1.0 + jnp.exp(-jnp.abs(x)))


def _layernorm(r, g, b):
    mu = jnp.mean(r, axis=-1, keepdims=True)
    c = r - mu
    var = jnp.mean(c * c, axis=-1, keepdims=True)
    return c * lax.rsqrt(var + EPS) * g + b


def _gla_masks():
    masks = np.zeros((N_LEVELS + 1, CS, CS), np.float32)
    for t in range(CS):
        for l in range(N_LEVELS):
            half = 1 << l
            start = (t // (2 * half)) * (2 * half)
            mid = start + half
            if t >= mid:
                masks[l, t, start:mid] = 1.0
        masks[N_LEVELS, t, t] = 1.0
    return masks


def _tri(n, strict):
    return np.tril(np.ones((n, n), np.float32), -1 if strict else 0)


def _proj_kernel(x_ref, wa_ref, wb_ref, oa_ref, ob_ref):
    xb = x_ref[...].astype(BF16)
    oa_ref[...] = _dot(xb, wa_ref[...])
    ob_ref[...] = _dot(xb, wb_ref[...])


def _proj(x, wa, wb):
    na, nb = wa.shape[1], wb.shape[1]
    return pl.pallas_call(
        _proj_kernel,
        grid=(NT // TM,),
        in_specs=[
            pl.BlockSpec((TM, D), lambda i: (i, 0)),
            pl.BlockSpec((D, na), lambda i: (0, 0)),
            pl.BlockSpec((D, nb), lambda i: (0, 0)),
        ],
        out_specs=[
            pl.BlockSpec((TM, na), lambda i: (i, 0)),
            pl.BlockSpec((TM, nb), lambda i: (i, 0)),
        ],
        out_shape=[jax.ShapeDtypeStruct((NT, na), F32), jax.ShapeDtypeStruct((NT, nb), F32)],
        compiler_params=_params(("parallel",)),
        name="proj",
    )(x, wa, wb)


def _rms_gate(o, gate, w):
    o = o * lax.rsqrt(jnp.mean(o * o, axis=-1, keepdims=True) + EPS) * w
    return o * _silu(gate)


def _level_decays(g, bc):
    width = g.shape[1]
    ng = CS // SUBLANES
    shape3 = (ng, SUBLANES, width)
    bc3 = bc.reshape(shape3)
    sub = lax.broadcasted_iota(jnp.int32, shape3, 1)

    def group_row(s):
        return jnp.broadcast_to(bc3[:, s:s + 1, :], shape3)

    last = group_row(SUBLANES - 1)
    refs = [None,
            jnp.where(sub < 4, group_row(1), group_row(5)),
            group_row(3)]
    for l in range(3, N_LEVELS):
        per_block = 1 << (l - 2)
        grouped = last.reshape(ng // per_block, per_block, SUBLANES, width)
        ref = jnp.broadcast_to(grouped[:, per_block // 2 - 1:per_block // 2], grouped.shape)
        refs.append(ref.reshape(shape3))
    decays = [jnp.exp(jnp.where((sub & 1) == 1, g.reshape(shape3), 0.0))]
    decays += [jnp.exp(-jnp.abs(bc3 - ref)) for ref in refs[1:]]
    to_end = jnp.exp(jnp.broadcast_to(last[ng - 1:ng], shape3) - bc3)
    return [d.reshape(CS, width) for d in decays], to_end.reshape(CS, width)


def _gla_chunk(q, k, v, g, st_ref, tri, masks_ref, heads, dk, dv):
    bc = _dot_sel(tri, g)
    z_levels, z_end = _level_decays(g, bc)
    z_cum = jnp.exp(bc)
    st = st_ref[...]
    outs = []
    for h in range(heads):
        ks = slice(h * dk, (h + 1) * dk)
        vs = slice(h * dv, (h + 1) * dv)
        qh, kh = q[:, ks], k[:, ks]
        vh = v[:, vs].astype(BF16)
        scores = _dot_nt(qh.astype(BF16), kh.astype(BF16)) * masks_ref[N_LEVELS]
        for l in range(N_LEVELS):
            zl = z_levels[l][:, ks]
            scores = scores + _dot_nt((qh * zl).astype(BF16), (kh * zl).astype(BF16)) * masks_ref[l]
        o = _dot(scores.astype(BF16), vh)
        o = o + _dot_nt((qh * z_cum[:, ks]).astype(BF16), st[:, ks].astype(BF16))
        outs.append(o)
        upd = _dot_tn(vh, (kh * z_end[:, ks]).astype(BF16))
        st_ref[:, ks] = st[:, ks] * z_cum[CS - 1:CS, ks] + upd
    return outs


def _even_prompt_kernel(z_ref, zgr_ref, ys_ref, lbp_ref, wgk_ref, bgk_ref, gnh_ref, gng_ref, tri_ref, masks_ref,
                        y_ref, shg_ref, sgla_ref, st_hg, st_gla):
    b = pl.program_id(0)
    c = pl.program_id(1)

    @pl.when((b < BATCH) & (c == 0))
    def _():
        st_hg[...] = jnp.zeros_like(st_hg)
        st_gla[...] = jnp.zeros_like(st_gla)

    @pl.when(b < BATCH)
    def _():
        tri = tri_ref[...]
        p = lbp_ref[...]
        pe = jnp.exp(p - jnp.max(p, axis=0, keepdims=True))
        lb = pe[0:1] / jnp.sum(pe, axis=0, keepdims=True)

        z = z_ref[...]
        hq, hf, hi, hg = z[:, 0:512], z[:, 512:1024], z[:, 1024:1536], z[:, 1536:2048]
        gq, gk, gv, gg = z[:, 2048:2304], z[:, 2304:2560], z[:, 2560:3072], z[:, 3072:3584]
        f = lb + (1.0 - lb) * _sigmoid(hf)
        k_hg = (1.0 - lb) * _sigmoid(-hf)
        o_hg = _gla_chunk(_silu(hq), k_hg, hi, jnp.log(f), st_hg, tri, masks_ref, HG_H, HG_DK, HG_DV)

        la = _log_sigmoid(_dot(zgr_ref[...].astype(BF16), wgk_ref[...]) + bgk_ref[...]) / GLA_GATE_NORM
        o_gla = _gla_chunk(gq * GLA_DK ** -0.5, gk, gv, la, st_gla, tri, masks_ref, GLA_H, GLA_DK, GLA_DV)

        for h in range(HG_H):
            cs = slice(h * 128, (h + 1) * 128)
            y_ref[:, cs] = _rms_gate(o_hg[h], hg[:, cs], gnh_ref[...]).astype(BF16)
        for h in range(GLA_H):
            cs = slice(h * 128, (h + 1) * 128)
            y_ref[:, 512 + h * 128:512 + (h + 1) * 128] = _rms_gate(o_gla[h], gg[:, cs], gng_ref[...]).astype(BF16)

    @pl.when((b < BATCH) & (c == NCHUNK - 1))
    def _():
        shg_ref[0] = st_hg[...].T
        sgla_ref[0] = st_gla[...].T

    @pl.when((b == BATCH) & (c == 0))
    def _():
        y_ref[...] = ys_ref[...]


def _chunk_rows(b, c):
    return (jnp.minimum(b * NCHUNK + c, NP // CS), 0)


def _per_sequence(b, c):
    return (jnp.minimum(b, BATCH - 1), 0, 0)


def _even_prompt(z, zgr, y_sample, lbp, wgk, bgk, gnh, gng, tri, masks):
    const2 = lambda b, c: (0, 0)
    return pl.pallas_call(
        _even_prompt_kernel,
        grid=(BATCH + 1, NCHUNK),
        in_specs=[
            pl.BlockSpec((CS, EVEN_MAIN), _chunk_rows),
            pl.BlockSpec((CS, LANES), _chunk_rows),
            pl.BlockSpec((NS, D), const2),
            pl.BlockSpec(lbp.shape, const2),
            pl.BlockSpec(wgk.shape, const2),
            pl.BlockSpec(bgk.shape, const2),
            pl.BlockSpec(gnh.shape, const2),
            pl.BlockSpec(gng.shape, const2),
            pl.BlockSpec(tri.shape, const2),
            pl.BlockSpec(masks.shape, lambda b, c: (0, 0, 0)),
        ],
        out_specs=[
            pl.BlockSpec((CS, D), _chunk_rows),
            pl.BlockSpec((1, HG_H * HG_DK, HG_DV), _per_sequence),
            pl.BlockSpec((1, GLA_H * GLA_DK, GLA_DV), _per_sequence),
        ],
        out_shape=[
            jax.ShapeDtypeStruct((NT, D), BF16),
            jax.ShapeDtypeStruct((BATCH, HG_H * HG_DK, HG_DV), F32),
            jax.ShapeDtypeStruct((BATCH, GLA_H * GLA_DK, GLA_DV), F32),
        ],
        scratch_shapes=[pltpu.VMEM((HG_DV, HG_H * HG_DK), F32), pltpu.VMEM((GLA_DV, GLA_H * GLA_DK), F32)],
        compiler_params=_params(("arbitrary", "arbitrary")),
        name="even_prompt",
    )(z, zgr, y_sample, lbp, wgk, bgk, gnh, gng, tri, masks)


def _even_sample_kernel(zr_ref, zt_ref, grt_ref, lbpt_ref, wgkt_ref, bgkt_ref, gnh_ref, gng_ref,
                        shg_ref, sgla_ref, y_ref, shg_out, sgla_out, o_scr):
    zt = zt_ref[0]
    hq_t, hf_t = zt[0:512], zt[512:1024]
    gq_t, gk_t = zt[2048:2304], zt[2304:2560]
    pt = lbpt_ref[...]
    pe = jnp.exp(pt - jnp.max(pt, axis=1, keepdims=True))
    lb = pe[:, 0:1] / jnp.sum(pe, axis=1, keepdims=True)
    a_hg = jnp.exp(jnp.log(lb + (1.0 - lb) * _sigmoid(hf_t)))
    k_hg = (1.0 - lb) * _sigmoid(-hf_t)
    q_hg = _silu(hq_t)
    la = _log_sigmoid(_dot(wgkt_ref[...], grt_ref[0].astype(BF16)) + bgkt_ref[...]) / GLA_GATE_NORM
    a_gla = jnp.exp(la)
    q_gla = gq_t * GLA_DK ** -0.5
    zr = zr_ref[...]
    hi, hg = zr[:, 1024:1536], zr[:, 1536:2048]
    gv, gg = zr[:, 2560:3072], zr[:, 3072:3584]

    for j in range(SG):
        for h in range(HG_H):
            ks = slice(h * HG_DK, (h + 1) * HG_DK)
            s_new = a_hg[ks, j:j + 1] * shg_ref[j, h] + k_hg[ks, j:j + 1] * hi[j:j + 1, h * 128:(h + 1) * 128]
            shg_out[j, h] = s_new
            o_scr[j:j + 1, h * 128:(h + 1) * 128] = jnp.sum(q_hg[ks, j:j + 1] * s_new, axis=0, keepdims=True)
        for h in range(GLA_H):
            ks = slice(h * GLA_DK, (h + 1) * GLA_DK)
            s_new = a_gla[ks, j:j + 1] * sgla_ref[j, h] + gk_t[ks, j:j + 1] * gv[j:j + 1, h * 128:(h + 1) * 128]
            sgla_out[j, h] = s_new
            o_scr[j:j + 1, 512 + h * 128:512 + (h + 1) * 128] = jnp.sum(
                q_gla[ks, j:j + 1] * s_new, axis=0, keepdims=True)

    o = o_scr[...]
    for h in range(HG_H):
        cs = slice(h * 128, (h + 1) * 128)
        y_ref[:, cs] = _rms_gate(o[:, cs], hg[:, cs], gnh_ref[...]).astype(BF16)
    for h in range(GLA_H):
        cs = slice(512 + h * 128, 512 + (h + 1) * 128)
        y_ref[:, cs] = _rms_gate(o[:, cs], gg[:, h * 128:(h + 1) * 128], gng_ref[...]).astype(BF16)


def _even_sample(z, zt3, grt3, lbpt, wgkt, bgkt, gnh, gng, s_hg, s_gla):
    c2 = lambda g: (0, 0)
    return pl.pallas_call(
        _even_sample_kernel,
        grid=(NS // SG,),
        in_specs=[
            pl.BlockSpec((SG, EVEN_MAIN), lambda g: (NP // SG + g, 0)),
            pl.BlockSpec((1, EVEN_MAIN, SG), lambda g: (g, 0, 0)),
            pl.BlockSpec((1, LANES, SG), lambda g: (g, 0, 0)),
            pl.BlockSpec(lbpt.shape, c2),
            pl.BlockSpec(wgkt.shape, c2),
            pl.BlockSpec(bgkt.shape, c2),
            pl.BlockSpec(gnh.shape, c2),
            pl.BlockSpec(gng.shape, c2),
            pl.BlockSpec((SG, HG_H, HG_DK, HG_DV), lambda g: (g, 0, 0, 0)),
            pl.BlockSpec((SG, GLA_H, GLA_DK, GLA_DV), lambda g: (g, 0, 0, 0)),
        ],
        out_specs=[
            pl.BlockSpec((SG, D), lambda g: (g, 0)),
            pl.BlockSpec((SG, HG_H, HG_DK, HG_DV), lambda g: (g, 0, 0, 0)),
            pl.BlockSpec((SG, GLA_H, GLA_DK, GLA_DV), lambda g: (g, 0, 0, 0)),
        ],
        out_shape=[
            jax.ShapeDtypeStruct((NS, D), BF16),
            jax.ShapeDtypeStruct((NS, HG_H, HG_DK, HG_DV), F32),
            jax.ShapeDtypeStruct((NS, GLA_H, GLA_DK, GLA_DV), F32),
        ],
        scratch_shapes=[pltpu.VMEM((SG, D), F32)],
        compiler_params=_params(("parallel",)),
        name="even_sample",
    )(z, zt3, grt3, lbpt, wgkt, bgkt, gnh, gng, s_hg, s_gla)


def _out_ln_kernel(x_ref, y_ref, w_ref, g_ref, b_ref, o_ref):
    r = ALPHA * x_ref[...] + _dot(y_ref[...], w_ref[...])
    o_ref[...] = _layernorm(r, g_ref[...], b_ref[...])


def _out_ln(x, y, w, g, b):
    c2 = lambda i: (0, 0)
    return pl.pallas_call(
        _out_ln_kernel,
        grid=(NT // TM,),
        in_specs=[
            pl.BlockSpec((TM, D), lambda i: (i, 0)),
            pl.BlockSpec((TM, D), lambda i: (i, 0)),
            pl.BlockSpec((D, D), c2),
            pl.BlockSpec((1, D), c2),
            pl.BlockSpec((1, D), c2),
        ],
        out_specs=pl.BlockSpec((TM, D), lambda i: (i, 0)),
        out_shape=jax.ShapeDtypeStruct((NT, D), F32),
        compiler_params=_params(("parallel",)),
        name="out_ln",
    )(x, y, w, g, b)


FF_SPLIT = 2


def _ffn_kernel(x_ref, w1_ref, w3_ref, w2_ref, g_ref, b_ref, o_ref):
    x = x_ref[...]
    xb = x.astype(BF16)
    step = D_FF_DENSE // FF_SPLIT
    acc = ALPHA * x
    for s in range(FF_SPLIT):
        cs = slice(s * step, (s + 1) * step)
        hmid = _silu(_dot(xb, w1_ref[:, cs])) * _dot(xb, w3_ref[:, cs])
        acc = acc + _dot(hmid.astype(BF16), w2_ref[cs, :])
    o_ref[...] = _layernorm(acc, g_ref[...], b_ref[...])


def _ffn(x, w1, w3, w2, g, b):
    c2 = lambda i: (0, 0)
    one = pl.Buffered(1)
    return pl.pallas_call(
        _ffn_kernel,
        grid=(NT // TM,),
        in_specs=[
            pl.BlockSpec((TM, D), lambda i: (i, 0)),
            pl.BlockSpec((D, D_FF_DENSE), c2, pipeline_mode=one),
            pl.BlockSpec((D, D_FF_DENSE), c2, pipeline_mode=one),
            pl.BlockSpec((D_FF_DENSE, D), c2, pipeline_mode=one),
            pl.BlockSpec((1, D), c2),
            pl.BlockSpec((1, D), c2),
        ],
        out_specs=pl.BlockSpec((TM, D), lambda i: (i, 0)),
        out_shape=jax.ShapeDtypeStruct((NT, D), F32),
        compiler_params=_params(("parallel",)),
        name="ffn_dense",
    )(x, w1, w3, w2, g, b)


def _mh_norm_gate(hh, o_pre, w):
    mu = jnp.mean(hh, axis=-1, keepdims=True)
    c = hh - mu
    var = jnp.mean(c * c, axis=-1, keepdims=True)
    return _sigmoid(o_pre) * (c * lax.rsqrt(var + EPS) * w)


def _odd_prompt_kernel(z_ref, zg_ref, ys_ref, bg_ref, cw_ref, cb_ref, hnw_ref, tri_ref,
                       y_ref, c_out, n_out, m_out, conv_out,
                       c_scr, n_scr, m_scr, u_scr):
    b = pl.program_id(0)
    c = pl.program_id(1)

    @pl.when((b < BATCH) & (c == 0))
    def _():
        c_scr[...] = jnp.zeros_like(c_scr)
        n_scr[...] = jnp.zeros_like(n_scr)
        m_scr[...] = jnp.zeros_like(m_scr)
        u_scr[0:8, :] = jnp.zeros((8, D), F32)

    @pl.when((b == BATCH) & (c == 0))
    def _():
        y_ref[...] = ys_ref[...]

    @pl.when(b < BATCH)
    def _():
        row = lax.broadcasted_iota(jnp.int32, (CS, CS), 0)
        col = lax.broadcasted_iota(jnp.int32, (CS, CS), 1)
        causal = col <= row

        u_scr[8:8 + CS, :] = z_ref[:, 0:D]
        uc = cb_ref[...]
        for j in range(CONV_W):
            uc = uc + u_scr[5 + j:5 + j + CS, :] * cw_ref[j:j + 1, :]
        tail = u_scr[CS:CS + 8, :]
        u_scr[0:8, :] = tail
        act = _silu(uc)
        q = act[:, 0:512] * ML_DK ** -0.5
        k = act[:, 512:1024]
        v = z_ref[:, D:2 * D]
        o_pre = z_ref[:, 2 * D:3 * D]

        gates = zg_ref[...] + bg_ref[...]
        lf = _log_sigmoid(gates)
        bcum = _dot_sel(tri_ref[...], lf)
        bcum_t = bcum.T
        gates_t = gates.T
        m_all = m_scr[...]

        for h in range(ML_H):
            ks = slice(h * ML_DK, (h + 1) * ML_DK)
            vs = slice(h * ML_DV, (h + 1) * ML_DV)
            qh, kh = q[:, ks], k[:, ks]
            vh = v[:, vs].astype(BF16)
            b_col = bcum[:, 4 + h:5 + h]
            b_row = bcum_t[4 + h:5 + h, :]
            i_col = gates[:, h:h + 1]
            i_row = gates_t[h:h + 1, :]
            m_prev = m_all[:, h:h + 1]
            log_d = jnp.where(causal, b_col - b_row + i_row, -jnp.inf)
            log_prev = b_col + m_prev
            m_t = jnp.maximum(jnp.max(log_d, axis=-1, keepdims=True), log_prev)
            d = jnp.exp(log_d - m_t)
            w_prev = jnp.exp(log_prev - m_t)
            scores = _dot_nt(qh.astype(BF16), kh.astype(BF16)) * d
            c_h = c_scr[h]
            n_h = n_scr[h:h + 1, :]
            num = _dot(scores.astype(BF16), vh) + w_prev * _dot(qh.astype(BF16), c_h.astype(BF16))
            den = jnp.sum(scores, axis=-1, keepdims=True) + w_prev * jnp.sum(qh * n_h, axis=-1, keepdims=True)
            hh = num / jnp.maximum(jnp.abs(den), jnp.exp(-m_t))
            m_new = m_t[CS - 1:CS, :]
            b_last = b_col[CS - 1:CS, :]
            w_c = jnp.exp(b_last + m_prev - m_new)
            w_s = jnp.exp(b_last - b_col + i_col - m_new)
            kw = kh * w_s
            c_scr[h] = w_c * c_h + _dot_tn(kw.astype(BF16), vh)
            n_scr[h:h + 1, :] = w_c * n_h + jnp.sum(kw, axis=0, keepdims=True)
            m_scr[:, h:h + 1] = m_new
            y_ref[:, vs] = _mh_norm_gate(hh, o_pre[:, vs], hnw_ref[:, vs]).astype(BF16)

        @pl.when(c == NCHUNK - 1)
        def _():
            c_out[0] = c_scr[...]
            n_out[0] = n_scr[0:ML_H, :]
            m_out[0] = m_scr[...]
            conv_out[0] = tail[8 - (CONV_W - 1):8, :]


def _odd_prompt(z, zg, y_sample, bg, cw, cb, hnw, tri):
    c2 = lambda b, c: (0, 0)
    per_sequence4 = lambda b, c: (jnp.minimum(b, BATCH - 1), 0, 0, 0)
    return pl.pallas_call(
        _odd_prompt_kernel,
        grid=(BATCH + 1, NCHUNK),
        in_specs=[
            pl.BlockSpec((CS, ODD_MAIN), _chunk_rows),
            pl.BlockSpec((CS, LANES), _chunk_rows),
            pl.BlockSpec((NS, D), c2),
            pl.BlockSpec((1, LANES), c2),
            pl.BlockSpec((CONV_W, D), c2),
            pl.BlockSpec((1, D), c2),
            pl.BlockSpec((1, D), c2),
            pl.BlockSpec((CS, CS), c2),
        ],
        out_specs=[
            pl.BlockSpec((CS, D), _chunk_rows),
            pl.BlockSpec((1, ML_H, ML_DK, ML_DV), per_sequence4),
            pl.BlockSpec((1, ML_H, ML_DK), _per_sequence),
            pl.BlockSpec((1, 1, LANES), _per_sequence),
            pl.BlockSpec((1, CONV_W - 1, D), _per_sequence),
        ],
        out_shape=[
            jax.ShapeDtypeStruct((NT, D), BF16),
            jax.ShapeDtypeStruct((BATCH, ML_H, ML_DK, ML_DV), F32),
            jax.ShapeDtypeStruct((BATCH, ML_H, ML_DK), F32),
            jax.ShapeDtypeStruct((BATCH, 1, LANES), F32),
            jax.ShapeDtypeStruct((BATCH, CONV_W - 1, D), F32),
        ],
        scratch_shapes=[
            pltpu.VMEM((ML_H, ML_DK, ML_DV), F32),
            pltpu.VMEM((8, ML_DK), F32),
            pltpu.VMEM((1, LANES), F32),
            pltpu.VMEM((CS + 8, D), F32),
        ],
        compiler_params=_params(("arbitrary", "arbitrary")),
        name="odd_prompt",
    )(z, zg, y_sample, bg, cw, cb, hnw, tri)


def _odd_sample_kernel(zr_ref, zg_ref, ut_ref, conv_ref, convt_ref, bg_ref, cw_ref, cwt_ref, cb_ref, cbt_ref,
                       hnw_ref, c_ref, n_ref, m_ref,
                       y_ref, c_out, n_out, m_out, conv_out, h_scr):
    zr = zr_ref[...]
    u = zr[:, 0:D]
    v = zr[:, D:2 * D]
    o_pre = zr[:, 2 * D:3 * D]
    uc = cb_ref[...] + u * cw_ref[CONV_W - 1:CONV_W, :]
    uc_t = cbt_ref[...] + ut_ref[0] * cwt_ref[:, CONV_W - 1:CONV_W]
    for j in range(CONV_W - 1):
        uc = uc + conv_ref[:, j * D:(j + 1) * D] * cw_ref[j:j + 1, :]
        uc_t = uc_t + convt_ref[0, j] * cwt_ref[:, j:j + 1]
        conv_out[:, j * D:(j + 1) * D] = conv_ref[:, (j + 1) * D:(j + 2) * D] if j + 1 < CONV_W - 1 else u
    act = _silu(uc)
    k_row = act[:, 512:1024]
    act_t = _silu(uc_t)
    q_t = act_t[0:512] * ML_DK ** -0.5
    k_t = act_t[512:1024]
    q_row = act[:, 0:512] * ML_DK ** -0.5

    gates = zg_ref[...] + bg_ref[...]
    lf = _log_sigmoid(gates)
    m_in = m_ref[...]
    m_out[...] = m_in

    for j in range(SG):
        for h in range(ML_H):
            ks = slice(h * ML_DK, (h + 1) * ML_DK)
            vs = slice(h * ML_DV, (h + 1) * ML_DV)
            ig = gates[j:j + 1, h:h + 1]
            log_prev = lf[j:j + 1, 4 + h:5 + h] + m_in[j:j + 1, h:h + 1]
            m_t = jnp.maximum(ig, log_prev)
            d = jnp.exp(ig - m_t)
            w_prev = jnp.exp(log_prev - m_t)
            c_new = w_prev * c_ref[j, h] + (d * k_t[ks, j:j + 1]) * v[j:j + 1, vs]
            n_new = w_prev * n_ref[j, h:h + 1, :] + d * k_row[j:j + 1, ks]
            c_out[j, h] = c_new
            n_out[j, h:h + 1, :] = n_new
            m_out[j:j + 1, h:h + 1] = m_t
            num = jnp.sum(q_t[ks, j:j + 1] * c_new, axis=0, keepdims=True)
            den = jnp.sum(q_row[j:j + 1, ks] * n_new, axis=-1, keepdims=True)
            h_scr[j:j + 1, vs] = num / jnp.maximum(jnp.abs(den), jnp.exp(-m_t))

    hh = h_scr[...]
    for h in range(ML_H):
        vs = slice(h * ML_DV, (h + 1) * ML_DV)
        y_ref[:, vs] = _mh_norm_gate(hh[:, vs], o_pre[:, vs], hnw_ref[:, vs]).astype(BF16)


def _odd_sample(z, zg, ut3, conv, convt, bg, cw, cwt, cb, cbt, hnw, c_in, n_in, m_in):
    c2 = lambda g: (0, 0)
    return pl.pallas_call(
        _odd_sample_kernel,
        grid=(NS // SG,),
        in_specs=[
            pl.BlockSpec((SG, ODD_MAIN), lambda g: (NP // SG + g, 0)),
            pl.BlockSpec((SG, LANES), lambda g: (NP // SG + g, 0)),
            pl.BlockSpec((1, D, SG), lambda g: (g, 0, 0)),
            pl.BlockSpec((SG, (CONV_W - 1) * D), lambda g: (g, 0)),
            pl.BlockSpec((1, CONV_W - 1, D, SG), lambda g: (g, 0, 0, 0)),
            pl.BlockSpec((1, LANES), c2),
            pl.BlockSpec((CONV_W, D), c2),
            pl.BlockSpec((D, CONV_W), c2),
            pl.BlockSpec((1, D), c2),
            pl.BlockSpec((D, 1), c2),
            pl.BlockSpec((1, D), c2),
            pl.BlockSpec((SG, ML_H, ML_DK, ML_DV), lambda g: (g, 0, 0, 0)),
            pl.BlockSpec((SG, ML_H, ML_DK), lambda g: (g, 0, 0)),
            pl.BlockSpec((SG, LANES), lambda g: (g, 0)),
        ],
        out_specs=[
            pl.BlockSpec((SG, D), lambda g: (g, 0)),
            pl.BlockSpec((SG, ML_H, ML_DK, ML_DV), lambda g: (g, 0, 0, 0)),
            pl.BlockSpec((SG, ML_H, ML_DK), lambda g: (g, 0, 0)),
            pl.BlockSpec((SG, LANES), lambda g: (g, 0)),
            pl.BlockSpec((SG, (CONV_W - 1) * D), lambda g: (g, 0)),
        ],
        out_shape=[
            jax.ShapeDtypeStruct((NS, D), BF16),
            jax.ShapeDtypeStruct((NS, ML_H, ML_DK, ML_DV), F32),
            jax.ShapeDtypeStruct((NS, ML_H, ML_DK), F32),
            jax.ShapeDtypeStruct((NS, LANES), F32),
            jax.ShapeDtypeStruct((NS, (CONV_W - 1) * D), F32),
        ],
        scratch_shapes=[pltpu.VMEM((SG, D), F32)],
        compiler_params=_params(("parallel",)),
        name="odd_sample",
    )(z, zg, ut3, conv, convt, bg, cw, cwt, cb, cbt, hnw, c_in, n_in, m_in)


def _out_ln_router_kernel(x_ref, y_ref, w_ref, g_ref, b_ref, wr_ref, tri_ref,
                          o_ref, op_ref, meta_ref, cnt_ref, tab_ref, carry, filled):
    i = pl.program_id(0)

    @pl.when(i == 0)
    def _():
        carry[...] = jnp.zeros_like(carry)
        tab_ref[...] = jnp.zeros_like(tab_ref)
        for e in range(N_EXPERTS):
            filled[e] = 0

    r = ALPHA * x_ref[...] + _dot(y_ref[...], w_ref[...])
    x3 = _layernorm(r, g_ref[...], b_ref[...])
    o_ref[...] = x3
    op_ref[...] = pltpu.pack_elementwise([x3[:, :D // 2], x3[:, D // 2:]], packed_dtype=BF16)

    lane = lax.broadcasted_iota(jnp.int32, (TM, LANES), 1).astype(F32)
    logits = jnp.where(lane < N_EXPERTS, _dot(x3.astype(BF16), wr_ref[...]), -jnp.inf)
    m1 = jnp.max(logits, axis=-1, keepdims=True)
    i1 = jnp.min(jnp.where(logits == m1, lane, float(LANES)), axis=-1, keepdims=True)
    rest = jnp.where(lane == i1, -jnp.inf, logits)
    m2 = jnp.max(rest, axis=-1, keepdims=True)
    i2 = jnp.min(jnp.where(rest == m2, lane, float(LANES)), axis=-1, keepdims=True)
    e2 = jnp.exp(m2 - m1)
    tot = 1.0 + e2
    w1 = 1.0 / tot
    w2 = e2 / tot

    sel1 = lane == i1
    sel2 = lane == i2
    onehot = jnp.where(sel1 | sel2, 1.0, 0.0)
    in_tile = _dot(tri_ref[...], onehot.astype(BF16))
    carry[...] = carry[...] + jnp.sum(onehot, axis=0, keepdims=True)
    cnt_ref[...] = carry[...]

    meta = jnp.where(lane == 0.0, i1, 0.0)
    meta = jnp.where(lane == 1.0, i2, meta)
    meta = jnp.where(lane == 2.0, w1, meta)
    meta = jnp.where(lane == 3.0, w2, meta)
    meta_ref[...] = meta

    token = (i * TM + lax.broadcasted_iota(jnp.int32, (TM, 1), 0)).astype(F32)
    digit_hi = jnp.floor(token * (1.0 / 256.0))
    rhs = jnp.where(lane == 0.0, digit_hi, jnp.where(lane == 1.0, token - 256.0 * digit_hi, 0.0))
    rhs = jnp.where(lane == i2 + float(SUBLANES), 1.0, rhs).astype(BF16)
    place = lax.broadcasted_iota(jnp.int32, (TM, TM), 0).astype(F32)
    rank_rows = jnp.where(onehot > 0.0, in_tile, -1.0).T
    entries = []
    for e in range(N_EXPERTS):
        hit = jnp.where(rank_rows[e:e + 1, :] == place, 1.0, 0.0).astype(BF16)
        got = _dot(hit, rhs)
        entries.append(256.0 * got[:, 0:1] + got[:, 1:2]
                       + float(NT) * got[:, SUBLANES + e:SUBLANES + e + 1])
    tile_counts = jnp.sum(onehot, axis=0, keepdims=True)
    counts = [jnp.sum(tile_counts[:, e:e + 1]).astype(jnp.int32) for e in range(N_EXPERTS)]
    for e in range(N_EXPERTS):
        start = filled[e]
        filled[e] = start + counts[e]
        tab_ref[pl.ds(start, LANES), e:e + 1] = entries[e][0:LANES]
        for part in range(1, TM // LANES):
            @pl.when(counts[e] > part * LANES)
            def _():
                tab_ref[pl.ds(start + part * LANES, LANES), e:e + 1] = entries[e][part * LANES:(part + 1) * LANES]


def _out_ln_router(x, y, w, g, b, wr, tri):
    c2 = lambda i: (0, 0)
    return pl.pallas_call(
        _out_ln_router_kernel,
        grid=(NT // TM,),
        in_specs=[
            pl.BlockSpec((TM, D), lambda i: (i, 0)),
            pl.BlockSpec((TM, D), lambda i: (i, 0)),
            pl.BlockSpec((D, D), c2),
            pl.BlockSpec((1, D), c2),
            pl.BlockSpec((1, D), c2),
            pl.BlockSpec((D, LANES), c2),
            pl.BlockSpec((TM, TM), c2),
        ],
        out_specs=[
            pl.BlockSpec((TM, D), lambda i: (i, 0)),
            pl.BlockSpec((TM, D // 2), lambda i: (i, 0)),
            pl.BlockSpec((TM, LANES), lambda i: (i, 0)),
            pl.BlockSpec((1, LANES), c2),
            pl.BlockSpec((MOE_CAP + TM, LANES), c2),
        ],
        out_shape=[
            jax.ShapeDtypeStruct((NT, D), F32),
            jax.ShapeDtypeStruct((NT, D // 2), U32),
            jax.ShapeDtypeStruct((NT, LANES), F32),
            jax.ShapeDtypeStruct((1, LANES), F32),
            jax.ShapeDtypeStruct((MOE_CAP + TM, LANES), F32),
        ],
        scratch_shapes=[pltpu.VMEM((1, LANES), F32), pltpu.SMEM((N_EXPERTS,), jnp.int32)],
        compiler_params=_params(("arbitrary",)),
        name="out_ln_router",
    )(x, y, w, g, b, wr, tri)


def _moe_ffn_kernel(te_ref, nu_ref, tb_ref, gnext_ref, gcur_ref, sprev_ref, scur_ref, xp_ref,
                    w1_ref, w3_ref, w2_ref, out_hbm, stage, yacc, xb_scr, sem_s):
    del tb_ref
    i = pl.program_id(0)
    j = pl.program_id(1)
    used = i < nu_ref[0]
    slot = i % 2
    other = 1 - slot
    rps = MOE_ROWS_PER_STEP

    def gather_rows(tab_ref, buf, part):
        for r in range(rps):
            stage[buf, part, pl.ds(r, 1), :] = xp_ref[pl.ds(tab_ref[part * rps + r], 1), :]

    def scatter(buf, r, dst):
        return pltpu.make_async_copy(yacc.at[buf, pl.ds(r, 1)], out_hbm.at[pl.ds(dst, 1)], sem_s)

    def wait_scatters(n):
        for _ in range(n):
            scatter(0, 0, 0).wait()

    def issue_neighbours():
        gather_rows(gnext_ref, other, j)
        for r in range(rps):
            rr = j * rps + r
            scatter(other, rr, sprev_ref[rr]).start()

    @pl.when(j == 0)
    def _():
        @pl.when(i == 0)
        def _():
            yacc[1] = jnp.zeros((TMM, D), F32)
            for part in range(MOE_NFF):
                gather_rows(gcur_ref, 0, part)

        @pl.when(i > 0)
        def _():
            wait_scatters(TMM)

        words = stage[slot].reshape(TMM, D // 2)
        for half in range(2):
            xb_scr[:, half * (D // 2):(half + 1) * (D // 2)] = pltpu.unpack_elementwise(
                words, index=half, packed_dtype=BF16, unpacked_dtype=F32).astype(BF16)
        yacc[slot] = jnp.zeros((TMM, D), F32)

    @pl.when(used)
    def _():
        issue_neighbours()
        xb = xb_scr[...]
        hmid = _silu(_dot(xb, w1_ref[...])) * _dot(xb, w3_ref[...])
        yacc[slot] += _dot(hmid.astype(BF16), w2_ref[...])

    @pl.when(jnp.logical_not(used))
    def _():
        issue_neighbours()

    @pl.when((i == N_MOE_TILES - 1) & (j == MOE_NFF - 1))
    def _():
        for r in range(TMM):
            scatter(slot, r, scur_ref[r]).start()
        wait_scatters(2 * TMM)


def _moe_ffn(tile_expert, n_used, tab_block, gsrc, sdst, xp, w1, w3, w2):
    nff = MOE_NFF

    def wcol(i, j, te, nu, tb):
        return (te[i], 0, jnp.where(i < nu[0], j, nff - 1))

    def wrow(i, j, te, nu, tb):
        return (te[i], jnp.where(i < nu[0], j, nff - 1), 0)

    smem = functools.partial(pl.BlockSpec, (MOE_TAB,), memory_space=pltpu.SMEM)
    grid_spec = pltpu.PrefetchScalarGridSpec(
        num_scalar_prefetch=3,
        grid=(N_MOE_TILES, nff),
        in_specs=[
            smem(lambda i, j, te, nu, tb: (tb[i + 2],)),
            smem(lambda i, j, te, nu, tb: (tb[i + 1],)),
            smem(lambda i, j, te, nu, tb: (tb[i],)),
            smem(lambda i, j, te, nu, tb: (tb[i + 1],)),
            pl.BlockSpec((NT, D // 2), lambda i, j, te, nu, tb: (0, 0), pipeline_mode=pl.Buffered(1)),
            pl.BlockSpec((None, D, TFF), wcol),
            pl.BlockSpec((None, D, TFF), wcol),
            pl.BlockSpec((None, TFF, D), wrow),
        ],
        out_specs=pl.BlockSpec(memory_space=pl.ANY),
        scratch_shapes=[
            pltpu.VMEM((2, MOE_NFF, MOE_ROWS_PER_STEP, D // 2), U32),
            pltpu.VMEM((2, TMM, D), F32),
            pltpu.VMEM((TMM, D), BF16),
            pltpu.SemaphoreType.DMA(()),
        ],
    )
    return pl.pallas_call(
        _moe_ffn_kernel,
        grid_spec=grid_spec,
        out_shape=jax.ShapeDtypeStruct((MOE_OUT_ROWS, D), F32),
        compiler_params=_params(("arbitrary", "arbitrary")),
        name="moe_ffn",
    )(tile_expert, n_used, tab_block, gsrc, gsrc, sdst, sdst, xp, w1, w3, w2)


def _combine_kernel(x_ref, meta_ref, y0_ref, y1_ref, g_ref, b_ref, o_ref):
    meta = meta_ref[...]
    moe = meta[:, 2:3] * y0_ref[...] + meta[:, 3:4] * y1_ref[...]
    o_ref[...] = _layernorm(ALPHA * x_ref[...] + moe, g_ref[...], b_ref[...])


def _combine(x, meta, ys, g, b):
    c2 = lambda i: (0, 0)
    return pl.pallas_call(
        _combine_kernel,
        grid=(NT // TM,),
        in_specs=[
            pl.BlockSpec((TM, D), lambda i: (i, 0)),
            pl.BlockSpec((TM, LANES), lambda i: (i, 0)),
            pl.BlockSpec((TM, D), lambda i: (i, 0)),
            pl.BlockSpec((TM, D), lambda i: (i + NT // TM, 0)),
            pl.BlockSpec((1, D), c2),
            pl.BlockSpec((1, D), c2),
        ],
        out_specs=pl.BlockSpec((TM, D), lambda i: (i, 0)),
        out_shape=jax.ShapeDtypeStruct((NT, D), F32),
        compiler_params=_params(("parallel",)),
        name="moe_combine",
    )(x, meta, ys, ys, g, b)


def _pad_cols(w, n):
    return jnp.pad(w, ((0, 0), (0, n - w.shape[1])))


def kernel(x_prompt, x_sample, state_hgrn, state_gla, state_mlstm_C, state_mlstm_n, state_mlstm_m,
           state_mlstm_conv, w_in_even, hg_lower_bounds, w_gk, b_gk, gn_hg, gn_gla, w_out_even,
           w1_dense, w3_dense, w2_dense, w_in_odd, b_gate_odd, conv_w, conv_b, hn_w, w_out_odd,
           w_router, w1_moe, w3_moe, w2_moe, ln1_g, ln1_b, ln2_g, ln2_b):
    assert x_prompt.shape == (BATCH, SEQ, D) and x_sample.shape == (NS, 1, D)
    assert w_in_even.shape[0] == 1 and w_in_odd.shape[0] == 1 and hg_lower_bounds.shape[0] == 2
    masks = jnp.asarray(_gla_masks(), F32)
    tri_cs = jnp.asarray(_tri(CS, False), BF16)
    tri_tm = jnp.asarray(_tri(TM, True), BF16)
    row = lambda a: a.reshape(1, -1)

    x0 = jnp.concatenate([x_prompt.reshape(NP, D), x_sample.reshape(NS, D)], axis=0)

    w_even = w_in_even[0].astype(BF16)
    z, zgr = _proj(x0, w_even[:, :EVEN_MAIN], _pad_cols(w_even[:, EVEN_MAIN:], LANES))
    wgk = jnp.pad(w_gk[0].astype(BF16), ((0, LANES - GLA_RANK), (0, 0)))
    lbp = hg_lower_bounds
    zs = z[NP:].reshape(NS // SG, SG, EVEN_MAIN).transpose(0, 2, 1)
    grs = zgr[NP:].reshape(NS // SG, SG, LANES).transpose(0, 2, 1)
    y_s, hg_s, gla_s = _even_sample(z, zs, grs, lbp.T, wgk.T, b_gk[0].reshape(-1, 1),
                                    row(gn_hg[0]), row(gn_gla[0]), state_hgrn[0], state_gla[0])
    y, hg_p, gla_p = _even_prompt(z, zgr, y_s, lbp, wgk, row(b_gk[0]), row(gn_hg[0]), row(gn_gla[0]), tri_cs, masks)
    x1 = _out_ln(x0, y, w_out_even[0].astype(BF16), row(ln1_g[0]), row(ln1_b[0]))
    x2 = _ffn(x1, w1_dense[0].astype(BF16), w3_dense[0].astype(BF16), w2_dense[0].astype(BF16),
              row(ln2_g[0]), row(ln2_b[0]))

    w_odd = w_in_odd[0].astype(BF16)
    zo, zog = _proj(x2, w_odd[:, :ODD_MAIN], _pad_cols(w_odd[:, ODD_MAIN:], LANES))
    bg = jnp.pad(b_gate_odd[0], (0, LANES - 2 * ML_H)).reshape(1, LANES)
    ut = zo[NP:, :D].reshape(NS // SG, SG, D).transpose(0, 2, 1)
    conv_in = state_mlstm_conv[0]
    conv_t = conv_in.reshape(NS // SG, SG, CONV_W - 1, D).transpose(0, 2, 3, 1)
    m_in = jnp.pad(state_mlstm_m[0], ((0, 0), (0, LANES - ML_H)))
    yo_s, c_s, n_s, m_s, conv_s = _odd_sample(
        zo, zog, ut, conv_in.reshape(NS, (CONV_W - 1) * D), conv_t, bg, conv_w[0], conv_w[0].T, row(conv_b[0]), conv_b[0].reshape(-1, 1),
        row(hn_w[0]), state_mlstm_C[0], state_mlstm_n[0], m_in)
    yo, c_p, n_p, m_p, conv_p = _odd_prompt(zo, zog, yo_s, bg, conv_w[0], row(conv_b[0]), row(hn_w[0]), tri_cs)

    wr = _pad_cols(w_router[0].astype(BF16), LANES)
    x3, x3p, meta, cnt, tab = _out_ln_router(
        x2, yo, w_out_odd[0].astype(BF16), row(ln1_g[1]), row(ln1_b[1]), wr, tri_tm)

    counts = cnt[0, :N_EXPERTS].astype(jnp.int32)
    padded = ((counts + TMM - 1) // TMM) * TMM
    ends = jnp.cumsum(padded)
    offsets = ends - padded
    tile = jnp.arange(N_MOE_TILES, dtype=jnp.int32)
    tile_expert = jnp.minimum(jnp.sum((tile * TMM)[:, None] >= ends[None, :], axis=1), N_EXPERTS - 1).astype(jnp.int32)
    n_tiles_used = ends[-1] // TMM
    in_use = tile < n_tiles_used
    local_tile = (tile * TMM - offsets[tile_expert]) // TMM
    blocks_per_expert = MOE_CAP // MOE_TAB
    spill_block = N_EXPERTS * blocks_per_expert
    n_spare = N_MOE_TILES - (2 * NT) // TMM
    entry = tab[:MOE_CAP, :N_EXPERTS].T.astype(jnp.int32)
    local = jnp.arange(MOE_CAP, dtype=jnp.int32)[None, :]
    valid = local < counts[:, None]
    padding = padded - counts
    pad_before = (jnp.cumsum(padding) - padding)[:, None]
    row_in_block = jnp.arange(MOE_TAB, dtype=jnp.int32)
    spare_rows = (2 * NT + TMM + jnp.sum(padding)
                  + jnp.arange(n_spare, dtype=jnp.int32)[:, None] * TMM + row_in_block[None, :])
    sdst = jnp.concatenate([
        jnp.where(valid, entry, 2 * NT + TMM + pad_before + local - counts[:, None]).reshape(-1),
        2 * NT + row_in_block, spare_rows.reshape(-1)])
    gsrc = jnp.concatenate([
        jnp.where(valid, entry - jnp.where(entry >= NT, NT, 0), 0).reshape(-1),
        jnp.zeros(((1 + n_spare) * MOE_TAB,), jnp.int32)])
    own_block = jnp.where(in_use, tile_expert * blocks_per_expert + local_tile, spill_block + 1 + tile - n_tiles_used)
    tab_block = jnp.concatenate([jnp.full((1,), spill_block, jnp.int32), own_block.astype(jnp.int32),
                                 jnp.full((1,), spill_block, jnp.int32)])

    ys = _moe_ffn(tile_expert, n_tiles_used.reshape(1), tab_block, gsrc, sdst, x3p,
                  w1_moe[0].astype(BF16), w3_moe[0].astype(BF16), w2_moe[0].astype(BF16))
    out = _combine(x3, meta, ys, row(ln2_g[1]), row(ln2_b[1]))

    y_prompt = out[:NP].reshape(BATCH, SEQ, D)
    y_sample = out[NP:].reshape(NS, 1, D)
    return (y_prompt, y_sample,
            hg_p.reshape(1, BATCH, HG_H, HG_DK, HG_DV), gla_p.reshape(1, BATCH, GLA_H, GLA_DK, GLA_DV),
            c_p[None], n_p[None], m_p[:, 0, :ML_H][None], conv_p[None],
            hg_s[None], gla_s[None], c_s[None], n_s[None], m_s[:, :ML_H][None], conv_s.reshape(1, NS, CONV_W - 1, D))
```

```python
import functools
import math

import jax
import jax.numpy as jnp
import numpy as np
from jax import lax
from jax.experimental import pallas as pl
from jax.experimental.pallas import tpu as pltpu

F32 = jnp.float32
BF16 = jnp.bfloat16
U32 = jnp.uint32

D = 1024
BATCH = 8
SEQ = 2048
DEC_BATCH = 128
NP = BATCH * SEQ
NS = DEC_BATCH
NT = NP + NS
HG_H, HG_DK, HG_DV = 4, 128, 128
GLA_H, GLA_DK, GLA_DV = 4, 64, 128
GLA_RANK = 16
GLA_GATE_NORM = 16.0
ML_H, ML_DK, ML_DV = 4, 128, 256
CONV_W = 4
D_FF_DENSE = 2816
D_FF_EXPERT = 3584
N_EXPERTS = 8
EPS = 1e-5
DEPTH = 2
ALPHA = (2.0 * DEPTH) ** 0.25
EVEN_MAIN = 3584
ODD_MAIN = 3072

LANES = 128
SUBLANES = 8
VMEM_LIMIT = 56 * 1024 * 1024

TM = 384
LAST_TILE = NT // TM - 1
CS = 128
NCHUNK = SEQ // CS
SG = 16
TMM = 512
TFF = 896
MOE_TAB = 512
MOE_CAP = -(-NT // MOE_TAB) * MOE_TAB
MOE_NFF = D_FF_EXPERT // TFF
MOE_ROWS_PER_STEP = TMM // MOE_NFF
N_MOE_TILES = -(-(2 * NT + N_EXPERTS * (TMM - 1)) // TMM)
MOE_SLOTS = N_MOE_TILES * TMM
MOE_OUT_ROWS = MOE_SLOTS + TMM
N_LEVELS = int(math.log2(CS))

assert NT % TM == 0 and NP % CS == 0 and NS % SG == 0 and D_FF_EXPERT % TFF == 0 and TMM % MOE_NFF == 0
assert TMM == MOE_TAB and NS == CS and LAST_TILE * TM <= NP


def _params(sem, limit=VMEM_LIMIT):
    return pltpu.CompilerParams(dimension_semantics=sem, vmem_limit_bytes=limit)


def _dot(a, b):
    return jnp.dot(a, b, preferred_element_type=F32)


def _dot_nt(a, b):
    return lax.dot_general(a, b, (((1,), (1,)), ((), ())), preferred_element_type=F32)


def _dot_tn(a, b):
    return lax.dot_general(a, b, (((0,), (0,)), ((), ())), preferred_element_type=F32)


def _split3(x):
    hi = x.astype(BF16)
    r1 = x - hi.astype(F32)
    mid = r1.astype(BF16)
    lo = (r1 - mid.astype(F32)).astype(BF16)
    return hi, mid, lo


def _dot_sel(sel, x):
    hi, mid, lo = _split3(x)
    return _dot(sel, hi) + _dot(sel, mid) + _dot(sel, lo)


def _sigmoid(x):
    return jax.nn.sigmoid(x)


def _silu(x):
    return x * jax.nn.sigmoid(x)


def _log_sigmoid(x):
    return jnp.minimum(x, 0.0) - jnp.log(1.0 + jnp.exp(-jnp.abs(x)))


def _layernorm(r, g, b):
    mu = jnp.mean(r, axis=-1, keepdims=True)
    c = r - mu
    var = jnp.mean(c * c, axis=-1, keepdims=True)
    return c * lax.rsqrt(var + EPS) * g + b


def _gla_masks():
    masks = np.zeros((N_LEVELS + 1, CS, CS), np.float32)
    for t in range(CS):
        for l in range(N_LEVELS):
            half = 1 << l
            start = (t // (2 * half)) * (2 * half)
            mid = start + half
            if t >= mid:
                masks[l, t, start:mid] = 1.0
        masks[N_LEVELS, t, t] = 1.0
    return masks


def _tri(n, strict):
    return np.tril(np.ones((n, n), np.float32), -1 if strict else 0)


def _token_tile(x_ref, tail_ref):
    return jnp.where(pl.program_id(0) == LAST_TILE, tail_ref[...], x_ref[...])


def _main_tile(i):
    return (jnp.minimum(i, LAST_TILE - 1), 0)


def _proj_kernel(x_ref, tail_ref, wa_ref, wb_ref, oa_ref, ob_ref):
    xb = _token_tile(x_ref, tail_ref).astype(BF16)
    oa_ref[...] = _dot(xb, wa_ref[...])
    ob_ref[...] = _dot(xb, wb_ref[...])


def _proj(x, tail, wa, wb):
    na, nb = wa.shape[1], wb.shape[1]
    return pl.pallas_call(
        _proj_kernel,
        grid=(NT // TM,),
        in_specs=[
            pl.BlockSpec((TM, D), _main_tile),
            pl.BlockSpec((TM, D), lambda i: (0, 0)),
            pl.BlockSpec((D, na), lambda i: (0, 0)),
            pl.BlockSpec((D, nb), lambda i: (0, 0)),
        ],
        out_specs=[
            pl.BlockSpec((TM, na), lambda i: (i, 0)),
            pl.BlockSpec((TM, nb), lambda i: (i, 0)),
        ],
        out_shape=[jax.ShapeDtypeStruct((NT, na), F32), jax.ShapeDtypeStruct((NT, nb), F32)],
        compiler_params=_params(("parallel",)),
        name="proj",
    )(x, tail, wa, wb)


def _rms_gate(o, gate, w):
    o = o * lax.rsqrt(jnp.mean(o * o, axis=-1, keepdims=True) + EPS) * w
    return o * _silu(gate)


def _level_decays(g, bc):
    width = g.shape[1]
    ng = CS // SUBLANES
    shape3 = (ng, SUBLANES, width)
    bc3 = bc.reshape(shape3)
    sub = lax.broadcasted_iota(jnp.int32, shape3, 1)

    def group_row(s):
        return jnp.broadcast_to(bc3[:, s:s + 1, :], shape3)

    last = group_row(SUBLANES - 1)
    refs = [None,
            jnp.where(sub < 4, group_row(1), group_row(5)),
            group_row(3)]
    for l in range(3, N_LEVELS):
        per_block = 1 << (l - 2)
        grouped = last.reshape(ng // per_block, per_block, SUBLANES, width)
        ref = jnp.broadcast_to(grouped[:, per_block // 2 - 1:per_block // 2], grouped.shape)
        refs.append(ref.reshape(shape3))
    decays = [jnp.exp(jnp.where((sub & 1) == 1, g.reshape(shape3), 0.0))]
    decays += [jnp.exp(-jnp.abs(bc3 - ref)) for ref in refs[1:]]
    to_end = jnp.exp(jnp.broadcast_to(last[ng - 1:ng], shape3) - bc3)
    return [d.reshape(CS, width) for d in decays], to_end.reshape(CS, width)


def _gla_chunk(q, k, v, g, st_ref, tri, masks_ref, heads, dk, dv):
    bc = _dot_sel(tri, g)
    z_levels, z_end = _level_decays(g, bc)
    z_cum = jnp.exp(bc)
    st = st_ref[...]
    outs = []
    for h in range(heads):
        ks = slice(h * dk, (h + 1) * dk)
        vs = slice(h * dv, (h + 1) * dv)
        qh, kh = q[:, ks], k[:, ks]
        vh = v[:, vs].astype(BF16)
        scores = _dot_nt(qh.astype(BF16), kh.astype(BF16)) * masks_ref[N_LEVELS]
        for l in range(N_LEVELS):
            zl = z_levels[l][:, ks]
            scores = scores + _dot_nt((qh * zl).astype(BF16), (kh * zl).astype(BF16)) * masks_ref[l]
        o = _dot(scores.astype(BF16), vh)
        o = o + _dot_nt((qh * z_cum[:, ks]).astype(BF16), st[:, ks].astype(BF16))
        outs.append(o)
        upd = _dot_tn(vh, (kh * z_end[:, ks]).astype(BF16))
        st_ref[:, ks] = st[:, ks] * z_cum[CS - 1:CS, ks] + upd
    return outs


def _even_prompt_kernel(z_ref, zgr_ref, ys_ref, lbp_ref, wgk_ref, bgk_ref, gnh_ref, gng_ref, tri_ref, masks_ref,
                        y_ref, shg_ref, sgla_ref, st_hg, st_gla):
    b = pl.program_id(0)
    c = pl.program_id(1)

    @pl.when((b < BATCH) & (c == 0))
    def _():
        st_hg[...] = jnp.zeros_like(st_hg)
        st_gla[...] = jnp.zeros_like(st_gla)

    @pl.when(b < BATCH)
    def _():
        tri = tri_ref[...]
        p = lbp_ref[...]
        pe = jnp.exp(p - jnp.max(p, axis=0, keepdims=True))
        lb = pe[0:1] / jnp.sum(pe, axis=0, keepdims=True)

        z = z_ref[...]
        hq, hf, hi, hg = z[:, 0:512], z[:, 512:1024], z[:, 1024:1536], z[:, 1536:2048]
        gq, gk, gv, gg = z[:, 2048:2304], z[:, 2304:2560], z[:, 2560:3072], z[:, 3072:3584]
        f = lb + (1.0 - lb) * _sigmoid(hf)
        k_hg = (1.0 - lb) * _sigmoid(-hf)
        o_hg = _gla_chunk(_silu(hq), k_hg, hi, jnp.log(f), st_hg, tri, masks_ref, HG_H, HG_DK, HG_DV)

        la = _log_sigmoid(_dot(zgr_ref[...].astype(BF16), wgk_ref[...]) + bgk_ref[...]) / GLA_GATE_NORM
        o_gla = _gla_chunk(gq * GLA_DK ** -0.5, gk, gv, la, st_gla, tri, masks_ref, GLA_H, GLA_DK, GLA_DV)

        for h in range(HG_H):
            cs = slice(h * 128, (h + 1) * 128)
            y_ref[:, cs] = _rms_gate(o_hg[h], hg[:, cs], gnh_ref[...]).astype(BF16)
        for h in range(GLA_H):
            cs = slice(h * 128, (h + 1) * 128)
            y_ref[:, 512 + h * 128:512 + (h + 1) * 128] = _rms_gate(o_gla[h], gg[:, cs], gng_ref[...]).astype(BF16)

    @pl.when((b < BATCH) & (c == NCHUNK - 1))
    def _():
        shg_ref[0] = st_hg[...].T
        sgla_ref[0] = st_gla[...].T

    @pl.when((b == BATCH) & (c == 0))
    def _():
        y_ref[...] = ys_ref[...]


def _chunk_rows(b, c):
    return (jnp.minimum(b * NCHUNK + c, NP // CS), 0)


def _per_sequence(b, c):
    return (jnp.minimum(b, BATCH - 1), 0, 0)


def _even_prompt(z, zgr, y_sample, lbp, wgk, bgk, gnh, gng, tri, masks):
    const2 = lambda b, c: (0, 0)
    return pl.pallas_call(
        _even_prompt_kernel,
        grid=(BATCH + 1, NCHUNK),
        in_specs=[
            pl.BlockSpec((CS, EVEN_MAIN), _chunk_rows),
            pl.BlockSpec((CS, LANES), _chunk_rows),
            pl.BlockSpec((NS, D), const2),
            pl.BlockSpec(lbp.shape, const2),
            pl.BlockSpec(wgk.shape, const2),
            pl.BlockSpec(bgk.shape, const2),
            pl.BlockSpec(gnh.shape, const2),
            pl.BlockSpec(gng.shape, const2),
            pl.BlockSpec(tri.shape, const2),
            pl.BlockSpec(masks.shape, lambda b, c: (0, 0, 0)),
        ],
        out_specs=[
            pl.BlockSpec((CS, D), _chunk_rows),
            pl.BlockSpec((1, HG_H * HG_DK, HG_DV), _per_sequence),
            pl.BlockSpec((1, GLA_H * GLA_DK, GLA_DV), _per_sequence),
        ],
        out_shape=[
            jax.ShapeDtypeStruct((NT, D), BF16),
            jax.ShapeDtypeStruct((BATCH, HG_H * HG_DK, HG_DV), F32),
            jax.ShapeDtypeStruct((BATCH, GLA_H * GLA_DK, GLA_DV), F32),
        ],
        scratch_shapes=[pltpu.VMEM((HG_DV, HG_H * HG_DK), F32), pltpu.VMEM((GLA_DV, GLA_H * GLA_DK), F32)],
        compiler_params=_params(("arbitrary", "arbitrary")),
        name="even_prompt",
    )(z, zgr, y_sample, lbp, wgk, bgk, gnh, gng, tri, masks)


def _even_sample_kernel(zr_ref, zt_ref, grt_ref, lbpt_ref, wgkt_ref, bgkt_ref, gnh_ref, gng_ref,
                        shg_ref, sgla_ref, y_ref, shg_out, sgla_out, o_scr):
    zt = zt_ref[0]
    hq_t, hf_t = zt[0:512], zt[512:1024]
    gq_t, gk_t = zt[2048:2304], zt[2304:2560]
    pt = lbpt_ref[...]
    pe = jnp.exp(pt - jnp.max(pt, axis=1, keepdims=True))
    lb = pe[:, 0:1] / jnp.sum(pe, axis=1, keepdims=True)
    a_hg = jnp.exp(jnp.log(lb + (1.0 - lb) * _sigmoid(hf_t)))
    k_hg = (1.0 - lb) * _sigmoid(-hf_t)
    q_hg = _silu(hq_t)
    la = _log_sigmoid(_dot(wgkt_ref[...], grt_ref[0].astype(BF16)) + bgkt_ref[...]) / GLA_GATE_NORM
    a_gla = jnp.exp(la)
    q_gla = gq_t * GLA_DK ** -0.5
    zr = zr_ref[...]
    hi, hg = zr[:, 1024:1536], zr[:, 1536:2048]
    gv, gg = zr[:, 2560:3072], zr[:, 3072:3584]

    for j in range(SG):
        for h in range(HG_H):
            ks = slice(h * HG_DK, (h + 1) * HG_DK)
            s_new = a_hg[ks, j:j + 1] * shg_ref[j, h] + k_hg[ks, j:j + 1] * hi[j:j + 1, h * 128:(h + 1) * 128]
            shg_out[j, h] = s_new
            o_scr[j:j + 1, h * 128:(h + 1) * 128] = jnp.sum(q_hg[ks, j:j + 1] * s_new, axis=0, keepdims=True)
        for h in range(GLA_H):
            ks = slice(h * GLA_DK, (h + 1) * GLA_DK)
            s_new = a_gla[ks, j:j + 1] * sgla_ref[j, h] + gk_t[ks, j:j + 1] * gv[j:j + 1, h * 128:(h + 1) * 128]
            sgla_out[j, h] = s_new
            o_scr[j:j + 1, 512 + h * 128:512 + (h + 1) * 128] = jnp.sum(
                q_gla[ks, j:j + 1] * s_new, axis=0, keepdims=True)

    o = o_scr[...]
    for h in range(HG_H):
        cs = slice(h * 128, (h + 1) * 128)
        y_ref[:, cs] = _rms_gate(o[:, cs], hg[:, cs], gnh_ref[...]).astype(BF16)
    for h in range(GLA_H):
        cs = slice(512 + h * 128, 512 + (h + 1) * 128)
        y_ref[:, cs] = _rms_gate(o[:, cs], gg[:, h * 128:(h + 1) * 128], gng_ref[...]).astype(BF16)


def _even_sample(z, zt3, grt3, lbpt, wgkt, bgkt, gnh, gng, s_hg, s_gla):
    c2 = lambda g: (0, 0)
    return pl.pallas_call(
        _even_sample_kernel,
        grid=(NS // SG,),
        in_specs=[
            pl.BlockSpec((SG, EVEN_MAIN), lambda g: (NP // SG + g, 0)),
            pl.BlockSpec((1, EVEN_MAIN, SG), lambda g: (g, 0, 0)),
            pl.BlockSpec((1, LANES, SG), lambda g: (g, 0, 0)),
            pl.BlockSpec(lbpt.shape, c2),
            pl.BlockSpec(wgkt.shape, c2),
            pl.BlockSpec(bgkt.shape, c2),
            pl.BlockSpec(gnh.shape, c2),
            pl.BlockSpec(gng.shape, c2),
            pl.BlockSpec((SG, HG_H, HG_DK, HG_DV), lambda g: (g, 0, 0, 0)),
            pl.BlockSpec((SG, GLA_H, GLA_DK, GLA_DV), lambda g: (g, 0, 0, 0)),
        ],
        out_specs=[
            pl.BlockSpec((SG, D), lambda g: (g, 0)),
            pl.BlockSpec((SG, HG_H, HG_DK, HG_DV), lambda g: (g, 0, 0, 0)),
            pl.BlockSpec((SG, GLA_H, GLA_DK, GLA_DV), lambda g: (g, 0, 0, 0)),
        ],
        out_shape=[
            jax.ShapeDtypeStruct((NS, D), BF16),
            jax.ShapeDtypeStruct((NS, HG_H, HG_DK, HG_DV), F32),
            jax.ShapeDtypeStruct((NS, GLA_H, GLA_DK, GLA_DV), F32),
        ],
        scratch_shapes=[pltpu.VMEM((SG, D), F32)],
        compiler_params=_params(("parallel",)),
        name="even_sample",
    )(z, zt3, grt3, lbpt, wgkt, bgkt, gnh, gng, s_hg, s_gla)


def _out_ln_kernel(x_ref, tail_ref, y_ref, w_ref, g_ref, b_ref, o_ref):
    r = ALPHA * _token_tile(x_ref, tail_ref) + _dot(y_ref[...], w_ref[...])
    o_ref[...] = _layernorm(r, g_ref[...], b_ref[...])


def _out_ln(x, tail, y, w, g, b):
    c2 = lambda i: (0, 0)
    return pl.pallas_call(
        _out_ln_kernel,
        grid=(NT // TM,),
        in_specs=[
            pl.BlockSpec((TM, D), _main_tile),
            pl.BlockSpec((TM, D), c2),
            pl.BlockSpec((TM, D), lambda i: (i, 0)),
            pl.BlockSpec((D, D), c2),
            pl.BlockSpec((1, D), c2),
            pl.BlockSpec((1, D), c2),
        ],
        out_specs=pl.BlockSpec((TM, D), lambda i: (i, 0)),
        out_shape=jax.ShapeDtypeStruct((NT, D), F32),
        compiler_params=_params(("parallel",)),
        name="out_ln",
    )(x, tail, y, w, g, b)


FF_SPLIT = 2


def _ffn_kernel(x_ref, w1_ref, w3_ref, w2_ref, g_ref, b_ref, o_ref):
    x = x_ref[...]
    xb = x.astype(BF16)
    step = D_FF_DENSE // FF_SPLIT
    acc = ALPHA * x
    for s in range(FF_SPLIT):
        cs = slice(s * step, (s + 1) * step)
        hmid = _silu(_dot(xb, w1_ref[:, cs])) * _dot(xb, w3_ref[:, cs])
        acc = acc + _dot(hmid.astype(BF16), w2_ref[cs, :])
    o_ref[...] = _layernorm(acc, g_ref[...], b_ref[...])


def _ffn(x, w1, w3, w2, g, b):
    c2 = lambda i: (0, 0)
    one = pl.Buffered(1)
    return pl.pallas_call(
        _ffn_kernel,
        grid=(NT // TM,),
        in_specs=[
            pl.BlockSpec((TM, D), lambda i: (i, 0)),
            pl.BlockSpec((D, D_FF_DENSE), c2, pipeline_mode=one),
            pl.BlockSpec((D, D_FF_DENSE), c2, pipeline_mode=one),
            pl.BlockSpec((D_FF_DENSE, D), c2, pipeline_mode=one),
            pl.BlockSpec((1, D), c2),
            pl.BlockSpec((1, D), c2),
        ],
        out_specs=pl.BlockSpec((TM, D), lambda i: (i, 0)),
        out_shape=jax.ShapeDtypeStruct((NT, D), F32),
        compiler_params=_params(("parallel",)),
        name="ffn_dense",
    )(x, w1, w3, w2, g, b)


def _mh_norm_gate(hh, o_pre, w):
    mu = jnp.mean(hh, axis=-1, keepdims=True)
    c = hh - mu
    var = jnp.mean(c * c, axis=-1, keepdims=True)
    return _sigmoid(o_pre) * (c * lax.rsqrt(var + EPS) * w)


def _odd_prompt_kernel(z_ref, zg_ref, ys_ref, bg_ref, cw_ref, cb_ref, hnw_ref, tri_ref,
                       y_ref, c_out, n_out, m_out, conv_out,
                       c_scr, n_scr, m_scr, u_scr):
    b = pl.program_id(0)
    c = pl.program_id(1)

    @pl.when((b < BATCH) & (c == 0))
    def _():
        c_scr[...] = jnp.zeros_like(c_scr)
        n_scr[...] = jnp.zeros_like(n_scr)
        m_scr[...] = jnp.zeros_like(m_scr)
        u_scr[0:8, :] = jnp.zeros((8, D), F32)

    @pl.when((b == BATCH) & (c == 0))
    def _():
        y_ref[...] = ys_ref[...]

    @pl.when(b < BATCH)
    def _():
        row = lax.broadcasted_iota(jnp.int32, (CS, CS), 0)
        col = lax.broadcasted_iota(jnp.int32, (CS, CS), 1)
        causal = col <= row

        u_scr[8:8 + CS, :] = z_ref[:, 0:D]
        uc = cb_ref[...]
        for j in range(CONV_W):
            uc = uc + u_scr[5 + j:5 + j + CS, :] * cw_ref[j:j + 1, :]
        tail = u_scr[CS:CS + 8, :]
        u_scr[0:8, :] = tail
        act = _silu(uc)
        q = act[:, 0:512] * ML_DK ** -0.5
        k = act[:, 512:1024]
        v = z_ref[:, D:2 * D]
        o_pre = z_ref[:, 2 * D:3 * D]

        gates = zg_ref[...] + bg_ref[...]
        lf = _log_sigmoid(gates)
        bcum = _dot_sel(tri_ref[...], lf)
        bcum_t = bcum.T
        gates_t = gates.T
        m_all = m_scr[...]

        for h in range(ML_H):
            ks = slice(h * ML_DK, (h + 1) * ML_DK)
            vs = slice(h * ML_DV, (h + 1) * ML_DV)
            qh, kh = q[:, ks], k[:, ks]
            vh = v[:, vs].astype(BF16)
            b_col = bcum[:, 4 + h:5 + h]
            b_row = bcum_t[4 + h:5 + h, :]
            i_col = gates[:, h:h + 1]
            i_row = gates_t[h:h + 1, :]
            m_prev = m_all[:, h:h + 1]
            log_d = jnp.where(causal, b_col - b_row + i_row, -jnp.inf)
            log_prev = b_col + m_prev
            m_t = jnp.maximum(jnp.max(log_d, axis=-1, keepdims=True), log_prev)
            d = jnp.exp(log_d - m_t)
            w_prev = jnp.exp(log_prev - m_t)
            scores = _dot_nt(qh.astype(BF16), kh.astype(BF16)) * d
            c_h = c_scr[h]
            n_h = n_scr[h:h + 1, :]
            num = _dot(scores.astype(BF16), vh) + w_prev * _dot(qh.astype(BF16), c_h.astype(BF16))
            den = jnp.sum(scores, axis=-1, keepdims=True) + w_prev * jnp.sum(qh * n_h, axis=-1, keepdims=True)
            hh = num / jnp.maximum(jnp.abs(den), jnp.exp(-m_t))
            m_new = m_t[CS - 1:CS, :]
            b_last = b_col[CS - 1:CS, :]
            w_c = jnp.exp(b_last + m_prev - m_new)
            w_s = jnp.exp(b_last - b_col + i_col - m_new)
            kw = kh * w_s
            c_scr[h] = w_c * c_h + _dot_tn(kw.astype(BF16), vh)
            n_scr[h:h + 1, :] = w_c * n_h + jnp.sum(kw, axis=0, keepdims=True)
            m_scr[:, h:h + 1] = m_new
            y_ref[:, vs] = _mh_norm_gate(hh, o_pre[:, vs], hnw_ref[:, vs]).astype(BF16)

        @pl.when(c == NCHUNK - 1)
        def _():
            c_out[0] = c_scr[...]
            n_out[0] = n_scr[0:ML_H, :]
            m_out[0] = m_scr[...]
            conv_out[0] = tail[8 - (CONV_W - 1):8, :]


def _odd_prompt(z, zg, y_sample, bg, cw, cb, hnw, tri):
    c2 = lambda b, c: (0, 0)
    per_sequence4 = lambda b, c: (jnp.minimum(b, BATCH - 1), 0, 0, 0)
    return pl.pallas_call(
        _odd_prompt_kernel,
        grid=(BATCH + 1, NCHUNK),
        in_specs=[
            pl.BlockSpec((CS, ODD_MAIN), _chunk_rows),
            pl.BlockSpec((CS, LANES), _chunk_rows),
            pl.BlockSpec((NS, D), c2),
            pl.BlockSpec((1, LANES), c2),
            pl.BlockSpec((CONV_W, D), c2),
            pl.BlockSpec((1, D), c2),
            pl.BlockSpec((1, D), c2),
            pl.BlockSpec((CS, CS), c2),
        ],
        out_specs=[
            pl.BlockSpec((CS, D), _chunk_rows),
            pl.BlockSpec((1, ML_H, ML_DK, ML_DV), per_sequence4),
            pl.BlockSpec((1, ML_H, ML_DK), _per_sequence),
            pl.BlockSpec((1, 1, LANES), _per_sequence),
            pl.BlockSpec((1, CONV_W - 1, D), _per_sequence),
        ],
        out_shape=[
            jax.ShapeDtypeStruct((NT, D), BF16),
            jax.ShapeDtypeStruct((BATCH, ML_H, ML_DK, ML_DV), F32),
            jax.ShapeDtypeStruct((BATCH, ML_H, ML_DK), F32),
            jax.ShapeDtypeStruct((BATCH, 1, LANES), F32),
            jax.ShapeDtypeStruct((BATCH, CONV_W - 1, D), F32),
        ],
        scratch_shapes=[
            pltpu.VMEM((ML_H, ML_DK, ML_DV), F32),
            pltpu.VMEM((8, ML_DK), F32),
            pltpu.VMEM((1, LANES), F32),
            pltpu.VMEM((CS + 8, D), F32),
        ],
        compiler_params=_params(("arbitrary", "arbitrary")),
        name="odd_prompt",
    )(z, zg, y_sample, bg, cw, cb, hnw, tri)


def _odd_sample_kernel(zr_ref, zg_ref, ut_ref, conv_ref, convt_ref, bg_ref, cw_ref, cwt_ref, cb_ref, cbt_ref,
                       hnw_ref, c_ref, n_ref, m_ref,
                       y_ref, c_out, n_out, m_out, conv_out, h_scr):
    zr = zr_ref[...]
    u = zr[:, 0:D]
    v = zr[:, D:2 * D]
    o_pre = zr[:, 2 * D:3 * D]
    uc = cb_ref[...] + u * cw_ref[CONV_W - 1:CONV_W, :]
    uc_t = cbt_ref[...] + ut_ref[0] * cwt_ref[:, CONV_W - 1:CONV_W]
    for j in range(CONV_W - 1):
        uc = uc + conv_ref[:, j * D:(j + 1) * D] * cw_ref[j:j + 1, :]
        uc_t = uc_t + convt_ref[0, j] * cwt_ref[:, j:j + 1]
        conv_out[:, j * D:(j + 1) * D] = conv_ref[:, (j + 1) * D:(j + 2) * D] if j + 1 < CONV_W - 1 else u
    act = _silu(uc)
    k_row = act[:, 512:1024]
    act_t = _silu(uc_t)
    q_t = act_t[0:512] * ML_DK ** -0.5
    k_t = act_t[512:1024]
    q_row = act[:, 0:512] * ML_DK ** -0.5

    gates = zg_ref[...] + bg_ref[...]
    lf = _log_sigmoid(gates)
    m_in = m_ref[...]
    m_out[...] = m_in

    for j in range(SG):
        for h in range(ML_H):
            ks = slice(h * ML_DK, (h + 1) * ML_DK)
            vs = slice(h * ML_DV, (h + 1) * ML_DV)
            ig = gates[j:j + 1, h:h + 1]
            log_prev = lf[j:j + 1, 4 + h:5 + h] + m_in[j:j + 1, h:h + 1]
            m_t = jnp.maximum(ig, log_prev)
            d = jnp.exp(ig - m_t)
            w_prev = jnp.exp(log_prev - m_t)
            c_new = w_prev * c_ref[j, h] + (d * k_t[ks, j:j + 1]) * v[j:j + 1, vs]
            n_new = w_prev * n_ref[j, h:h + 1, :] + d * k_row[j:j + 1, ks]
            c_out[j, h] = c_new
            n_out[j, h:h + 1, :] = n_new
            m_out[j:j + 1, h:h + 1] = m_t
            num = jnp.sum(q_t[ks, j:j + 1] * c_new, axis=0, keepdims=True)
            den = jnp.sum(q_row[j:j + 1, ks] * n_new, axis=-1, keepdims=True)
            h_scr[j:j + 1, vs] = num / jnp.maximum(jnp.abs(den), jnp.exp(-m_t))

    hh = h_scr[...]
    for h in range(ML_H):
        vs = slice(h * ML_DV, (h + 1) * ML_DV)
        y_ref[:, vs] = _mh_norm_gate(hh[:, vs], o_pre[:, vs], hnw_ref[:, vs]).astype(BF16)


def _odd_sample(z, zg, ut3, conv, convt, bg, cw, cwt, cb, cbt, hnw, c_in, n_in, m_in):
    c2 = lambda g: (0, 0)
    return pl.pallas_call(
        _odd_sample_kernel,
        grid=(NS // SG,),
        in_specs=[
            pl.BlockSpec((SG, ODD_MAIN), lambda g: (NP // SG + g, 0)),
            pl.BlockSpec((SG, LANES), lambda g: (NP // SG + g, 0)),
            pl.BlockSpec((1, D, SG), lambda g: (g, 0, 0)),
            pl.BlockSpec((SG, (CONV_W - 1) * D), lambda g: (g, 0)),
            pl.BlockSpec((1, CONV_W - 1, D, SG), lambda g: (g, 0, 0, 0)),
            pl.BlockSpec((1, LANES), c2),
            pl.BlockSpec((CONV_W, D), c2),
            pl.BlockSpec((D, CONV_W), c2),
            pl.BlockSpec((1, D), c2),
            pl.BlockSpec((D, 1), c2),
            pl.BlockSpec((1, D), c2),
            pl.BlockSpec((SG, ML_H, ML_DK, ML_DV), lambda g: (g, 0, 0, 0)),
            pl.BlockSpec((SG, ML_H, ML_DK), lambda g: (g, 0, 0)),
            pl.BlockSpec((SG, LANES), lambda g: (g, 0)),
        ],
        out_specs=[
            pl.BlockSpec((SG, D), lambda g: (g, 0)),
            pl.BlockSpec((SG, ML_H, ML_DK, ML_DV), lambda g: (g, 0, 0, 0)),
            pl.BlockSpec((SG, ML_H, ML_DK), lambda g: (g, 0, 0)),
            pl.BlockSpec((SG, LANES), lambda g: (g, 0)),
            pl.BlockSpec((SG, (CONV_W - 1) * D), lambda g: (g, 0)),
        ],
        out_shape=[
            jax.ShapeDtypeStruct((NS, D), BF16),
            jax.ShapeDtypeStruct((NS, ML_H, ML_DK, ML_DV), F32),
            jax.ShapeDtypeStruct((NS, ML_H, ML_DK), F32),
            jax.ShapeDtypeStruct((NS, LANES), F32),
            jax.ShapeDtypeStruct((NS, (CONV_W - 1) * D), F32),
        ],
        scratch_shapes=[pltpu.VMEM((SG, D), F32)],
        compiler_params=_params(("parallel",)),
        name="odd_sample",
    )(z, zg, ut3, conv, convt, bg, cw, cwt, cb, cbt, hnw, c_in, n_in, m_in)


def _out_ln_router_kernel(x_ref, y_ref, w_ref, g_ref, b_ref, wr_ref, tri_ref,
                          o_ref, op_ref, meta_ref, cnt_ref, tab_ref, carry, filled):
    i = pl.program_id(0)

    @pl.when(i == 0)
    def _():
        carry[...] = jnp.zeros_like(carry)
        tab_ref[...] = jnp.zeros_like(tab_ref)
        for e in range(N_EXPERTS):
            filled[e] = 0

    r = ALPHA * x_ref[...] + _dot(y_ref[...], w_ref[...])
    x3 = _layernorm(r, g_ref[...], b_ref[...])
    o_ref[...] = x3
    op_ref[...] = pltpu.pack_elementwise([x3[:, :D // 2], x3[:, D // 2:]], packed_dtype=BF16)

    lane = lax.broadcasted_iota(jnp.int32, (TM, LANES), 1).astype(F32)
    logits = jnp.where(lane < N_EXPERTS, _dot(x3.astype(BF16), wr_ref[...]), -jnp.inf)
    m1 = jnp.max(logits, axis=-1, keepdims=True)
    i1 = jnp.min(jnp.where(logits == m1, lane, float(LANES)), axis=-1, keepdims=True)
    rest = jnp.where(lane == i1, -jnp.inf, logits)
    m2 = jnp.max(rest, axis=-1, keepdims=True)
    i2 = jnp.min(jnp.where(rest == m2, lane, float(LANES)), axis=-1, keepdims=True)
    e2 = jnp.exp(m2 - m1)
    tot = 1.0 + e2
    w1 = 1.0 / tot
    w2 = e2 / tot

    sel1 = lane == i1
    sel2 = lane == i2
    onehot = jnp.where(sel1 | sel2, 1.0, 0.0)
    in_tile = _dot(tri_ref[...], onehot.astype(BF16))
    carry[...] = carry[...] + jnp.sum(onehot, axis=0, keepdims=True)
    cnt_ref[...] = carry[...]

    meta = jnp.where(lane == 0.0, i1, 0.0)
    meta = jnp.where(lane == 1.0, i2, meta)
    meta = jnp.where(lane == 2.0, w1, meta)
    meta = jnp.where(lane == 3.0, w2, meta)
    meta_ref[...] = meta

    token = (i * TM + lax.broadcasted_iota(jnp.int32, (TM, 1), 0)).astype(F32)
    digit_hi = jnp.floor(token * (1.0 / 256.0))
    rhs = jnp.where(lane == 0.0, digit_hi, jnp.where(lane == 1.0, token - 256.0 * digit_hi, 0.0))
    rhs = jnp.where(lane == i2 + float(SUBLANES), 1.0, rhs).astype(BF16)
    place = lax.broadcasted_iota(jnp.int32, (TM, TM), 0).astype(F32)
    rank_rows = jnp.where(onehot > 0.0, in_tile, -1.0).T
    entries = []
    for e in range(N_EXPERTS):
        hit = jnp.where(rank_rows[e:e + 1, :] == place, 1.0, 0.0).astype(BF16)
        got = _dot(hit, rhs)
        entries.append(256.0 * got[:, 0:1] + got[:, 1:2]
                       + float(NT) * got[:, SUBLANES + e:SUBLANES + e + 1])
    tile_counts = jnp.sum(onehot, axis=0, keepdims=True)
    counts = [jnp.sum(tile_counts[:, e:e + 1]).astype(jnp.int32) for e in range(N_EXPERTS)]
    for e in range(N_EXPERTS):
        start = filled[e]
        filled[e] = start + counts[e]
        tab_ref[pl.ds(start, LANES), e:e + 1] = entries[e][0:LANES]
        for part in range(1, TM // LANES):
            @pl.when(counts[e] > part * LANES)
            def _():
                tab_ref[pl.ds(start + part * LANES, LANES), e:e + 1] = entries[e][part * LANES:(part + 1) * LANES]


def _out_ln_router(x, y, w, g, b, wr, tri):
    c2 = lambda i: (0, 0)
    return pl.pallas_call(
        _out_ln_router_kernel,
        grid=(NT // TM,),
        in_specs=[
            pl.BlockSpec((TM, D), lambda i: (i, 0)),
            pl.BlockSpec((TM, D), lambda i: (i, 0)),
            pl.BlockSpec((D, D), c2),
            pl.BlockSpec((1, D), c2),
            pl.BlockSpec((1, D), c2),
            pl.BlockSpec((D, LANES), c2),
            pl.BlockSpec((TM, TM), c2),
        ],
        out_specs=[
            pl.BlockSpec((TM, D), lambda i: (i, 0)),
            pl.BlockSpec((TM, D // 2), lambda i: (i, 0)),
            pl.BlockSpec((TM, LANES), lambda i: (i, 0)),
            pl.BlockSpec((1, LANES), c2),
            pl.BlockSpec((MOE_CAP + TM, LANES), c2),
        ],
        out_shape=[
            jax.ShapeDtypeStruct((NT, D), F32),
            jax.ShapeDtypeStruct((NT, D // 2), U32),
            jax.ShapeDtypeStruct((NT, LANES), F32),
            jax.ShapeDtypeStruct((1, LANES), F32),
            jax.ShapeDtypeStruct((MOE_CAP + TM, LANES), F32),
        ],
        scratch_shapes=[pltpu.VMEM((1, LANES), F32), pltpu.SMEM((N_EXPERTS,), jnp.int32)],
        compiler_params=_params(("arbitrary",)),
        name="out_ln_router",
    )(x, y, w, g, b, wr, tri)


def _moe_ffn_kernel(te_ref, nu_ref, tb_ref, gnext_ref, gcur_ref, sprev_ref, scur_ref, xp_ref,
                    w1_ref, w3_ref, w2_ref, out_hbm, stage, yacc, xb_scr, sem_s):
    del tb_ref
    i = pl.program_id(0)
    j = pl.program_id(1)
    used = i < nu_ref[0]
    slot = i % 2
    other = 1 - slot
    rps = MOE_ROWS_PER_STEP

    def gather_rows(tab_ref, buf, part):
        for r in range(rps):
            stage[buf, part, pl.ds(r, 1), :] = xp_ref[pl.ds(tab_ref[part * rps + r], 1), :]

    def scatter(buf, r, dst):
        return pltpu.make_async_copy(yacc.at[buf, pl.ds(r, 1)], out_hbm.at[pl.ds(dst, 1)], sem_s)

    def wait_scatters(n):
        for _ in range(n):
            scatter(0, 0, 0).wait()

    def issue_neighbours():
        gather_rows(gnext_ref, other, j)
        for r in range(rps):
            rr = j * rps + r
            scatter(other, rr, sprev_ref[rr]).start()

    @pl.when(j == 0)
    def _():
        @pl.when(i == 0)
        def _():
            yacc[1] = jnp.zeros((TMM, D), F32)
            for part in range(MOE_NFF):
                gather_rows(gcur_ref, 0, part)

        @pl.when(i > 0)
        def _():
            wait_scatters(TMM)

        words = stage[slot].reshape(TMM, D // 2)
        for half in range(2):
            xb_scr[:, half * (D // 2):(half + 1) * (D // 2)] = pltpu.unpack_elementwise(
                words, index=half, packed_dtype=BF16, unpacked_dtype=F32).astype(BF16)
        yacc[slot] = jnp.zeros((TMM, D), F32)

    @pl.when(used)
    def _():
        issue_neighbours()
        xb = xb_scr[...]
        hmid = _silu(_dot(xb, w1_ref[...])) * _dot(xb, w3_ref[...])
        yacc[slot] += _dot(hmid.astype(BF16), w2_ref[...])

    @pl.when(jnp.logical_not(used))
    def _():
        issue_neighbours()

    @pl.when((i == N_MOE_TILES - 1) & (j == MOE_NFF - 1))
    def _():
        for r in range(TMM):
            scatter(slot, r, scur_ref[r]).start()
        wait_scatters(2 * TMM)


def _moe_ffn(tile_expert, n_used, tab_block, gsrc, sdst, xp, w1, w3, w2):
    nff = MOE_NFF

    def wcol(i, j, te, nu, tb):
        return (te[i], 0, jnp.where(i < nu[0], j, nff - 1))

    def wrow(i, j, te, nu, tb):
        return (te[i], jnp.where(i < nu[0], j, nff - 1), 0)

    smem = functools.partial(pl.BlockSpec, (MOE_TAB,), memory_space=pltpu.SMEM)
    grid_spec = pltpu.PrefetchScalarGridSpec(
        num_scalar_prefetch=3,
        grid=(N_MOE_TILES, nff),
        in_specs=[
            smem(lambda i, j, te, nu, tb: (tb[i + 2],)),
            smem(lambda i, j, te, nu, tb: (tb[i + 1],)),
            smem(lambda i, j, te, nu, tb: (tb[i],)),
            smem(lambda i, j, te, nu, tb: (tb[i + 1],)),
            pl.BlockSpec((NT, D // 2), lambda i, j, te, nu, tb: (0, 0), pipeline_mode=pl.Buffered(1)),
            pl.BlockSpec((None, D, TFF), wcol),
            pl.BlockSpec((None, D, TFF), wcol),
            pl.BlockSpec((None, TFF, D), wrow),
        ],
        out_specs=pl.BlockSpec(memory_space=pl.ANY),
        scratch_shapes=[
            pltpu.VMEM((2, MOE_NFF, MOE_ROWS_PER_STEP, D // 2), U32),
            pltpu.VMEM((2, TMM, D), F32),
            pltpu.VMEM((TMM, D), BF16),
            pltpu.SemaphoreType.DMA(()),
        ],
    )
    return pl.pallas_call(
        _moe_ffn_kernel,
        grid_spec=grid_spec,
        out_shape=jax.ShapeDtypeStruct((MOE_OUT_ROWS, D), F32),
        compiler_params=_params(("arbitrary", "arbitrary")),
        name="moe_ffn",
    )(tile_expert, n_used, tab_block, gsrc, gsrc, sdst, sdst, xp, w1, w3, w2)


def _combine_kernel(x_ref, meta_ref, y0_ref, y1_ref, g_ref, b_ref, o_ref):
    meta = meta_ref[...]
    moe = meta[:, 2:3] * y0_ref[...] + meta[:, 3:4] * y1_ref[...]
    o_ref[...] = _layernorm(ALPHA * x_ref[...] + moe, g_ref[...], b_ref[...])


def _combine(x, meta, ys, g, b):
    c2 = lambda i: (0, 0)
    return pl.pallas_call(
        _combine_kernel,
        grid=(NT // TM,),
        in_specs=[
            pl.BlockSpec((TM, D), lambda i: (i, 0)),
            pl.BlockSpec((TM, LANES), lambda i: (i, 0)),
            pl.BlockSpec((TM, D), lambda i: (i, 0)),
            pl.BlockSpec((TM, D), lambda i: (i + NT // TM, 0)),
            pl.BlockSpec((1, D), c2),
            pl.BlockSpec((1, D), c2),
        ],
        out_specs=pl.BlockSpec((TM, D), lambda i: (i, 0)),
        out_shape=jax.ShapeDtypeStruct((NT, D), F32),
        compiler_params=_params(("parallel",)),
        name="moe_combine",
    )(x, meta, ys, ys, g, b)


def _pad_cols(w, n):
    return jnp.pad(w, ((0, 0), (0, n - w.shape[1])))


def kernel(x_prompt, x_sample, state_hgrn, state_gla, state_mlstm_C, state_mlstm_n, state_mlstm_m,
           state_mlstm_conv, w_in_even, hg_lower_bounds, w_gk, b_gk, gn_hg, gn_gla, w_out_even,
           w1_dense, w3_dense, w2_dense, w_in_odd, b_gate_odd, conv_w, conv_b, hn_w, w_out_odd,
           w_router, w1_moe, w3_moe, w2_moe, ln1_g, ln1_b, ln2_g, ln2_b):
    assert x_prompt.shape == (BATCH, SEQ, D) and x_sample.shape == (NS, 1, D)
    assert w_in_even.shape[0] == 1 and w_in_odd.shape[0] == 1 and hg_lower_bounds.shape[0] == 2
    masks = jnp.asarray(_gla_masks(), F32)
    tri_cs = jnp.asarray(_tri(CS, False), BF16)
    tri_tm = jnp.asarray(_tri(TM, True), BF16)
    row = lambda a: a.reshape(1, -1)

    x0 = x_prompt.reshape(NP, D)
    x0_tail = jnp.concatenate([x0[LAST_TILE * TM:], x_sample.reshape(NS, D)], axis=0)

    w_even = w_in_even[0].astype(BF16)
    z, zgr = _proj(x0, x0_tail, w_even[:, :EVEN_MAIN], _pad_cols(w_even[:, EVEN_MAIN:], LANES))
    wgk = jnp.pad(w_gk[0].astype(BF16), ((0, LANES - GLA_RANK), (0, 0)))
    lbp = hg_lower_bounds
    zs = z[NP:].reshape(NS // SG, SG, EVEN_MAIN).transpose(0, 2, 1)
    grs = zgr[NP:].reshape(NS // SG, SG, LANES).transpose(0, 2, 1)
    y_s, hg_s, gla_s = _even_sample(z, zs, grs, lbp.T, wgk.T, b_gk[0].reshape(-1, 1),
                                    row(gn_hg[0]), row(gn_gla[0]), state_hgrn[0], state_gla[0])
    y, hg_p, gla_p = _even_prompt(z, zgr, y_s, lbp, wgk, row(b_gk[0]), row(gn_hg[0]), row(gn_gla[0]), tri_cs, masks)
    x1 = _out_ln(x0, x0_tail, y, w_out_even[0].astype(BF16), row(ln1_g[0]), row(ln1_b[0]))
    x2 = _ffn(x1, w1_dense[0].astype(BF16), w3_dense[0].astype(BF16), w2_dense[0].astype(BF16),
              row(ln2_g[0]), row(ln2_b[0]))

    w_odd = w_in_odd[0].astype(BF16)
    zo, zog = _proj(x2, x2[LAST_TILE * TM:], w_odd[:, :ODD_MAIN], _pad_cols(w_odd[:, ODD_MAIN:], LANES))
    bg = jnp.pad(b_gate_odd[0], (0, LANES - 2 * ML_H)).reshape(1, LANES)
    ut = zo[NP:, :D].reshape(NS // SG, SG, D).transpose(0, 2, 1)
    conv_in = state_mlstm_conv[0]
    conv_t = conv_in.reshape(NS // SG, SG, CONV_W - 1, D).transpose(0, 2, 3, 1)
    m_in = jnp.pad(state_mlstm_m[0], ((0, 0), (0, LANES - ML_H)))
    yo_s, c_s, n_s, m_s, conv_s = _odd_sample(
        zo, zog, ut, conv_in.reshape(NS, (CONV_W - 1) * D), conv_t, bg, conv_w[0], conv_w[0].T, row(conv_b[0]), conv_b[0].reshape(-1, 1),
        row(hn_w[0]), state_mlstm_C[0], state_mlstm_n[0], m_in)
    yo, c_p, n_p, m_p, conv_p = _odd_prompt(zo, zog, yo_s, bg, conv_w[0], row(conv_b[0]), row(hn_w[0]), tri_cs)

    wr = _pad_cols(w_router[0].astype(BF16), LANES)
    x3, x3p, meta, cnt, tab = _out_ln_router(
        x2, yo, w_out_odd[0].astype(BF16), row(ln1_g[1]), row(ln1_b[1]), wr, tri_tm)

    counts = cnt[0, :N_EXPERTS].astype(jnp.int32)
    padded = ((counts + TMM - 1) // TMM) * TMM
    ends = jnp.cumsum(padded)
    offsets = ends - padded
    tile = jnp.arange(N_MOE_TILES, dtype=jnp.int32)
    tile_expert = jnp.minimum(jnp.sum((tile * TMM)[:, None] >= ends[None, :], axis=1), N_EXPERTS - 1).astype(jnp.int32)
    n_tiles_used = ends[-1] // TMM
    in_use = tile < n_tiles_used
    local_tile = (tile * TMM - offsets[tile_expert]) // TMM
    blocks_per_expert = MOE_CAP // MOE_TAB
    spill_block = N_EXPERTS * blocks_per_expert
    n_spare = N_MOE_TILES - (2 * NT) // TMM
    entry = tab[:MOE_CAP, :N_EXPERTS].T.astype(jnp.int32)
    local = jnp.arange(MOE_CAP, dtype=jnp.int32)[None, :]
    valid = local < counts[:, None]
    padding = padded - counts
    pad_before = (jnp.cumsum(padding) - padding)[:, None]
    row_in_block = jnp.arange(MOE_TAB, dtype=jnp.int32)
    spare_rows = (2 * NT + TMM + jnp.sum(padding)
                  + jnp.arange(n_spare, dtype=jnp.int32)[:, None] * TMM + row_in_block[None, :])
    sdst = jnp.concatenate([
        jnp.where(valid, entry, 2 * NT + TMM + pad_before + local - counts[:, None]).reshape(-1),
        2 * NT + row_in_block, spare_rows.reshape(-1)])
    gsrc = jnp.concatenate([
        jnp.where(valid, entry - jnp.where(entry >= NT, NT, 0), 0).reshape(-1),
        jnp.zeros(((1 + n_spare) * MOE_TAB,), jnp.int32)])
    own_block = jnp.where(in_use, tile_expert * blocks_per_expert + local_tile, spill_block + 1 + tile - n_tiles_used)
    tab_block = jnp.concatenate([jnp.full((1,), spill_block, jnp.int32), own_block.astype(jnp.int32),
                                 jnp.full((1,), spill_block, jnp.int32)])

    ys = _moe_ffn(tile_expert, n_tiles_used.reshape(1), tab_block, gsrc, sdst, x3p,
                  w1_moe[0].astype(BF16), w3_moe[0].astype(BF16), w2_moe[0].astype(BF16))
    out = _combine(x3, meta, ys, row(ln2_g[1]), row(ln2_b[1]))

    y_prompt = out[:NP].reshape(BATCH, SEQ, D)
    y_sample = out[NP:].reshape(NS, 1, D)
    return (y_prompt, y_sample,
            hg_p.reshape(1, BATCH, HG_H, HG_DK, HG_DV), gla_p.reshape(1, BATCH, GLA_H, GLA_DK, GLA_DV),
            c_p[None], n_p[None], m_p[:, 0, :ML_H][None], conv_p[None],
            hg_s[None], gla_s[None], c_s[None], n_s[None], m_s[:, :ML_H][None], conv_s.reshape(1, NS, CONV_W - 1, D))
```

```python
import functools
import math

import jax
import jax.numpy as jnp
import numpy as np
from jax import lax
from jax.experimental import pallas as pl
from jax.experimental.pallas import tpu as pltpu

F32 = jnp.float32
BF16 = jnp.bfloat16
U32 = jnp.uint32

D = 1024
BATCH = 8
SEQ = 2048
DEC_BATCH = 128
NP = BATCH * SEQ
NS = DEC_BATCH
NT = NP + NS
HG_H, HG_DK, HG_DV = 4, 128, 128
GLA_H, GLA_DK, GLA_DV = 4, 64, 128
GLA_RANK = 16
GLA_GATE_NORM = 16.0
ML_H, ML_DK, ML_DV = 4, 128, 256
CONV_W = 4
D_FF_DENSE = 2816
D_FF_EXPERT = 3584
N_EXPERTS = 8
EPS = 1e-5
DEPTH = 2
ALPHA = (2.0 * DEPTH) ** 0.25
EVEN_MAIN = 3584
ODD_MAIN = 3072

LANES = 128
SUBLANES = 8
VMEM_LIMIT = 56 * 1024 * 1024

TM = 384
LAST_TILE = NT // TM - 1
TC = 128
CS = 128
NCHUNK = SEQ // CS
SG = 16
TMM = 512
TFF = 896
MOE_TAB = 512
MOE_CAP = -(-NT // MOE_TAB) * MOE_TAB
MOE_NFF = D_FF_EXPERT // TFF
MOE_ROWS_PER_STEP = TMM // MOE_NFF
N_MOE_TILES = -(-(2 * NT + N_EXPERTS * (TMM - 1)) // TMM)
MOE_SLOTS = N_MOE_TILES * TMM
MOE_OUT_ROWS = MOE_SLOTS + TMM
N_LEVELS = int(math.log2(CS))

assert NT % TM == 0 and NP % CS == 0 and NS % SG == 0 and D_FF_EXPERT % TFF == 0 and TMM % MOE_NFF == 0
assert TMM == MOE_TAB and NS == CS and LAST_TILE * TM <= NP


def _params(sem, limit=VMEM_LIMIT):
    return pltpu.CompilerParams(dimension_semantics=sem, vmem_limit_bytes=limit)


def _dot(a, b):
    return jnp.dot(a, b, preferred_element_type=F32)


def _dot_nt(a, b):
    return lax.dot_general(a, b, (((1,), (1,)), ((), ())), preferred_element_type=F32)


def _dot_tn(a, b):
    return lax.dot_general(a, b, (((0,), (0,)), ((), ())), preferred_element_type=F32)


def _split3(x):
    hi = x.astype(BF16)
    r1 = x - hi.astype(F32)
    mid = r1.astype(BF16)
    lo = (r1 - mid.astype(F32)).astype(BF16)
    return hi, mid, lo


def _dot_sel(sel, x):
    hi, mid, lo = _split3(x)
    return _dot(sel, hi) + _dot(sel, mid) + _dot(sel, lo)


def _sigmoid(x):
    return jax.nn.sigmoid(x)


def _silu(x):
    return x * jax.nn.sigmoid(x)


def _log_sigmoid(x):
    return jnp.minimum(x, 0.0) - jnp.log(1.0 + jnp.exp(-jnp.abs(x)))


def _layernorm(r, g, b):
    mu = jnp.mean(r, axis=-1, keepdims=True)
    c = r - mu
    var = jnp.mean(c * c, axis=-1, keepdims=True)
    return c * lax.rsqrt(var + EPS) * g + b


def _gla_masks():
    masks = np.zeros((N_LEVELS + 1, CS, CS), np.float32)
    for t in range(CS):
        for l in range(N_LEVELS):
            half = 1 << l
            start = (t // (2 * half)) * (2 * half)
            mid = start + half
            if t >= mid:
                masks[l, t, start:mid] = 1.0
        masks[N_LEVELS, t, t] = 1.0
    return masks


def _tri(n, strict):
    return np.tril(np.ones((n, n), np.float32), -1 if strict else 0)


def _token_tile(x_ref, tail_ref):
    return jnp.where(pl.program_id(0) == LAST_TILE, tail_ref[...], x_ref[...])


def _main_tile(i):
    return (jnp.minimum(i, LAST_TILE - 1), 0)


def _proj_kernel(x_ref, tail_ref, wa_ref, wb_ref, oa_ref, ob_ref):
    xb = _token_tile(x_ref, tail_ref).astype(BF16)
    oa_ref[...] = _dot(xb, wa_ref[...])
    ob_ref[...] = _dot(xb, wb_ref[...])


def _proj(x, tail, wa, wb):
    na, nb = wa.shape[1], wb.shape[1]
    return pl.pallas_call(
        _proj_kernel,
        grid=(NT // TM,),
        in_specs=[
            pl.BlockSpec((TM, D), _main_tile),
            pl.BlockSpec((TM, D), lambda i: (0, 0)),
            pl.BlockSpec((D, na), lambda i: (0, 0)),
            pl.BlockSpec((D, nb), lambda i: (0, 0)),
        ],
        out_specs=[
            pl.BlockSpec((TM, na), lambda i: (i, 0)),
            pl.BlockSpec((TM, nb), lambda i: (i, 0)),
        ],
        out_shape=[jax.ShapeDtypeStruct((NT, na), F32), jax.ShapeDtypeStruct((NT, nb), F32)],
        compiler_params=_params(("parallel",)),
        name="proj",
    )(x, tail, wa, wb)


def _rms_gate(o, gate, w):
    o = o * lax.rsqrt(jnp.mean(o * o, axis=-1, keepdims=True) + EPS) * w
    return o * _silu(gate)


def _level_decays(g, bc):
    width = g.shape[1]
    ng = CS // SUBLANES
    shape3 = (ng, SUBLANES, width)
    bc3 = bc.reshape(shape3)
    sub = lax.broadcasted_iota(jnp.int32, shape3, 1)

    def group_row(s):
        return jnp.broadcast_to(bc3[:, s:s + 1, :], shape3)

    last = group_row(SUBLANES - 1)
    refs = [None,
            jnp.where(sub < 4, group_row(1), group_row(5)),
            group_row(3)]
    for l in range(3, N_LEVELS):
        per_block = 1 << (l - 2)
        grouped = last.reshape(ng // per_block, per_block, SUBLANES, width)
        ref = jnp.broadcast_to(grouped[:, per_block // 2 - 1:per_block // 2], grouped.shape)
        refs.append(ref.reshape(shape3))
    decays = [jnp.exp(jnp.where((sub & 1) == 1, g.reshape(shape3), 0.0))]
    decays += [jnp.exp(-jnp.abs(bc3 - ref)) for ref in refs[1:]]
    to_end = jnp.exp(jnp.broadcast_to(last[ng - 1:ng], shape3) - bc3)
    return [d.reshape(CS, width) for d in decays], to_end.reshape(CS, width)


def _gla_chunk(q, k, v, g, st_ref, tri, masks_ref, heads, dk, dv):
    bc = _dot_sel(tri, g)
    z_levels, z_end = _level_decays(g, bc)
    z_cum = jnp.exp(bc)
    st = st_ref[...]
    outs = []
    for h in range(heads):
        ks = slice(h * dk, (h + 1) * dk)
        vs = slice(h * dv, (h + 1) * dv)
        qh, kh = q[:, ks], k[:, ks]
        vh = v[:, vs].astype(BF16)
        scores = _dot_nt(qh.astype(BF16), kh.astype(BF16)) * masks_ref[N_LEVELS]
        for l in range(N_LEVELS):
            zl = z_levels[l][:, ks]
            scores = scores + _dot_nt((qh * zl).astype(BF16), (kh * zl).astype(BF16)) * masks_ref[l]
        o = _dot(scores.astype(BF16), vh)
        o = o + _dot_nt((qh * z_cum[:, ks]).astype(BF16), st[:, ks].astype(BF16))
        outs.append(o)
        upd = _dot_tn(vh, (kh * z_end[:, ks]).astype(BF16))
        st_ref[:, ks] = st[:, ks] * z_cum[CS - 1:CS, ks] + upd
    return outs


def _even_prompt_kernel(z_ref, zgr_ref, ys_ref, lbp_ref, wgk_ref, bgk_ref, gnh_ref, gng_ref, tri_ref, masks_ref,
                        y_ref, shg_ref, sgla_ref, st_hg, st_gla):
    b = pl.program_id(0)
    c = pl.program_id(1)

    @pl.when((b < BATCH) & (c == 0))
    def _():
        st_hg[...] = jnp.zeros_like(st_hg)
        st_gla[...] = jnp.zeros_like(st_gla)

    @pl.when(b < BATCH)
    def _():
        tri = tri_ref[...]
        p = lbp_ref[...]
        pe = jnp.exp(p - jnp.max(p, axis=0, keepdims=True))
        lb = pe[0:1] / jnp.sum(pe, axis=0, keepdims=True)

        z = z_ref[...]
        hq, hf, hi, hg = z[:, 0:512], z[:, 512:1024], z[:, 1024:1536], z[:, 1536:2048]
        gq, gk, gv, gg = z[:, 2048:2304], z[:, 2304:2560], z[:, 2560:3072], z[:, 3072:3584]
        f = lb + (1.0 - lb) * _sigmoid(hf)
        k_hg = (1.0 - lb) * _sigmoid(-hf)
        o_hg = _gla_chunk(_silu(hq), k_hg, hi, jnp.log(f), st_hg, tri, masks_ref, HG_H, HG_DK, HG_DV)

        la = _log_sigmoid(_dot(zgr_ref[...].astype(BF16), wgk_ref[...]) + bgk_ref[...]) / GLA_GATE_NORM
        o_gla = _gla_chunk(gq * GLA_DK ** -0.5, gk, gv, la, st_gla, tri, masks_ref, GLA_H, GLA_DK, GLA_DV)

        for h in range(HG_H):
            cs = slice(h * 128, (h + 1) * 128)
            y_ref[:, cs] = _rms_gate(o_hg[h], hg[:, cs], gnh_ref[...]).astype(BF16)
        for h in range(GLA_H):
            cs = slice(h * 128, (h + 1) * 128)
            y_ref[:, 512 + h * 128:512 + (h + 1) * 128] = _rms_gate(o_gla[h], gg[:, cs], gng_ref[...]).astype(BF16)

    @pl.when((b < BATCH) & (c == NCHUNK - 1))
    def _():
        shg_ref[0] = st_hg[...].T
        sgla_ref[0] = st_gla[...].T

    @pl.when((b == BATCH) & (c == 0))
    def _():
        y_ref[...] = ys_ref[...]


def _chunk_rows(b, c):
    return (jnp.minimum(b * NCHUNK + c, NP // CS), 0)


def _per_sequence(b, c):
    return (jnp.minimum(b, BATCH - 1), 0, 0)


def _even_prompt(z, zgr, y_sample, lbp, wgk, bgk, gnh, gng, tri, masks):
    const2 = lambda b, c: (0, 0)
    return pl.pallas_call(
        _even_prompt_kernel,
        grid=(BATCH + 1, NCHUNK),
        in_specs=[
            pl.BlockSpec((CS, EVEN_MAIN), _chunk_rows),
            pl.BlockSpec((CS, LANES), _chunk_rows),
            pl.BlockSpec((NS, D), const2),
            pl.BlockSpec(lbp.shape, const2),
            pl.BlockSpec(wgk.shape, const2),
            pl.BlockSpec(bgk.shape, const2),
            pl.BlockSpec(gnh.shape, const2),
            pl.BlockSpec(gng.shape, const2),
            pl.BlockSpec(tri.shape, const2),
            pl.BlockSpec(masks.shape, lambda b, c: (0, 0, 0)),
        ],
        out_specs=[
            pl.BlockSpec((CS, D), _chunk_rows),
            pl.BlockSpec((1, HG_H * HG_DK, HG_DV), _per_sequence),
            pl.BlockSpec((1, GLA_H * GLA_DK, GLA_DV), _per_sequence),
        ],
        out_shape=[
            jax.ShapeDtypeStruct((NT, D), BF16),
            jax.ShapeDtypeStruct((BATCH, HG_H * HG_DK, HG_DV), F32),
            jax.ShapeDtypeStruct((BATCH, GLA_H * GLA_DK, GLA_DV), F32),
        ],
        scratch_shapes=[pltpu.VMEM((HG_DV, HG_H * HG_DK), F32), pltpu.VMEM((GLA_DV, GLA_H * GLA_DK), F32)],
        compiler_params=_params(("arbitrary", "arbitrary")),
        name="even_prompt",
    )(z, zgr, y_sample, lbp, wgk, bgk, gnh, gng, tri, masks)


def _even_sample_kernel(zr_ref, zt_ref, grt_ref, lbpt_ref, wgkt_ref, bgkt_ref, gnh_ref, gng_ref,
                        shg_ref, sgla_ref, y_ref, shg_out, sgla_out, o_scr):
    zt = zt_ref[0]
    hq_t, hf_t = zt[0:512], zt[512:1024]
    gq_t, gk_t = zt[2048:2304], zt[2304:2560]
    pt = lbpt_ref[...]
    pe = jnp.exp(pt - jnp.max(pt, axis=1, keepdims=True))
    lb = pe[:, 0:1] / jnp.sum(pe, axis=1, keepdims=True)
    a_hg = jnp.exp(jnp.log(lb + (1.0 - lb) * _sigmoid(hf_t)))
    k_hg = (1.0 - lb) * _sigmoid(-hf_t)
    q_hg = _silu(hq_t)
    la = _log_sigmoid(_dot(wgkt_ref[...], grt_ref[0].astype(BF16)) + bgkt_ref[...]) / GLA_GATE_NORM
    a_gla = jnp.exp(la)
    q_gla = gq_t * GLA_DK ** -0.5
    zr = zr_ref[...]
    hi, hg = zr[:, 1024:1536], zr[:, 1536:2048]
    gv, gg = zr[:, 2560:3072], zr[:, 3072:3584]

    for j in range(SG):
        for h in range(HG_H):
            ks = slice(h * HG_DK, (h + 1) * HG_DK)
            s_new = a_hg[ks, j:j + 1] * shg_ref[j, h] + k_hg[ks, j:j + 1] * hi[j:j + 1, h * 128:(h + 1) * 128]
            shg_out[j, h] = s_new
            o_scr[j:j + 1, h * 128:(h + 1) * 128] = jnp.sum(q_hg[ks, j:j + 1] * s_new, axis=0, keepdims=True)
        for h in range(GLA_H):
            ks = slice(h * GLA_DK, (h + 1) * GLA_DK)
            s_new = a_gla[ks, j:j + 1] * sgla_ref[j, h] + gk_t[ks, j:j + 1] * gv[j:j + 1, h * 128:(h + 1) * 128]
            sgla_out[j, h] = s_new
            o_scr[j:j + 1, 512 + h * 128:512 + (h + 1) * 128] = jnp.sum(
                q_gla[ks, j:j + 1] * s_new, axis=0, keepdims=True)

    o = o_scr[...]
    for h in range(HG_H):
        cs = slice(h * 128, (h + 1) * 128)
        y_ref[:, cs] = _rms_gate(o[:, cs], hg[:, cs], gnh_ref[...]).astype(BF16)
    for h in range(GLA_H):
        cs = slice(512 + h * 128, 512 + (h + 1) * 128)
        y_ref[:, cs] = _rms_gate(o[:, cs], gg[:, h * 128:(h + 1) * 128], gng_ref[...]).astype(BF16)


def _even_sample(z, zt3, grt3, lbpt, wgkt, bgkt, gnh, gng, s_hg, s_gla):
    c2 = lambda g: (0, 0)
    return pl.pallas_call(
        _even_sample_kernel,
        grid=(NS // SG,),
        in_specs=[
            pl.BlockSpec((SG, EVEN_MAIN), lambda g: (NP // SG + g, 0)),
            pl.BlockSpec((1, EVEN_MAIN, SG), lambda g: (g, 0, 0)),
            pl.BlockSpec((1, LANES, SG), lambda g: (g, 0, 0)),
            pl.BlockSpec(lbpt.shape, c2),
            pl.BlockSpec(wgkt.shape, c2),
            pl.BlockSpec(bgkt.shape, c2),
            pl.BlockSpec(gnh.shape, c2),
            pl.BlockSpec(gng.shape, c2),
            pl.BlockSpec((SG, HG_H, HG_DK, HG_DV), lambda g: (g, 0, 0, 0)),
            pl.BlockSpec((SG, GLA_H, GLA_DK, GLA_DV), lambda g: (g, 0, 0, 0)),
        ],
        out_specs=[
            pl.BlockSpec((SG, D), lambda g: (g, 0)),
            pl.BlockSpec((SG, HG_H, HG_DK, HG_DV), lambda g: (g, 0, 0, 0)),
            pl.BlockSpec((SG, GLA_H, GLA_DK, GLA_DV), lambda g: (g, 0, 0, 0)),
        ],
        out_shape=[
            jax.ShapeDtypeStruct((NS, D), BF16),
            jax.ShapeDtypeStruct((NS, HG_H, HG_DK, HG_DV), F32),
            jax.ShapeDtypeStruct((NS, GLA_H, GLA_DK, GLA_DV), F32),
        ],
        scratch_shapes=[pltpu.VMEM((SG, D), F32)],
        compiler_params=_params(("parallel",)),
        name="even_sample",
    )(z, zt3, grt3, lbpt, wgkt, bgkt, gnh, gng, s_hg, s_gla)


def _out_ln_kernel(x_ref, tail_ref, y_ref, w_ref, g_ref, b_ref, o_ref):
    r = ALPHA * _token_tile(x_ref, tail_ref) + _dot(y_ref[...], w_ref[...])
    o_ref[...] = _layernorm(r, g_ref[...], b_ref[...])


def _out_ln(x, tail, y, w, g, b):
    c2 = lambda i: (0, 0)
    return pl.pallas_call(
        _out_ln_kernel,
        grid=(NT // TM,),
        in_specs=[
            pl.BlockSpec((TM, D), _main_tile),
            pl.BlockSpec((TM, D), c2),
            pl.BlockSpec((TM, D), lambda i: (i, 0)),
            pl.BlockSpec((D, D), c2),
            pl.BlockSpec((1, D), c2),
            pl.BlockSpec((1, D), c2),
        ],
        out_specs=pl.BlockSpec((TM, D), lambda i: (i, 0)),
        out_shape=jax.ShapeDtypeStruct((NT, D), F32),
        compiler_params=_params(("parallel",)),
        name="out_ln",
    )(x, tail, y, w, g, b)


FF_SPLIT = 2


def _ffn_kernel(x_ref, w1_ref, w3_ref, w2_ref, g_ref, b_ref, o_ref):
    x = x_ref[...]
    xb = x.astype(BF16)
    step = D_FF_DENSE // FF_SPLIT
    acc = ALPHA * x
    for s in range(FF_SPLIT):
        cs = slice(s * step, (s + 1) * step)
        hmid = _silu(_dot(xb, w1_ref[:, cs])) * _dot(xb, w3_ref[:, cs])
        acc = acc + _dot(hmid.astype(BF16), w2_ref[cs, :])
    o_ref[...] = _layernorm(acc, g_ref[...], b_ref[...])


def _ffn(x, w1, w3, w2, g, b):
    c2 = lambda i: (0, 0)
    one = pl.Buffered(1)
    return pl.pallas_call(
        _ffn_kernel,
        grid=(NT // TM,),
        in_specs=[
            pl.BlockSpec((TM, D), lambda i: (i, 0)),
            pl.BlockSpec((D, D_FF_DENSE), c2, pipeline_mode=one),
            pl.BlockSpec((D, D_FF_DENSE), c2, pipeline_mode=one),
            pl.BlockSpec((D_FF_DENSE, D), c2, pipeline_mode=one),
            pl.BlockSpec((1, D), c2),
            pl.BlockSpec((1, D), c2),
        ],
        out_specs=pl.BlockSpec((TM, D), lambda i: (i, 0)),
        out_shape=jax.ShapeDtypeStruct((NT, D), F32),
        compiler_params=_params(("parallel",)),
        name="ffn_dense",
    )(x, w1, w3, w2, g, b)


def _mh_norm_gate(hh, o_pre, w):
    mu = jnp.mean(hh, axis=-1, keepdims=True)
    c = hh - mu
    var = jnp.mean(c * c, axis=-1, keepdims=True)
    return _sigmoid(o_pre) * (c * lax.rsqrt(var + EPS) * w)


def _odd_prompt_kernel(z_ref, zg_ref, ys_ref, bg_ref, cw_ref, cb_ref, hnw_ref, tri_ref,
                       y_ref, c_out, n_out, m_out, conv_out,
                       c_scr, n_scr, m_scr, u_scr):
    b = pl.program_id(0)
    c = pl.program_id(1)

    @pl.when((b < BATCH) & (c == 0))
    def _():
        c_scr[...] = jnp.zeros_like(c_scr)
        n_scr[...] = jnp.zeros_like(n_scr)
        m_scr[...] = jnp.zeros_like(m_scr)
        u_scr[0:8, :] = jnp.zeros((8, D), F32)

    @pl.when((b == BATCH) & (c == 0))
    def _():
        y_ref[...] = ys_ref[...]

    @pl.when(b < BATCH)
    def _():
        row = lax.broadcasted_iota(jnp.int32, (CS, CS), 0)
        col = lax.broadcasted_iota(jnp.int32, (CS, CS), 1)
        causal = col <= row

        u_scr[8:8 + CS, :] = z_ref[:, 0:D]
        uc = cb_ref[...]
        for j in range(CONV_W):
            uc = uc + u_scr[5 + j:5 + j + CS, :] * cw_ref[j:j + 1, :]
        tail = u_scr[CS:CS + 8, :]
        u_scr[0:8, :] = tail
        act = _silu(uc)
        q = act[:, 0:512] * ML_DK ** -0.5
        k = act[:, 512:1024]
        v = z_ref[:, D:2 * D]
        o_pre = z_ref[:, 2 * D:3 * D]

        gates = zg_ref[...] + bg_ref[...]
        lf = _log_sigmoid(gates)
        bcum = _dot_sel(tri_ref[...], lf)
        bcum_t = bcum.T
        gates_t = gates.T
        m_all = m_scr[...]

        for h in range(ML_H):
            ks = slice(h * ML_DK, (h + 1) * ML_DK)
            vs = slice(h * ML_DV, (h + 1) * ML_DV)
            qh, kh = q[:, ks], k[:, ks]
            vh = v[:, vs].astype(BF16)
            b_col = bcum[:, 4 + h:5 + h]
            b_row = bcum_t[4 + h:5 + h, :]
            i_col = gates[:, h:h + 1]
            i_row = gates_t[h:h + 1, :]
            m_prev = m_all[:, h:h + 1]
            log_d = jnp.where(causal, b_col - b_row + i_row, -jnp.inf)
            log_prev = b_col + m_prev
            m_t = jnp.maximum(jnp.max(log_d, axis=-1, keepdims=True), log_prev)
            d = jnp.exp(log_d - m_t)
            w_prev = jnp.exp(log_prev - m_t)
            scores = _dot_nt(qh.astype(BF16), kh.astype(BF16)) * d
            c_h = c_scr[h]
            n_h = n_scr[h:h + 1, :]
            num = _dot(scores.astype(BF16), vh) + w_prev * _dot(qh.astype(BF16), c_h.astype(BF16))
            den = jnp.sum(scores, axis=-1, keepdims=True) + w_prev * jnp.sum(qh * n_h, axis=-1, keepdims=True)
            hh = num / jnp.maximum(jnp.abs(den), jnp.exp(-m_t))
            m_new = m_t[CS - 1:CS, :]
            b_last = b_col[CS - 1:CS, :]
            w_c = jnp.exp(b_last + m_prev - m_new)
            w_s = jnp.exp(b_last - b_col + i_col - m_new)
            kw = kh * w_s
            c_scr[h] = w_c * c_h + _dot_tn(kw.astype(BF16), vh)
            n_scr[h:h + 1, :] = w_c * n_h + jnp.sum(kw, axis=0, keepdims=True)
            m_scr[:, h:h + 1] = m_new
            y_ref[:, vs] = _mh_norm_gate(hh, o_pre[:, vs], hnw_ref[:, vs]).astype(BF16)

        @pl.when(c == NCHUNK - 1)
        def _():
            c_out[0] = c_scr[...]
            n_out[0] = n_scr[0:ML_H, :]
            m_out[0] = m_scr[...]
            conv_out[0] = tail[8 - (CONV_W - 1):8, :]


def _odd_prompt(z, zg, y_sample, bg, cw, cb, hnw, tri):
    c2 = lambda b, c: (0, 0)
    per_sequence4 = lambda b, c: (jnp.minimum(b, BATCH - 1), 0, 0, 0)
    return pl.pallas_call(
        _odd_prompt_kernel,
        grid=(BATCH + 1, NCHUNK),
        in_specs=[
            pl.BlockSpec((CS, ODD_MAIN), _chunk_rows),
            pl.BlockSpec((CS, LANES), _chunk_rows),
            pl.BlockSpec((NS, D), c2),
            pl.BlockSpec((1, LANES), c2),
            pl.BlockSpec((CONV_W, D), c2),
            pl.BlockSpec((1, D), c2),
            pl.BlockSpec((1, D), c2),
            pl.BlockSpec((CS, CS), c2),
        ],
        out_specs=[
            pl.BlockSpec((CS, D), _chunk_rows),
            pl.BlockSpec((1, ML_H, ML_DK, ML_DV), per_sequence4),
            pl.BlockSpec((1, ML_H, ML_DK), _per_sequence),
            pl.BlockSpec((1, 1, LANES), _per_sequence),
            pl.BlockSpec((1, CONV_W - 1, D), _per_sequence),
        ],
        out_shape=[
            jax.ShapeDtypeStruct((NT, D), BF16),
            jax.ShapeDtypeStruct((BATCH, ML_H, ML_DK, ML_DV), F32),
            jax.ShapeDtypeStruct((BATCH, ML_H, ML_DK), F32),
            jax.ShapeDtypeStruct((BATCH, 1, LANES), F32),
            jax.ShapeDtypeStruct((BATCH, CONV_W - 1, D), F32),
        ],
        scratch_shapes=[
            pltpu.VMEM((ML_H, ML_DK, ML_DV), F32),
            pltpu.VMEM((8, ML_DK), F32),
            pltpu.VMEM((1, LANES), F32),
            pltpu.VMEM((CS + 8, D), F32),
        ],
        compiler_params=_params(("arbitrary", "arbitrary")),
        name="odd_prompt",
    )(z, zg, y_sample, bg, cw, cb, hnw, tri)


def _odd_sample_kernel(zr_ref, zg_ref, ut_ref, conv_ref, convt_ref, bg_ref, cw_ref, cwt_ref, cb_ref, cbt_ref,
                       hnw_ref, c_ref, n_ref, m_ref,
                       y_ref, c_out, n_out, m_out, conv_out, h_scr):
    zr = zr_ref[...]
    u = zr[:, 0:D]
    v = zr[:, D:2 * D]
    o_pre = zr[:, 2 * D:3 * D]
    uc = cb_ref[...] + u * cw_ref[CONV_W - 1:CONV_W, :]
    uc_t = cbt_ref[...] + ut_ref[0] * cwt_ref[:, CONV_W - 1:CONV_W]
    for j in range(CONV_W - 1):
        uc = uc + conv_ref[:, j * D:(j + 1) * D] * cw_ref[j:j + 1, :]
        uc_t = uc_t + convt_ref[0, j] * cwt_ref[:, j:j + 1]
        conv_out[:, j * D:(j + 1) * D] = conv_ref[:, (j + 1) * D:(j + 2) * D] if j + 1 < CONV_W - 1 else u
    act = _silu(uc)
    k_row = act[:, 512:1024]
    act_t = _silu(uc_t)
    q_t = act_t[0:512] * ML_DK ** -0.5
    k_t = act_t[512:1024]
    q_row = act[:, 0:512] * ML_DK ** -0.5

    gates = zg_ref[...] + bg_ref[...]
    lf = _log_sigmoid(gates)
    m_in = m_ref[...]
    m_out[...] = m_in

    for j in range(SG):
        for h in range(ML_H):
            ks = slice(h * ML_DK, (h + 1) * ML_DK)
            vs = slice(h * ML_DV, (h + 1) * ML_DV)
            ig = gates[j:j + 1, h:h + 1]
            log_prev = lf[j:j + 1, 4 + h:5 + h] + m_in[j:j + 1, h:h + 1]
            m_t = jnp.maximum(ig, log_prev)
            d = jnp.exp(ig - m_t)
            w_prev = jnp.exp(log_prev - m_t)
            c_new = w_prev * c_ref[j, h] + (d * k_t[ks, j:j + 1]) * v[j:j + 1, vs]
            n_new = w_prev * n_ref[j, h:h + 1, :] + d * k_row[j:j + 1, ks]
            c_out[j, h] = c_new
            n_out[j, h:h + 1, :] = n_new
            m_out[j:j + 1, h:h + 1] = m_t
            num = jnp.sum(q_t[ks, j:j + 1] * c_new, axis=0, keepdims=True)
            den = jnp.sum(q_row[j:j + 1, ks] * n_new, axis=-1, keepdims=True)
            h_scr[j:j + 1, vs] = num / jnp.maximum(jnp.abs(den), jnp.exp(-m_t))

    hh = h_scr[...]
    for h in range(ML_H):
        vs = slice(h * ML_DV, (h + 1) * ML_DV)
        y_ref[:, vs] = _mh_norm_gate(hh[:, vs], o_pre[:, vs], hnw_ref[:, vs]).astype(BF16)


def _odd_sample(z, zg, ut3, conv, convt, bg, cw, cwt, cb, cbt, hnw, c_in, n_in, m_in):
    c2 = lambda g: (0, 0)
    return pl.pallas_call(
        _odd_sample_kernel,
        grid=(NS // SG,),
        in_specs=[
            pl.BlockSpec((SG, ODD_MAIN), lambda g: (NP // SG + g, 0)),
            pl.BlockSpec((SG, LANES), lambda g: (NP // SG + g, 0)),
            pl.BlockSpec((1, D, SG), lambda g: (g, 0, 0)),
            pl.BlockSpec((SG, (CONV_W - 1) * D), lambda g: (g, 0)),
            pl.BlockSpec((1, CONV_W - 1, D, SG), lambda g: (g, 0, 0, 0)),
            pl.BlockSpec((1, LANES), c2),
            pl.BlockSpec((CONV_W, D), c2),
            pl.BlockSpec((D, CONV_W), c2),
            pl.BlockSpec((1, D), c2),
            pl.BlockSpec((D, 1), c2),
            pl.BlockSpec((1, D), c2),
            pl.BlockSpec((SG, ML_H, ML_DK, ML_DV), lambda g: (g, 0, 0, 0)),
            pl.BlockSpec((SG, ML_H, ML_DK), lambda g: (g, 0, 0)),
            pl.BlockSpec((SG, LANES), lambda g: (g, 0)),
        ],
        out_specs=[
            pl.BlockSpec((SG, D), lambda g: (g, 0)),
            pl.BlockSpec((SG, ML_H, ML_DK, ML_DV), lambda g: (g, 0, 0, 0)),
            pl.BlockSpec((SG, ML_H, ML_DK), lambda g: (g, 0, 0)),
            pl.BlockSpec((SG, LANES), lambda g: (g, 0)),
            pl.BlockSpec((SG, (CONV_W - 1) * D), lambda g: (g, 0)),
        ],
        out_shape=[
            jax.ShapeDtypeStruct((NS, D), BF16),
            jax.ShapeDtypeStruct((NS, ML_H, ML_DK, ML_DV), F32),
            jax.ShapeDtypeStruct((NS, ML_H, ML_DK), F32),
            jax.ShapeDtypeStruct((NS, LANES), F32),
            jax.ShapeDtypeStruct((NS, (CONV_W - 1) * D), F32),
        ],
        scratch_shapes=[pltpu.VMEM((SG, D), F32)],
        compiler_params=_params(("parallel",)),
        name="odd_sample",
    )(z, zg, ut3, conv, convt, bg, cw, cwt, cb, cbt, hnw, c_in, n_in, m_in)


def _out_ln_router_kernel(x_ref, y_ref, w_ref, g_ref, b_ref, wr_ref, tri_ref,
                          o_ref, op_ref, meta_ref, cnt_ref, tab_ref, carry, filled):
    i = pl.program_id(0)

    @pl.when(i == 0)
    def _():
        carry[...] = jnp.zeros_like(carry)
        tab_ref[...] = jnp.zeros_like(tab_ref)
        for e in range(N_EXPERTS):
            filled[e] = 0

    r = ALPHA * x_ref[...] + _dot(y_ref[...], w_ref[...])
    x3 = _layernorm(r, g_ref[...], b_ref[...])
    o_ref[...] = x3
    op_ref[...] = pltpu.pack_elementwise([x3[:, :D // 2], x3[:, D // 2:]], packed_dtype=BF16)

    lane = lax.broadcasted_iota(jnp.int32, (TM, LANES), 1).astype(F32)
    logits = jnp.where(lane < N_EXPERTS, _dot(x3.astype(BF16), wr_ref[...]), -jnp.inf)
    m1 = jnp.max(logits, axis=-1, keepdims=True)
    i1 = jnp.min(jnp.where(logits == m1, lane, float(LANES)), axis=-1, keepdims=True)
    rest = jnp.where(lane == i1, -jnp.inf, logits)
    m2 = jnp.max(rest, axis=-1, keepdims=True)
    i2 = jnp.min(jnp.where(rest == m2, lane, float(LANES)), axis=-1, keepdims=True)
    e2 = jnp.exp(m2 - m1)
    tot = 1.0 + e2
    w1 = 1.0 / tot
    w2 = e2 / tot

    sel1 = lane == i1
    sel2 = lane == i2
    onehot = jnp.where(sel1 | sel2, 1.0, 0.0)
    in_tile = _dot(tri_ref[...], onehot.astype(BF16))
    carry[...] = carry[...] + jnp.sum(onehot, axis=0, keepdims=True)
    cnt_ref[...] = carry[...]

    meta = jnp.where(lane == 0.0, i1, 0.0)
    meta = jnp.where(lane == 1.0, i2, meta)
    meta = jnp.where(lane == 2.0, w1, meta)
    meta = jnp.where(lane == 3.0, w2, meta)
    meta_ref[...] = meta

    token = (i * TM + lax.broadcasted_iota(jnp.int32, (TM, 1), 0)).astype(F32)
    digit_hi = jnp.floor(token * (1.0 / 256.0))
    rhs = jnp.where(lane == 0.0, digit_hi, jnp.where(lane == 1.0, token - 256.0 * digit_hi, 0.0))
    rhs = jnp.where(lane == i2 + float(SUBLANES), 1.0, rhs).astype(BF16)
    place = lax.broadcasted_iota(jnp.int32, (TM, TM), 0).astype(F32)
    rank_rows = jnp.where(onehot > 0.0, in_tile, -1.0).T
    entries = []
    for e in range(N_EXPERTS):
        hit = jnp.where(rank_rows[e:e + 1, :] == place, 1.0, 0.0).astype(BF16)
        got = _dot(hit, rhs)
        entries.append(256.0 * got[:, 0:1] + got[:, 1:2]
                       + float(NT) * got[:, SUBLANES + e:SUBLANES + e + 1])
    tile_counts = jnp.sum(onehot, axis=0, keepdims=True)
    counts = [jnp.sum(tile_counts[:, e:e + 1]).astype(jnp.int32) for e in range(N_EXPERTS)]
    for e in range(N_EXPERTS):
        start = filled[e]
        filled[e] = start + counts[e]
        tab_ref[pl.ds(start, LANES), e:e + 1] = entries[e][0:LANES]
        for part in range(1, TM // LANES):
            @pl.when(counts[e] > part * LANES)
            def _():
                tab_ref[pl.ds(start + part * LANES, LANES), e:e + 1] = entries[e][part * LANES:(part + 1) * LANES]


def _out_ln_router(x, y, w, g, b, wr, tri):
    c2 = lambda i: (0, 0)
    return pl.pallas_call(
        _out_ln_router_kernel,
        grid=(NT // TM,),
        in_specs=[
            pl.BlockSpec((TM, D), lambda i: (i, 0)),
            pl.BlockSpec((TM, D), lambda i: (i, 0)),
            pl.BlockSpec((D, D), c2),
            pl.BlockSpec((1, D), c2),
            pl.BlockSpec((1, D), c2),
            pl.BlockSpec((D, LANES), c2),
            pl.BlockSpec((TM, TM), c2),
        ],
        out_specs=[
            pl.BlockSpec((TM, D), lambda i: (i, 0)),
            pl.BlockSpec((TM, D // 2), lambda i: (i, 0)),
            pl.BlockSpec((TM, LANES), lambda i: (i, 0)),
            pl.BlockSpec((1, LANES), c2),
            pl.BlockSpec((MOE_CAP + TM, LANES), c2),
        ],
        out_shape=[
            jax.ShapeDtypeStruct((NT, D), F32),
            jax.ShapeDtypeStruct((NT, D // 2), U32),
            jax.ShapeDtypeStruct((NT, LANES), F32),
            jax.ShapeDtypeStruct((1, LANES), F32),
            jax.ShapeDtypeStruct((MOE_CAP + TM, LANES), F32),
        ],
        scratch_shapes=[pltpu.VMEM((1, LANES), F32), pltpu.SMEM((N_EXPERTS,), jnp.int32)],
        compiler_params=_params(("arbitrary",)),
        name="out_ln_router",
    )(x, y, w, g, b, wr, tri)


def _moe_ffn_kernel(te_ref, nu_ref, tb_ref, gnext_ref, gcur_ref, sprev_ref, scur_ref, xp_ref,
                    w1_ref, w3_ref, w2_ref, out_hbm, stage, yacc, xb_scr, sem_s):
    del tb_ref
    i = pl.program_id(0)
    j = pl.program_id(1)
    used = i < nu_ref[0]
    slot = i % 2
    other = 1 - slot
    rps = MOE_ROWS_PER_STEP

    def gather_rows(tab_ref, buf, part):
        for r in range(rps):
            stage[buf, part, pl.ds(r, 1), :] = xp_ref[pl.ds(tab_ref[part * rps + r], 1), :]

    def scatter(buf, r, dst):
        return pltpu.make_async_copy(yacc.at[buf, pl.ds(r, 1)], out_hbm.at[pl.ds(dst, 1)], sem_s)

    def wait_scatters(n):
        for _ in range(n):
            scatter(0, 0, 0).wait()

    def issue_neighbours():
        gather_rows(gnext_ref, other, j)
        for r in range(rps):
            rr = j * rps + r
            scatter(other, rr, sprev_ref[rr]).start()

    @pl.when(j == 0)
    def _():
        @pl.when(i == 0)
        def _():
            yacc[1] = jnp.zeros((TMM, D), F32)
            for part in range(MOE_NFF):
                gather_rows(gcur_ref, 0, part)

        @pl.when(i > 0)
        def _():
            wait_scatters(TMM)

        words = stage[slot].reshape(TMM, D // 2)
        for half in range(2):
            xb_scr[:, half * (D // 2):(half + 1) * (D // 2)] = pltpu.unpack_elementwise(
                words, index=half, packed_dtype=BF16, unpacked_dtype=F32).astype(BF16)
        yacc[slot] = jnp.zeros((TMM, D), F32)

    @pl.when(used)
    def _():
        issue_neighbours()
        xb = xb_scr[...]
        hmid = _silu(_dot(xb, w1_ref[...])) * _dot(xb, w3_ref[...])
        yacc[slot] += _dot(hmid.astype(BF16), w2_ref[...])

    @pl.when(jnp.logical_not(used))
    def _():
        issue_neighbours()

    @pl.when((i == N_MOE_TILES - 1) & (j == MOE_NFF - 1))
    def _():
        for r in range(TMM):
            scatter(slot, r, scur_ref[r]).start()
        wait_scatters(2 * TMM)


def _moe_ffn(tile_expert, n_used, tab_block, gsrc, sdst, xp, w1, w3, w2):
    nff = MOE_NFF

    def wcol(i, j, te, nu, tb):
        return (te[i], 0, jnp.where(i < nu[0], j, nff - 1))

    def wrow(i, j, te, nu, tb):
        return (te[i], jnp.where(i < nu[0], j, nff - 1), 0)

    smem = functools.partial(pl.BlockSpec, (MOE_TAB,), memory_space=pltpu.SMEM)
    grid_spec = pltpu.PrefetchScalarGridSpec(
        num_scalar_prefetch=3,
        grid=(N_MOE_TILES, nff),
        in_specs=[
            smem(lambda i, j, te, nu, tb: (tb[i + 2],)),
            smem(lambda i, j, te, nu, tb: (tb[i + 1],)),
            smem(lambda i, j, te, nu, tb: (tb[i],)),
            smem(lambda i, j, te, nu, tb: (tb[i + 1],)),
            pl.BlockSpec((NT, D // 2), lambda i, j, te, nu, tb: (0, 0), pipeline_mode=pl.Buffered(1)),
            pl.BlockSpec((None, D, TFF), wcol),
            pl.BlockSpec((None, D, TFF), wcol),
            pl.BlockSpec((None, TFF, D), wrow),
        ],
        out_specs=pl.BlockSpec(memory_space=pl.ANY),
        scratch_shapes=[
            pltpu.VMEM((2, MOE_NFF, MOE_ROWS_PER_STEP, D // 2), U32),
            pltpu.VMEM((2, TMM, D), F32),
            pltpu.VMEM((TMM, D), BF16),
            pltpu.SemaphoreType.DMA(()),
        ],
    )
    return pl.pallas_call(
        _moe_ffn_kernel,
        grid_spec=grid_spec,
        out_shape=jax.ShapeDtypeStruct((MOE_OUT_ROWS, D), F32),
        compiler_params=_params(("arbitrary", "arbitrary")),
        name="moe_ffn",
    )(tile_expert, n_used, tab_block, gsrc, gsrc, sdst, sdst, xp, w1, w3, w2)


def _combine_kernel(x_ref, meta_ref, y0_ref, y1_ref, g_ref, b_ref, prompt_ref, sample_ref):
    i = pl.program_id(0)
    meta = meta_ref[...]
    moe = meta[:, 2:3] * y0_ref[...] + meta[:, 3:4] * y1_ref[...]
    out = _layernorm(ALPHA * x_ref[...] + moe, g_ref[...], b_ref[...])

    @pl.when(i < NP // TC)
    def _():
        prompt_ref[...] = out

    @pl.when(i >= NP // TC)
    def _():
        sample_ref[...] = out


def _combine(x, meta, ys, g, b):
    c2 = lambda i: (0, 0)
    return pl.pallas_call(
        _combine_kernel,
        grid=(NT // TC,),
        in_specs=[
            pl.BlockSpec((TC, D), lambda i: (i, 0)),
            pl.BlockSpec((TC, LANES), lambda i: (i, 0)),
            pl.BlockSpec((TC, D), lambda i: (i, 0)),
            pl.BlockSpec((TC, D), lambda i: (i + NT // TC, 0)),
            pl.BlockSpec((1, D), c2),
            pl.BlockSpec((1, D), c2),
        ],
        out_specs=[
            pl.BlockSpec((TC, D), lambda i: (jnp.minimum(i, NP // TC - 1), 0)),
            pl.BlockSpec((TC, D), lambda i: (jnp.maximum(i - NP // TC, 0), 0)),
        ],
        out_shape=[jax.ShapeDtypeStruct((NP, D), F32), jax.ShapeDtypeStruct((NS, D), F32)],
        compiler_params=_params(("arbitrary",)),
        name="moe_combine",
    )(x, meta, ys, ys, g, b)


def _pad_cols(w, n):
    return jnp.pad(w, ((0, 0), (0, n - w.shape[1])))


def kernel(x_prompt, x_sample, state_hgrn, state_gla, state_mlstm_C, state_mlstm_n, state_mlstm_m,
           state_mlstm_conv, w_in_even, hg_lower_bounds, w_gk, b_gk, gn_hg, gn_gla, w_out_even,
           w1_dense, w3_dense, w2_dense, w_in_odd, b_gate_odd, conv_w, conv_b, hn_w, w_out_odd,
           w_router, w1_moe, w3_moe, w2_moe, ln1_g, ln1_b, ln2_g, ln2_b):
    assert x_prompt.shape == (BATCH, SEQ, D) and x_sample.shape == (NS, 1, D)
    assert w_in_even.shape[0] == 1 and w_in_odd.shape[0] == 1 and hg_lower_bounds.shape[0] == 2
    masks = jnp.asarray(_gla_masks(), F32)
    tri_cs = jnp.asarray(_tri(CS, False), BF16)
    tri_tm = jnp.asarray(_tri(TM, True), BF16)
    row = lambda a: a.reshape(1, -1)

    x0 = x_prompt.reshape(NP, D)
    x0_tail = jnp.concatenate([x0[LAST_TILE * TM:], x_sample.reshape(NS, D)], axis=0)

    w_even = w_in_even[0].astype(BF16)
    z, zgr = _proj(x0, x0_tail, w_even[:, :EVEN_MAIN], _pad_cols(w_even[:, EVEN_MAIN:], LANES))
    wgk = jnp.pad(w_gk[0].astype(BF16), ((0, LANES - GLA_RANK), (0, 0)))
    lbp = hg_lower_bounds
    zs = z[NP:].reshape(NS // SG, SG, EVEN_MAIN).transpose(0, 2, 1)
    grs = zgr[NP:].reshape(NS // SG, SG, LANES).transpose(0, 2, 1)
    y_s, hg_s, gla_s = _even_sample(z, zs, grs, lbp.T, wgk.T, b_gk[0].reshape(-1, 1),
                                    row(gn_hg[0]), row(gn_gla[0]), state_hgrn[0], state_gla[0])
    y, hg_p, gla_p = _even_prompt(z, zgr, y_s, lbp, wgk, row(b_gk[0]), row(gn_hg[0]), row(gn_gla[0]), tri_cs, masks)
    x1 = _out_ln(x0, x0_tail, y, w_out_even[0].astype(BF16), row(ln1_g[0]), row(ln1_b[0]))
    x2 = _ffn(x1, w1_dense[0].astype(BF16), w3_dense[0].astype(BF16), w2_dense[0].astype(BF16),
              row(ln2_g[0]), row(ln2_b[0]))

    w_odd = w_in_odd[0].astype(BF16)
    zo, zog = _proj(x2, x2[LAST_TILE * TM:], w_odd[:, :ODD_MAIN], _pad_cols(w_odd[:, ODD_MAIN:], LANES))
    bg = jnp.pad(b_gate_odd[0], (0, LANES - 2 * ML_H)).reshape(1, LANES)
    ut = zo[NP:, :D].reshape(NS // SG, SG, D).transpose(0, 2, 1)
    conv_in = state_mlstm_conv[0]
    conv_t = conv_in.reshape(NS // SG, SG, CONV_W - 1, D).transpose(0, 2, 3, 1)
    m_in = jnp.pad(state_mlstm_m[0], ((0, 0), (0, LANES - ML_H)))
    yo_s, c_s, n_s, m_s, conv_s = _odd_sample(
        zo, zog, ut, conv_in.reshape(NS, (CONV_W - 1) * D), conv_t, bg, conv_w[0], conv_w[0].T, row(conv_b[0]), conv_b[0].reshape(-1, 1),
        row(hn_w[0]), state_mlstm_C[0], state_mlstm_n[0], m_in)
    yo, c_p, n_p, m_p, conv_p = _odd_prompt(zo, zog, yo_s, bg, conv_w[0], row(conv_b[0]), row(hn_w[0]), tri_cs)

    wr = _pad_cols(w_router[0].astype(BF16), LANES)
    x3, x3p, meta, cnt, tab = _out_ln_router(
        x2, yo, w_out_odd[0].astype(BF16), row(ln1_g[1]), row(ln1_b[1]), wr, tri_tm)

    counts = cnt[0, :N_EXPERTS].astype(jnp.int32)
    padded = ((counts + TMM - 1) // TMM) * TMM
    ends = jnp.cumsum(padded)
    offsets = ends - padded
    tile = jnp.arange(N_MOE_TILES, dtype=jnp.int32)
    tile_expert = jnp.minimum(jnp.sum((tile * TMM)[:, None] >= ends[None, :], axis=1), N_EXPERTS - 1).astype(jnp.int32)
    n_tiles_used = ends[-1] // TMM
    in_use = tile < n_tiles_used
    local_tile = (tile * TMM - offsets[tile_expert]) // TMM
    blocks_per_expert = MOE_CAP // MOE_TAB
    spill_block = N_EXPERTS * blocks_per_expert
    n_spare = N_MOE_TILES - (2 * NT) // TMM
    entry = tab[:MOE_CAP, :N_EXPERTS].T.astype(jnp.int32)
    local = jnp.arange(MOE_CAP, dtype=jnp.int32)[None, :]
    valid = local < counts[:, None]
    padding = padded - counts
    pad_before = (jnp.cumsum(padding) - padding)[:, None]
    row_in_block = jnp.arange(MOE_TAB, dtype=jnp.int32)
    spare_rows = (2 * NT + TMM + jnp.sum(padding)
                  + jnp.arange(n_spare, dtype=jnp.int32)[:, None] * TMM + row_in_block[None, :])
    sdst = jnp.concatenate([
        jnp.where(valid, entry, 2 * NT + TMM + pad_before + local - counts[:, None]).reshape(-1),
        2 * NT + row_in_block, spare_rows.reshape(-1)])
    gsrc = jnp.concatenate([
        jnp.where(valid, entry - jnp.where(entry >= NT, NT, 0), 0).reshape(-1),
        jnp.zeros(((1 + n_spare) * MOE_TAB,), jnp.int32)])
    own_block = jnp.where(in_use, tile_expert * blocks_per_expert + local_tile, spill_block + 1 + tile - n_tiles_used)
    tab_block = jnp.concatenate([jnp.full((1,), spill_block, jnp.int32), own_block.astype(jnp.int32),
                                 jnp.full((1,), spill_block, jnp.int32)])

    ys = _moe_ffn(tile_expert, n_tiles_used.reshape(1), tab_block, gsrc, sdst, x3p,
                  w1_moe[0].astype(BF16), w3_moe[0].astype(BF16), w2_moe[0].astype(BF16))
    out_prompt, out_sample = _combine(x3, meta, ys, row(ln2_g[1]), row(ln2_b[1]))

    return (out_prompt.reshape(BATCH, SEQ, D), out_sample.reshape(NS, 1, D),
            hg_p.reshape(1, BATCH, HG_H, HG_DK, HG_DV), gla_p.reshape(1, BATCH, GLA_H, GLA_DK, GLA_DV),
            c_p[None], n_p[None], m_p[:, 0, :ML_H][None], conv_p[None],
            hg_s[None], gla_s[None], c_s[None], n_s[None], m_s[:, :ML_H][None], conv_s.reshape(1, NS, CONV_W - 1, D))
```

```python
import functools
import math

import jax
import jax.numpy as jnp
import numpy as np
from jax import lax
from jax.experimental import pallas as pl
from jax.experimental.pallas import tpu as pltpu

F32 = jnp.float32
BF16 = jnp.bfloat16
U32 = jnp.uint32

D = 1024
BATCH = 8
SEQ = 2048
DEC_BATCH = 128
NP = BATCH * SEQ
NS = DEC_BATCH
NT = NP + NS
HG_H, HG_DK, HG_DV = 4, 128, 128
GLA_H, GLA_DK, GLA_DV = 4, 64, 128
GLA_RANK = 16
GLA_GATE_NORM = 16.0
ML_H, ML_DK, ML_DV = 4, 128, 256
CONV_W = 4
D_FF_DENSE = 2816
D_FF_EXPERT = 3584
N_EXPERTS = 8
EPS = 1e-5
DEPTH = 2
ALPHA = (2.0 * DEPTH) ** 0.25
EVEN_MAIN = 3584
ODD_MAIN = 3072

LANES = 128
SUBLANES = 8
VMEM_LIMIT = 56 * 1024 * 1024

TM = 384
LAST_TILE = NT // TM - 1
CS = 128
NCHUNK = SEQ // CS
SG = 16
TMM = 512
TFF = 896
MOE_TAB = 512
MOE_CAP = -(-NT // MOE_TAB) * MOE_TAB
MOE_NFF = D_FF_EXPERT // TFF
MOE_ROWS_PER_STEP = TMM // MOE_NFF
N_MOE_TILES = -(-(2 * NT + N_EXPERTS * (TMM - 1)) // TMM)
MOE_SLOTS = N_MOE_TILES * TMM
MOE_OUT_ROWS = MOE_SLOTS + TMM
N_LEVELS = int(math.log2(CS))

assert NT % TM == 0 and NP % CS == 0 and NS % SG == 0 and D_FF_EXPERT % TFF == 0 and TMM % MOE_NFF == 0
assert TMM == MOE_TAB and NS == CS and LAST_TILE * TM <= NP


def _params(sem, limit=VMEM_LIMIT):
    return pltpu.CompilerParams(dimension_semantics=sem, vmem_limit_bytes=limit)


def _dot(a, b):
    return jnp.dot(a, b, preferred_element_type=F32)


def _dot_nt(a, b):
    return lax.dot_general(a, b, (((1,), (1,)), ((), ())), preferred_element_type=F32)


def _dot_tn(a, b):
    return lax.dot_general(a, b, (((0,), (0,)), ((), ())), preferred_element_type=F32)


def _split3(x):
    hi = x.astype(BF16)
    r1 = x - hi.astype(F32)
    mid = r1.astype(BF16)
    lo = (r1 - mid.astype(F32)).astype(BF16)
    return hi, mid, lo


def _dot_sel(sel, x):
    hi, mid, lo = _split3(x)
    return _dot(sel, hi) + _dot(sel, mid) + _dot(sel, lo)


def _sigmoid(x):
    return jax.nn.sigmoid(x)


def _silu(x):
    return x * jax.nn.sigmoid(x)


def _log_sigmoid(x):
    return jnp.minimum(x, 0.0) - jnp.log(1.0 + jnp.exp(-jnp.abs(x)))


def _layernorm(r, g, b):
    mu = jnp.mean(r, axis=-1, keepdims=True)
    c = r - mu
    var = jnp.mean(c * c, axis=-1, keepdims=True)
    return c * lax.rsqrt(var + EPS) * g + b


def _gla_masks():
    masks = np.zeros((N_LEVELS + 1, CS, CS), np.float32)
    for t in range(CS):
        for l in range(N_LEVELS):
            half = 1 << l
            start = (t // (2 * half)) * (2 * half)
            mid = start + half
            if t >= mid:
                masks[l, t, start:mid] = 1.0
        masks[N_LEVELS, t, t] = 1.0
    return masks


def _tri(n, strict):
    return np.tril(np.ones((n, n), np.float32), -1 if strict else 0)


def _token_tile(x_ref, tail_ref):
    return jnp.where(pl.program_id(0) == LAST_TILE, tail_ref[...], x_ref[...])


def _main_tile(i):
    return (jnp.minimum(i, LAST_TILE - 1), 0)


def _proj_kernel(x_ref, tail_ref, wa_ref, wb_ref, oa_ref, ob_ref):
    xb = _token_tile(x_ref, tail_ref).astype(BF16)
    oa_ref[...] = _dot(xb, wa_ref[...])
    ob_ref[...] = _dot(xb, wb_ref[...])


def _proj(x, tail, wa, wb):
    na, nb = wa.shape[1], wb.shape[1]
    return pl.pallas_call(
        _proj_kernel,
        grid=(NT // TM,),
        in_specs=[
            pl.BlockSpec((TM, D), _main_tile),
            pl.BlockSpec((TM, D), lambda i: (0, 0)),
            pl.BlockSpec((D, na), lambda i: (0, 0)),
            pl.BlockSpec((D, nb), lambda i: (0, 0)),
        ],
        out_specs=[
            pl.BlockSpec((TM, na), lambda i: (i, 0)),
            pl.BlockSpec((TM, nb), lambda i: (i, 0)),
        ],
        out_shape=[jax.ShapeDtypeStruct((NT, na), F32), jax.ShapeDtypeStruct((NT, nb), F32)],
        compiler_params=_params(("parallel",)),
        name="proj",
    )(x, tail, wa, wb)


def _rms_gate(o, gate, w):
    o = o * lax.rsqrt(jnp.mean(o * o, axis=-1, keepdims=True) + EPS) * w
    return o * _silu(gate)


def _level_decays(g, bc):
    width = g.shape[1]
    ng = CS // SUBLANES
    shape3 = (ng, SUBLANES, width)
    bc3 = bc.reshape(shape3)
    sub = lax.broadcasted_iota(jnp.int32, shape3, 1)

    def group_row(s):
        return jnp.broadcast_to(bc3[:, s:s + 1, :], shape3)

    last = group_row(SUBLANES - 1)
    refs = [None,
            jnp.where(sub < 4, group_row(1), group_row(5)),
            group_row(3)]
    for l in range(3, N_LEVELS):
        per_block = 1 << (l - 2)
        grouped = last.reshape(ng // per_block, per_block, SUBLANES, width)
        ref = jnp.broadcast_to(grouped[:, per_block // 2 - 1:per_block // 2], grouped.shape)
        refs.append(ref.reshape(shape3))
    decays = [jnp.exp(jnp.where((sub & 1) == 1, g.reshape(shape3), 0.0))]
    decays += [jnp.exp(-jnp.abs(bc3 - ref)) for ref in refs[1:]]
    to_end = jnp.exp(jnp.broadcast_to(last[ng - 1:ng], shape3) - bc3)
    return [d.reshape(CS, width) for d in decays], to_end.reshape(CS, width)


def _gla_chunk(q, k, v, g, st_ref, tri, masks_ref, heads, dk, dv):
    bc = _dot_sel(tri, g)
    z_levels, z_end = _level_decays(g, bc)
    z_cum = jnp.exp(bc)
    st = st_ref[...]
    outs = []
    for h in range(heads):
        ks = slice(h * dk, (h + 1) * dk)
        vs = slice(h * dv, (h + 1) * dv)
        qh, kh = q[:, ks], k[:, ks]
        vh = v[:, vs].astype(BF16)
        scores = _dot_nt(qh.astype(BF16), kh.astype(BF16)) * masks_ref[N_LEVELS]
        for l in range(N_LEVELS):
            zl = z_levels[l][:, ks]
            scores = scores + _dot_nt((qh * zl).astype(BF16), (kh * zl).astype(BF16)) * masks_ref[l]
        o = _dot(scores.astype(BF16), vh)
        o = o + _dot_nt((qh * z_cum[:, ks]).astype(BF16), st[:, ks].astype(BF16))
        outs.append(o)
        upd = _dot_tn(vh, (kh * z_end[:, ks]).astype(BF16))
        st_ref[:, ks] = st[:, ks] * z_cum[CS - 1:CS, ks] + upd
    return outs


def _even_prompt_kernel(z_ref, zgr_ref, ys_ref, lbp_ref, wgk_ref, bgk_ref, gnh_ref, gng_ref, tri_ref, masks_ref,
                        y_ref, shg_ref, sgla_ref, st_hg, st_gla):
    b = pl.program_id(0)
    c = pl.program_id(1)

    @pl.when((b < BATCH) & (c == 0))
    def _():
        st_hg[...] = jnp.zeros_like(st_hg)
        st_gla[...] = jnp.zeros_like(st_gla)

    @pl.when(b < BATCH)
    def _():
        tri = tri_ref[...]
        p = lbp_ref[...]
        pe = jnp.exp(p - jnp.max(p, axis=0, keepdims=True))
        lb = pe[0:1] / jnp.sum(pe, axis=0, keepdims=True)

        z = z_ref[...]
        hq, hf, hi, hg = z[:, 0:512], z[:, 512:1024], z[:, 1024:1536], z[:, 1536:2048]
        gq, gk, gv, gg = z[:, 2048:2304], z[:, 2304:2560], z[:, 2560:3072], z[:, 3072:3584]
        f = lb + (1.0 - lb) * _sigmoid(hf)
        k_hg = (1.0 - lb) * _sigmoid(-hf)
        o_hg = _gla_chunk(_silu(hq), k_hg, hi, jnp.log(f), st_hg, tri, masks_ref, HG_H, HG_DK, HG_DV)

        la = _log_sigmoid(_dot(zgr_ref[...].astype(BF16), wgk_ref[...]) + bgk_ref[...]) / GLA_GATE_NORM
        o_gla = _gla_chunk(gq * GLA_DK ** -0.5, gk, gv, la, st_gla, tri, masks_ref, GLA_H, GLA_DK, GLA_DV)

        for h in range(HG_H):
            cs = slice(h * 128, (h + 1) * 128)
            y_ref[:, cs] = _rms_gate(o_hg[h], hg[:, cs], gnh_ref[...]).astype(BF16)
        for h in range(GLA_H):
            cs = slice(h * 128, (h + 1) * 128)
            y_ref[:, 512 + h * 128:512 + (h + 1) * 128] = _rms_gate(o_gla[h], gg[:, cs], gng_ref[...]).astype(BF16)

    @pl.when((b < BATCH) & (c == NCHUNK - 1))
    def _():
        shg_ref[0] = st_hg[...].T
        sgla_ref[0] = st_gla[...].T

    @pl.when((b == BATCH) & (c == 0))
    def _():
        y_ref[...] = ys_ref[...]


def _chunk_rows(b, c):
    return (jnp.minimum(b * NCHUNK + c, NP // CS), 0)


def _per_sequence(b, c):
    return (jnp.minimum(b, BATCH - 1), 0, 0)


def _even_prompt(z, zgr, y_sample, lbp, wgk, bgk, gnh, gng, tri, masks):
    const2 = lambda b, c: (0, 0)
    return pl.pallas_call(
        _even_prompt_kernel,
        grid=(BATCH + 1, NCHUNK),
        in_specs=[
            pl.BlockSpec((CS, EVEN_MAIN), _chunk_rows),
            pl.BlockSpec((CS, LANES), _chunk_rows),
            pl.BlockSpec((NS, D), const2),
            pl.BlockSpec(lbp.shape, const2),
            pl.BlockSpec(wgk.shape, const2),
            pl.BlockSpec(bgk.shape, const2),
            pl.BlockSpec(gnh.shape, const2),
            pl.BlockSpec(gng.shape, const2),
            pl.BlockSpec(tri.shape, const2),
            pl.BlockSpec(masks.shape, lambda b, c: (0, 0, 0)),
        ],
        out_specs=[
            pl.BlockSpec((CS, D), _chunk_rows),
            pl.BlockSpec((1, HG_H * HG_DK, HG_DV), _per_sequence),
            pl.BlockSpec((1, GLA_H * GLA_DK, GLA_DV), _per_sequence),
        ],
        out_shape=[
            jax.ShapeDtypeStruct((NT, D), BF16),
            jax.ShapeDtypeStruct((BATCH, HG_H * HG_DK, HG_DV), F32),
            jax.ShapeDtypeStruct((BATCH, GLA_H * GLA_DK, GLA_DV), F32),
        ],
        scratch_shapes=[pltpu.VMEM((HG_DV, HG_H * HG_DK), F32), pltpu.VMEM((GLA_DV, GLA_H * GLA_DK), F32)],
        compiler_params=_params(("arbitrary", "arbitrary")),
        name="even_prompt",
    )(z, zgr, y_sample, lbp, wgk, bgk, gnh, gng, tri, masks)


def _even_sample_kernel(zr_ref, zt_ref, grt_ref, lbpt_ref, wgkt_ref, bgkt_ref, gnh_ref, gng_ref,
                        shg_ref, sgla_ref, y_ref, shg_out, sgla_out, o_scr):
    zt = zt_ref[0]
    hq_t, hf_t = zt[0:512], zt[512:1024]
    gq_t, gk_t = zt[2048:2304], zt[2304:2560]
    pt = lbpt_ref[...]
    pe = jnp.exp(pt - jnp.max(pt, axis=1, keepdims=True))
    lb = pe[:, 0:1] / jnp.sum(pe, axis=1, keepdims=True)
    a_hg = jnp.exp(jnp.log(lb + (1.0 - lb) * _sigmoid(hf_t)))
    k_hg = (1.0 - lb) * _sigmoid(-hf_t)
    q_hg = _silu(hq_t)
    la = _log_sigmoid(_dot(wgkt_ref[...], grt_ref[0].astype(BF16)) + bgkt_ref[...]) / GLA_GATE_NORM
    a_gla = jnp.exp(la)
    q_gla = gq_t * GLA_DK ** -0.5
    zr = zr_ref[...]
    hi, hg = zr[:, 1024:1536], zr[:, 1536:2048]
    gv, gg = zr[:, 2560:3072], zr[:, 3072:3584]

    for j in range(SG):
        for h in range(HG_H):
            ks = slice(h * HG_DK, (h + 1) * HG_DK)
            s_new = a_hg[ks, j:j + 1] * shg_ref[j, h] + k_hg[ks, j:j + 1] * hi[j:j + 1, h * 128:(h + 1) * 128]
            shg_out[j, h] = s_new
            o_scr[j:j + 1, h * 128:(h + 1) * 128] = jnp.sum(q_hg[ks, j:j + 1] * s_new, axis=0, keepdims=True)
        for h in range(GLA_H):
            ks = slice(h * GLA_DK, (h + 1) * GLA_DK)
            s_new = a_gla[ks, j:j + 1] * sgla_ref[j, h] + gk_t[ks, j:j + 1] * gv[j:j + 1, h * 128:(h + 1) * 128]
            sgla_out[j, h] = s_new
            o_scr[j:j + 1, 512 + h * 128:512 + (h + 1) * 128] = jnp.sum(
                q_gla[ks, j:j + 1] * s_new, axis=0, keepdims=True)

    o = o_scr[...]
    for h in range(HG_H):
        cs = slice(h * 128, (h + 1) * 128)
        y_ref[:, cs] = _rms_gate(o[:, cs], hg[:, cs], gnh_ref[...]).astype(BF16)
    for h in range(GLA_H):
        cs = slice(512 + h * 128, 512 + (h + 1) * 128)
        y_ref[:, cs] = _rms_gate(o[:, cs], gg[:, h * 128:(h + 1) * 128], gng_ref[...]).astype(BF16)


def _even_sample(z, zt3, grt3, lbpt, wgkt, bgkt, gnh, gng, s_hg, s_gla):
    c2 = lambda g: (0, 0)
    return pl.pallas_call(
        _even_sample_kernel,
        grid=(NS // SG,),
        in_specs=[
            pl.BlockSpec((SG, EVEN_MAIN), lambda g: (NP // SG + g, 0)),
            pl.BlockSpec((1, EVEN_MAIN, SG), lambda g: (g, 0, 0)),
            pl.BlockSpec((1, LANES, SG), lambda g: (g, 0, 0)),
            pl.BlockSpec(lbpt.shape, c2),
            pl.BlockSpec(wgkt.shape, c2),
            pl.BlockSpec(bgkt.shape, c2),
            pl.BlockSpec(gnh.shape, c2),
            pl.BlockSpec(gng.shape, c2),
            pl.BlockSpec((SG, HG_H, HG_DK, HG_DV), lambda g: (g, 0, 0, 0)),
            pl.BlockSpec((SG, GLA_H, GLA_DK, GLA_DV), lambda g: (g, 0, 0, 0)),
        ],
        out_specs=[
            pl.BlockSpec((SG, D), lambda g: (g, 0)),
            pl.BlockSpec((SG, HG_H, HG_DK, HG_DV), lambda g: (g, 0, 0, 0)),
            pl.BlockSpec((SG, GLA_H, GLA_DK, GLA_DV), lambda g: (g, 0, 0, 0)),
        ],
        out_shape=[
            jax.ShapeDtypeStruct((NS, D), BF16),
            jax.ShapeDtypeStruct((NS, HG_H, HG_DK, HG_DV), F32),
            jax.ShapeDtypeStruct((NS, GLA_H, GLA_DK, GLA_DV), F32),
        ],
        scratch_shapes=[pltpu.VMEM((SG, D), F32)],
        compiler_params=_params(("parallel",)),
        name="even_sample",
    )(z, zt3, grt3, lbpt, wgkt, bgkt, gnh, gng, s_hg, s_gla)


def _out_ln_kernel(x_ref, tail_ref, y_ref, w_ref, g_ref, b_ref, o_ref):
    r = ALPHA * _token_tile(x_ref, tail_ref) + _dot(y_ref[...], w_ref[...])
    o_ref[...] = _layernorm(r, g_ref[...], b_ref[...])


def _out_ln(x, tail, y, w, g, b):
    c2 = lambda i: (0, 0)
    return pl.pallas_call(
        _out_ln_kernel,
        grid=(NT // TM,),
        in_specs=[
            pl.BlockSpec((TM, D), _main_tile),
            pl.BlockSpec((TM, D), c2),
            pl.BlockSpec((TM, D), lambda i: (i, 0)),
            pl.BlockSpec((D, D), c2),
            pl.BlockSpec((1, D), c2),
            pl.BlockSpec((1, D), c2),
        ],
        out_specs=pl.BlockSpec((TM, D), lambda i: (i, 0)),
        out_shape=jax.ShapeDtypeStruct((NT, D), F32),
        compiler_params=_params(("parallel",)),
        name="out_ln",
    )(x, tail, y, w, g, b)


FF_SPLIT = 2


def _ffn_kernel(x_ref, w1_ref, w3_ref, w2_ref, g_ref, b_ref, o_ref):
    x = x_ref[...]
    xb = x.astype(BF16)
    step = D_FF_DENSE // FF_SPLIT
    acc = ALPHA * x
    for s in range(FF_SPLIT):
        cs = slice(s * step, (s + 1) * step)
        hmid = _silu(_dot(xb, w1_ref[:, cs])) * _dot(xb, w3_ref[:, cs])
        acc = acc + _dot(hmid.astype(BF16), w2_ref[cs, :])
    o_ref[...] = _layernorm(acc, g_ref[...], b_ref[...])


def _ffn(x, w1, w3, w2, g, b):
    c2 = lambda i: (0, 0)
    one = pl.Buffered(1)
    return pl.pallas_call(
        _ffn_kernel,
        grid=(NT // TM,),
        in_specs=[
            pl.BlockSpec((TM, D), lambda i: (i, 0)),
            pl.BlockSpec((D, D_FF_DENSE), c2, pipeline_mode=one),
            pl.BlockSpec((D, D_FF_DENSE), c2, pipeline_mode=one),
            pl.BlockSpec((D_FF_DENSE, D), c2, pipeline_mode=one),
            pl.BlockSpec((1, D), c2),
            pl.BlockSpec((1, D), c2),
        ],
        out_specs=pl.BlockSpec((TM, D), lambda i: (i, 0)),
        out_shape=jax.ShapeDtypeStruct((NT, D), F32),
        compiler_params=_params(("parallel",)),
        name="ffn_dense",
    )(x, w1, w3, w2, g, b)


def _mh_norm_gate(hh, o_pre, w):
    mu = jnp.mean(hh, axis=-1, keepdims=True)
    c = hh - mu
    var = jnp.mean(c * c, axis=-1, keepdims=True)
    return _sigmoid(o_pre) * (c * lax.rsqrt(var + EPS) * w)


def _odd_prompt_kernel(z_ref, zg_ref, ys_ref, bg_ref, cw_ref, cb_ref, hnw_ref, tri_ref,
                       y_ref, c_out, n_out, m_out, conv_out,
                       c_scr, n_scr, m_scr, u_scr):
    b = pl.program_id(0)
    c = pl.program_id(1)

    @pl.when((b < BATCH) & (c == 0))
    def _():
        c_scr[...] = jnp.zeros_like(c_scr)
        n_scr[...] = jnp.zeros_like(n_scr)
        m_scr[...] = jnp.zeros_like(m_scr)
        u_scr[0:8, :] = jnp.zeros((8, D), F32)

    @pl.when((b == BATCH) & (c == 0))
    def _():
        y_ref[...] = ys_ref[...]

    @pl.when(b < BATCH)
    def _():
        row = lax.broadcasted_iota(jnp.int32, (CS, CS), 0)
        col = lax.broadcasted_iota(jnp.int32, (CS, CS), 1)
        causal = col <= row

        u_scr[8:8 + CS, :] = z_ref[:, 0:D]
        uc = cb_ref[...]
        for j in range(CONV_W):
            uc = uc + u_scr[5 + j:5 + j + CS, :] * cw_ref[j:j + 1, :]
        tail = u_scr[CS:CS + 8, :]
        u_scr[0:8, :] = tail
        act = _silu(uc)
        q = act[:, 0:512] * ML_DK ** -0.5
        k = act[:, 512:1024]
        v = z_ref[:, D:2 * D]
        o_pre = z_ref[:, 2 * D:3 * D]

        gates = zg_ref[...] + bg_ref[...]
        lf = _log_sigmoid(gates)
        bcum = _dot_sel(tri_ref[...], lf)
        bcum_t = bcum.T
        gates_t = gates.T
        m_all = m_scr[...]

        for h in range(ML_H):
            ks = slice(h * ML_DK, (h + 1) * ML_DK)
            vs = slice(h * ML_DV, (h + 1) * ML_DV)
            qh, kh = q[:, ks], k[:, ks]
            vh = v[:, vs].astype(BF16)
            b_col = bcum[:, 4 + h:5 + h]
            b_row = bcum_t[4 + h:5 + h, :]
            i_col = gates[:, h:h + 1]
            i_row = gates_t[h:h + 1, :]
            m_prev = m_all[:, h:h + 1]
            log_d = jnp.where(causal, b_col - b_row + i_row, -jnp.inf)
            log_prev = b_col + m_prev
            m_t = jnp.maximum(jnp.max(log_d, axis=-1, keepdims=True), log_prev)
            d = jnp.exp(log_d - m_t)
            w_prev = jnp.exp(log_prev - m_t)
            scores = _dot_nt(qh.astype(BF16), kh.astype(BF16)) * d
            c_h = c_scr[h]
            n_h = n_scr[h:h + 1, :]
            num = _dot(scores.astype(BF16), vh) + w_prev * _dot(qh.astype(BF16), c_h.astype(BF16))
            den = jnp.sum(scores, axis=-1, keepdims=True) + w_prev * jnp.sum(qh * n_h, axis=-1, keepdims=True)
            hh = num / jnp.maximum(jnp.abs(den), jnp.exp(-m_t))
            m_new = m_t[CS - 1:CS, :]
            b_last = b_col[CS - 1:CS, :]
            w_c = jnp.exp(b_last + m_prev - m_new)
            w_s = jnp.exp(b_last - b_col + i_col - m_new)
            kw = kh * w_s
            c_scr[h] = w_c * c_h + _dot_tn(kw.astype(BF16), vh)
            n_scr[h:h + 1, :] = w_c * n_h + jnp.sum(kw, axis=0, keepdims=True)
            m_scr[:, h:h + 1] = m_new
            y_ref[:, vs] = _mh_norm_gate(hh, o_pre[:, vs], hnw_ref[:, vs]).astype(BF16)

        @pl.when(c == NCHUNK - 1)
        def _():
            c_out[0] = c_scr[...]
            n_out[0] = n_scr[0:ML_H, :]
            m_out[0] = m_scr[...]
            conv_out[0] = tail[8 - (CONV_W - 1):8, :]


def _odd_prompt(z, zg, y_sample, bg, cw, cb, hnw, tri):
    c2 = lambda b, c: (0, 0)
    per_sequence4 = lambda b, c: (jnp.minimum(b, BATCH - 1), 0, 0, 0)
    return pl.pallas_call(
        _odd_prompt_kernel,
        grid=(BATCH + 1, NCHUNK),
        in_specs=[
            pl.BlockSpec((CS, ODD_MAIN), _chunk_rows),
            pl.BlockSpec((CS, LANES), _chunk_rows),
            pl.BlockSpec((NS, D), c2),
            pl.BlockSpec((1, LANES), c2),
            pl.BlockSpec((CONV_W, D), c2),
            pl.BlockSpec((1, D), c2),
            pl.BlockSpec((1, D), c2),
            pl.BlockSpec((CS, CS), c2),
        ],
        out_specs=[
            pl.BlockSpec((CS, D), _chunk_rows),
            pl.BlockSpec((1, ML_H, ML_DK, ML_DV), per_sequence4),
            pl.BlockSpec((1, ML_H, ML_DK), _per_sequence),
            pl.BlockSpec((1, 1, LANES), _per_sequence),
            pl.BlockSpec((1, CONV_W - 1, D), _per_sequence),
        ],
        out_shape=[
            jax.ShapeDtypeStruct((NT, D), BF16),
            jax.ShapeDtypeStruct((BATCH, ML_H, ML_DK, ML_DV), F32),
            jax.ShapeDtypeStruct((BATCH, ML_H, ML_DK), F32),
            jax.ShapeDtypeStruct((BATCH, 1, LANES), F32),
            jax.ShapeDtypeStruct((BATCH, CONV_W - 1, D), F32),
        ],
        scratch_shapes=[
            pltpu.VMEM((ML_H, ML_DK, ML_DV), F32),
            pltpu.VMEM((8, ML_DK), F32),
            pltpu.VMEM((1, LANES), F32),
            pltpu.VMEM((CS + 8, D), F32),
        ],
        compiler_params=_params(("arbitrary", "arbitrary")),
        name="odd_prompt",
    )(z, zg, y_sample, bg, cw, cb, hnw, tri)


def _odd_sample_kernel(zr_ref, zg_ref, ut_ref, conv_ref, convt_ref, bg_ref, cw_ref, cwt_ref, cb_ref, cbt_ref,
                       hnw_ref, c_ref, n_ref, m_ref,
                       y_ref, c_out, n_out, m_out, conv_out, h_scr):
    zr = zr_ref[...]
    u = zr[:, 0:D]
    v = zr[:, D:2 * D]
    o_pre = zr[:, 2 * D:3 * D]
    uc = cb_ref[...] + u * cw_ref[CONV_W - 1:CONV_W, :]
    uc_t = cbt_ref[...] + ut_ref[0] * cwt_ref[:, CONV_W - 1:CONV_W]
    for j in range(CONV_W - 1):
        uc = uc + conv_ref[:, j * D:(j + 1) * D] * cw_ref[j:j + 1, :]
        uc_t = uc_t + convt_ref[0, j] * cwt_ref[:, j:j + 1]
        conv_out[:, j * D:(j + 1) * D] = conv_ref[:, (j + 1) * D:(j + 2) * D] if j + 1 < CONV_W - 1 else u
    act = _silu(uc)
    k_row = act[:, 512:1024]
    act_t = _silu(uc_t)
    q_t = act_t[0:512] * ML_DK ** -0.5
    k_t = act_t[512:1024]
    q_row = act[:, 0:512] * ML_DK ** -0.5

    gates = zg_ref[...] + bg_ref[...]
    lf = _log_sigmoid(gates)
    m_in = m_ref[...]
    m_out[...] = m_in

    for j in range(SG):
        for h in range(ML_H):
            ks = slice(h * ML_DK, (h + 1) * ML_DK)
            vs = slice(h * ML_DV, (h + 1) * ML_DV)
            ig = gates[j:j + 1, h:h + 1]
            log_prev = lf[j:j + 1, 4 + h:5 + h] + m_in[j:j + 1, h:h + 1]
            m_t = jnp.maximum(ig, log_prev)
            d = jnp.exp(ig - m_t)
            w_prev = jnp.exp(log_prev - m_t)
            c_new = w_prev * c_ref[j, h] + (d * k_t[ks, j:j + 1]) * v[j:j + 1, vs]
            n_new = w_prev * n_ref[j, h:h + 1, :] + d * k_row[j:j + 1, ks]
            c_out[j, h] = c_new
            n_out[j, h:h + 1, :] = n_new
            m_out[j:j + 1, h:h + 1] = m_t
            num = jnp.sum(q_t[ks, j:j + 1] * c_new, axis=0, keepdims=True)
            den = jnp.sum(q_row[j:j + 1, ks] * n_new, axis=-1, keepdims=True)
            h_scr[j:j + 1, vs] = num / jnp.maximum(jnp.abs(den), jnp.exp(-m_t))

    hh = h_scr[...]
    for h in range(ML_H):
        vs = slice(h * ML_DV, (h + 1) * ML_DV)
        y_ref[:, vs] = _mh_norm_gate(hh[:, vs], o_pre[:, vs], hnw_ref[:, vs]).astype(BF16)


def _odd_sample(z, zg, ut3, conv, convt, bg, cw, cwt, cb, cbt, hnw, c_in, n_in, m_in):
    c2 = lambda g: (0, 0)
    return pl.pallas_call(
        _odd_sample_kernel,
        grid=(NS // SG,),
        in_specs=[
            pl.BlockSpec((SG, ODD_MAIN), lambda g: (NP // SG + g, 0)),
            pl.BlockSpec((SG, LANES), lambda g: (NP // SG + g, 0)),
            pl.BlockSpec((1, D, SG), lambda g: (g, 0, 0)),
            pl.BlockSpec((SG, (CONV_W - 1) * D), lambda g: (g, 0)),
            pl.BlockSpec((1, CONV_W - 1, D, SG), lambda g: (g, 0, 0, 0)),
            pl.BlockSpec((1, LANES), c2),
            pl.BlockSpec((CONV_W, D), c2),
            pl.BlockSpec((D, CONV_W), c2),
            pl.BlockSpec((1, D), c2),
            pl.BlockSpec((D, 1), c2),
            pl.BlockSpec((1, D), c2),
            pl.BlockSpec((SG, ML_H, ML_DK, ML_DV), lambda g: (g, 0, 0, 0)),
            pl.BlockSpec((SG, ML_H, ML_DK), lambda g: (g, 0, 0)),
            pl.BlockSpec((SG, LANES), lambda g: (g, 0)),
        ],
        out_specs=[
            pl.BlockSpec((SG, D), lambda g: (g, 0)),
            pl.BlockSpec((SG, ML_H, ML_DK, ML_DV), lambda g: (g, 0, 0, 0)),
            pl.BlockSpec((SG, ML_H, ML_DK), lambda g: (g, 0, 0)),
            pl.BlockSpec((SG, LANES), lambda g: (g, 0)),
            pl.BlockSpec((SG, (CONV_W - 1) * D), lambda g: (g, 0)),
        ],
        out_shape=[
            jax.ShapeDtypeStruct((NS, D), BF16),
            jax.ShapeDtypeStruct((NS, ML_H, ML_DK, ML_DV), F32),
            jax.ShapeDtypeStruct((NS, ML_H, ML_DK), F32),
            jax.ShapeDtypeStruct((NS, LANES), F32),
            jax.ShapeDtypeStruct((NS, (CONV_W - 1) * D), F32),
        ],
        scratch_shapes=[pltpu.VMEM((SG, D), F32)],
        compiler_params=_params(("parallel",)),
        name="odd_sample",
    )(z, zg, ut3, conv, convt, bg, cw, cwt, cb, cbt, hnw, c_in, n_in, m_in)


def _out_ln_router_kernel(x_ref, y_ref, w_ref, g_ref, b_ref, wr_ref, tri_ref,
                          o_ref, op_ref, meta_ref, cnt_ref, tab_ref, carry, filled):
    i = pl.program_id(0)

    @pl.when(i == 0)
    def _():
        carry[...] = jnp.zeros_like(carry)
        tab_ref[...] = jnp.zeros_like(tab_ref)
        for e in range(N_EXPERTS):
            filled[e] = 0

    r = ALPHA * x_ref[...] + _dot(y_ref[...], w_ref[...])
    x3 = _layernorm(r, g_ref[...], b_ref[...])
    o_ref[...] = x3
    op_ref[...] = pltpu.pack_elementwise([x3[:, :D // 2], x3[:, D // 2:]], packed_dtype=BF16)

    lane = lax.broadcasted_iota(jnp.int32, (TM, LANES), 1).astype(F32)
    logits = jnp.where(lane < N_EXPERTS, _dot(x3.astype(BF16), wr_ref[...]), -jnp.inf)
    m1 = jnp.max(logits, axis=-1, keepdims=True)
    i1 = jnp.min(jnp.where(logits == m1, lane, float(LANES)), axis=-1, keepdims=True)
    rest = jnp.where(lane == i1, -jnp.inf, logits)
    m2 = jnp.max(rest, axis=-1, keepdims=True)
    i2 = jnp.min(jnp.where(rest == m2, lane, float(LANES)), axis=-1, keepdims=True)
    e2 = jnp.exp(m2 - m1)
    tot = 1.0 + e2
    w1 = 1.0 / tot
    w2 = e2 / tot

    sel1 = lane == i1
    sel2 = lane == i2
    onehot = jnp.where(sel1 | sel2, 1.0, 0.0)
    in_tile = _dot(tri_ref[...], onehot.astype(BF16))
    carry[...] = carry[...] + jnp.sum(onehot, axis=0, keepdims=True)
    cnt_ref[...] = carry[...]

    meta = jnp.where(lane == 0.0, i1, 0.0)
    meta = jnp.where(lane == 1.0, i2, meta)
    meta = jnp.where(lane == 2.0, w1, meta)
    meta = jnp.where(lane == 3.0, w2, meta)
    meta_ref[...] = meta

    token = (i * TM + lax.broadcasted_iota(jnp.int32, (TM, 1), 0)).astype(F32)
    digit_hi = jnp.floor(token * (1.0 / 256.0))
    rhs = jnp.where(lane == 0.0, digit_hi, jnp.where(lane == 1.0, token - 256.0 * digit_hi, 0.0))
    rhs = jnp.where(lane == i2 + float(SUBLANES), 1.0, rhs).astype(BF16)
    place = lax.broadcasted_iota(jnp.int32, (LANES, TM), 0).astype(F32)
    rank_rows = jnp.where(onehot > 0.0, in_tile, -1.0).T

    def entries(e, part):
        hit = jnp.where(rank_rows[e:e + 1, :] == place + float(part * LANES), 1.0, 0.0).astype(BF16)
        got = _dot(hit, rhs)
        return 256.0 * got[:, 0:1] + got[:, 1:2] + float(NT) * got[:, SUBLANES + e:SUBLANES + e + 1]

    first = [entries(e, 0) for e in range(N_EXPERTS)]
    tile_counts = jnp.sum(onehot, axis=0, keepdims=True)
    counts = [jnp.sum(tile_counts[:, e:e + 1]).astype(jnp.int32) for e in range(N_EXPERTS)]
    for e in range(N_EXPERTS):
        start = filled[e]
        filled[e] = start + counts[e]
        tab_ref[pl.ds(start, LANES), e:e + 1] = first[e]
        for part in range(1, TM // LANES):
            @pl.when(counts[e] > part * LANES)
            def _():
                tab_ref[pl.ds(start + part * LANES, LANES), e:e + 1] = entries(e, part)


def _out_ln_router(x, y, w, g, b, wr, tri):
    c2 = lambda i: (0, 0)
    return pl.pallas_call(
        _out_ln_router_kernel,
        grid=(NT // TM,),
        in_specs=[
            pl.BlockSpec((TM, D), lambda i: (i, 0)),
            pl.BlockSpec((TM, D), lambda i: (i, 0)),
            pl.BlockSpec((D, D), c2),
            pl.BlockSpec((1, D), c2),
            pl.BlockSpec((1, D), c2),
            pl.BlockSpec((D, LANES), c2),
            pl.BlockSpec((TM, TM), c2),
        ],
        out_specs=[
            pl.BlockSpec((TM, D), lambda i: (i, 0)),
            pl.BlockSpec((TM, D // 2), lambda i: (i, 0)),
            pl.BlockSpec((TM, LANES), lambda i: (i, 0)),
            pl.BlockSpec((1, LANES), c2),
            pl.BlockSpec((MOE_CAP + TM, LANES), c2),
        ],
        out_shape=[
            jax.ShapeDtypeStruct((NT, D), F32),
            jax.ShapeDtypeStruct((NT, D // 2), U32),
            jax.ShapeDtypeStruct((NT, LANES), F32),
            jax.ShapeDtypeStruct((1, LANES), F32),
            jax.ShapeDtypeStruct((MOE_CAP + TM, LANES), F32),
        ],
        scratch_shapes=[pltpu.VMEM((1, LANES), F32), pltpu.SMEM((N_EXPERTS,), jnp.int32)],
        compiler_params=_params(("arbitrary",)),
        name="out_ln_router",
    )(x, y, w, g, b, wr, tri)


def _moe_ffn_kernel(te_ref, nu_ref, tb_ref, gnext_ref, gcur_ref, sprev_ref, scur_ref, xp_ref,
                    w1_ref, w3_ref, w2_ref, out_hbm, stage, yacc, xb_scr, sem_s):
    del tb_ref
    i = pl.program_id(0)
    j = pl.program_id(1)
    used = i < nu_ref[0]
    slot = i % 2
    other = 1 - slot
    rps = MOE_ROWS_PER_STEP

    def gather_rows(tab_ref, buf, part):
        for r in range(rps):
            stage[buf, part, pl.ds(r, 1), :] = xp_ref[pl.ds(tab_ref[part * rps + r], 1), :]

    def scatter(buf, r, dst):
        return pltpu.make_async_copy(yacc.at[buf, pl.ds(r, 1)], out_hbm.at[pl.ds(dst, 1)], sem_s)

    def wait_scatters(n):
        for _ in range(n):
            scatter(0, 0, 0).wait()

    def issue_neighbours():
        gather_rows(gnext_ref, other, j)
        for r in range(rps):
            rr = j * rps + r
            scatter(other, rr, sprev_ref[rr]).start()

    @pl.when(j == 0)
    def _():
        @pl.when(i == 0)
        def _():
            yacc[1] = jnp.zeros((TMM, D), F32)
            for part in range(MOE_NFF):
                gather_rows(gcur_ref, 0, part)

        @pl.when(i > 0)
        def _():
            wait_scatters(TMM)

        words = stage[slot].reshape(TMM, D // 2)
        for half in range(2):
            xb_scr[:, half * (D // 2):(half + 1) * (D // 2)] = pltpu.unpack_elementwise(
                words, index=half, packed_dtype=BF16, unpacked_dtype=F32).astype(BF16)
        yacc[slot] = jnp.zeros((TMM, D), F32)

    @pl.when(used)
    def _():
        issue_neighbours()
        xb = xb_scr[...]
        hmid = _silu(_dot(xb, w1_ref[...])) * _dot(xb, w3_ref[...])
        yacc[slot] += _dot(hmid.astype(BF16), w2_ref[...])

    @pl.when(jnp.logical_not(used))
    def _():
        issue_neighbours()

    @pl.when((i == N_MOE_TILES - 1) & (j == MOE_NFF - 1))
    def _():
        for r in range(TMM):
            scatter(slot, r, scur_ref[r]).start()
        wait_scatters(2 * TMM)


def _moe_ffn(tile_expert, n_used, tab_block, gsrc, sdst, xp, w1, w3, w2):
    nff = MOE_NFF

    def wcol(i, j, te, nu, tb):
        return (te[i], 0, jnp.where(i < nu[0], j, nff - 1))

    def wrow(i, j, te, nu, tb):
        return (te[i], jnp.where(i < nu[0], j, nff - 1), 0)

    smem = functools.partial(pl.BlockSpec, (MOE_TAB,), memory_space=pltpu.SMEM)
    grid_spec = pltpu.PrefetchScalarGridSpec(
        num_scalar_prefetch=3,
        grid=(N_MOE_TILES, nff),
        in_specs=[
            smem(lambda i, j, te, nu, tb: (tb[i + 2],)),
            smem(lambda i, j, te, nu, tb: (tb[i + 1],)),
            smem(lambda i, j, te, nu, tb: (tb[i],)),
            smem(lambda i, j, te, nu, tb: (tb[i + 1],)),
            pl.BlockSpec((NT, D // 2), lambda i, j, te, nu, tb: (0, 0), pipeline_mode=pl.Buffered(1)),
            pl.BlockSpec((None, D, TFF), wcol),
            pl.BlockSpec((None, D, TFF), wcol),
            pl.BlockSpec((None, TFF, D), wrow),
        ],
        out_specs=pl.BlockSpec(memory_space=pl.ANY),
        scratch_shapes=[
            pltpu.VMEM((2, MOE_NFF, MOE_ROWS_PER_STEP, D // 2), U32),
            pltpu.VMEM((2, TMM, D), F32),
            pltpu.VMEM((TMM, D), BF16),
            pltpu.SemaphoreType.DMA(()),
        ],
    )
    return pl.pallas_call(
        _moe_ffn_kernel,
        grid_spec=grid_spec,
        out_shape=jax.ShapeDtypeStruct((MOE_OUT_ROWS, D), F32),
        compiler_params=_params(("arbitrary", "arbitrary")),
        name="moe_ffn",
    )(tile_expert, n_used, tab_block, gsrc, gsrc, sdst, sdst, xp, w1, w3, w2)


def _combine_kernel(x_ref, meta_ref, y0_ref, y1_ref, g_ref, b_ref, o_ref):
    meta = meta_ref[...]
    moe = meta[:, 2:3] * y0_ref[...] + meta[:, 3:4] * y1_ref[...]
    o_ref[...] = _layernorm(ALPHA * x_ref[...] + moe, g_ref[...], b_ref[...])


def _combine(x, meta, ys, g, b):
    c2 = lambda i: (0, 0)
    return pl.pallas_call(
        _combine_kernel,
        grid=(NT // TM,),
        in_specs=[
            pl.BlockSpec((TM, D), lambda i: (i, 0)),
            pl.BlockSpec((TM, LANES), lambda i: (i, 0)),
            pl.BlockSpec((TM, D), lambda i: (i, 0)),
            pl.BlockSpec((TM, D), lambda i: (i + NT // TM, 0)),
            pl.BlockSpec((1, D), c2),
            pl.BlockSpec((1, D), c2),
        ],
        out_specs=pl.BlockSpec((TM, D), lambda i: (i, 0)),
        out_shape=jax.ShapeDtypeStruct((NT, D), F32),
        compiler_params=_params(("parallel",)),
        name="moe_combine",
    )(x, meta, ys, ys, g, b)


def _pad_cols(w, n):
    return jnp.pad(w, ((0, 0), (0, n - w.shape[1])))


def kernel(x_prompt, x_sample, state_hgrn, state_gla, state_mlstm_C, state_mlstm_n, state_mlstm_m,
           state_mlstm_conv, w_in_even, hg_lower_bounds, w_gk, b_gk, gn_hg, gn_gla, w_out_even,
           w1_dense, w3_dense, w2_dense, w_in_odd, b_gate_odd, conv_w, conv_b, hn_w, w_out_odd,
           w_router, w1_moe, w3_moe, w2_moe, ln1_g, ln1_b, ln2_g, ln2_b):
    assert x_prompt.shape == (BATCH, SEQ, D) and x_sample.shape == (NS, 1, D)
    assert w_in_even.shape[0] == 1 and w_in_odd.shape[0] == 1 and hg_lower_bounds.shape[0] == 2
    masks = jnp.asarray(_gla_masks(), F32)
    tri_cs = jnp.asarray(_tri(CS, False), BF16)
    tri_tm = jnp.asarray(_tri(TM, True), BF16)
    row = lambda a: a.reshape(1, -1)

    x0 = x_prompt.reshape(NP, D)
    x0_tail = jnp.concatenate([x0[LAST_TILE * TM:], x_sample.reshape(NS, D)], axis=0)

    w_even = w_in_even[0].astype(BF16)
    z, zgr = _proj(x0, x0_tail, w_even[:, :EVEN_MAIN], _pad_cols(w_even[:, EVEN_MAIN:], LANES))
    wgk = jnp.pad(w_gk[0].astype(BF16), ((0, LANES - GLA_RANK), (0, 0)))
    lbp = hg_lower_bounds
    zs = z[NP:].reshape(NS // SG, SG, EVEN_MAIN).transpose(0, 2, 1)
    grs = zgr[NP:].reshape(NS // SG, SG, LANES).transpose(0, 2, 1)
    y_s, hg_s, gla_s = _even_sample(z, zs, grs, lbp.T, wgk.T, b_gk[0].reshape(-1, 1),
                                    row(gn_hg[0]), row(gn_gla[0]), state_hgrn[0], state_gla[0])
    y, hg_p, gla_p = _even_prompt(z, zgr, y_s, lbp, wgk, row(b_gk[0]), row(gn_hg[0]), row(gn_gla[0]), tri_cs, masks)
    x1 = _out_ln(x0, x0_tail, y, w_out_even[0].astype(BF16), row(ln1_g[0]), row(ln1_b[0]))
    x2 = _ffn(x1, w1_dense[0].astype(BF16), w3_dense[0].astype(BF16), w2_dense[0].astype(BF16),
              row(ln2_g[0]), row(ln2_b[0]))

    w_odd = w_in_odd[0].astype(BF16)
    zo, zog = _proj(x2, x2[LAST_TILE * TM:], w_odd[:, :ODD_MAIN], _pad_cols(w_odd[:, ODD_MAIN:], LANES))
    bg = jnp.pad(b_gate_odd[0], (0, LANES - 2 * ML_H)).reshape(1, LANES)
    ut = zo[NP:, :D].reshape(NS // SG, SG, D).transpose(0, 2, 1)
    conv_in = state_mlstm_conv[0]
    conv_t = conv_in.reshape(NS // SG, SG, CONV_W - 1, D).transpose(0, 2, 3, 1)
    m_in = jnp.pad(state_mlstm_m[0], ((0, 0), (0, LANES - ML_H)))
    yo_s, c_s, n_s, m_s, conv_s = _odd_sample(
        zo, zog, ut, conv_in.reshape(NS, (CONV_W - 1) * D), conv_t, bg, conv_w[0], conv_w[0].T, row(conv_b[0]), conv_b[0].reshape(-1, 1),
        row(hn_w[0]), state_mlstm_C[0], state_mlstm_n[0], m_in)
    yo, c_p, n_p, m_p, conv_p = _odd_prompt(zo, zog, yo_s, bg, conv_w[0], row(conv_b[0]), row(hn_w[0]), tri_cs)

    wr = _pad_cols(w_router[0].astype(BF16), LANES)
    x3, x3p, meta, cnt, tab = _out_ln_router(
        x2, yo, w_out_odd[0].astype(BF16), row(ln1_g[1]), row(ln1_b[1]), wr, tri_tm)

    counts = cnt[0, :N_EXPERTS].astype(jnp.int32)
    padded = ((counts + TMM - 1) // TMM) * TMM
    ends = jnp.cumsum(padded)
    offsets = ends - padded
    tile = jnp.arange(N_MOE_TILES, dtype=jnp.int32)
    tile_expert = jnp.minimum(jnp.sum((tile * TMM)[:, None] >= ends[None, :], axis=1), N_EXPERTS - 1).astype(jnp.int32)
    n_tiles_used = ends[-1] // TMM
    in_use = tile < n_tiles_used
    local_tile = (tile * TMM - offsets[tile_expert]) // TMM
    blocks_per_expert = MOE_CAP // MOE_TAB
    spill_block = N_EXPERTS * blocks_per_expert
    n_spare = N_MOE_TILES - (2 * NT) // TMM
    entry = tab[:MOE_CAP, :N_EXPERTS].T.astype(jnp.int32)
    local = jnp.arange(MOE_CAP, dtype=jnp.int32)[None, :]
    valid = local < counts[:, None]
    padding = padded - counts
    pad_before = (jnp.cumsum(padding) - padding)[:, None]
    row_in_block = jnp.arange(MOE_TAB, dtype=jnp.int32)
    spare_rows = (2 * NT + TMM + jnp.sum(padding)
                  + jnp.arange(n_spare, dtype=jnp.int32)[:, None] * TMM + row_in_block[None, :])
    sdst = jnp.concatenate([
        jnp.where(valid, entry, 2 * NT + TMM + pad_before + local - counts[:, None]).reshape(-1),
        2 * NT + row_in_block, spare_rows.reshape(-1)])
    gsrc = jnp.concatenate([
        jnp.where(valid, entry - jnp.where(entry >= NT, NT, 0), 0).reshape(-1),
        jnp.zeros(((1 + n_spare) * MOE_TAB,), jnp.int32)])
    own_block = jnp.where(in_use, tile_expert * blocks_per_expert + local_tile, spill_block + 1 + tile - n_tiles_used)
    tab_block = jnp.concatenate([jnp.full((1,), spill_block, jnp.int32), own_block.astype(jnp.int32),
                                 jnp.full((1,), spill_block, jnp.int32)])

    ys = _moe_ffn(tile_expert, n_tiles_used.reshape(1), tab_block, gsrc, sdst, x3p,
                  w1_moe[0].astype(BF16), w3_moe[0].astype(BF16), w2_moe[0].astype(BF16))
    out = _combine(x3, meta, ys, row(ln2_g[1]), row(ln2_b[1]))

    y_prompt = out[:NP].reshape(BATCH, SEQ, D)
    y_sample = out[NP:].reshape(NS, 1, D)
    return (y_prompt, y_sample,
            hg_p.reshape(1, BATCH, HG_H, HG_DK, HG_DV), gla_p.reshape(1, BATCH, GLA_H, GLA_DK, GLA_DV),
            c_p[None], n_p[None], m_p[:, 0, :ML_H][None], conv_p[None],
            hg_s[None], gla_s[None], c_s[None], n_s[None], m_s[:, :ML_H][None], conv_s.reshape(1, NS, CONV_W - 1, D))
```

```python
import functools
import math

import jax
import jax.numpy as jnp
import numpy as np
from jax import lax
from jax.experimental import pallas as pl
from jax.experimental.pallas import tpu as pltpu

F32 = jnp.float32
BF16 = jnp.bfloat16
U32 = jnp.uint32

D = 1024
BATCH = 8
SEQ = 2048
DEC_BATCH = 128
NP = BATCH * SEQ
NS = DEC_BATCH
NT = NP + NS
HG_H, HG_DK, HG_DV = 4, 128, 128
GLA_H, GLA_DK, GLA_DV = 4, 64, 128
GLA_RANK = 16
GLA_GATE_NORM = 16.0
ML_H, ML_DK, ML_DV = 4, 128, 256
CONV_W = 4
D_FF_DENSE = 2816
D_FF_EXPERT = 3584
N_EXPERTS = 8
EPS = 1e-5
DEPTH = 2
ALPHA = (2.0 * DEPTH) ** 0.25
EVEN_MAIN = 3584
ODD_MAIN = 3072

LANES = 128
SUBLANES = 8
VMEM_LIMIT = 56 * 1024 * 1024

TM = 384
LAST_TILE = NT // TM - 1
CS = 128
NCHUNK = SEQ // CS
SG = 16
TMM = 512
TFF = 896
MOE_TAB = 512
MOE_CAP = -(-NT // MOE_TAB) * MOE_TAB
MOE_NFF = D_FF_EXPERT // TFF
MOE_ROWS_PER_STEP = TMM // MOE_NFF
N_MOE_TILES = -(-(2 * NT + N_EXPERTS * (TMM - 1)) // TMM)
MOE_SLOTS = N_MOE_TILES * TMM
MOE_OUT_ROWS = MOE_SLOTS + TMM
N_LEVELS = int(math.log2(CS))

assert NT % TM == 0 and NP % CS == 0 and NS % SG == 0 and D_FF_EXPERT % TFF == 0 and TMM % MOE_NFF == 0
assert TMM == MOE_TAB and NS == CS and LAST_TILE * TM <= NP


def _params(sem, limit=VMEM_LIMIT):
    return pltpu.CompilerParams(dimension_semantics=sem, vmem_limit_bytes=limit)


def _dot(a, b):
    return jnp.dot(a, b, preferred_element_type=F32)


def _dot_nt(a, b):
    return lax.dot_general(a, b, (((1,), (1,)), ((), ())), preferred_element_type=F32)


def _dot_tn(a, b):
    return lax.dot_general(a, b, (((0,), (0,)), ((), ())), preferred_element_type=F32)


def _split3(x):
    hi = x.astype(BF16)
    r1 = x - hi.astype(F32)
    mid = r1.astype(BF16)
    lo = (r1 - mid.astype(F32)).astype(BF16)
    return hi, mid, lo


def _dot_sel(sel, x):
    hi, mid, lo = _split3(x)
    return _dot(sel, hi) + _dot(sel, mid) + _dot(sel, lo)


def _sigmoid(x):
    return jax.nn.sigmoid(x)


def _silu(x):
    return x * jax.nn.sigmoid(x)


def _log_sigmoid(x):
    return jnp.minimum(x, 0.0) - jnp.log(1.0 + jnp.exp(-jnp.abs(x)))


def _layernorm(r, g, b):
    mu = jnp.mean(r, axis=-1, keepdims=True)
    c = r - mu
    var = jnp.mean(c * c, axis=-1, keepdims=True)
    return c * lax.rsqrt(var + EPS) * g + b


def _gla_masks():
    masks = np.zeros((N_LEVELS + 1, CS, CS), np.float32)
    for t in range(CS):
        for l in range(N_LEVELS):
            half = 1 << l
            start = (t // (2 * half)) * (2 * half)
            mid = start + half
            if t >= mid:
                masks[l, t, start:mid] = 1.0
        masks[N_LEVELS, t, t] = 1.0
    return masks


def _tri(n, strict):
    return np.tril(np.ones((n, n), np.float32), -1 if strict else 0)


def _token_tile(x_ref, tail_ref):
    return jnp.where(pl.program_id(0) == LAST_TILE, tail_ref[...], x_ref[...])


def _main_tile(i):
    return (jnp.minimum(i, LAST_TILE - 1), 0)


def _proj_kernel(x_ref, tail_ref, wa_ref, wb_ref, oa_ref, ob_ref):
    xb = _token_tile(x_ref, tail_ref).astype(BF16)
    oa_ref[...] = _dot(xb, wa_ref[...])
    ob_ref[...] = _dot(xb, wb_ref[...])


def _proj(x, tail, wa, wb):
    na, nb = wa.shape[1], wb.shape[1]
    return pl.pallas_call(
        _proj_kernel,
        grid=(NT // TM,),
        in_specs=[
            pl.BlockSpec((TM, D), _main_tile),
            pl.BlockSpec((TM, D), lambda i: (0, 0)),
            pl.BlockSpec((D, na), lambda i: (0, 0)),
            pl.BlockSpec((D, nb), lambda i: (0, 0)),
        ],
        out_specs=[
            pl.BlockSpec((TM, na), lambda i: (i, 0)),
            pl.BlockSpec((TM, nb), lambda i: (i, 0)),
        ],
        out_shape=[jax.ShapeDtypeStruct((NT, na), F32), jax.ShapeDtypeStruct((NT, nb), F32)],
        compiler_params=_params(("parallel",)),
        name="proj",
    )(x, tail, wa, wb)


def _rms_gate(o, gate, w):
    o = o * lax.rsqrt(jnp.mean(o * o, axis=-1, keepdims=True) + EPS) * w
    return o * _silu(gate)


def _level_decays(g, bc):
    width = g.shape[1]
    ng = CS // SUBLANES
    shape3 = (ng, SUBLANES, width)
    bc3 = bc.reshape(shape3)
    sub = lax.broadcasted_iota(jnp.int32, shape3, 1)

    def group_row(s):
        return jnp.broadcast_to(bc3[:, s:s + 1, :], shape3)

    last = group_row(SUBLANES - 1)
    refs = [None,
            jnp.where(sub < 4, group_row(1), group_row(5)),
            group_row(3)]
    for l in range(3, N_LEVELS):
        per_block = 1 << (l - 2)
        grouped = last.reshape(ng // per_block, per_block, SUBLANES, width)
        ref = jnp.broadcast_to(grouped[:, per_block // 2 - 1:per_block // 2], grouped.shape)
        refs.append(ref.reshape(shape3))
    decays = [jnp.exp(jnp.where((sub & 1) == 1, g.reshape(shape3), 0.0))]
    decays += [jnp.exp(-jnp.abs(bc3 - ref)) for ref in refs[1:]]
    to_end = jnp.exp(jnp.broadcast_to(last[ng - 1:ng], shape3) - bc3)
    return [d.reshape(CS, width) for d in decays], to_end.reshape(CS, width)


def _gla_chunk(q, k, v, g, st_ref, tri, masks_ref, heads, dk, dv):
    bc = _dot_sel(tri, g)
    z_levels, z_end = _level_decays(g, bc)
    z_cum = jnp.exp(bc)
    st = st_ref[...]
    outs = []
    for h in range(heads):
        ks = slice(h * dk, (h + 1) * dk)
        vs = slice(h * dv, (h + 1) * dv)
        qh, kh = q[:, ks], k[:, ks]
        vh = v[:, vs].astype(BF16)
        scores = _dot_nt(qh.astype(BF16), kh.astype(BF16)) * masks_ref[N_LEVELS]
        for l in range(N_LEVELS):
            zl = z_levels[l][:, ks]
            scores = scores + _dot_nt((qh * zl).astype(BF16), (kh * zl).astype(BF16)) * masks_ref[l]
        o = _dot(scores.astype(BF16), vh)
        o = o + _dot_nt((qh * z_cum[:, ks]).astype(BF16), st[:, ks].astype(BF16))
        outs.append(o)
        upd = _dot_tn(vh, (kh * z_end[:, ks]).astype(BF16))
        st_ref[:, ks] = st[:, ks] * z_cum[CS - 1:CS, ks] + upd
    return outs


def _even_prompt_kernel(z_ref, zgr_ref, ys_ref, lbp_ref, wgk_ref, bgk_ref, gnh_ref, gng_ref, tri_ref, masks_ref,
                        y_ref, shg_ref, sgla_ref, st_hg, st_gla):
    b = pl.program_id(0)
    c = pl.program_id(1)

    @pl.when((b < BATCH) & (c == 0))
    def _():
        st_hg[...] = jnp.zeros_like(st_hg)
        st_gla[...] = jnp.zeros_like(st_gla)

    @pl.when(b < BATCH)
    def _():
        tri = tri_ref[...]
        p = lbp_ref[...]
        pe = jnp.exp(p - jnp.max(p, axis=0, keepdims=True))
        lb = pe[0:1] / jnp.sum(pe, axis=0, keepdims=True)

        z = z_ref[...]
        hq, hf, hi, hg = z[:, 0:512], z[:, 512:1024], z[:, 1024:1536], z[:, 1536:2048]
        gq, gk, gv, gg = z[:, 2048:2304], z[:, 2304:2560], z[:, 2560:3072], z[:, 3072:3584]
        f = lb + (1.0 - lb) * _sigmoid(hf)
        k_hg = (1.0 - lb) * _sigmoid(-hf)
        o_hg = _gla_chunk(_silu(hq), k_hg, hi, jnp.log(f), st_hg, tri, masks_ref, HG_H, HG_DK, HG_DV)

        la = _log_sigmoid(_dot(zgr_ref[...].astype(BF16), wgk_ref[...]) + bgk_ref[...]) / GLA_GATE_NORM
        o_gla = _gla_chunk(gq * GLA_DK ** -0.5, gk, gv, la, st_gla, tri, masks_ref, GLA_H, GLA_DK, GLA_DV)

        for h in range(HG_H):
            cs = slice(h * 128, (h + 1) * 128)
            y_ref[:, cs] = _rms_gate(o_hg[h], hg[:, cs], gnh_ref[...]).astype(BF16)
        for h in range(GLA_H):
            cs = slice(h * 128, (h + 1) * 128)
            y_ref[:, 512 + h * 128:512 + (h + 1) * 128] = _rms_gate(o_gla[h], gg[:, cs], gng_ref[...]).astype(BF16)

    @pl.when((b < BATCH) & (c == NCHUNK - 1))
    def _():
        shg_ref[0] = st_hg[...].T
        sgla_ref[0] = st_gla[...].T

    @pl.when((b == BATCH) & (c == 0))
    def _():
        y_ref[...] = ys_ref[...]


def _chunk_rows(b, c):
    return (jnp.minimum(b * NCHUNK + c, NP // CS), 0)


def _per_sequence(b, c):
    return (jnp.minimum(b, BATCH - 1), 0, 0)


def _even_prompt(z, zgr, y_sample, lbp, wgk, bgk, gnh, gng, tri, masks):
    const2 = lambda b, c: (0, 0)
    return pl.pallas_call(
        _even_prompt_kernel,
        grid=(BATCH + 1, NCHUNK),
        in_specs=[
            pl.BlockSpec((CS, EVEN_MAIN), _chunk_rows),
            pl.BlockSpec((CS, LANES), _chunk_rows),
            pl.BlockSpec((NS, D), const2),
            pl.BlockSpec(lbp.shape, const2),
            pl.BlockSpec(wgk.shape, const2),
            pl.BlockSpec(bgk.shape, const2),
            pl.BlockSpec(gnh.shape, const2),
            pl.BlockSpec(gng.shape, const2),
            pl.BlockSpec(tri.shape, const2),
            pl.BlockSpec(masks.shape, lambda b, c: (0, 0, 0)),
        ],
        out_specs=[
            pl.BlockSpec((CS, D), _chunk_rows),
            pl.BlockSpec((1, HG_H * HG_DK, HG_DV), _per_sequence),
            pl.BlockSpec((1, GLA_H * GLA_DK, GLA_DV), _per_sequence),
        ],
        out_shape=[
            jax.ShapeDtypeStruct((NT, D), BF16),
            jax.ShapeDtypeStruct((BATCH, HG_H * HG_DK, HG_DV), F32),
            jax.ShapeDtypeStruct((BATCH, GLA_H * GLA_DK, GLA_DV), F32),
        ],
        scratch_shapes=[pltpu.VMEM((HG_DV, HG_H * HG_DK), F32), pltpu.VMEM((GLA_DV, GLA_H * GLA_DK), F32)],
        compiler_params=_params(("arbitrary", "arbitrary")),
        name="even_prompt",
    )(z, zgr, y_sample, lbp, wgk, bgk, gnh, gng, tri, masks)


def _even_sample_kernel(zr_ref, zt_ref, grt_ref, lbpt_ref, wgkt_ref, bgkt_ref, gnh_ref, gng_ref,
                        shg_ref, sgla_ref, y_ref, shg_out, sgla_out, o_scr):
    zt = zt_ref[0]
    hq_t, hf_t = zt[0:512], zt[512:1024]
    gq_t, gk_t = zt[2048:2304], zt[2304:2560]
    pt = lbpt_ref[...]
    pe = jnp.exp(pt - jnp.max(pt, axis=1, keepdims=True))
    lb = pe[:, 0:1] / jnp.sum(pe, axis=1, keepdims=True)
    a_hg = jnp.exp(jnp.log(lb + (1.0 - lb) * _sigmoid(hf_t)))
    k_hg = (1.0 - lb) * _sigmoid(-hf_t)
    q_hg = _silu(hq_t)
    la = _log_sigmoid(_dot(wgkt_ref[...], grt_ref[0].astype(BF16)) + bgkt_ref[...]) / GLA_GATE_NORM
    a_gla = jnp.exp(la)
    q_gla = gq_t * GLA_DK ** -0.5
    zr = zr_ref[...]
    hi, hg = zr[:, 1024:1536], zr[:, 1536:2048]
    gv, gg = zr[:, 2560:3072], zr[:, 3072:3584]

    for j in range(SG):
        for h in range(HG_H):
            ks = slice(h * HG_DK, (h + 1) * HG_DK)
            s_new = a_hg[ks, j:j + 1] * shg_ref[j, h] + k_hg[ks, j:j + 1] * hi[j:j + 1, h * 128:(h + 1) * 128]
            shg_out[j, h] = s_new
            o_scr[j:j + 1, h * 128:(h + 1) * 128] = jnp.sum(q_hg[ks, j:j + 1] * s_new, axis=0, keepdims=True)
        for h in range(GLA_H):
            ks = slice(h * GLA_DK, (h + 1) * GLA_DK)
            s_new = a_gla[ks, j:j + 1] * sgla_ref[j, h] + gk_t[ks, j:j + 1] * gv[j:j + 1, h * 128:(h + 1) * 128]
            sgla_out[j, h] = s_new
            o_scr[j:j + 1, 512 + h * 128:512 + (h + 1) * 128] = jnp.sum(
                q_gla[ks, j:j + 1] * s_new, axis=0, keepdims=True)

    o = o_scr[...]
    for h in range(HG_H):
        cs = slice(h * 128, (h + 1) * 128)
        y_ref[:, cs] = _rms_gate(o[:, cs], hg[:, cs], gnh_ref[...]).astype(BF16)
    for h in range(GLA_H):
        cs = slice(512 + h * 128, 512 + (h + 1) * 128)
        y_ref[:, cs] = _rms_gate(o[:, cs], gg[:, h * 128:(h + 1) * 128], gng_ref[...]).astype(BF16)


def _even_sample(z, zt3, grt3, lbpt, wgkt, bgkt, gnh, gng, s_hg, s_gla):
    c2 = lambda g: (0, 0)
    return pl.pallas_call(
        _even_sample_kernel,
        grid=(NS // SG,),
        in_specs=[
            pl.BlockSpec((SG, EVEN_MAIN), lambda g: (NP // SG + g, 0)),
            pl.BlockSpec((1, EVEN_MAIN, SG), lambda g: (g, 0, 0)),
            pl.BlockSpec((1, LANES, SG), lambda g: (g, 0, 0)),
            pl.BlockSpec(lbpt.shape, c2),
            pl.BlockSpec(wgkt.shape, c2),
            pl.BlockSpec(bgkt.shape, c2),
            pl.BlockSpec(gnh.shape, c2),
            pl.BlockSpec(gng.shape, c2),
            pl.BlockSpec((SG, HG_H, HG_DK, HG_DV), lambda g: (g, 0, 0, 0)),
            pl.BlockSpec((SG, GLA_H, GLA_DK, GLA_DV), lambda g: (g, 0, 0, 0)),
        ],
        out_specs=[
            pl.BlockSpec((SG, D), lambda g: (g, 0)),
            pl.BlockSpec((SG, HG_H, HG_DK, HG_DV), lambda g: (g, 0, 0, 0)),
            pl.BlockSpec((SG, GLA_H, GLA_DK, GLA_DV), lambda g: (g, 0, 0, 0)),
        ],
        out_shape=[
            jax.ShapeDtypeStruct((NS, D), BF16),
            jax.ShapeDtypeStruct((NS, HG_H, HG_DK, HG_DV), F32),
            jax.ShapeDtypeStruct((NS, GLA_H, GLA_DK, GLA_DV), F32),
        ],
        scratch_shapes=[pltpu.VMEM((SG, D), F32)],
        compiler_params=_params(("parallel",)),
        name="even_sample",
    )(z, zt3, grt3, lbpt, wgkt, bgkt, gnh, gng, s_hg, s_gla)


FF_SPLIT = 2


def _ffn_kernel(x_ref, tail_ref, y_ref, wo_ref, g1_ref, b1_ref, w1_ref, w3_ref, w2_ref, g_ref, b_ref, o_ref):
    x = _layernorm(ALPHA * _token_tile(x_ref, tail_ref) + _dot(y_ref[...], wo_ref[...]), g1_ref[...], b1_ref[...])
    xb = x.astype(BF16)
    step = D_FF_DENSE // FF_SPLIT
    acc = ALPHA * x
    for s in range(FF_SPLIT):
        cs = slice(s * step, (s + 1) * step)
        hmid = _silu(_dot(xb, w1_ref[:, cs])) * _dot(xb, w3_ref[:, cs])
        acc = acc + _dot(hmid.astype(BF16), w2_ref[cs, :])
    o_ref[...] = _layernorm(acc, g_ref[...], b_ref[...])


def _ffn(x, tail, y, wo, g1, b1, w1, w3, w2, g, b):
    c2 = lambda i: (0, 0)
    one = pl.Buffered(1)
    return pl.pallas_call(
        _ffn_kernel,
        grid=(NT // TM,),
        in_specs=[
            pl.BlockSpec((TM, D), _main_tile),
            pl.BlockSpec((TM, D), c2),
            pl.BlockSpec((TM, D), lambda i: (i, 0)),
            pl.BlockSpec((D, D), c2, pipeline_mode=one),
            pl.BlockSpec((1, D), c2),
            pl.BlockSpec((1, D), c2),
            pl.BlockSpec((D, D_FF_DENSE), c2, pipeline_mode=one),
            pl.BlockSpec((D, D_FF_DENSE), c2, pipeline_mode=one),
            pl.BlockSpec((D_FF_DENSE, D), c2, pipeline_mode=one),
            pl.BlockSpec((1, D), c2),
            pl.BlockSpec((1, D), c2),
        ],
        out_specs=pl.BlockSpec((TM, D), lambda i: (i, 0)),
        out_shape=jax.ShapeDtypeStruct((NT, D), F32),
        compiler_params=_params(("parallel",)),
        name="ffn_dense",
    )(x, tail, y, wo, g1, b1, w1, w3, w2, g, b)


def _mh_norm_gate(hh, o_pre, w):
    mu = jnp.mean(hh, axis=-1, keepdims=True)
    c = hh - mu
    var = jnp.mean(c * c, axis=-1, keepdims=True)
    return _sigmoid(o_pre) * (c * lax.rsqrt(var + EPS) * w)


def _odd_prompt_kernel(z_ref, zg_ref, ys_ref, bg_ref, cw_ref, cb_ref, hnw_ref, tri_ref,
                       y_ref, c_out, n_out, m_out, conv_out,
                       c_scr, n_scr, m_scr, u_scr):
    b = pl.program_id(0)
    c = pl.program_id(1)

    @pl.when((b < BATCH) & (c == 0))
    def _():
        c_scr[...] = jnp.zeros_like(c_scr)
        n_scr[...] = jnp.zeros_like(n_scr)
        m_scr[...] = jnp.zeros_like(m_scr)
        u_scr[0:8, :] = jnp.zeros((8, D), F32)

    @pl.when((b == BATCH) & (c == 0))
    def _():
        y_ref[...] = ys_ref[...]

    @pl.when(b < BATCH)
    def _():
        row = lax.broadcasted_iota(jnp.int32, (CS, CS), 0)
        col = lax.broadcasted_iota(jnp.int32, (CS, CS), 1)
        causal = col <= row

        u_scr[8:8 + CS, :] = z_ref[:, 0:D]
        uc = cb_ref[...]
        for j in range(CONV_W):
            uc = uc + u_scr[5 + j:5 + j + CS, :] * cw_ref[j:j + 1, :]
        tail = u_scr[CS:CS + 8, :]
        u_scr[0:8, :] = tail
        act = _silu(uc)
        q = act[:, 0:512] * ML_DK ** -0.5
        k = act[:, 512:1024]
        v = z_ref[:, D:2 * D]
        o_pre = z_ref[:, 2 * D:3 * D]

        gates = zg_ref[...] + bg_ref[...]
        lf = _log_sigmoid(gates)
        bcum = _dot_sel(tri_ref[...], lf)
        bcum_t = bcum.T
        gates_t = gates.T
        m_all = m_scr[...]

        for h in range(ML_H):
            ks = slice(h * ML_DK, (h + 1) * ML_DK)
            vs = slice(h * ML_DV, (h + 1) * ML_DV)
            qh, kh = q[:, ks], k[:, ks]
            vh = v[:, vs].astype(BF16)
            b_col = bcum[:, 4 + h:5 + h]
            b_row = bcum_t[4 + h:5 + h, :]
            i_col = gates[:, h:h + 1]
            i_row = gates_t[h:h + 1, :]
            m_prev = m_all[:, h:h + 1]
            log_d = jnp.where(causal, b_col - b_row + i_row, -jnp.inf)
            log_prev = b_col + m_prev
            m_t = jnp.maximum(jnp.max(log_d, axis=-1, keepdims=True), log_prev)
            d = jnp.exp(log_d - m_t)
            w_prev = jnp.exp(log_prev - m_t)
            scores = _dot_nt(qh.astype(BF16), kh.astype(BF16)) * d
            c_h = c_scr[h]
            n_h = n_scr[h:h + 1, :]
            num = _dot(scores.astype(BF16), vh) + w_prev * _dot(qh.astype(BF16), c_h.astype(BF16))
            den = jnp.sum(scores, axis=-1, keepdims=True) + w_prev * jnp.sum(qh * n_h, axis=-1, keepdims=True)
            hh = num / jnp.maximum(jnp.abs(den), jnp.exp(-m_t))
            m_new = m_t[CS - 1:CS, :]
            b_last = b_col[CS - 1:CS, :]
            w_c = jnp.exp(b_last + m_prev - m_new)
            w_s = jnp.exp(b_last - b_col + i_col - m_new)
            kw = kh * w_s
            c_scr[h] = w_c * c_h + _dot_tn(kw.astype(BF16), vh)
            n_scr[h:h + 1, :] = w_c * n_h + jnp.sum(kw, axis=0, keepdims=True)
            m_scr[:, h:h + 1] = m_new
            y_ref[:, vs] = _mh_norm_gate(hh, o_pre[:, vs], hnw_ref[:, vs]).astype(BF16)

        @pl.when(c == NCHUNK - 1)
        def _():
            c_out[0] = c_scr[...]
            n_out[0] = n_scr[0:ML_H, :]
            m_out[0] = m_scr[...]
            conv_out[0] = tail[8 - (CONV_W - 1):8, :]


def _odd_prompt(z, zg, y_sample, bg, cw, cb, hnw, tri):
    c2 = lambda b, c: (0, 0)
    per_sequence4 = lambda b, c: (jnp.minimum(b, BATCH - 1), 0, 0, 0)
    return pl.pallas_call(
        _odd_prompt_kernel,
        grid=(BATCH + 1, NCHUNK),
        in_specs=[
            pl.BlockSpec((CS, ODD_MAIN), _chunk_rows),
            pl.BlockSpec((CS, LANES), _chunk_rows),
            pl.BlockSpec((NS, D), c2),
            pl.BlockSpec((1, LANES), c2),
            pl.BlockSpec((CONV_W, D), c2),
            pl.BlockSpec((1, D), c2),
            pl.BlockSpec((1, D), c2),
            pl.BlockSpec((CS, CS), c2),
        ],
        out_specs=[
            pl.BlockSpec((CS, D), _chunk_rows),
            pl.BlockSpec((1, ML_H, ML_DK, ML_DV), per_sequence4),
            pl.BlockSpec((1, ML_H, ML_DK), _per_sequence),
            pl.BlockSpec((1, 1, LANES), _per_sequence),
            pl.BlockSpec((1, CONV_W - 1, D), _per_sequence),
        ],
        out_shape=[
            jax.ShapeDtypeStruct((NT, D), BF16),
            jax.ShapeDtypeStruct((BATCH, ML_H, ML_DK, ML_DV), F32),
            jax.ShapeDtypeStruct((BATCH, ML_H, ML_DK), F32),
            jax.ShapeDtypeStruct((BATCH, 1, LANES), F32),
            jax.ShapeDtypeStruct((BATCH, CONV_W - 1, D), F32),
        ],
        scratch_shapes=[
            pltpu.VMEM((ML_H, ML_DK, ML_DV), F32),
            pltpu.VMEM((8, ML_DK), F32),
            pltpu.VMEM((1, LANES), F32),
            pltpu.VMEM((CS + 8, D), F32),
        ],
        compiler_params=_params(("arbitrary", "arbitrary")),
        name="odd_prompt",
    )(z, zg, y_sample, bg, cw, cb, hnw, tri)


def _odd_sample_kernel(zr_ref, zg_ref, ut_ref, conv_ref, convt_ref, bg_ref, cw_ref, cwt_ref, cb_ref, cbt_ref,
                       hnw_ref, c_ref, n_ref, m_ref,
                       y_ref, c_out, n_out, m_out, conv_out, h_scr):
    zr = zr_ref[...]
    u = zr[:, 0:D]
    v = zr[:, D:2 * D]
    o_pre = zr[:, 2 * D:3 * D]
    uc = cb_ref[...] + u * cw_ref[CONV_W - 1:CONV_W, :]
    uc_t = cbt_ref[...] + ut_ref[0] * cwt_ref[:, CONV_W - 1:CONV_W]
    for j in range(CONV_W - 1):
        uc = uc + conv_ref[:, j * D:(j + 1) * D] * cw_ref[j:j + 1, :]
        uc_t = uc_t + convt_ref[0, j] * cwt_ref[:, j:j + 1]
        conv_out[:, j * D:(j + 1) * D] = conv_ref[:, (j + 1) * D:(j + 2) * D] if j + 1 < CONV_W - 1 else u
    act = _silu(uc)
    k_row = act[:, 512:1024]
    act_t = _silu(uc_t)
    q_t = act_t[0:512] * ML_DK ** -0.5
    k_t = act_t[512:1024]
    q_row = act[:, 0:512] * ML_DK ** -0.5

    gates = zg_ref[...] + bg_ref[...]
    lf = _log_sigmoid(gates)
    m_in = m_ref[...]
    m_out[...] = m_in

    for j in range(SG):
        for h in range(ML_H):
            ks = slice(h * ML_DK, (h + 1) * ML_DK)
            vs = slice(h * ML_DV, (h + 1) * ML_DV)
            ig = gates[j:j + 1, h:h + 1]
            log_prev = lf[j:j + 1, 4 + h:5 + h] + m_in[j:j + 1, h:h + 1]
            m_t = jnp.maximum(ig, log_prev)
            d = jnp.exp(ig - m_t)
            w_prev = jnp.exp(log_prev - m_t)
            c_new = w_prev * c_ref[j, h] + (d * k_t[ks, j:j + 1]) * v[j:j + 1, vs]
            n_new = w_prev * n_ref[j, h:h + 1, :] + d * k_row[j:j + 1, ks]
            c_out[j, h] = c_new
            n_out[j, h:h + 1, :] = n_new
            m_out[j:j + 1, h:h + 1] = m_t
            num = jnp.sum(q_t[ks, j:j + 1] * c_new, axis=0, keepdims=True)
            den = jnp.sum(q_row[j:j + 1, ks] * n_new, axis=-1, keepdims=True)
            h_scr[j:j + 1, vs] = num / jnp.maximum(jnp.abs(den), jnp.exp(-m_t))

    hh = h_scr[...]
    for h in range(ML_H):
        vs = slice(h * ML_DV, (h + 1) * ML_DV)
        y_ref[:, vs] = _mh_norm_gate(hh[:, vs], o_pre[:, vs], hnw_ref[:, vs]).astype(BF16)


def _odd_sample(z, zg, ut3, conv, convt, bg, cw, cwt, cb, cbt, hnw, c_in, n_in, m_in):
    c2 = lambda g: (0, 0)
    return pl.pallas_call(
        _odd_sample_kernel,
        grid=(NS // SG,),
        in_specs=[
            pl.BlockSpec((SG, ODD_MAIN), lambda g: (NP // SG + g, 0)),
            pl.BlockSpec((SG, LANES), lambda g: (NP // SG + g, 0)),
            pl.BlockSpec((1, D, SG), lambda g: (g, 0, 0)),
            pl.BlockSpec((SG, (CONV_W - 1) * D), lambda g: (g, 0)),
            pl.BlockSpec((1, CONV_W - 1, D, SG), lambda g: (g, 0, 0, 0)),
            pl.BlockSpec((1, LANES), c2),
            pl.BlockSpec((CONV_W, D), c2),
            pl.BlockSpec((D, CONV_W), c2),
            pl.BlockSpec((1, D), c2),
            pl.BlockSpec((D, 1), c2),
            pl.BlockSpec((1, D), c2),
            pl.BlockSpec((SG, ML_H, ML_DK, ML_DV), lambda g: (g, 0, 0, 0)),
            pl.BlockSpec((SG, ML_H, ML_DK), lambda g: (g, 0, 0)),
            pl.BlockSpec((SG, LANES), lambda g: (g, 0)),
        ],
        out_specs=[
            pl.BlockSpec((SG, D), lambda g: (g, 0)),
            pl.BlockSpec((SG, ML_H, ML_DK, ML_DV), lambda g: (g, 0, 0, 0)),
            pl.BlockSpec((SG, ML_H, ML_DK), lambda g: (g, 0, 0)),
            pl.BlockSpec((SG, LANES), lambda g: (g, 0)),
            pl.BlockSpec((SG, (CONV_W - 1) * D), lambda g: (g, 0)),
        ],
        out_shape=[
            jax.ShapeDtypeStruct((NS, D), BF16),
            jax.ShapeDtypeStruct((NS, ML_H, ML_DK, ML_DV), F32),
            jax.ShapeDtypeStruct((NS, ML_H, ML_DK), F32),
            jax.ShapeDtypeStruct((NS, LANES), F32),
            jax.ShapeDtypeStruct((NS, (CONV_W - 1) * D), F32),
        ],
        scratch_shapes=[pltpu.VMEM((SG, D), F32)],
        compiler_params=_params(("parallel",)),
        name="odd_sample",
    )(z, zg, ut3, conv, convt, bg, cw, cwt, cb, cbt, hnw, c_in, n_in, m_in)


def _out_ln_router_kernel(x_ref, y_ref, w_ref, g_ref, b_ref, wr_ref, tri_ref,
                          o_ref, op_ref, meta_ref, cnt_ref, tab_ref, carry, filled):
    i = pl.program_id(0)

    @pl.when(i == 0)
    def _():
        carry[...] = jnp.zeros_like(carry)
        tab_ref[...] = jnp.zeros_like(tab_ref)
        for e in range(N_EXPERTS):
            filled[e] = 0

    r = ALPHA * x_ref[...] + _dot(y_ref[...], w_ref[...])
    x3 = _layernorm(r, g_ref[...], b_ref[...])
    o_ref[...] = x3
    op_ref[...] = pltpu.pack_elementwise([x3[:, :D // 2], x3[:, D // 2:]], packed_dtype=BF16)

    lane = lax.broadcasted_iota(jnp.int32, (TM, LANES), 1).astype(F32)
    logits = jnp.where(lane < N_EXPERTS, _dot(x3.astype(BF16), wr_ref[...]), -jnp.inf)
    m1 = jnp.max(logits, axis=-1, keepdims=True)
    i1 = jnp.min(jnp.where(logits == m1, lane, float(LANES)), axis=-1, keepdims=True)
    rest = jnp.where(lane == i1, -jnp.inf, logits)
    m2 = jnp.max(rest, axis=-1, keepdims=True)
    i2 = jnp.min(jnp.where(rest == m2, lane, float(LANES)), axis=-1, keepdims=True)
    e2 = jnp.exp(m2 - m1)
    tot = 1.0 + e2
    w1 = 1.0 / tot
    w2 = e2 / tot

    sel1 = lane == i1
    sel2 = lane == i2
    onehot = jnp.where(sel1 | sel2, 1.0, 0.0)
    in_tile = _dot(tri_ref[...], onehot.astype(BF16))
    carry[...] = carry[...] + jnp.sum(onehot, axis=0, keepdims=True)
    cnt_ref[...] = carry[...]

    meta = jnp.where(lane == 0.0, i1, 0.0)
    meta = jnp.where(lane == 1.0, i2, meta)
    meta = jnp.where(lane == 2.0, w1, meta)
    meta = jnp.where(lane == 3.0, w2, meta)
    meta_ref[...] = meta

    token = (i * TM + lax.broadcasted_iota(jnp.int32, (TM, 1), 0)).astype(F32)
    digit_hi = jnp.floor(token * (1.0 / 256.0))
    rhs = jnp.where(lane == 0.0, digit_hi, jnp.where(lane == 1.0, token - 256.0 * digit_hi, 0.0))
    rhs = jnp.where(lane == i2 + float(SUBLANES), 1.0, rhs).astype(BF16)
    place = lax.broadcasted_iota(jnp.int32, (LANES, TM), 0).astype(F32)
    rank_rows = jnp.where(onehot > 0.0, in_tile, -1.0).T

    def entries(e, part):
        hit = jnp.where(rank_rows[e:e + 1, :] == place + float(part * LANES), 1.0, 0.0).astype(BF16)
        got = _dot(hit, rhs)
        return 256.0 * got[:, 0:1] + got[:, 1:2] + float(NT) * got[:, SUBLANES + e:SUBLANES + e + 1]

    first = [entries(e, 0) for e in range(N_EXPERTS)]
    tile_counts = jnp.sum(onehot, axis=0, keepdims=True)
    counts = [jnp.sum(tile_counts[:, e:e + 1]).astype(jnp.int32) for e in range(N_EXPERTS)]
    for e in range(N_EXPERTS):
        start = filled[e]
        filled[e] = start + counts[e]
        tab_ref[pl.ds(start, LANES), e:e + 1] = first[e]
        for part in range(1, TM // LANES):
            @pl.when(counts[e] > part * LANES)
            def _():
                tab_ref[pl.ds(start + part * LANES, LANES), e:e + 1] = entries(e, part)


def _out_ln_router(x, y, w, g, b, wr, tri):
    c2 = lambda i: (0, 0)
    return pl.pallas_call(
        _out_ln_router_kernel,
        grid=(NT // TM,),
        in_specs=[
            pl.BlockSpec((TM, D), lambda i: (i, 0)),
            pl.BlockSpec((TM, D), lambda i: (i, 0)),
            pl.BlockSpec((D, D), c2),
            pl.BlockSpec((1, D), c2),
            pl.BlockSpec((1, D), c2),
            pl.BlockSpec((D, LANES), c2),
            pl.BlockSpec((TM, TM), c2),
        ],
        out_specs=[
            pl.BlockSpec((TM, D), lambda i: (i, 0)),
            pl.BlockSpec((TM, D // 2), lambda i: (i, 0)),
            pl.BlockSpec((TM, LANES), lambda i: (i, 0)),
            pl.BlockSpec((1, LANES), c2),
            pl.BlockSpec((MOE_CAP + TM, LANES), c2),
        ],
        out_shape=[
            jax.ShapeDtypeStruct((NT, D), F32),
            jax.ShapeDtypeStruct((NT, D // 2), U32),
            jax.ShapeDtypeStruct((NT, LANES), F32),
            jax.ShapeDtypeStruct((1, LANES), F32),
            jax.ShapeDtypeStruct((MOE_CAP + TM, LANES), F32),
        ],
        scratch_shapes=[pltpu.VMEM((1, LANES), F32), pltpu.SMEM((N_EXPERTS,), jnp.int32)],
        compiler_params=_params(("arbitrary",)),
        name="out_ln_router",
    )(x, y, w, g, b, wr, tri)


def _moe_ffn_kernel(te_ref, nu_ref, tb_ref, gnext_ref, gcur_ref, sprev_ref, scur_ref, xp_ref,
                    w1_ref, w3_ref, w2_ref, out_hbm, stage, yacc, xb_scr, sem_s):
    del tb_ref
    i = pl.program_id(0)
    j = pl.program_id(1)
    used = i < nu_ref[0]
    slot = i % 2
    other = 1 - slot
    rps = MOE_ROWS_PER_STEP

    def gather_rows(tab_ref, buf, part):
        for r in range(rps):
            stage[buf, part, pl.ds(r, 1), :] = xp_ref[pl.ds(tab_ref[part * rps + r], 1), :]

    def scatter(buf, r, dst):
        return pltpu.make_async_copy(yacc.at[buf, pl.ds(r, 1)], out_hbm.at[pl.ds(dst, 1)], sem_s)

    def wait_scatters(n):
        for _ in range(n):
            scatter(0, 0, 0).wait()

    def issue_neighbours():
        gather_rows(gnext_ref, other, j)
        for r in range(rps):
            rr = j * rps + r
            scatter(other, rr, sprev_ref[rr]).start()

    @pl.when(j == 0)
    def _():
        @pl.when(i == 0)
        def _():
            yacc[1] = jnp.zeros((TMM, D), F32)
            for part in range(MOE_NFF):
                gather_rows(gcur_ref, 0, part)

        @pl.when(i > 0)
        def _():
            wait_scatters(TMM)

        words = stage[slot].reshape(TMM, D // 2)
        for half in range(2):
            xb_scr[:, half * (D // 2):(half + 1) * (D // 2)] = pltpu.unpack_elementwise(
                words, index=half, packed_dtype=BF16, unpacked_dtype=F32).astype(BF16)
        yacc[slot] = jnp.zeros((TMM, D), F32)

    @pl.when(used)
    def _():
        issue_neighbours()
        xb = xb_scr[...]
        hmid = _silu(_dot(xb, w1_ref[...])) * _dot(xb, w3_ref[...])
        yacc[slot] += _dot(hmid.astype(BF16), w2_ref[...])

    @pl.when(jnp.logical_not(used))
    def _():
        issue_neighbours()

    @pl.when((i == N_MOE_TILES - 1) & (j == MOE_NFF - 1))
    def _():
        for r in range(TMM):
            scatter(slot, r, scur_ref[r]).start()
        wait_scatters(2 * TMM)


def _moe_ffn(tile_expert, n_used, tab_block, gsrc, sdst, xp, w1, w3, w2):
    nff = MOE_NFF

    def wcol(i, j, te, nu, tb):
        return (te[i], 0, jnp.where(i < nu[0], j, nff - 1))

    def wrow(i, j, te, nu, tb):
        return (te[i], jnp.where(i < nu[0], j, nff - 1), 0)

    smem = functools.partial(pl.BlockSpec, (MOE_TAB,), memory_space=pltpu.SMEM)
    grid_spec = pltpu.PrefetchScalarGridSpec(
        num_scalar_prefetch=3,
        grid=(N_MOE_TILES, nff),
        in_specs=[
            smem(lambda i, j, te, nu, tb: (tb[i + 2],)),
            smem(lambda i, j, te, nu, tb: (tb[i + 1],)),
            smem(lambda i, j, te, nu, tb: (tb[i],)),
            smem(lambda i, j, te, nu, tb: (tb[i + 1],)),
            pl.BlockSpec((NT, D // 2), lambda i, j, te, nu, tb: (0, 0), pipeline_mode=pl.Buffered(1)),
            pl.BlockSpec((None, D, TFF), wcol),
            pl.BlockSpec((None, D, TFF), wcol),
            pl.BlockSpec((None, TFF, D), wrow),
        ],
        out_specs=pl.BlockSpec(memory_space=pl.ANY),
        scratch_shapes=[
            pltpu.VMEM((2, MOE_NFF, MOE_ROWS_PER_STEP, D // 2), U32),
            pltpu.VMEM((2, TMM, D), F32),
            pltpu.VMEM((TMM, D), BF16),
            pltpu.SemaphoreType.DMA(()),
        ],
    )
    return pl.pallas_call(
        _moe_ffn_kernel,
        grid_spec=grid_spec,
        out_shape=jax.ShapeDtypeStruct((MOE_OUT_ROWS, D), F32),
        compiler_params=_params(("arbitrary", "arbitrary")),
        name="moe_ffn",
    )(tile_expert, n_used, tab_block, gsrc, gsrc, sdst, sdst, xp, w1, w3, w2)


def _combine_kernel(x_ref, meta_ref, y0_ref, y1_ref, g_ref, b_ref, o_ref):
    meta = meta_ref[...]
    moe = meta[:, 2:3] * y0_ref[...] + meta[:, 3:4] * y1_ref[...]
    o_ref[...] = _layernorm(ALPHA * x_ref[...] + moe, g_ref[...], b_ref[...])


def _combine(x, meta, ys, g, b):
    c2 = lambda i: (0, 0)
    return pl.pallas_call(
        _combine_kernel,
        grid=(NT // TM,),
        in_specs=[
            pl.BlockSpec((TM, D), lambda i: (i, 0)),
            pl.BlockSpec((TM, LANES), lambda i: (i, 0)),
            pl.BlockSpec((TM, D), lambda i: (i, 0)),
            pl.BlockSpec((TM, D), lambda i: (i + NT // TM, 0)),
            pl.BlockSpec((1, D), c2),
            pl.BlockSpec((1, D), c2),
        ],
        out_specs=pl.BlockSpec((TM, D), lambda i: (i, 0)),
        out_shape=jax.ShapeDtypeStruct((NT, D), F32),
        compiler_params=_params(("parallel",)),
        name="moe_combine",
    )(x, meta, ys, ys, g, b)


def _pad_cols(w, n):
    return jnp.pad(w, ((0, 0), (0, n - w.shape[1])))


def kernel(x_prompt, x_sample, state_hgrn, state_gla, state_mlstm_C, state_mlstm_n, state_mlstm_m,
           state_mlstm_conv, w_in_even, hg_lower_bounds, w_gk, b_gk, gn_hg, gn_gla, w_out_even,
           w1_dense, w3_dense, w2_dense, w_in_odd, b_gate_odd, conv_w, conv_b, hn_w, w_out_odd,
           w_router, w1_moe, w3_moe, w2_moe, ln1_g, ln1_b, ln2_g, ln2_b):
    assert x_prompt.shape == (BATCH, SEQ, D) and x_sample.shape == (NS, 1, D)
    assert w_in_even.shape[0] == 1 and w_in_odd.shape[0] == 1 and hg_lower_bounds.shape[0] == 2
    masks = jnp.asarray(_gla_masks(), F32)
    tri_cs = jnp.asarray(_tri(CS, False), BF16)
    tri_tm = jnp.asarray(_tri(TM, True), BF16)
    row = lambda a: a.reshape(1, -1)

    x0 = x_prompt.reshape(NP, D)
    x0_tail = jnp.concatenate([x0[LAST_TILE * TM:], x_sample.reshape(NS, D)], axis=0)

    w_even = w_in_even[0].astype(BF16)
    z, zgr = _proj(x0, x0_tail, w_even[:, :EVEN_MAIN], _pad_cols(w_even[:, EVEN_MAIN:], LANES))
    wgk = jnp.pad(w_gk[0].astype(BF16), ((0, LANES - GLA_RANK), (0, 0)))
    lbp = hg_lower_bounds
    zs = z[NP:].reshape(NS // SG, SG, EVEN_MAIN).transpose(0, 2, 1)
    grs = zgr[NP:].reshape(NS // SG, SG, LANES).transpose(0, 2, 1)
    y_s, hg_s, gla_s = _even_sample(z, zs, grs, lbp.T, wgk.T, b_gk[0].reshape(-1, 1),
                                    row(gn_hg[0]), row(gn_gla[0]), state_hgrn[0], state_gla[0])
    y, hg_p, gla_p = _even_prompt(z, zgr, y_s, lbp, wgk, row(b_gk[0]), row(gn_hg[0]), row(gn_gla[0]), tri_cs, masks)
    x2 = _ffn(x0, x0_tail, y, w_out_even[0].astype(BF16), row(ln1_g[0]), row(ln1_b[0]),
              w1_dense[0].astype(BF16), w3_dense[0].astype(BF16), w2_dense[0].astype(BF16),
              row(ln2_g[0]), row(ln2_b[0]))

    w_odd = w_in_odd[0].astype(BF16)
    zo, zog = _proj(x2, x2[LAST_TILE * TM:], w_odd[:, :ODD_MAIN], _pad_cols(w_odd[:, ODD_MAIN:], LANES))
    bg = jnp.pad(b_gate_odd[0], (0, LANES - 2 * ML_H)).reshape(1, LANES)
    ut = zo[NP:, :D].reshape(NS // SG, SG, D).transpose(0, 2, 1)
    conv_in = state_mlstm_conv[0]
    conv_t = conv_in.reshape(NS // SG, SG, CONV_W - 1, D).transpose(0, 2, 3, 1)
    m_in = jnp.pad(state_mlstm_m[0], ((0, 0), (0, LANES - ML_H)))
    yo_s, c_s, n_s, m_s, conv_s = _odd_sample(
        zo, zog, ut, conv_in.reshape(NS, (CONV_W - 1) * D), conv_t, bg, conv_w[0], conv_w[0].T, row(conv_b[0]), conv_b[0].reshape(-1, 1),
        row(hn_w[0]), state_mlstm_C[0], state_mlstm_n[0], m_in)
    yo, c_p, n_p, m_p, conv_p = _odd_prompt(zo, zog, yo_s, bg, conv_w[0], row(conv_b[0]), row(hn_w[0]), tri_cs)

    wr = _pad_cols(w_router[0].astype(BF16), LANES)
    x3, x3p, meta, cnt, tab = _out_ln_router(
        x2, yo, w_out_odd[0].astype(BF16), row(ln1_g[1]), row(ln1_b[1]), wr, tri_tm)

    counts = cnt[0, :N_EXPERTS].astype(jnp.int32)
    padded = ((counts + TMM - 1) // TMM) * TMM
    ends = jnp.cumsum(padded)
    offsets = ends - padded
    tile = jnp.arange(N_MOE_TILES, dtype=jnp.int32)
    tile_expert = jnp.minimum(jnp.sum((tile * TMM)[:, None] >= ends[None, :], axis=1), N_EXPERTS - 1).astype(jnp.int32)
    n_tiles_used = ends[-1] // TMM
    in_use = tile < n_tiles_used
    local_tile = (tile * TMM - offsets[tile_expert]) // TMM
    blocks_per_expert = MOE_CAP // MOE_TAB
    spill_block = N_EXPERTS * blocks_per_expert
    n_spare = N_MOE_TILES - (2 * NT) // TMM
    entry = tab[:MOE_CAP, :N_EXPERTS].T.astype(jnp.int32)
    local = jnp.arange(MOE_CAP, dtype=jnp.int32)[None, :]
    valid = local < counts[:, None]
    padding = padded - counts
    pad_before = (jnp.cumsum(padding) - padding)[:, None]
    row_in_block = jnp.arange(MOE_TAB, dtype=jnp.int32)
    spare_rows = (2 * NT + TMM + jnp.sum(padding)
                  + jnp.arange(n_spare, dtype=jnp.int32)[:, None] * TMM + row_in_block[None, :])
    sdst = jnp.concatenate([
        jnp.where(valid, entry, 2 * NT + TMM + pad_before + local - counts[:, None]).reshape(-1),
        2 * NT + row_in_block, spare_rows.reshape(-1)])
    gsrc = jnp.concatenate([
        jnp.where(valid, entry - jnp.where(entry >= NT, NT, 0), 0).reshape(-1),
        jnp.zeros(((1 + n_spare) * MOE_TAB,), jnp.int32)])
    own_block = jnp.where(in_use, tile_expert * blocks_per_expert + local_tile, spill_block + 1 + tile - n_tiles_used)
    tab_block = jnp.concatenate([jnp.full((1,), spill_block, jnp.int32), own_block.astype(jnp.int32),
                                 jnp.full((1,), spill_block, jnp.int32)])

    ys = _moe_ffn(tile_expert, n_tiles_used.reshape(1), tab_block, gsrc, sdst, x3p,
                  w1_moe[0].astype(BF16), w3_moe[0].astype(BF16), w2_moe[0].astype(BF16))
    out = _combine(x3, meta, ys, row(ln2_g[1]), row(ln2_b[1]))

    y_prompt = out[:NP].reshape(BATCH, SEQ, D)
    y_sample = out[NP:].reshape(NS, 1, D)
    return (y_prompt, y_sample,
            hg_p.reshape(1, BATCH, HG_H, HG_DK, HG_DV), gla_p.reshape(1, BATCH, GLA_H, GLA_DK, GLA_DV),
            c_p[None], n_p[None], m_p[:, 0, :ML_H][None], conv_p[None],
            hg_s[None], gla_s[None], c_s[None], n_s[None], m_s[:, :ML_H][None], conv_s.reshape(1, NS, CONV_W - 1, D))
```

```python
import functools
import math

import jax
import jax.numpy as jnp
import numpy as np
from jax import lax
from jax.experimental import pallas as pl
from jax.experimental.pallas import tpu as pltpu

F32 = jnp.float32
BF16 = jnp.bfloat16
U32 = jnp.uint32

D = 1024
BATCH = 8
SEQ = 2048
DEC_BATCH = 128
NP = BATCH * SEQ
NS = DEC_BATCH
NT = NP + NS
HG_H, HG_DK, HG_DV = 4, 128, 128
GLA_H, GLA_DK, GLA_DV = 4, 64, 128
GLA_RANK = 16
GLA_GATE_NORM = 16.0
ML_H, ML_DK, ML_DV = 4, 128, 256
CONV_W = 4
D_FF_DENSE = 2816
D_FF_EXPERT = 3584
N_EXPERTS = 8
EPS = 1e-5
DEPTH = 2
ALPHA = (2.0 * DEPTH) ** 0.25
EVEN_MAIN = 3584
ODD_MAIN = 3072

LANES = 128
SUBLANES = 8
VMEM_LIMIT = 56 * 1024 * 1024

TM = 384
LAST_TILE = NT // TM - 1
CS = 128
NCHUNK = SEQ // CS
SG = 16
TMM = 512
TFF = 896
MOE_TAB = 512
MOE_CAP = -(-NT // MOE_TAB) * MOE_TAB
MOE_NFF = D_FF_EXPERT // TFF
MOE_ROWS_PER_STEP = TMM // MOE_NFF
N_MOE_TILES = -(-(2 * NT + N_EXPERTS * (TMM - 1)) // TMM)
MOE_SLOTS = N_MOE_TILES * TMM
MOE_OUT_ROWS = MOE_SLOTS + TMM
N_LEVELS = int(math.log2(CS))

assert NT % TM == 0 and NP % CS == 0 and NS % SG == 0 and D_FF_EXPERT % TFF == 0 and TMM % MOE_NFF == 0
assert TMM == MOE_TAB and NS == CS and LAST_TILE * TM <= NP


def _params(sem, limit=VMEM_LIMIT):
    return pltpu.CompilerParams(dimension_semantics=sem, vmem_limit_bytes=limit)


def _dot(a, b):
    return jnp.dot(a, b, preferred_element_type=F32)


def _dot_nt(a, b):
    return lax.dot_general(a, b, (((1,), (1,)), ((), ())), preferred_element_type=F32)


def _dot_tn(a, b):
    return lax.dot_general(a, b, (((0,), (0,)), ((), ())), preferred_element_type=F32)


def _split3(x):
    hi = x.astype(BF16)
    r1 = x - hi.astype(F32)
    mid = r1.astype(BF16)
    lo = (r1 - mid.astype(F32)).astype(BF16)
    return hi, mid, lo


def _dot_sel(sel, x):
    hi, mid, lo = _split3(x)
    return _dot(sel, hi) + _dot(sel, mid) + _dot(sel, lo)


def _sigmoid(x):
    return jax.nn.sigmoid(x)


def _silu(x):
    return x * jax.nn.sigmoid(x)


def _log_sigmoid(x):
    return jnp.minimum(x, 0.0) - jnp.log(1.0 + jnp.exp(-jnp.abs(x)))


def _layernorm(r, g, b):
    mu = jnp.mean(r, axis=-1, keepdims=True)
    c = r - mu
    var = jnp.mean(c * c, axis=-1, keepdims=True)
    return c * lax.rsqrt(var + EPS) * g + b


def _gla_masks():
    masks = np.zeros((N_LEVELS + 1, CS, CS), np.float32)
    for t in range(CS):
        for l in range(N_LEVELS):
            half = 1 << l
            start = (t // (2 * half)) * (2 * half)
            mid = start + half
            if t >= mid:
                masks[l, t, start:mid] = 1.0
        masks[N_LEVELS, t, t] = 1.0
    return masks


def _tri(n, strict):
    return np.tril(np.ones((n, n), np.float32), -1 if strict else 0)


def _token_tile(x_ref, tail_ref):
    return jnp.where(pl.program_id(0) == LAST_TILE, tail_ref[...], x_ref[...])


def _main_tile(i):
    return (jnp.minimum(i, LAST_TILE - 1), 0)


def _proj_kernel(x_ref, tail_ref, wa_ref, wb_ref, oa_ref, ob_ref):
    xb = _token_tile(x_ref, tail_ref).astype(BF16)
    oa_ref[...] = _dot(xb, wa_ref[...])
    ob_ref[...] = _dot(xb, wb_ref[...])


def _proj(x, tail, wa, wb):
    na, nb = wa.shape[1], wb.shape[1]
    return pl.pallas_call(
        _proj_kernel,
        grid=(NT // TM,),
        in_specs=[
            pl.BlockSpec((TM, D), _main_tile),
            pl.BlockSpec((TM, D), lambda i: (0, 0)),
            pl.BlockSpec((D, na), lambda i: (0, 0)),
            pl.BlockSpec((D, nb), lambda i: (0, 0)),
        ],
        out_specs=[
            pl.BlockSpec((TM, na), lambda i: (i, 0)),
            pl.BlockSpec((TM, nb), lambda i: (i, 0)),
        ],
        out_shape=[jax.ShapeDtypeStruct((NT, na), F32), jax.ShapeDtypeStruct((NT, nb), F32)],
        compiler_params=_params(("parallel",)),
        name="proj",
    )(x, tail, wa, wb)


def _rms_gate(o, gate, w):
    o = o * lax.rsqrt(jnp.mean(o * o, axis=-1, keepdims=True) + EPS) * w
    return o * _silu(gate)


def _level_decays(g, bc):
    width = g.shape[1]
    ng = CS // SUBLANES
    shape3 = (ng, SUBLANES, width)
    bc3 = bc.reshape(shape3)
    sub = lax.broadcasted_iota(jnp.int32, shape3, 1)

    def group_row(s):
        return jnp.broadcast_to(bc3[:, s:s + 1, :], shape3)

    last = group_row(SUBLANES - 1)
    refs = [None,
            jnp.where(sub < 4, group_row(1), group_row(5)),
            group_row(3)]
    for l in range(3, N_LEVELS):
        per_block = 1 << (l - 2)
        grouped = last.reshape(ng // per_block, per_block, SUBLANES, width)
        ref = jnp.broadcast_to(grouped[:, per_block // 2 - 1:per_block // 2], grouped.shape)
        refs.append(ref.reshape(shape3))
    decays = [jnp.exp(jnp.where((sub & 1) == 1, g.reshape(shape3), 0.0))]
    decays += [jnp.exp(-jnp.abs(bc3 - ref)) for ref in refs[1:]]
    to_end = jnp.exp(jnp.broadcast_to(last[ng - 1:ng], shape3) - bc3)
    return [d.reshape(CS, width) for d in decays], to_end.reshape(CS, width)


def _gla_chunk(q, k, v, g, st_ref, tri, masks_ref, heads, dk, dv):
    bc = _dot_sel(tri, g)
    z_levels, z_end = _level_decays(g, bc)
    z_cum = jnp.exp(bc)
    st = st_ref[...]
    outs = []
    for h in range(heads):
        ks = slice(h * dk, (h + 1) * dk)
        vs = slice(h * dv, (h + 1) * dv)
        qh, kh = q[:, ks], k[:, ks]
        vh = v[:, vs].astype(BF16)
        scores = _dot_nt(qh.astype(BF16), kh.astype(BF16)) * masks_ref[N_LEVELS]
        for l in range(N_LEVELS):
            zl = z_levels[l][:, ks]
            scores = scores + _dot_nt((qh * zl).astype(BF16), (kh * zl).astype(BF16)) * masks_ref[l]
        o = _dot(scores.astype(BF16), vh)
        o = o + _dot_nt((qh * z_cum[:, ks]).astype(BF16), st[:, ks].astype(BF16))
        outs.append(o)
        upd = _dot_tn(vh, (kh * z_end[:, ks]).astype(BF16))
        st_ref[:, ks] = st[:, ks] * z_cum[CS - 1:CS, ks] + upd
    return outs


def _even_prompt_kernel(z_ref, zgr_ref, ys_ref, lbp_ref, wgk_ref, bgk_ref, gnh_ref, gng_ref, tri_ref, masks_ref,
                        y_ref, shg_ref, sgla_ref, st_hg, st_gla):
    b = pl.program_id(0)
    c = pl.program_id(1)

    @pl.when((b < BATCH) & (c == 0))
    def _():
        st_hg[...] = jnp.zeros_like(st_hg)
        st_gla[...] = jnp.zeros_like(st_gla)

    @pl.when(b < BATCH)
    def _():
        tri = tri_ref[...]
        p = lbp_ref[...]
        pe = jnp.exp(p - jnp.max(p, axis=0, keepdims=True))
        lb = pe[0:1] / jnp.sum(pe, axis=0, keepdims=True)

        z = z_ref[...]
        hq, hf, hi, hg = z[:, 0:512], z[:, 512:1024], z[:, 1024:1536], z[:, 1536:2048]
        gq, gk, gv, gg = z[:, 2048:2304], z[:, 2304:2560], z[:, 2560:3072], z[:, 3072:3584]
        f = lb + (1.0 - lb) * _sigmoid(hf)
        k_hg = (1.0 - lb) * _sigmoid(-hf)
        o_hg = _gla_chunk(_silu(hq), k_hg, hi, jnp.log(f), st_hg, tri, masks_ref, HG_H, HG_DK, HG_DV)

        la = _log_sigmoid(_dot(zgr_ref[...].astype(BF16), wgk_ref[...]) + bgk_ref[...]) / GLA_GATE_NORM
        o_gla = _gla_chunk(gq * GLA_DK ** -0.5, gk, gv, la, st_gla, tri, masks_ref, GLA_H, GLA_DK, GLA_DV)

        for h in range(HG_H):
            cs = slice(h * 128, (h + 1) * 128)
            y_ref[:, cs] = _rms_gate(o_hg[h], hg[:, cs], gnh_ref[...]).astype(BF16)
        for h in range(GLA_H):
            cs = slice(h * 128, (h + 1) * 128)
            y_ref[:, 512 + h * 128:512 + (h + 1) * 128] = _rms_gate(o_gla[h], gg[:, cs], gng_ref[...]).astype(BF16)

    @pl.when((b < BATCH) & (c == NCHUNK - 1))
    def _():
        shg_ref[0] = st_hg[...].T
        sgla_ref[0] = st_gla[...].T

    @pl.when((b == BATCH) & (c == 0))
    def _():
        y_ref[...] = ys_ref[...]


def _chunk_rows(b, c):
    return (jnp.minimum(b * NCHUNK + c, NP // CS), 0)


def _per_sequence(b, c):
    return (jnp.minimum(b, BATCH - 1), 0, 0)


def _even_prompt(z, zgr, y_sample, lbp, wgk, bgk, gnh, gng, tri, masks):
    const2 = lambda b, c: (0, 0)
    return pl.pallas_call(
        _even_prompt_kernel,
        grid=(BATCH + 1, NCHUNK),
        in_specs=[
            pl.BlockSpec((CS, EVEN_MAIN), _chunk_rows),
            pl.BlockSpec((CS, LANES), _chunk_rows),
            pl.BlockSpec((NS, D), const2),
            pl.BlockSpec(lbp.shape, const2),
            pl.BlockSpec(wgk.shape, const2),
            pl.BlockSpec(bgk.shape, const2),
            pl.BlockSpec(gnh.shape, const2),
            pl.BlockSpec(gng.shape, const2),
            pl.BlockSpec(tri.shape, const2),
            pl.BlockSpec(masks.shape, lambda b, c: (0, 0, 0)),
        ],
        out_specs=[
            pl.BlockSpec((CS, D), _chunk_rows),
            pl.BlockSpec((1, HG_H * HG_DK, HG_DV), _per_sequence),
            pl.BlockSpec((1, GLA_H * GLA_DK, GLA_DV), _per_sequence),
        ],
        out_shape=[
            jax.ShapeDtypeStruct((NT, D), BF16),
            jax.ShapeDtypeStruct((BATCH, HG_H * HG_DK, HG_DV), F32),
            jax.ShapeDtypeStruct((BATCH, GLA_H * GLA_DK, GLA_DV), F32),
        ],
        scratch_shapes=[pltpu.VMEM((HG_DV, HG_H * HG_DK), F32), pltpu.VMEM((GLA_DV, GLA_H * GLA_DK), F32)],
        compiler_params=_params(("arbitrary", "arbitrary")),
        name="even_prompt",
    )(z, zgr, y_sample, lbp, wgk, bgk, gnh, gng, tri, masks)


def _even_sample_kernel(zr_ref, zt_ref, grt_ref, lbpt_ref, wgkt_ref, bgkt_ref, gnh_ref, gng_ref,
                        shg_ref, sgla_ref, y_ref, shg_out, sgla_out, o_scr):
    zt = zt_ref[0]
    hq_t, hf_t = zt[0:512], zt[512:1024]
    gq_t, gk_t = zt[2048:2304], zt[2304:2560]
    pt = lbpt_ref[...]
    pe = jnp.exp(pt - jnp.max(pt, axis=1, keepdims=True))
    lb = pe[:, 0:1] / jnp.sum(pe, axis=1, keepdims=True)
    a_hg = jnp.exp(jnp.log(lb + (1.0 - lb) * _sigmoid(hf_t)))
    k_hg = (1.0 - lb) * _sigmoid(-hf_t)
    q_hg = _silu(hq_t)
    la = _log_sigmoid(_dot(wgkt_ref[...], grt_ref[0].astype(BF16)) + bgkt_ref[...]) / GLA_GATE_NORM
    a_gla = jnp.exp(la)
    q_gla = gq_t * GLA_DK ** -0.5
    zr = zr_ref[...]
    hi, hg = zr[:, 1024:1536], zr[:, 1536:2048]
    gv, gg = zr[:, 2560:3072], zr[:, 3072:3584]

    for j in range(SG):
        for h in range(HG_H):
            ks = slice(h * HG_DK, (h + 1) * HG_DK)
            s_new = a_hg[ks, j:j + 1] * shg_ref[j, h] + k_hg[ks, j:j + 1] * hi[j:j + 1, h * 128:(h + 1) * 128]
            shg_out[j, h] = s_new
            o_scr[j:j + 1, h * 128:(h + 1) * 128] = jnp.sum(q_hg[ks, j:j + 1] * s_new, axis=0, keepdims=True)
        for h in range(GLA_H):
            ks = slice(h * GLA_DK, (h + 1) * GLA_DK)
            s_new = a_gla[ks, j:j + 1] * sgla_ref[j, h] + gk_t[ks, j:j + 1] * gv[j:j + 1, h * 128:(h + 1) * 128]
            sgla_out[j, h] = s_new
            o_scr[j:j + 1, 512 + h * 128:512 + (h + 1) * 128] = jnp.sum(
                q_gla[ks, j:j + 1] * s_new, axis=0, keepdims=True)

    o = o_scr[...]
    for h in range(HG_H):
        cs = slice(h * 128, (h + 1) * 128)
        y_ref[:, cs] = _rms_gate(o[:, cs], hg[:, cs], gnh_ref[...]).astype(BF16)
    for h in range(GLA_H):
        cs = slice(512 + h * 128, 512 + (h + 1) * 128)
        y_ref[:, cs] = _rms_gate(o[:, cs], gg[:, h * 128:(h + 1) * 128], gng_ref[...]).astype(BF16)


def _even_sample(z, zt3, grt3, lbpt, wgkt, bgkt, gnh, gng, s_hg, s_gla):
    c2 = lambda g: (0, 0)
    return pl.pallas_call(
        _even_sample_kernel,
        grid=(NS // SG,),
        in_specs=[
            pl.BlockSpec((SG, EVEN_MAIN), lambda g: (NP // SG + g, 0)),
            pl.BlockSpec((1, EVEN_MAIN, SG), lambda g: (g, 0, 0)),
            pl.BlockSpec((1, LANES, SG), lambda g: (g, 0, 0)),
            pl.BlockSpec(lbpt.shape, c2),
            pl.BlockSpec(wgkt.shape, c2),
            pl.BlockSpec(bgkt.shape, c2),
            pl.BlockSpec(gnh.shape, c2),
            pl.BlockSpec(gng.shape, c2),
            pl.BlockSpec((SG, HG_H, HG_DK, HG_DV), lambda g: (g, 0, 0, 0)),
            pl.BlockSpec((SG, GLA_H, GLA_DK, GLA_DV), lambda g: (g, 0, 0, 0)),
        ],
        out_specs=[
            pl.BlockSpec((SG, D), lambda g: (g, 0)),
            pl.BlockSpec((SG, HG_H, HG_DK, HG_DV), lambda g: (g, 0, 0, 0)),
            pl.BlockSpec((SG, GLA_H, GLA_DK, GLA_DV), lambda g: (g, 0, 0, 0)),
        ],
        out_shape=[
            jax.ShapeDtypeStruct((NS, D), BF16),
            jax.ShapeDtypeStruct((NS, HG_H, HG_DK, HG_DV), F32),
            jax.ShapeDtypeStruct((NS, GLA_H, GLA_DK, GLA_DV), F32),
        ],
        scratch_shapes=[pltpu.VMEM((SG, D), F32)],
        compiler_params=_params(("parallel",)),
        name="even_sample",
    )(z, zt3, grt3, lbpt, wgkt, bgkt, gnh, gng, s_hg, s_gla)


FF_SPLIT = 1


def _ffn_kernel(x_ref, tail_ref, y_ref, wo_ref, g1_ref, b1_ref, w1_ref, w3_ref, w2_ref, g_ref, b_ref, o_ref):
    x = _layernorm(ALPHA * _token_tile(x_ref, tail_ref) + _dot(y_ref[...], wo_ref[...]), g1_ref[...], b1_ref[...])
    xb = x.astype(BF16)
    step = D_FF_DENSE // FF_SPLIT
    acc = ALPHA * x
    for s in range(FF_SPLIT):
        cs = slice(s * step, (s + 1) * step)
        hmid = _silu(_dot(xb, w1_ref[:, cs])) * _dot(xb, w3_ref[:, cs])
        acc = acc + _dot(hmid.astype(BF16), w2_ref[cs, :])
    o_ref[...] = _layernorm(acc, g_ref[...], b_ref[...])


def _ffn(x, tail, y, wo, g1, b1, w1, w3, w2, g, b):
    c2 = lambda i: (0, 0)
    one = pl.Buffered(1)
    return pl.pallas_call(
        _ffn_kernel,
        grid=(NT // TM,),
        in_specs=[
            pl.BlockSpec((TM, D), _main_tile),
            pl.BlockSpec((TM, D), c2),
            pl.BlockSpec((TM, D), lambda i: (i, 0)),
            pl.BlockSpec((D, D), c2, pipeline_mode=one),
            pl.BlockSpec((1, D), c2),
            pl.BlockSpec((1, D), c2),
            pl.BlockSpec((D, D_FF_DENSE), c2, pipeline_mode=one),
            pl.BlockSpec((D, D_FF_DENSE), c2, pipeline_mode=one),
            pl.BlockSpec((D_FF_DENSE, D), c2, pipeline_mode=one),
            pl.BlockSpec((1, D), c2),
            pl.BlockSpec((1, D), c2),
        ],
        out_specs=pl.BlockSpec((TM, D), lambda i: (i, 0)),
        out_shape=jax.ShapeDtypeStruct((NT, D), F32),
        compiler_params=_params(("parallel",)),
        name="ffn_dense",
    )(x, tail, y, wo, g1, b1, w1, w3, w2, g, b)


def _mh_norm_gate(hh, o_pre, w):
    mu = jnp.mean(hh, axis=-1, keepdims=True)
    c = hh - mu
    var = jnp.mean(c * c, axis=-1, keepdims=True)
    return _sigmoid(o_pre) * (c * lax.rsqrt(var + EPS) * w)


def _odd_prompt_kernel(z_ref, zg_ref, ys_ref, bg_ref, cw_ref, cb_ref, hnw_ref, tri_ref,
                       y_ref, c_out, n_out, m_out, conv_out,
                       c_scr, n_scr, m_scr, u_scr):
    b = pl.program_id(0)
    c = pl.program_id(1)

    @pl.when((b < BATCH) & (c == 0))
    def _():
        c_scr[...] = jnp.zeros_like(c_scr)
        n_scr[...] = jnp.zeros_like(n_scr)
        m_scr[...] = jnp.zeros_like(m_scr)
        u_scr[0:8, :] = jnp.zeros((8, D), F32)

    @pl.when((b == BATCH) & (c == 0))
    def _():
        y_ref[...] = ys_ref[...]

    @pl.when(b < BATCH)
    def _():
        row = lax.broadcasted_iota(jnp.int32, (CS, CS), 0)
        col = lax.broadcasted_iota(jnp.int32, (CS, CS), 1)
        causal = col <= row

        u_scr[8:8 + CS, :] = z_ref[:, 0:D]
        uc = cb_ref[...]
        for j in range(CONV_W):
            uc = uc + u_scr[5 + j:5 + j + CS, :] * cw_ref[j:j + 1, :]
        tail = u_scr[CS:CS + 8, :]
        u_scr[0:8, :] = tail
        act = _silu(uc)
        q = act[:, 0:512] * ML_DK ** -0.5
        k = act[:, 512:1024]
        v = z_ref[:, D:2 * D]
        o_pre = z_ref[:, 2 * D:3 * D]

        gates = zg_ref[...] + bg_ref[...]
        lf = _log_sigmoid(gates)
        bcum = _dot_sel(tri_ref[...], lf)
        bcum_t = bcum.T
        gates_t = gates.T
        m_all = m_scr[...]

        for h in range(ML_H):
            ks = slice(h * ML_DK, (h + 1) * ML_DK)
            vs = slice(h * ML_DV, (h + 1) * ML_DV)
            qh, kh = q[:, ks], k[:, ks]
            vh = v[:, vs].astype(BF16)
            b_col = bcum[:, 4 + h:5 + h]
            b_row = bcum_t[4 + h:5 + h, :]
            i_col = gates[:, h:h + 1]
            i_row = gates_t[h:h + 1, :]
            m_prev = m_all[:, h:h + 1]
            log_d = jnp.where(causal, b_col - b_row + i_row, -jnp.inf)
            log_prev = b_col + m_prev
            m_t = jnp.maximum(jnp.max(log_d, axis=-1, keepdims=True), log_prev)
            d = jnp.exp(log_d - m_t)
            w_prev = jnp.exp(log_prev - m_t)
            scores = _dot_nt(qh.astype(BF16), kh.astype(BF16)) * d
            c_h = c_scr[h]
            n_h = n_scr[h:h + 1, :]
            num = _dot(scores.astype(BF16), vh) + w_prev * _dot(qh.astype(BF16), c_h.astype(BF16))
            den = jnp.sum(scores, axis=-1, keepdims=True) + w_prev * jnp.sum(qh * n_h, axis=-1, keepdims=True)
            hh = num / jnp.maximum(jnp.abs(den), jnp.exp(-m_t))
            m_new = m_t[CS - 1:CS, :]
            b_last = b_col[CS - 1:CS, :]
            w_c = jnp.exp(b_last + m_prev - m_new)
            w_s = jnp.exp(b_last - b_col + i_col - m_new)
            kw = kh * w_s
            c_scr[h] = w_c * c_h + _dot_tn(kw.astype(BF16), vh)
            n_scr[h:h + 1, :] = w_c * n_h + jnp.sum(kw, axis=0, keepdims=True)
            m_scr[:, h:h + 1] = m_new
            y_ref[:, vs] = _mh_norm_gate(hh, o_pre[:, vs], hnw_ref[:, vs]).astype(BF16)

        @pl.when(c == NCHUNK - 1)
        def _():
            c_out[0] = c_scr[...]
            n_out[0] = n_scr[0:ML_H, :]
            m_out[0] = m_scr[...]
            conv_out[0] = tail[8 - (CONV_W - 1):8, :]


def _odd_prompt(z, zg, y_sample, bg, cw, cb, hnw, tri):
    c2 = lambda b, c: (0, 0)
    per_sequence4 = lambda b, c: (jnp.minimum(b, BATCH - 1), 0, 0, 0)
    return pl.pallas_call(
        _odd_prompt_kernel,
        grid=(BATCH + 1, NCHUNK),
        in_specs=[
            pl.BlockSpec((CS, ODD_MAIN), _chunk_rows),
            pl.BlockSpec((CS, LANES), _chunk_rows),
            pl.BlockSpec((NS, D), c2),
            pl.BlockSpec((1, LANES), c2),
            pl.BlockSpec((CONV_W, D), c2),
            pl.BlockSpec((1, D), c2),
            pl.BlockSpec((1, D), c2),
            pl.BlockSpec((CS, CS), c2),
        ],
        out_specs=[
            pl.BlockSpec((CS, D), _chunk_rows),
            pl.BlockSpec((1, ML_H, ML_DK, ML_DV), per_sequence4),
            pl.BlockSpec((1, ML_H, ML_DK), _per_sequence),
            pl.BlockSpec((1, 1, LANES), _per_sequence),
            pl.BlockSpec((1, CONV_W - 1, D), _per_sequence),
        ],
        out_shape=[
            jax.ShapeDtypeStruct((NT, D), BF16),
            jax.ShapeDtypeStruct((BATCH, ML_H, ML_DK, ML_DV), F32),
            jax.ShapeDtypeStruct((BATCH, ML_H, ML_DK), F32),
            jax.ShapeDtypeStruct((BATCH, 1, LANES), F32),
            jax.ShapeDtypeStruct((BATCH, CONV_W - 1, D), F32),
        ],
        scratch_shapes=[
            pltpu.VMEM((ML_H, ML_DK, ML_DV), F32),
            pltpu.VMEM((8, ML_DK), F32),
            pltpu.VMEM((1, LANES), F32),
            pltpu.VMEM((CS + 8, D), F32),
        ],
        compiler_params=_params(("arbitrary", "arbitrary")),
        name="odd_prompt",
    )(z, zg, y_sample, bg, cw, cb, hnw, tri)


def _odd_sample_kernel(zr_ref, zg_ref, ut_ref, conv_ref, convt_ref, bg_ref, cw_ref, cwt_ref, cb_ref, cbt_ref,
                       hnw_ref, c_ref, n_ref, m_ref,
                       y_ref, c_out, n_out, m_out, conv_out, h_scr):
    zr = zr_ref[...]
    u = zr[:, 0:D]
    v = zr[:, D:2 * D]
    o_pre = zr[:, 2 * D:3 * D]
    uc = cb_ref[...] + u * cw_ref[CONV_W - 1:CONV_W, :]
    uc_t = cbt_ref[...] + ut_ref[0] * cwt_ref[:, CONV_W - 1:CONV_W]
    for j in range(CONV_W - 1):
        uc = uc + conv_ref[:, j * D:(j + 1) * D] * cw_ref[j:j + 1, :]
        uc_t = uc_t + convt_ref[0, j] * cwt_ref[:, j:j + 1]
        conv_out[:, j * D:(j + 1) * D] = conv_ref[:, (j + 1) * D:(j + 2) * D] if j + 1 < CONV_W - 1 else u
    act = _silu(uc)
    k_row = act[:, 512:1024]
    act_t = _silu(uc_t)
    q_t = act_t[0:512] * ML_DK ** -0.5
    k_t = act_t[512:1024]
    q_row = act[:, 0:512] * ML_DK ** -0.5

    gates = zg_ref[...] + bg_ref[...]
    lf = _log_sigmoid(gates)
    m_in = m_ref[...]
    m_out[...] = m_in

    for j in range(SG):
        for h in range(ML_H):
            ks = slice(h * ML_DK, (h + 1) * ML_DK)
            vs = slice(h * ML_DV, (h + 1) * ML_DV)
            ig = gates[j:j + 1, h:h + 1]
            log_prev = lf[j:j + 1, 4 + h:5 + h] + m_in[j:j + 1, h:h + 1]
            m_t = jnp.maximum(ig, log_prev)
            d = jnp.exp(ig - m_t)
            w_prev = jnp.exp(log_prev - m_t)
            c_new = w_prev * c_ref[j, h] + (d * k_t[ks, j:j + 1]) * v[j:j + 1, vs]
            n_new = w_prev * n_ref[j, h:h + 1, :] + d * k_row[j:j + 1, ks]
            c_out[j, h] = c_new
            n_out[j, h:h + 1, :] = n_new
            m_out[j:j + 1, h:h + 1] = m_t
            num = jnp.sum(q_t[ks, j:j + 1] * c_new, axis=0, keepdims=True)
            den = jnp.sum(q_row[j:j + 1, ks] * n_new, axis=-1, keepdims=True)
            h_scr[j:j + 1, vs] = num / jnp.maximum(jnp.abs(den), jnp.exp(-m_t))

    hh = h_scr[...]
    for h in range(ML_H):
        vs = slice(h * ML_DV, (h + 1) * ML_DV)
        y_ref[:, vs] = _mh_norm_gate(hh[:, vs], o_pre[:, vs], hnw_ref[:, vs]).astype(BF16)


def _odd_sample(z, zg, ut3, conv, convt, bg, cw, cwt, cb, cbt, hnw, c_in, n_in, m_in):
    c2 = lambda g: (0, 0)
    return pl.pallas_call(
        _odd_sample_kernel,
        grid=(NS // SG,),
        in_specs=[
            pl.BlockSpec((SG, ODD_MAIN), lambda g: (NP // SG + g, 0)),
            pl.BlockSpec((SG, LANES), lambda g: (NP // SG + g, 0)),
            pl.BlockSpec((1, D, SG), lambda g: (g, 0, 0)),
            pl.BlockSpec((SG, (CONV_W - 1) * D), lambda g: (g, 0)),
            pl.BlockSpec((1, CONV_W - 1, D, SG), lambda g: (g, 0, 0, 0)),
            pl.BlockSpec((1, LANES), c2),
            pl.BlockSpec((CONV_W, D), c2),
            pl.BlockSpec((D, CONV_W), c2),
            pl.BlockSpec((1, D), c2),
            pl.BlockSpec((D, 1), c2),
            pl.BlockSpec((1, D), c2),
            pl.BlockSpec((SG, ML_H, ML_DK, ML_DV), lambda g: (g, 0, 0, 0)),
            pl.BlockSpec((SG, ML_H, ML_DK), lambda g: (g, 0, 0)),
            pl.BlockSpec((SG, LANES), lambda g: (g, 0)),
        ],
        out_specs=[
            pl.BlockSpec((SG, D), lambda g: (g, 0)),
            pl.BlockSpec((SG, ML_H, ML_DK, ML_DV), lambda g: (g, 0, 0, 0)),
            pl.BlockSpec((SG, ML_H, ML_DK), lambda g: (g, 0, 0)),
            pl.BlockSpec((SG, LANES), lambda g: (g, 0)),
            pl.BlockSpec((SG, (CONV_W - 1) * D), lambda g: (g, 0)),
        ],
        out_shape=[
            jax.ShapeDtypeStruct((NS, D), BF16),
            jax.ShapeDtypeStruct((NS, ML_H, ML_DK, ML_DV), F32),
            jax.ShapeDtypeStruct((NS, ML_H, ML_DK), F32),
            jax.ShapeDtypeStruct((NS, LANES), F32),
            jax.ShapeDtypeStruct((NS, (CONV_W - 1) * D), F32),
        ],
        scratch_shapes=[pltpu.VMEM((SG, D), F32)],
        compiler_params=_params(("parallel",)),
        name="odd_sample",
    )(z, zg, ut3, conv, convt, bg, cw, cwt, cb, cbt, hnw, c_in, n_in, m_in)


def _out_ln_router_kernel(x_ref, y_ref, w_ref, g_ref, b_ref, wr_ref, tri_ref,
                          o_ref, op_ref, meta_ref, cnt_ref, tab_ref, carry, filled):
    i = pl.program_id(0)

    @pl.when(i == 0)
    def _():
        carry[...] = jnp.zeros_like(carry)
        tab_ref[...] = jnp.zeros_like(tab_ref)
        for e in range(N_EXPERTS):
            filled[e] = 0

    r = ALPHA * x_ref[...] + _dot(y_ref[...], w_ref[...])
    x3 = _layernorm(r, g_ref[...], b_ref[...])
    o_ref[...] = x3
    op_ref[...] = pltpu.pack_elementwise([x3[:, :D // 2], x3[:, D // 2:]], packed_dtype=BF16)

    lane = lax.broadcasted_iota(jnp.int32, (TM, LANES), 1).astype(F32)
    logits = jnp.where(lane < N_EXPERTS, _dot(x3.astype(BF16), wr_ref[...]), -jnp.inf)
    m1 = jnp.max(logits, axis=-1, keepdims=True)
    i1 = jnp.min(jnp.where(logits == m1, lane, float(LANES)), axis=-1, keepdims=True)
    rest = jnp.where(lane == i1, -jnp.inf, logits)
    m2 = jnp.max(rest, axis=-1, keepdims=True)
    i2 = jnp.min(jnp.where(rest == m2, lane, float(LANES)), axis=-1, keepdims=True)
    e2 = jnp.exp(m2 - m1)
    tot = 1.0 + e2
    w1 = 1.0 / tot
    w2 = e2 / tot

    sel1 = lane == i1
    sel2 = lane == i2
    onehot = jnp.where(sel1 | sel2, 1.0, 0.0)
    in_tile = _dot(tri_ref[...], onehot.astype(BF16))
    carry[...] = carry[...] + jnp.sum(onehot, axis=0, keepdims=True)
    cnt_ref[...] = carry[...]

    meta = jnp.where(lane == 0.0, i1, 0.0)
    meta = jnp.where(lane == 1.0, i2, meta)
    meta = jnp.where(lane == 2.0, w1, meta)
    meta = jnp.where(lane == 3.0, w2, meta)
    meta_ref[...] = meta

    token = (i * TM + lax.broadcasted_iota(jnp.int32, (TM, 1), 0)).astype(F32)
    digit_hi = jnp.floor(token * (1.0 / 256.0))
    rhs = jnp.where(lane == 0.0, digit_hi, jnp.where(lane == 1.0, token - 256.0 * digit_hi, 0.0))
    rhs = jnp.where(lane == i2 + float(SUBLANES), 1.0, rhs).astype(BF16)
    place = lax.broadcasted_iota(jnp.int32, (LANES, TM), 0).astype(F32)
    rank_rows = jnp.where(onehot > 0.0, in_tile, -1.0).T

    def entries(e, part):
        hit = jnp.where(rank_rows[e:e + 1, :] == place + float(part * LANES), 1.0, 0.0).astype(BF16)
        got = _dot(hit, rhs)
        return 256.0 * got[:, 0:1] + got[:, 1:2] + float(NT) * got[:, SUBLANES + e:SUBLANES + e + 1]

    first = [entries(e, 0) for e in range(N_EXPERTS)]
    tile_counts = jnp.sum(onehot, axis=0, keepdims=True)
    counts = [jnp.sum(tile_counts[:, e:e + 1]).astype(jnp.int32) for e in range(N_EXPERTS)]
    for e in range(N_EXPERTS):
        start = filled[e]
        filled[e] = start + counts[e]
        tab_ref[pl.ds(start, LANES), e:e + 1] = first[e]
        for part in range(1, TM // LANES):
            @pl.when(counts[e] > part * LANES)
            def _():
                tab_ref[pl.ds(start + part * LANES, LANES), e:e + 1] = entries(e, part)


def _out_ln_router(x, y, w, g, b, wr, tri):
    c2 = lambda i: (0, 0)
    return pl.pallas_call(
        _out_ln_router_kernel,
        grid=(NT // TM,),
        in_specs=[
            pl.BlockSpec((TM, D), lambda i: (i, 0)),
            pl.BlockSpec((TM, D), lambda i: (i, 0)),
            pl.BlockSpec((D, D), c2),
            pl.BlockSpec((1, D), c2),
            pl.BlockSpec((1, D), c2),
            pl.BlockSpec((D, LANES), c2),
            pl.BlockSpec((TM, TM), c2),
        ],
        out_specs=[
            pl.BlockSpec((TM, D), lambda i: (i, 0)),
            pl.BlockSpec((TM, D // 2), lambda i: (i, 0)),
            pl.BlockSpec((TM, LANES), lambda i: (i, 0)),
            pl.BlockSpec((1, LANES), c2),
            pl.BlockSpec((MOE_CAP + TM, LANES), c2),
        ],
        out_shape=[
            jax.ShapeDtypeStruct((NT, D), F32),
            jax.ShapeDtypeStruct((NT, D // 2), U32),
            jax.ShapeDtypeStruct((NT, LANES), F32),
            jax.ShapeDtypeStruct((1, LANES), F32),
            jax.ShapeDtypeStruct((MOE_CAP + TM, LANES), F32),
        ],
        scratch_shapes=[pltpu.VMEM((1, LANES), F32), pltpu.SMEM((N_EXPERTS,), jnp.int32)],
        compiler_params=_params(("arbitrary",)),
        name="out_ln_router",
    )(x, y, w, g, b, wr, tri)


def _moe_ffn_kernel(te_ref, nu_ref, tb_ref, gnext_ref, gcur_ref, sprev_ref, scur_ref, xp_ref,
                    w1_ref, w3_ref, w2_ref, out_hbm, stage, yacc, xb_scr, sem_s):
    del tb_ref
    i = pl.program_id(0)
    j = pl.program_id(1)
    used = i < nu_ref[0]
    slot = i % 2
    other = 1 - slot
    rps = MOE_ROWS_PER_STEP

    def gather_rows(tab_ref, buf, part):
        for r in range(rps):
            stage[buf, part, pl.ds(r, 1), :] = xp_ref[pl.ds(tab_ref[part * rps + r], 1), :]

    def scatter(buf, r, dst):
        return pltpu.make_async_copy(yacc.at[buf, pl.ds(r, 1)], out_hbm.at[pl.ds(dst, 1)], sem_s)

    def wait_scatters(n):
        for _ in range(n):
            scatter(0, 0, 0).wait()

    def issue_neighbours():
        gather_rows(gnext_ref, other, j)
        for r in range(rps):
            rr = j * rps + r
            scatter(other, rr, sprev_ref[rr]).start()

    @pl.when(j == 0)
    def _():
        @pl.when(i == 0)
        def _():
            yacc[1] = jnp.zeros((TMM, D), F32)
            for part in range(MOE_NFF):
                gather_rows(gcur_ref, 0, part)

        @pl.when(i > 0)
        def _():
            wait_scatters(TMM)

        words = stage[slot].reshape(TMM, D // 2)
        for half in range(2):
            xb_scr[:, half * (D // 2):(half + 1) * (D // 2)] = pltpu.unpack_elementwise(
                words, index=half, packed_dtype=BF16, unpacked_dtype=F32).astype(BF16)
        yacc[slot] = jnp.zeros((TMM, D), F32)

    @pl.when(used)
    def _():
        issue_neighbours()
        xb = xb_scr[...]
        hmid = _silu(_dot(xb, w1_ref[...])) * _dot(xb, w3_ref[...])
        yacc[slot] += _dot(hmid.astype(BF16), w2_ref[...])

    @pl.when(jnp.logical_not(used))
    def _():
        issue_neighbours()

    @pl.when((i == N_MOE_TILES - 1) & (j == MOE_NFF - 1))
    def _():
        for r in range(TMM):
            scatter(slot, r, scur_ref[r]).start()
        wait_scatters(2 * TMM)


def _moe_ffn(tile_expert, n_used, tab_block, gsrc, sdst, xp, w1, w3, w2):
    nff = MOE_NFF

    def wcol(i, j, te, nu, tb):
        return (te[i], 0, jnp.where(i < nu[0], j, nff - 1))

    def wrow(i, j, te, nu, tb):
        return (te[i], jnp.where(i < nu[0], j, nff - 1), 0)

    smem = functools.partial(pl.BlockSpec, (MOE_TAB,), memory_space=pltpu.SMEM)
    grid_spec = pltpu.PrefetchScalarGridSpec(
        num_scalar_prefetch=3,
        grid=(N_MOE_TILES, nff),
        in_specs=[
            smem(lambda i, j, te, nu, tb: (tb[i + 2],)),
            smem(lambda i, j, te, nu, tb: (tb[i + 1],)),
            smem(lambda i, j, te, nu, tb: (tb[i],)),
            smem(lambda i, j, te, nu, tb: (tb[i + 1],)),
            pl.BlockSpec((NT, D // 2), lambda i, j, te, nu, tb: (0, 0), pipeline_mode=pl.Buffered(1)),
            pl.BlockSpec((None, D, TFF), wcol),
            pl.BlockSpec((None, D, TFF), wcol),
            pl.BlockSpec((None, TFF, D), wrow),
        ],
        out_specs=pl.BlockSpec(memory_space=pl.ANY),
        scratch_shapes=[
            pltpu.VMEM((2, MOE_NFF, MOE_ROWS_PER_STEP, D // 2), U32),
            pltpu.VMEM((2, TMM, D), F32),
            pltpu.VMEM((TMM, D), BF16),
            pltpu.SemaphoreType.DMA(()),
        ],
    )
    return pl.pallas_call(
        _moe_ffn_kernel,
        grid_spec=grid_spec,
        out_shape=jax.ShapeDtypeStruct((MOE_OUT_ROWS, D), F32),
        compiler_params=_params(("arbitrary", "arbitrary")),
        name="moe_ffn",
    )(tile_expert, n_used, tab_block, gsrc, gsrc, sdst, sdst, xp, w1, w3, w2)


def _combine_kernel(x_ref, meta_ref, y0_ref, y1_ref, g_ref, b_ref, o_ref):
    meta = meta_ref[...]
    moe = meta[:, 2:3] * y0_ref[...] + meta[:, 3:4] * y1_ref[...]
    o_ref[...] = _layernorm(ALPHA * x_ref[...] + moe, g_ref[...], b_ref[...])


def _combine(x, meta, ys, g, b):
    c2 = lambda i: (0, 0)
    return pl.pallas_call(
        _combine_kernel,
        grid=(NT // TM,),
        in_specs=[
            pl.BlockSpec((TM, D), lambda i: (i, 0)),
            pl.BlockSpec((TM, LANES), lambda i: (i, 0)),
            pl.BlockSpec((TM, D), lambda i: (i, 0)),
            pl.BlockSpec((TM, D), lambda i: (i + NT // TM, 0)),
            pl.BlockSpec((1, D), c2),
            pl.BlockSpec((1, D), c2),
        ],
        out_specs=pl.BlockSpec((TM, D), lambda i: (i, 0)),
        out_shape=jax.ShapeDtypeStruct((NT, D), F32),
        compiler_params=_params(("parallel",)),
        name="moe_combine",
    )(x, meta, ys, ys, g, b)


def _pad_cols(w, n):
    return jnp.pad(w, ((0, 0), (0, n - w.shape[1])))


def kernel(x_prompt, x_sample, state_hgrn, state_gla, state_mlstm_C, state_mlstm_n, state_mlstm_m,
           state_mlstm_conv, w_in_even, hg_lower_bounds, w_gk, b_gk, gn_hg, gn_gla, w_out_even,
           w1_dense, w3_dense, w2_dense, w_in_odd, b_gate_odd, conv_w, conv_b, hn_w, w_out_odd,
           w_router, w1_moe, w3_moe, w2_moe, ln1_g, ln1_b, ln2_g, ln2_b):
    assert x_prompt.shape == (BATCH, SEQ, D) and x_sample.shape == (NS, 1, D)
    assert w_in_even.shape[0] == 1 and w_in_odd.shape[0] == 1 and hg_lower_bounds.shape[0] == 2
    masks = jnp.asarray(_gla_masks(), F32)
    tri_cs = jnp.asarray(_tri(CS, False), BF16)
    tri_tm = jnp.asarray(_tri(TM, True), BF16)
    row = lambda a: a.reshape(1, -1)

    x0 = x_prompt.reshape(NP, D)
    x0_tail = jnp.concatenate([x0[LAST_TILE * TM:], x_sample.reshape(NS, D)], axis=0)

    w_even = w_in_even[0].astype(BF16)
    z, zgr = _proj(x0, x0_tail, w_even[:, :EVEN_MAIN], _pad_cols(w_even[:, EVEN_MAIN:], LANES))
    wgk = jnp.pad(w_gk[0].astype(BF16), ((0, LANES - GLA_RANK), (0, 0)))
    lbp = hg_lower_bounds
    zs = z[NP:].reshape(NS // SG, SG, EVEN_MAIN).transpose(0, 2, 1)
    grs = zgr[NP:].reshape(NS // SG, SG, LANES).transpose(0, 2, 1)
    y_s, hg_s, gla_s = _even_sample(z, zs, grs, lbp.T, wgk.T, b_gk[0].reshape(-1, 1),
                                    row(gn_hg[0]), row(gn_gla[0]), state_hgrn[0], state_gla[0])
    y, hg_p, gla_p = _even_prompt(z, zgr, y_s, lbp, wgk, row(b_gk[0]), row(gn_hg[0]), row(gn_gla[0]), tri_cs, masks)
    x2 = _ffn(x0, x0_tail, y, w_out_even[0].astype(BF16), row(ln1_g[0]), row(ln1_b[0]),
              w1_dense[0].astype(BF16), w3_dense[0].astype(BF16), w2_dense[0].astype(BF16),
              row(ln2_g[0]), row(ln2_b[0]))

    w_odd = w_in_odd[0].astype(BF16)
    zo, zog = _proj(x2, x2[LAST_TILE * TM:], w_odd[:, :ODD_MAIN], _pad_cols(w_odd[:, ODD_MAIN:], LANES))
    bg = jnp.pad(b_gate_odd[0], (0, LANES - 2 * ML_H)).reshape(1, LANES)
    ut = zo[NP:, :D].reshape(NS // SG, SG, D).transpose(0, 2, 1)
    conv_in = state_mlstm_conv[0]
    conv_t = conv_in.reshape(NS // SG, SG, CONV_W - 1, D).transpose(0, 2, 3, 1)
    m_in = jnp.pad(state_mlstm_m[0], ((0, 0), (0, LANES - ML_H)))
    yo_s, c_s, n_s, m_s, conv_s = _odd_sample(
        zo, zog, ut, conv_in.reshape(NS, (CONV_W - 1) * D), conv_t, bg, conv_w[0], conv_w[0].T, row(conv_b[0]), conv_b[0].reshape(-1, 1),
        row(hn_w[0]), state_mlstm_C[0], state_mlstm_n[0], m_in)
    yo, c_p, n_p, m_p, conv_p = _odd_prompt(zo, zog, yo_s, bg, conv_w[0], row(conv_b[0]), row(hn_w[0]), tri_cs)

    wr = _pad_cols(w_router[0].astype(BF16), LANES)
    x3, x3p, meta, cnt, tab = _out_ln_router(
        x2, yo, w_out_odd[0].astype(BF16), row(ln1_g[1]), row(ln1_b[1]), wr, tri_tm)

    counts = cnt[0, :N_EXPERTS].astype(jnp.int32)
    padded = ((counts + TMM - 1) // TMM) * TMM
    ends = jnp.cumsum(padded)
    offsets = ends - padded
    tile = jnp.arange(N_MOE_TILES, dtype=jnp.int32)
    tile_expert = jnp.minimum(jnp.sum((tile * TMM)[:, None] >= ends[None, :], axis=1), N_EXPERTS - 1).astype(jnp.int32)
    n_tiles_used = ends[-1] // TMM
    in_use = tile < n_tiles_used
    local_tile = (tile * TMM - offsets[tile_expert]) // TMM
    blocks_per_expert = MOE_CAP // MOE_TAB
    spill_block = N_EXPERTS * blocks_per_expert
    n_spare = N_MOE_TILES - (2 * NT) // TMM
    entry = tab[:MOE_CAP, :N_EXPERTS].T.astype(jnp.int32)
    local = jnp.arange(MOE_CAP, dtype=jnp.int32)[None, :]
    valid = local < counts[:, None]
    padding = padded - counts
    pad_before = (jnp.cumsum(padding) - padding)[:, None]
    row_in_block = jnp.arange(MOE_TAB, dtype=jnp.int32)
    spare_rows = (2 * NT + TMM + jnp.sum(padding)
                  + jnp.arange(n_spare, dtype=jnp.int32)[:, None] * TMM + row_in_block[None, :])
    sdst = jnp.concatenate([
        jnp.where(valid, entry, 2 * NT + TMM + pad_before + local - counts[:, None]).reshape(-1),
        2 * NT + row_in_block, spare_rows.reshape(-1)])
    gsrc = jnp.concatenate([
        jnp.where(valid, entry - jnp.where(entry >= NT, NT, 0), 0).reshape(-1),
        jnp.zeros(((1 + n_spare) * MOE_TAB,), jnp.int32)])
    own_block = jnp.where(in_use, tile_expert * blocks_per_expert + local_tile, spill_block + 1 + tile - n_tiles_used)
    tab_block = jnp.concatenate([jnp.full((1,), spill_block, jnp.int32), own_block.astype(jnp.int32),
                                 jnp.full((1,), spill_block, jnp.int32)])

    ys = _moe_ffn(tile_expert, n_tiles_used.reshape(1), tab_block, gsrc, sdst, x3p,
                  w1_moe[0].astype(BF16), w3_moe[0].astype(BF16), w2_moe[0].astype(BF16))
    out = _combine(x3, meta, ys, row(ln2_g[1]), row(ln2_b[1]))

    y_prompt = out[:NP].reshape(BATCH, SEQ, D)
    y_sample = out[NP:].reshape(NS, 1, D)
    return (y_prompt, y_sample,
            hg_p.reshape(1, BATCH, HG_H, HG_DK, HG_DV), gla_p.reshape(1, BATCH, GLA_H, GLA_DK, GLA_DV),
            c_p[None], n_p[None], m_p[:, 0, :ML_H][None], conv_p[None],
            hg_s[None], gla_s[None], c_s[None], n_s[None], m_s[:, :ML_H][None], conv_s.reshape(1, NS, CONV_W - 1, D))
```

```python
import functools
import math

import jax
import jax.numpy as jnp
import numpy as np
from jax import lax
from jax.experimental import pallas as pl
from jax.experimental.pallas import tpu as pltpu

F32 = jnp.float32
BF16 = jnp.bfloat16
U32 = jnp.uint32

D = 1024
BATCH = 8
SEQ = 2048
DEC_BATCH = 128
NP = BATCH * SEQ
NS = DEC_BATCH
NT = NP + NS
HG_H, HG_DK, HG_DV = 4, 128, 128
GLA_H, GLA_DK, GLA_DV = 4, 64, 128
GLA_RANK = 16
GLA_GATE_NORM = 16.0
ML_H, ML_DK, ML_DV = 4, 128, 256
CONV_W = 4
D_FF_DENSE = 2816
D_FF_EXPERT = 3584
N_EXPERTS = 8
EPS = 1e-5
DEPTH = 2
ALPHA = (2.0 * DEPTH) ** 0.25
EVEN_MAIN = 3584
ODD_MAIN = 3072

LANES = 128
SUBLANES = 8
VMEM_LIMIT = 56 * 1024 * 1024

TM = 384
LAST_TILE = NT // TM - 1
CS = 128
NCHUNK = SEQ // CS
SG = 16
TMM = 512
TFF = 896
MOE_TAB = 512
MOE_CAP = -(-NT // MOE_TAB) * MOE_TAB
MOE_NFF = D_FF_EXPERT // TFF
MOE_ROWS_PER_STEP = TMM // MOE_NFF
N_MOE_TILES = -(-(2 * NT + N_EXPERTS * (TMM - 1)) // TMM)
MOE_SLOTS = N_MOE_TILES * TMM
MOE_OUT_ROWS = MOE_SLOTS + TMM
N_LEVELS = int(math.log2(CS))

assert NT % TM == 0 and NP % CS == 0 and NS % SG == 0 and D_FF_EXPERT % TFF == 0 and TMM % MOE_NFF == 0
assert TMM == MOE_TAB and NS == CS and LAST_TILE * TM <= NP


def _params(sem, limit=VMEM_LIMIT):
    return pltpu.CompilerParams(dimension_semantics=sem, vmem_limit_bytes=limit)


def _dot(a, b):
    return jnp.dot(a, b, preferred_element_type=F32)


def _dot_nt(a, b):
    return lax.dot_general(a, b, (((1,), (1,)), ((), ())), preferred_element_type=F32)


def _dot_tn(a, b):
    return lax.dot_general(a, b, (((0,), (0,)), ((), ())), preferred_element_type=F32)


def _split3(x):
    hi = x.astype(BF16)
    r1 = x - hi.astype(F32)
    mid = r1.astype(BF16)
    lo = (r1 - mid.astype(F32)).astype(BF16)
    return hi, mid, lo


def _dot_sel(sel, x):
    hi, mid, lo = _split3(x)
    return _dot(sel, hi) + _dot(sel, mid) + _dot(sel, lo)


def _sigmoid(x):
    return jax.nn.sigmoid(x)


def _silu(x):
    return x * jax.nn.sigmoid(x)


def _log_sigmoid(x):
    return jnp.minimum(x, 0.0) - jnp.log(1.0 + jnp.exp(-jnp.abs(x)))


def _layernorm(r, g, b):
    mu = jnp.mean(r, axis=-1, keepdims=True)
    c = r - mu
    var = jnp.mean(c * c, axis=-1, keepdims=True)
    return c * lax.rsqrt(var + EPS) * g + b


def _gla_masks():
    masks = np.zeros((N_LEVELS + 1, CS, CS), np.float32)
    for t in range(CS):
        for l in range(N_LEVELS):
            half = 1 << l
            start = (t // (2 * half)) * (2 * half)
            mid = start + half
            if t >= mid:
                masks[l, t, start:mid] = 1.0
        masks[N_LEVELS, t, t] = 1.0
    return masks


def _tri(n, strict):
    return np.tril(np.ones((n, n), np.float32), -1 if strict else 0)


def _token_tile(x_ref, tail_ref):
    return jnp.where(pl.program_id(0) == LAST_TILE, tail_ref[...], x_ref[...])


def _main_tile(i):
    return (jnp.minimum(i, LAST_TILE - 1), 0)


def _proj_kernel(x_ref, tail_ref, wa_ref, wb_ref, oa_ref, ob_ref):
    xb = _token_tile(x_ref, tail_ref).astype(BF16)
    oa_ref[...] = _dot(xb, wa_ref[...])
    ob_ref[...] = _dot(xb, wb_ref[...])


def _proj(x, tail, wa, wb):
    na, nb = wa.shape[1], wb.shape[1]
    return pl.pallas_call(
        _proj_kernel,
        grid=(NT // TM,),
        in_specs=[
            pl.BlockSpec((TM, D), _main_tile),
            pl.BlockSpec((TM, D), lambda i: (0, 0)),
            pl.BlockSpec((D, na), lambda i: (0, 0)),
            pl.BlockSpec((D, nb), lambda i: (0, 0)),
        ],
        out_specs=[
            pl.BlockSpec((TM, na), lambda i: (i, 0)),
            pl.BlockSpec((TM, nb), lambda i: (i, 0)),
        ],
        out_shape=[jax.ShapeDtypeStruct((NT, na), F32), jax.ShapeDtypeStruct((NT, nb), F32)],
        compiler_params=_params(("parallel",)),
        name="proj",
    )(x, tail, wa, wb)


def _rms_gate(o, gate, w):
    o = o * lax.rsqrt(jnp.mean(o * o, axis=-1, keepdims=True) + EPS) * w
    return o * _silu(gate)


def _level_decays(g, bc):
    width = g.shape[1]
    ng = CS // SUBLANES
    shape3 = (ng, SUBLANES, width)
    bc3 = bc.reshape(shape3)
    sub = lax.broadcasted_iota(jnp.int32, shape3, 1)

    def group_row(s):
        return jnp.broadcast_to(bc3[:, s:s + 1, :], shape3)

    last = group_row(SUBLANES - 1)
    refs = [None,
            jnp.where(sub < 4, group_row(1), group_row(5)),
            group_row(3)]
    for l in range(3, N_LEVELS):
        per_block = 1 << (l - 2)
        grouped = last.reshape(ng // per_block, per_block, SUBLANES, width)
        ref = jnp.broadcast_to(grouped[:, per_block // 2 - 1:per_block // 2], grouped.shape)
        refs.append(ref.reshape(shape3))
    decays = [jnp.exp(jnp.where((sub & 1) == 1, g.reshape(shape3), 0.0))]
    decays += [jnp.exp(-jnp.abs(bc3 - ref)) for ref in refs[1:]]
    to_end = jnp.exp(jnp.broadcast_to(last[ng - 1:ng], shape3) - bc3)
    return [d.reshape(CS, width) for d in decays], to_end.reshape(CS, width)


def _gla_chunk(q, k, v, g, st_ref, tri, masks_ref, heads, dk, dv):
    bc = _dot_sel(tri, g)
    z_levels, z_end = _level_decays(g, bc)
    z_cum = jnp.exp(bc)
    st = st_ref[...]
    outs = []
    for h in range(heads):
        ks = slice(h * dk, (h + 1) * dk)
        vs = slice(h * dv, (h + 1) * dv)
        qh, kh = q[:, ks], k[:, ks]
        vh = v[:, vs].astype(BF16)
        scores = _dot_nt(qh.astype(BF16), kh.astype(BF16)) * masks_ref[N_LEVELS]
        for l in range(N_LEVELS):
            zl = z_levels[l][:, ks]
            scores = scores + _dot_nt((qh * zl).astype(BF16), (kh * zl).astype(BF16)) * masks_ref[l]
        o = _dot(scores.astype(BF16), vh)
        o = o + _dot_nt((qh * z_cum[:, ks]).astype(BF16), st[:, ks].astype(BF16))
        outs.append(o)
        upd = _dot_tn(vh, (kh * z_end[:, ks]).astype(BF16))
        st_ref[:, ks] = st[:, ks] * z_cum[CS - 1:CS, ks] + upd
    return outs


def _even_prompt_kernel(z_ref, zgr_ref, ys_ref, lbp_ref, wgk_ref, bgk_ref, gnh_ref, gng_ref, tri_ref, masks_ref,
                        y_ref, shg_ref, sgla_ref, st_hg, st_gla):
    b = pl.program_id(0)
    c = pl.program_id(1)

    @pl.when((b < BATCH) & (c == 0))
    def _():
        st_hg[...] = jnp.zeros_like(st_hg)
        st_gla[...] = jnp.zeros_like(st_gla)

    @pl.when(b < BATCH)
    def _():
        tri = tri_ref[...]
        p = lbp_ref[...]
        pe = jnp.exp(p - jnp.max(p, axis=0, keepdims=True))
        lb = pe[0:1] / jnp.sum(pe, axis=0, keepdims=True)

        z = z_ref[...]
        hq, hf, hi, hg = z[:, 0:512], z[:, 512:1024], z[:, 1024:1536], z[:, 1536:2048]
        gq, gk, gv, gg = z[:, 2048:2304], z[:, 2304:2560], z[:, 2560:3072], z[:, 3072:3584]
        f = lb + (1.0 - lb) * _sigmoid(hf)
        k_hg = (1.0 - lb) * _sigmoid(-hf)
        o_hg = _gla_chunk(_silu(hq), k_hg, hi, jnp.log(f), st_hg, tri, masks_ref, HG_H, HG_DK, HG_DV)

        la = _log_sigmoid(_dot(zgr_ref[...].astype(BF16), wgk_ref[...]) + bgk_ref[...]) / GLA_GATE_NORM
        o_gla = _gla_chunk(gq * GLA_DK ** -0.5, gk, gv, la, st_gla, tri, masks_ref, GLA_H, GLA_DK, GLA_DV)

        for h in range(HG_H):
            cs = slice(h * 128, (h + 1) * 128)
            y_ref[:, cs] = _rms_gate(o_hg[h], hg[:, cs], gnh_ref[...]).astype(BF16)
        for h in range(GLA_H):
            cs = slice(h * 128, (h + 1) * 128)
            y_ref[:, 512 + h * 128:512 + (h + 1) * 128] = _rms_gate(o_gla[h], gg[:, cs], gng_ref[...]).astype(BF16)

    @pl.when((b < BATCH) & (c == NCHUNK - 1))
    def _():
        shg_ref[0] = st_hg[...].T
        sgla_ref[0] = st_gla[...].T

    @pl.when((b == BATCH) & (c == 0))
    def _():
        y_ref[...] = ys_ref[...]


def _chunk_rows(b, c):
    return (jnp.minimum(b * NCHUNK + c, NP // CS), 0)


def _per_sequence(b, c):
    return (jnp.minimum(b, BATCH - 1), 0, 0)


def _even_prompt(z, zgr, y_sample, lbp, wgk, bgk, gnh, gng, tri, masks):
    const2 = lambda b, c: (0, 0)
    return pl.pallas_call(
        _even_prompt_kernel,
        grid=(BATCH + 1, NCHUNK),
        in_specs=[
            pl.BlockSpec((CS, EVEN_MAIN), _chunk_rows),
            pl.BlockSpec((CS, LANES), _chunk_rows),
            pl.BlockSpec((NS, D), const2),
            pl.BlockSpec(lbp.shape, const2),
            pl.BlockSpec(wgk.shape, const2),
            pl.BlockSpec(bgk.shape, const2),
            pl.BlockSpec(gnh.shape, const2),
            pl.BlockSpec(gng.shape, const2),
            pl.BlockSpec(tri.shape, const2),
            pl.BlockSpec(masks.shape, lambda b, c: (0, 0, 0)),
        ],
        out_specs=[
            pl.BlockSpec((CS, D), _chunk_rows),
            pl.BlockSpec((1, HG_H * HG_DK, HG_DV), _per_sequence),
            pl.BlockSpec((1, GLA_H * GLA_DK, GLA_DV), _per_sequence),
        ],
        out_shape=[
            jax.ShapeDtypeStruct((NT, D), BF16),
            jax.ShapeDtypeStruct((BATCH, HG_H * HG_DK, HG_DV), F32),
            jax.ShapeDtypeStruct((BATCH, GLA_H * GLA_DK, GLA_DV), F32),
        ],
        scratch_shapes=[pltpu.VMEM((HG_DV, HG_H * HG_DK), F32), pltpu.VMEM((GLA_DV, GLA_H * GLA_DK), F32)],
        compiler_params=_params(("arbitrary", "arbitrary")),
        name="even_prompt",
    )(z, zgr, y_sample, lbp, wgk, bgk, gnh, gng, tri, masks)


def _even_sample_kernel(zr_ref, zt_ref, grt_ref, lbpt_ref, wgkt_ref, bgkt_ref, gnh_ref, gng_ref,
                        shg_ref, sgla_ref, y_ref, shg_out, sgla_out, o_scr):
    zt = zt_ref[0]
    hq_t, hf_t = zt[0:512], zt[512:1024]
    gq_t, gk_t = zt[2048:2304], zt[2304:2560]
    pt = lbpt_ref[...]
    pe = jnp.exp(pt - jnp.max(pt, axis=1, keepdims=True))
    lb = pe[:, 0:1] / jnp.sum(pe, axis=1, keepdims=True)
    a_hg = jnp.exp(jnp.log(lb + (1.0 - lb) * _sigmoid(hf_t)))
    k_hg = (1.0 - lb) * _sigmoid(-hf_t)
    q_hg = _silu(hq_t)
    la = _log_sigmoid(_dot(wgkt_ref[...], grt_ref[0].astype(BF16)) + bgkt_ref[...]) / GLA_GATE_NORM
    a_gla = jnp.exp(la)
    q_gla = gq_t * GLA_DK ** -0.5
    zr = zr_ref[...]
    hi, hg = zr[:, 1024:1536], zr[:, 1536:2048]
    gv, gg = zr[:, 2560:3072], zr[:, 3072:3584]

    for j in range(SG):
        for h in range(HG_H):
            ks = slice(h * HG_DK, (h + 1) * HG_DK)
            s_new = a_hg[ks, j:j + 1] * shg_ref[j, h] + k_hg[ks, j:j + 1] * hi[j:j + 1, h * 128:(h + 1) * 128]
            shg_out[j, h] = s_new
            o_scr[j:j + 1, h * 128:(h + 1) * 128] = jnp.sum(q_hg[ks, j:j + 1] * s_new, axis=0, keepdims=True)
        for h in range(GLA_H):
            ks = slice(h * GLA_DK, (h + 1) * GLA_DK)
            s_new = a_gla[ks, j:j + 1] * sgla_ref[j, h] + gk_t[ks, j:j + 1] * gv[j:j + 1, h * 128:(h + 1) * 128]
            sgla_out[j, h] = s_new
            o_scr[j:j + 1, 512 + h * 128:512 + (h + 1) * 128] = jnp.sum(
                q_gla[ks, j:j + 1] * s_new, axis=0, keepdims=True)

    o = o_scr[...]
    for h in range(HG_H):
        cs = slice(h * 128, (h + 1) * 128)
        y_ref[:, cs] = _rms_gate(o[:, cs], hg[:, cs], gnh_ref[...]).astype(BF16)
    for h in range(GLA_H):
        cs = slice(512 + h * 128, 512 + (h + 1) * 128)
        y_ref[:, cs] = _rms_gate(o[:, cs], gg[:, h * 128:(h + 1) * 128], gng_ref[...]).astype(BF16)


def _even_sample(z, zt3, grt3, lbpt, wgkt, bgkt, gnh, gng, s_hg, s_gla):
    c2 = lambda g: (0, 0)
    return pl.pallas_call(
        _even_sample_kernel,
        grid=(NS // SG,),
        in_specs=[
            pl.BlockSpec((SG, EVEN_MAIN), lambda g: (NP // SG + g, 0)),
            pl.BlockSpec((1, EVEN_MAIN, SG), lambda g: (g, 0, 0)),
            pl.BlockSpec((1, LANES, SG), lambda g: (g, 0, 0)),
            pl.BlockSpec(lbpt.shape, c2),
            pl.BlockSpec(wgkt.shape, c2),
            pl.BlockSpec(bgkt.shape, c2),
            pl.BlockSpec(gnh.shape, c2),
            pl.BlockSpec(gng.shape, c2),
            pl.BlockSpec((SG, HG_H, HG_DK, HG_DV), lambda g: (g, 0, 0, 0)),
            pl.BlockSpec((SG, GLA_H, GLA_DK, GLA_DV), lambda g: (g, 0, 0, 0)),
        ],
        out_specs=[
            pl.BlockSpec((SG, D), lambda g: (g, 0)),
            pl.BlockSpec((SG, HG_H, HG_DK, HG_DV), lambda g: (g, 0, 0, 0)),
            pl.BlockSpec((SG, GLA_H, GLA_DK, GLA_DV), lambda g: (g, 0, 0, 0)),
        ],
        out_shape=[
            jax.ShapeDtypeStruct((NS, D), BF16),
            jax.ShapeDtypeStruct((NS, HG_H, HG_DK, HG_DV), F32),
            jax.ShapeDtypeStruct((NS, GLA_H, GLA_DK, GLA_DV), F32),
        ],
        scratch_shapes=[pltpu.VMEM((SG, D), F32)],
        compiler_params=_params(("parallel",)),
        name="even_sample",
    )(z, zt3, grt3, lbpt, wgkt, bgkt, gnh, gng, s_hg, s_gla)


FF_SPLIT = 1


def _ffn_kernel(x_ref, tail_ref, y_ref, wo_ref, g1_ref, b1_ref, w1_ref, w3_ref, w2_ref, g_ref, b_ref,
                wa_ref, wb_ref, o_ref, za_ref, zb_ref):
    x = _layernorm(ALPHA * _token_tile(x_ref, tail_ref) + _dot(y_ref[...], wo_ref[...]), g1_ref[...], b1_ref[...])
    xb = x.astype(BF16)
    step = D_FF_DENSE // FF_SPLIT
    acc = ALPHA * x
    for s in range(FF_SPLIT):
        cs = slice(s * step, (s + 1) * step)
        hmid = _silu(_dot(xb, w1_ref[:, cs])) * _dot(xb, w3_ref[:, cs])
        acc = acc + _dot(hmid.astype(BF16), w2_ref[cs, :])
    out = _layernorm(acc, g_ref[...], b_ref[...])
    o_ref[...] = out
    ob = out.astype(BF16)
    za_ref[...] = _dot(ob, wa_ref[...])
    zb_ref[...] = _dot(ob, wb_ref[...])


def _ffn(x, tail, y, wo, g1, b1, w1, w3, w2, g, b, wa, wb):
    c2 = lambda i: (0, 0)
    one = pl.Buffered(1)
    na, nb = wa.shape[1], wb.shape[1]
    return pl.pallas_call(
        _ffn_kernel,
        grid=(NT // TM,),
        in_specs=[
            pl.BlockSpec((TM, D), _main_tile),
            pl.BlockSpec((TM, D), c2),
            pl.BlockSpec((TM, D), lambda i: (i, 0)),
            pl.BlockSpec((D, D), c2, pipeline_mode=one),
            pl.BlockSpec((1, D), c2),
            pl.BlockSpec((1, D), c2),
            pl.BlockSpec((D, D_FF_DENSE), c2, pipeline_mode=one),
            pl.BlockSpec((D, D_FF_DENSE), c2, pipeline_mode=one),
            pl.BlockSpec((D_FF_DENSE, D), c2, pipeline_mode=one),
            pl.BlockSpec((1, D), c2),
            pl.BlockSpec((1, D), c2),
            pl.BlockSpec((D, na), c2, pipeline_mode=one),
            pl.BlockSpec((D, nb), c2, pipeline_mode=one),
        ],
        out_specs=[
            pl.BlockSpec((TM, D), lambda i: (i, 0)),
            pl.BlockSpec((TM, na), lambda i: (i, 0)),
            pl.BlockSpec((TM, nb), lambda i: (i, 0)),
        ],
        out_shape=[jax.ShapeDtypeStruct((NT, D), F32), jax.ShapeDtypeStruct((NT, na), F32),
                   jax.ShapeDtypeStruct((NT, nb), F32)],
        compiler_params=_params(("parallel",)),
        name="ffn_dense",
    )(x, tail, y, wo, g1, b1, w1, w3, w2, g, b, wa, wb)


def _mh_norm_gate(hh, o_pre, w):
    mu = jnp.mean(hh, axis=-1, keepdims=True)
    c = hh - mu
    var = jnp.mean(c * c, axis=-1, keepdims=True)
    return _sigmoid(o_pre) * (c * lax.rsqrt(var + EPS) * w)


def _odd_prompt_kernel(z_ref, zg_ref, ys_ref, bg_ref, cw_ref, cb_ref, hnw_ref, tri_ref,
                       y_ref, c_out, n_out, m_out, conv_out,
                       c_scr, n_scr, m_scr, u_scr):
    b = pl.program_id(0)
    c = pl.program_id(1)

    @pl.when((b < BATCH) & (c == 0))
    def _():
        c_scr[...] = jnp.zeros_like(c_scr)
        n_scr[...] = jnp.zeros_like(n_scr)
        m_scr[...] = jnp.zeros_like(m_scr)
        u_scr[0:8, :] = jnp.zeros((8, D), F32)

    @pl.when((b == BATCH) & (c == 0))
    def _():
        y_ref[...] = ys_ref[...]

    @pl.when(b < BATCH)
    def _():
        row = lax.broadcasted_iota(jnp.int32, (CS, CS), 0)
        col = lax.broadcasted_iota(jnp.int32, (CS, CS), 1)
        causal = col <= row

        u_scr[8:8 + CS, :] = z_ref[:, 0:D]
        uc = cb_ref[...]
        for j in range(CONV_W):
            uc = uc + u_scr[5 + j:5 + j + CS, :] * cw_ref[j:j + 1, :]
        tail = u_scr[CS:CS + 8, :]
        u_scr[0:8, :] = tail
        act = _silu(uc)
        q = act[:, 0:512] * ML_DK ** -0.5
        k = act[:, 512:1024]
        v = z_ref[:, D:2 * D]
        o_pre = z_ref[:, 2 * D:3 * D]

        gates = zg_ref[...] + bg_ref[...]
        lf = _log_sigmoid(gates)
        bcum = _dot_sel(tri_ref[...], lf)
        bcum_t = bcum.T
        gates_t = gates.T
        m_all = m_scr[...]

        for h in range(ML_H):
            ks = slice(h * ML_DK, (h + 1) * ML_DK)
            vs = slice(h * ML_DV, (h + 1) * ML_DV)
            qh, kh = q[:, ks], k[:, ks]
            vh = v[:, vs].astype(BF16)
            b_col = bcum[:, 4 + h:5 + h]
            b_row = bcum_t[4 + h:5 + h, :]
            i_col = gates[:, h:h + 1]
            i_row = gates_t[h:h + 1, :]
            m_prev = m_all[:, h:h + 1]
            log_d = jnp.where(causal, b_col - b_row + i_row, -jnp.inf)
            log_prev = b_col + m_prev
            m_t = jnp.maximum(jnp.max(log_d, axis=-1, keepdims=True), log_prev)
            d = jnp.exp(log_d - m_t)
            w_prev = jnp.exp(log_prev - m_t)
            scores = _dot_nt(qh.astype(BF16), kh.astype(BF16)) * d
            c_h = c_scr[h]
            n_h = n_scr[h:h + 1, :]
            num = _dot(scores.astype(BF16), vh) + w_prev * _dot(qh.astype(BF16), c_h.astype(BF16))
            den = jnp.sum(scores, axis=-1, keepdims=True) + w_prev * jnp.sum(qh * n_h, axis=-1, keepdims=True)
            hh = num / jnp.maximum(jnp.abs(den), jnp.exp(-m_t))
            m_new = m_t[CS - 1:CS, :]
            b_last = b_col[CS - 1:CS, :]
            w_c = jnp.exp(b_last + m_prev - m_new)
            w_s = jnp.exp(b_last - b_col + i_col - m_new)
            kw = kh * w_s
            c_scr[h] = w_c * c_h + _dot_tn(kw.astype(BF16), vh)
            n_scr[h:h + 1, :] = w_c * n_h + jnp.sum(kw, axis=0, keepdims=True)
            m_scr[:, h:h + 1] = m_new
            y_ref[:, vs] = _mh_norm_gate(hh, o_pre[:, vs], hnw_ref[:, vs]).astype(BF16)

        @pl.when(c == NCHUNK - 1)
        def _():
            c_out[0] = c_scr[...]
            n_out[0] = n_scr[0:ML_H, :]
            m_out[0] = m_scr[...]
            conv_out[0] = tail[8 - (CONV_W - 1):8, :]


def _odd_prompt(z, zg, y_sample, bg, cw, cb, hnw, tri):
    c2 = lambda b, c: (0, 0)
    per_sequence4 = lambda b, c: (jnp.minimum(b, BATCH - 1), 0, 0, 0)
    return pl.pallas_call(
        _odd_prompt_kernel,
        grid=(BATCH + 1, NCHUNK),
        in_specs=[
            pl.BlockSpec((CS, ODD_MAIN), _chunk_rows),
            pl.BlockSpec((CS, LANES), _chunk_rows),
            pl.BlockSpec((NS, D), c2),
            pl.BlockSpec((1, LANES), c2),
            pl.BlockSpec((CONV_W, D), c2),
            pl.BlockSpec((1, D), c2),
            pl.BlockSpec((1, D), c2),
            pl.BlockSpec((CS, CS), c2),
        ],
        out_specs=[
            pl.BlockSpec((CS, D), _chunk_rows),
            pl.BlockSpec((1, ML_H, ML_DK, ML_DV), per_sequence4),
            pl.BlockSpec((1, ML_H, ML_DK), _per_sequence),
            pl.BlockSpec((1, 1, LANES), _per_sequence),
            pl.BlockSpec((1, CONV_W - 1, D), _per_sequence),
        ],
        out_shape=[
            jax.ShapeDtypeStruct((NT, D), BF16),
            jax.ShapeDtypeStruct((BATCH, ML_H, ML_DK, ML_DV), F32),
            jax.ShapeDtypeStruct((BATCH, ML_H, ML_DK), F32),
            jax.ShapeDtypeStruct((BATCH, 1, LANES), F32),
            jax.ShapeDtypeStruct((BATCH, CONV_W - 1, D), F32),
        ],
        scratch_shapes=[
            pltpu.VMEM((ML_H, ML_DK, ML_DV), F32),
            pltpu.VMEM((8, ML_DK), F32),
            pltpu.VMEM((1, LANES), F32),
            pltpu.VMEM((CS + 8, D), F32),
        ],
        compiler_params=_params(("arbitrary", "arbitrary")),
        name="odd_prompt",
    )(z, zg, y_sample, bg, cw, cb, hnw, tri)


def _odd_sample_kernel(zr_ref, zg_ref, ut_ref, conv_ref, convt_ref, bg_ref, cw_ref, cwt_ref, cb_ref, cbt_ref,
                       hnw_ref, c_ref, n_ref, m_ref,
                       y_ref, c_out, n_out, m_out, conv_out, h_scr):
    zr = zr_ref[...]
    u = zr[:, 0:D]
    v = zr[:, D:2 * D]
    o_pre = zr[:, 2 * D:3 * D]
    uc = cb_ref[...] + u * cw_ref[CONV_W - 1:CONV_W, :]
    uc_t = cbt_ref[...] + ut_ref[0] * cwt_ref[:, CONV_W - 1:CONV_W]
    for j in range(CONV_W - 1):
        uc = uc + conv_ref[:, j * D:(j + 1) * D] * cw_ref[j:j + 1, :]
        uc_t = uc_t + convt_ref[0, j] * cwt_ref[:, j:j + 1]
        conv_out[:, j * D:(j + 1) * D] = conv_ref[:, (j + 1) * D:(j + 2) * D] if j + 1 < CONV_W - 1 else u
    act = _silu(uc)
    k_row = act[:, 512:1024]
    act_t = _silu(uc_t)
    q_t = act_t[0:512] * ML_DK ** -0.5
    k_t = act_t[512:1024]
    q_row = act[:, 0:512] * ML_DK ** -0.5

    gates = zg_ref[...] + bg_ref[...]
    lf = _log_sigmoid(gates)
    m_in = m_ref[...]
    m_out[...] = m_in

    for j in range(SG):
        for h in range(ML_H):
            ks = slice(h * ML_DK, (h + 1) * ML_DK)
            vs = slice(h * ML_DV, (h + 1) * ML_DV)
            ig = gates[j:j + 1, h:h + 1]
            log_prev = lf[j:j + 1, 4 + h:5 + h] + m_in[j:j + 1, h:h + 1]
            m_t = jnp.maximum(ig, log_prev)
            d = jnp.exp(ig - m_t)
            w_prev = jnp.exp(log_prev - m_t)
            c_new = w_prev * c_ref[j, h] + (d * k_t[ks, j:j + 1]) * v[j:j + 1, vs]
            n_new = w_prev * n_ref[j, h:h + 1, :] + d * k_row[j:j + 1, ks]
            c_out[j, h] = c_new
            n_out[j, h:h + 1, :] = n_new
            m_out[j:j + 1, h:h + 1] = m_t
            num = jnp.sum(q_t[ks, j:j + 1] * c_new, axis=0, keepdims=True)
            den = jnp.sum(q_row[j:j + 1, ks] * n_new, axis=-1, keepdims=True)
            h_scr[j:j + 1, vs] = num / jnp.maximum(jnp.abs(den), jnp.exp(-m_t))

    hh = h_scr[...]
    for h in range(ML_H):
        vs = slice(h * ML_DV, (h + 1) * ML_DV)
        y_ref[:, vs] = _mh_norm_gate(hh[:, vs], o_pre[:, vs], hnw_ref[:, vs]).astype(BF16)


def _odd_sample(z, zg, ut3, conv, convt, bg, cw, cwt, cb, cbt, hnw, c_in, n_in, m_in):
    c2 = lambda g: (0, 0)
    return pl.pallas_call(
        _odd_sample_kernel,
        grid=(NS // SG,),
        in_specs=[
            pl.BlockSpec((SG, ODD_MAIN), lambda g: (NP // SG + g, 0)),
            pl.BlockSpec((SG, LANES), lambda g: (NP // SG + g, 0)),
            pl.BlockSpec((1, D, SG), lambda g: (g, 0, 0)),
            pl.BlockSpec((SG, (CONV_W - 1) * D), lambda g: (g, 0)),
            pl.BlockSpec((1, CONV_W - 1, D, SG), lambda g: (g, 0, 0, 0)),
            pl.BlockSpec((1, LANES), c2),
            pl.BlockSpec((CONV_W, D), c2),
            pl.BlockSpec((D, CONV_W), c2),
            pl.BlockSpec((1, D), c2),
            pl.BlockSpec((D, 1), c2),
            pl.BlockSpec((1, D), c2),
            pl.BlockSpec((SG, ML_H, ML_DK, ML_DV), lambda g: (g, 0, 0, 0)),
            pl.BlockSpec((SG, ML_H, ML_DK), lambda g: (g, 0, 0)),
            pl.BlockSpec((SG, LANES), lambda g: (g, 0)),
        ],
        out_specs=[
            pl.BlockSpec((SG, D), lambda g: (g, 0)),
            pl.BlockSpec((SG, ML_H, ML_DK, ML_DV), lambda g: (g, 0, 0, 0)),
            pl.BlockSpec((SG, ML_H, ML_DK), lambda g: (g, 0, 0)),
            pl.BlockSpec((SG, LANES), lambda g: (g, 0)),
            pl.BlockSpec((SG, (CONV_W - 1) * D), lambda g: (g, 0)),
        ],
        out_shape=[
            jax.ShapeDtypeStruct((NS, D), BF16),
            jax.ShapeDtypeStruct((NS, ML_H, ML_DK, ML_DV), F32),
            jax.ShapeDtypeStruct((NS, ML_H, ML_DK), F32),
            jax.ShapeDtypeStruct((NS, LANES), F32),
            jax.ShapeDtypeStruct((NS, (CONV_W - 1) * D), F32),
        ],
        scratch_shapes=[pltpu.VMEM((SG, D), F32)],
        compiler_params=_params(("parallel",)),
        name="odd_sample",
    )(z, zg, ut3, conv, convt, bg, cw, cwt, cb, cbt, hnw, c_in, n_in, m_in)


def _out_ln_router_kernel(x_ref, y_ref, w_ref, g_ref, b_ref, wr_ref, tri_ref,
                          o_ref, op_ref, meta_ref, cnt_ref, tab_ref, carry, filled):
    i = pl.program_id(0)

    @pl.when(i == 0)
    def _():
        carry[...] = jnp.zeros_like(carry)
        tab_ref[...] = jnp.zeros_like(tab_ref)
        for e in range(N_EXPERTS):
            filled[e] = 0

    r = ALPHA * x_ref[...] + _dot(y_ref[...], w_ref[...])
    x3 = _layernorm(r, g_ref[...], b_ref[...])
    o_ref[...] = x3
    op_ref[...] = pltpu.pack_elementwise([x3[:, :D // 2], x3[:, D // 2:]], packed_dtype=BF16)

    lane = lax.broadcasted_iota(jnp.int32, (TM, LANES), 1).astype(F32)
    logits = jnp.where(lane < N_EXPERTS, _dot(x3.astype(BF16), wr_ref[...]), -jnp.inf)
    m1 = jnp.max(logits, axis=-1, keepdims=True)
    i1 = jnp.min(jnp.where(logits == m1, lane, float(LANES)), axis=-1, keepdims=True)
    rest = jnp.where(lane == i1, -jnp.inf, logits)
    m2 = jnp.max(rest, axis=-1, keepdims=True)
    i2 = jnp.min(jnp.where(rest == m2, lane, float(LANES)), axis=-1, keepdims=True)
    e2 = jnp.exp(m2 - m1)
    tot = 1.0 + e2
    w1 = 1.0 / tot
    w2 = e2 / tot

    sel1 = lane == i1
    sel2 = lane == i2
    onehot = jnp.where(sel1 | sel2, 1.0, 0.0)
    in_tile = _dot(tri_ref[...], onehot.astype(BF16))
    carry[...] = carry[...] + jnp.sum(onehot, axis=0, keepdims=True)
    cnt_ref[...] = carry[...]

    meta = jnp.where(lane == 0.0, i1, 0.0)
    meta = jnp.where(lane == 1.0, i2, meta)
    meta = jnp.where(lane == 2.0, w1, meta)
    meta = jnp.where(lane == 3.0, w2, meta)
    meta_ref[...] = meta

    token = (i * TM + lax.broadcasted_iota(jnp.int32, (TM, 1), 0)).astype(F32)
    digit_hi = jnp.floor(token * (1.0 / 256.0))
    rhs = jnp.where(lane == 0.0, digit_hi, jnp.where(lane == 1.0, token - 256.0 * digit_hi, 0.0))
    rhs = jnp.where(lane == i2 + float(SUBLANES), 1.0, rhs).astype(BF16)
    place = lax.broadcasted_iota(jnp.int32, (LANES, TM), 0).astype(F32)
    rank_rows = jnp.where(onehot > 0.0, in_tile, -1.0).T

    def entries(e, part):
        hit = jnp.where(rank_rows[e:e + 1, :] == place + float(part * LANES), 1.0, 0.0).astype(BF16)
        got = _dot(hit, rhs)
        return 256.0 * got[:, 0:1] + got[:, 1:2] + float(NT) * got[:, SUBLANES + e:SUBLANES + e + 1]

    first = [entries(e, 0) for e in range(N_EXPERTS)]
    tile_counts = jnp.sum(onehot, axis=0, keepdims=True)
    counts = [jnp.sum(tile_counts[:, e:e + 1]).astype(jnp.int32) for e in range(N_EXPERTS)]
    for e in range(N_EXPERTS):
        start = filled[e]
        filled[e] = start + counts[e]
        tab_ref[pl.ds(start, LANES), e:e + 1] = first[e]
        for part in range(1, TM // LANES):
            @pl.when(counts[e] > part * LANES)
            def _():
                tab_ref[pl.ds(start + part * LANES, LANES), e:e + 1] = entries(e, part)


def _out_ln_router(x, y, w, g, b, wr, tri):
    c2 = lambda i: (0, 0)
    return pl.pallas_call(
        _out_ln_router_kernel,
        grid=(NT // TM,),
        in_specs=[
            pl.BlockSpec((TM, D), lambda i: (i, 0)),
            pl.BlockSpec((TM, D), lambda i: (i, 0)),
            pl.BlockSpec((D, D), c2),
            pl.BlockSpec((1, D), c2),
            pl.BlockSpec((1, D), c2),
            pl.BlockSpec((D, LANES), c2),
            pl.BlockSpec((TM, TM), c2),
        ],
        out_specs=[
            pl.BlockSpec((TM, D), lambda i: (i, 0)),
            pl.BlockSpec((TM, D // 2), lambda i: (i, 0)),
            pl.BlockSpec((TM, LANES), lambda i: (i, 0)),
            pl.BlockSpec((1, LANES), c2),
            pl.BlockSpec((MOE_CAP + TM, LANES), c2),
        ],
        out_shape=[
            jax.ShapeDtypeStruct((NT, D), F32),
            jax.ShapeDtypeStruct((NT, D // 2), U32),
            jax.ShapeDtypeStruct((NT, LANES), F32),
            jax.ShapeDtypeStruct((1, LANES), F32),
            jax.ShapeDtypeStruct((MOE_CAP + TM, LANES), F32),
        ],
        scratch_shapes=[pltpu.VMEM((1, LANES), F32), pltpu.SMEM((N_EXPERTS,), jnp.int32)],
        compiler_params=_params(("arbitrary",)),
        name="out_ln_router",
    )(x, y, w, g, b, wr, tri)


def _moe_ffn_kernel(te_ref, nu_ref, tb_ref, gnext_ref, gcur_ref, sprev_ref, scur_ref, xp_ref,
                    w1_ref, w3_ref, w2_ref, out_hbm, stage, yacc, xb_scr, sem_s):
    del tb_ref
    i = pl.program_id(0)
    j = pl.program_id(1)
    used = i < nu_ref[0]
    slot = i % 2
    other = 1 - slot
    rps = MOE_ROWS_PER_STEP

    def gather_rows(tab_ref, buf, part):
        for r in range(rps):
            stage[buf, part, pl.ds(r, 1), :] = xp_ref[pl.ds(tab_ref[part * rps + r], 1), :]

    def scatter(buf, r, dst):
        return pltpu.make_async_copy(yacc.at[buf, pl.ds(r, 1)], out_hbm.at[pl.ds(dst, 1)], sem_s)

    def wait_scatters(n):
        for _ in range(n):
            scatter(0, 0, 0).wait()

    def issue_neighbours():
        gather_rows(gnext_ref, other, j)
        for r in range(rps):
            rr = j * rps + r
            scatter(other, rr, sprev_ref[rr]).start()

    @pl.when(j == 0)
    def _():
        @pl.when(i == 0)
        def _():
            yacc[1] = jnp.zeros((TMM, D), F32)
            for part in range(MOE_NFF):
                gather_rows(gcur_ref, 0, part)

        @pl.when(i > 0)
        def _():
            wait_scatters(TMM)

        words = stage[slot].reshape(TMM, D // 2)
        for half in range(2):
            xb_scr[:, half * (D // 2):(half + 1) * (D // 2)] = pltpu.unpack_elementwise(
                words, index=half, packed_dtype=BF16, unpacked_dtype=F32).astype(BF16)
        yacc[slot] = jnp.zeros((TMM, D), F32)

    @pl.when(used)
    def _():
        issue_neighbours()
        xb = xb_scr[...]
        hmid = _silu(_dot(xb, w1_ref[...])) * _dot(xb, w3_ref[...])
        yacc[slot] += _dot(hmid.astype(BF16), w2_ref[...])

    @pl.when(jnp.logical_not(used))
    def _():
        issue_neighbours()

    @pl.when((i == N_MOE_TILES - 1) & (j == MOE_NFF - 1))
    def _():
        for r in range(TMM):
            scatter(slot, r, scur_ref[r]).start()
        wait_scatters(2 * TMM)


def _moe_ffn(tile_expert, n_used, tab_block, gsrc, sdst, xp, w1, w3, w2):
    nff = MOE_NFF

    def wcol(i, j, te, nu, tb):
        return (te[i], 0, jnp.where(i < nu[0], j, nff - 1))

    def wrow(i, j, te, nu, tb):
        return (te[i], jnp.where(i < nu[0], j, nff - 1), 0)

    smem = functools.partial(pl.BlockSpec, (MOE_TAB,), memory_space=pltpu.SMEM)
    grid_spec = pltpu.PrefetchScalarGridSpec(
        num_scalar_prefetch=3,
        grid=(N_MOE_TILES, nff),
        in_specs=[
            smem(lambda i, j, te, nu, tb: (tb[i + 2],)),
            smem(lambda i, j, te, nu, tb: (tb[i + 1],)),
            smem(lambda i, j, te, nu, tb: (tb[i],)),
            smem(lambda i, j, te, nu, tb: (tb[i + 1],)),
            pl.BlockSpec((NT, D // 2), lambda i, j, te, nu, tb: (0, 0), pipeline_mode=pl.Buffered(1)),
            pl.BlockSpec((None, D, TFF), wcol),
            pl.BlockSpec((None, D, TFF), wcol),
            pl.BlockSpec((None, TFF, D), wrow),
        ],
        out_specs=pl.BlockSpec(memory_space=pl.ANY),
        scratch_shapes=[
            pltpu.VMEM((2, MOE_NFF, MOE_ROWS_PER_STEP, D // 2), U32),
            pltpu.VMEM((2, TMM, D), F32),
            pltpu.VMEM((TMM, D), BF16),
            pltpu.SemaphoreType.DMA(()),
        ],
    )
    return pl.pallas_call(
        _moe_ffn_kernel,
        grid_spec=grid_spec,
        out_shape=jax.ShapeDtypeStruct((MOE_OUT_ROWS, D), F32),
        compiler_params=_params(("arbitrary", "arbitrary")),
        name="moe_ffn",
    )(tile_expert, n_used, tab_block, gsrc, gsrc, sdst, sdst, xp, w1, w3, w2)


def _combine_kernel(x_ref, meta_ref, y0_ref, y1_ref, g_ref, b_ref, o_ref):
    meta = meta_ref[...]
    moe = meta[:, 2:3] * y0_ref[...] + meta[:, 3:4] * y1_ref[...]
    o_ref[...] = _layernorm(ALPHA * x_ref[...] + moe, g_ref[...], b_ref[...])


def _combine(x, meta, ys, g, b):
    c2 = lambda i: (0, 0)
    return pl.pallas_call(
        _combine_kernel,
        grid=(NT // TM,),
        in_specs=[
            pl.BlockSpec((TM, D), lambda i: (i, 0)),
            pl.BlockSpec((TM, LANES), lambda i: (i, 0)),
            pl.BlockSpec((TM, D), lambda i: (i, 0)),
            pl.BlockSpec((TM, D), lambda i: (i + NT // TM, 0)),
            pl.BlockSpec((1, D), c2),
            pl.BlockSpec((1, D), c2),
        ],
        out_specs=pl.BlockSpec((TM, D), lambda i: (i, 0)),
        out_shape=jax.ShapeDtypeStruct((NT, D), F32),
        compiler_params=_params(("parallel",)),
        name="moe_combine",
    )(x, meta, ys, ys, g, b)


def _pad_cols(w, n):
    return jnp.pad(w, ((0, 0), (0, n - w.shape[1])))


def kernel(x_prompt, x_sample, state_hgrn, state_gla, state_mlstm_C, state_mlstm_n, state_mlstm_m,
           state_mlstm_conv, w_in_even, hg_lower_bounds, w_gk, b_gk, gn_hg, gn_gla, w_out_even,
           w1_dense, w3_dense, w2_dense, w_in_odd, b_gate_odd, conv_w, conv_b, hn_w, w_out_odd,
           w_router, w1_moe, w3_moe, w2_moe, ln1_g, ln1_b, ln2_g, ln2_b):
    assert x_prompt.shape == (BATCH, SEQ, D) and x_sample.shape == (NS, 1, D)
    assert w_in_even.shape[0] == 1 and w_in_odd.shape[0] == 1 and hg_lower_bounds.shape[0] == 2
    masks = jnp.asarray(_gla_masks(), F32)
    tri_cs = jnp.asarray(_tri(CS, False), BF16)
    tri_tm = jnp.asarray(_tri(TM, True), BF16)
    row = lambda a: a.reshape(1, -1)

    x0 = x_prompt.reshape(NP, D)
    x0_tail = jnp.concatenate([x0[LAST_TILE * TM:], x_sample.reshape(NS, D)], axis=0)

    w_even = w_in_even[0].astype(BF16)
    z, zgr = _proj(x0, x0_tail, w_even[:, :EVEN_MAIN], _pad_cols(w_even[:, EVEN_MAIN:], LANES))
    wgk = jnp.pad(w_gk[0].astype(BF16), ((0, LANES - GLA_RANK), (0, 0)))
    lbp = hg_lower_bounds
    zs = z[NP:].reshape(NS // SG, SG, EVEN_MAIN).transpose(0, 2, 1)
    grs = zgr[NP:].reshape(NS // SG, SG, LANES).transpose(0, 2, 1)
    y_s, hg_s, gla_s = _even_sample(z, zs, grs, lbp.T, wgk.T, b_gk[0].reshape(-1, 1),
                                    row(gn_hg[0]), row(gn_gla[0]), state_hgrn[0], state_gla[0])
    y, hg_p, gla_p = _even_prompt(z, zgr, y_s, lbp, wgk, row(b_gk[0]), row(gn_hg[0]), row(gn_gla[0]), tri_cs, masks)
    w_odd = w_in_odd[0].astype(BF16)
    x2, zo, zog = _ffn(x0, x0_tail, y, w_out_even[0].astype(BF16), row(ln1_g[0]), row(ln1_b[0]),
                       w1_dense[0].astype(BF16), w3_dense[0].astype(BF16), w2_dense[0].astype(BF16),
                       row(ln2_g[0]), row(ln2_b[0]),
                       w_odd[:, :ODD_MAIN], _pad_cols(w_odd[:, ODD_MAIN:], LANES))

    bg = jnp.pad(b_gate_odd[0], (0, LANES - 2 * ML_H)).reshape(1, LANES)
    ut = zo[NP:, :D].reshape(NS // SG, SG, D).transpose(0, 2, 1)
    conv_in = state_mlstm_conv[0]
    conv_t = conv_in.reshape(NS // SG, SG, CONV_W - 1, D).transpose(0, 2, 3, 1)
    m_in = jnp.pad(state_mlstm_m[0], ((0, 0), (0, LANES - ML_H)))
    yo_s, c_s, n_s, m_s, conv_s = _odd_sample(
        zo, zog, ut, conv_in.reshape(NS, (CONV_W - 1) * D), conv_t, bg, conv_w[0], conv_w[0].T, row(conv_b[0]), conv_b[0].reshape(-1, 1),
        row(hn_w[0]), state_mlstm_C[0], state_mlstm_n[0], m_in)
    yo, c_p, n_p, m_p, conv_p = _odd_prompt(zo, zog, yo_s, bg, conv_w[0], row(conv_b[0]), row(hn_w[0]), tri_cs)

    wr = _pad_cols(w_router[0].astype(BF16), LANES)
    x3, x3p, meta, cnt, tab = _out_ln_router(
        x2, yo, w_out_odd[0].astype(BF16), row(ln1_g[1]), row(ln1_b[1]), wr, tri_tm)

    counts = cnt[0, :N_EXPERTS].astype(jnp.int32)
    padded = ((counts + TMM - 1) // TMM) * TMM
    ends = jnp.cumsum(padded)
    offsets = ends - padded
    tile = jnp.arange(N_MOE_TILES, dtype=jnp.int32)
    tile_expert = jnp.minimum(jnp.sum((tile * TMM)[:, None] >= ends[None, :], axis=1), N_EXPERTS - 1).astype(jnp.int32)
    n_tiles_used = ends[-1] // TMM
    in_use = tile < n_tiles_used
    local_tile = (tile * TMM - offsets[tile_expert]) // TMM
    blocks_per_expert = MOE_CAP // MOE_TAB
    spill_block = N_EXPERTS * blocks_per_expert
    n_spare = N_MOE_TILES - (2 * NT) // TMM
    entry = tab[:MOE_CAP, :N_EXPERTS].T.astype(jnp.int32)
    local = jnp.arange(MOE_CAP, dtype=jnp.int32)[None, :]
    valid = local < counts[:, None]
    padding = padded - counts
    pad_before = (jnp.cumsum(padding) - padding)[:, None]
    row_in_block = jnp.arange(MOE_TAB, dtype=jnp.int32)
    spare_rows = (2 * NT + TMM + jnp.sum(padding)
                  + jnp.arange(n_spare, dtype=jnp.int32)[:, None] * TMM + row_in_block[None, :])
    sdst = jnp.concatenate([
        jnp.where(valid, entry, 2 * NT + TMM + pad_before + local - counts[:, None]).reshape(-1),
        2 * NT + row_in_block, spare_rows.reshape(-1)])
    gsrc = jnp.concatenate([
        jnp.where(valid, entry - jnp.where(entry >= NT, NT, 0), 0).reshape(-1),
        jnp.zeros(((1 + n_spare) * MOE_TAB,), jnp.int32)])
    own_block = jnp.where(in_use, tile_expert * blocks_per_expert + local_tile, spill_block + 1 + tile - n_tiles_used)
    tab_block = jnp.concatenate([jnp.full((1,), spill_block, jnp.int32), own_block.astype(jnp.int32),
                                 jnp.full((1,), spill_block, jnp.int32)])

    ys = _moe_ffn(tile_expert, n_tiles_used.reshape(1), tab_block, gsrc, sdst, x3p,
                  w1_moe[0].astype(BF16), w3_moe[0].astype(BF16), w2_moe[0].astype(BF16))
    out = _combine(x3, meta, ys, row(ln2_g[1]), row(ln2_b[1]))

    y_prompt = out[:NP].reshape(BATCH, SEQ, D)
    y_sample = out[NP:].reshape(NS, 1, D)
    return (y_prompt, y_sample,
            hg_p.reshape(1, BATCH, HG_H, HG_DK, HG_DV), gla_p.reshape(1, BATCH, GLA_H, GLA_DK, GLA_DV),
            c_p[None], n_p[None], m_p[:, 0, :ML_H][None], conv_p[None],
            hg_s[None], gla_s[None], c_s[None], n_s[None], m_s[:, :ML_H][None], conv_s.reshape(1, NS, CONV_W - 1, D))
```
